```python
import math
import jax, jax.numpy as jnp
from jax import lax
import numpy as np


D_MODEL = 1024
BATCH = 2
SEQ = 8192
DEPTH = 2

HEAD_DIM = 128
BLOCK = 128
GRID_W = 64
EPS = 1e-6
NEG_INF = -1e30
RET_HEADS = D_MODEL // 256
RET_DK = 128
RET_DV = 256
RET_CHUNK = 128
RET_THETA = 10000.0
SWA_HEADS = D_MODEL // HEAD_DIM
SWA_KV_HEADS = 2
WINDOW = 128
T5_BUCKETS = 32
T5_MAX_DIST = 128
AX_HEADS = D_MODEL // HEAD_DIM
AX_KV_HEADS = 2
AX_THETA = 10000.0
D_FF = 4 * D_MODEL

N_EVEN = (DEPTH + 1) // 2
N_ODD = DEPTH // 2
RET_Q = RET_HEADS * RET_DK
RET_V = RET_HEADS * RET_DV
SWA_Q = SWA_HEADS * HEAD_DIM
SWA_KV = SWA_KV_HEADS * HEAD_DIM
EVEN_IN = 2 * RET_Q + 2 * RET_V + SWA_Q + 2 * SWA_KV
EVEN_OUT = RET_V + SWA_Q
AX_Q = AX_HEADS * HEAD_DIM
AX_KV = AX_KV_HEADS * HEAD_DIM
ODD_IN = AX_Q + 2 * AX_KV

kernel_name = "hybrid_retention_swa_axialrope_encoder"


def rms_norm(x, g):
    xf = x.astype(jnp.float32)
    y = xf * lax.rsqrt(jnp.mean(xf * xf, axis=-1, keepdims=True) + EPS)
    return (y * g.astype(jnp.float32)).astype(x.dtype)


def split_cols(a, sizes):
    offs, o = [], 0
    for s in sizes[:-1]:
        o += s
        offs.append(o)
    return jnp.split(a, offs, axis=-1)


def rope_angles(pos, dim, theta):
    inv = theta ** (-jnp.arange(0, dim, 2, dtype=jnp.float32) / dim)
    return pos.astype(jnp.float32)[:, None] * inv[None, :]


def apply_rope(x, ang):
    d2 = x.shape[-1] // 2
    xf = x.astype(jnp.float32)
    x1, x2 = xf[..., :d2], xf[..., d2:]
    c = jnp.cos(ang)[None, :, None, :]
    s = jnp.sin(ang)[None, :, None, :]
    return jnp.concatenate([x1 * c - x2 * s, x2 * c + x1 * s], axis=-1).astype(x.dtype)


def retention_direction(q, k, v, log_gamma, strict):
    Bn, S, H, dk = q.shape
    dv = v.shape[-1]
    C = RET_CHUNK
    nc = S // C
    qc = q.reshape(Bn, nc, C, H, dk)
    kc = k.reshape(Bn, nc, C, H, dk)
    vc = v.reshape(Bn, nc, C, H, dv)
    idx = jnp.arange(C, dtype=jnp.float32)
    diff = idx[:, None] - idx[None, :]
    mask = (diff > 0) if strict else (diff >= 0)
    decay = jnp.where(mask[None], jnp.exp(jnp.maximum(diff, 0.0)[None] * log_gamma[:, None, None]), 0.0)
    scores = jnp.einsum('bnrhd,bnjhd->bnhrj', qc, kc) * decay[None, None]
    inner = jnp.einsum('bnhrj,bnjhe->bnrhe', scores, vc)
    k_w = kc * jnp.exp((C - 1 - idx)[:, None] * log_gamma[None, :])[None, None, :, :, None]
    U = jnp.einsum('bnjhd,bnjhe->nbhde', k_w, vc)
    chunk_decay = jnp.exp(C * log_gamma)[None, :, None, None]

    def step(state, u):
        return chunk_decay * state + u, state

    _, prev = lax.scan(step, jnp.zeros((Bn, H, dk, dv), jnp.float32), U)
    q_w = qc * jnp.exp((idx + 1)[:, None] * log_gamma[None, :])[None, None, :, :, None]
    cross = jnp.einsum('bnrhd,nbhde->bnrhe', q_w, prev)
    return (inner + cross).reshape(Bn, S, H, dv)


def retention_mixer(q, k, v, g, decay_logit, gn_gain):
    Bn, S, _ = q.shape
    dt = v.dtype
    q = q.reshape(Bn, S, RET_HEADS, RET_DK)
    k = k.reshape(Bn, S, RET_HEADS, RET_DK)
    v = v.reshape(Bn, S, RET_HEADS, RET_DV).astype(jnp.float32)
    ang = rope_angles(jnp.arange(S), RET_DK, RET_THETA)
    q = apply_rope(q, ang).astype(jnp.float32)
    k = apply_rope(k, ang).astype(jnp.float32) * (RET_DK ** -0.5)
    log_gamma = jax.nn.log_sigmoid(decay_logit.astype(jnp.float32))
    fwd = retention_direction(q, k, v, log_gamma[0], False)
    bwd = jnp.flip(retention_direction(jnp.flip(q, 1), jnp.flip(k, 1), jnp.flip(v, 1),
                                       log_gamma[1], True), 1)
    y = rms_norm(fwd + bwd, gn_gain.reshape(RET_HEADS, RET_DV))
    y = y.reshape(Bn, S, RET_V).astype(dt)
    return jax.nn.silu(g) * y


def t5_bucket(rel):
    nb = T5_BUCKETS // 2
    max_exact = nb // 2
    ret = jnp.where(rel > 0, nb, 0)
    n = jnp.abs(rel)
    nf = jnp.maximum(n, 1).astype(jnp.float32)
    large = max_exact + (jnp.log(nf / max_exact) / math.log(T5_MAX_DIST / max_exact)
                         * (nb - max_exact)).astype(jnp.int32)
    large = jnp.minimum(large, nb - 1)
    return ret + jnp.where(n < max_exact, n, large)


def window_attention(q, k, v, sink, t5_table):
    Bn, S, Hq, D = q.shape
    Hkv = k.shape[2]
    G = Hq // Hkv
    nb = S // BLOCK
    pad = ((0, 0), (BLOCK, BLOCK), (0, 0), (0, 0))
    kp = jnp.pad(k, pad).reshape(Bn, nb + 2, BLOCK, Hkv, D)
    vp = jnp.pad(v, pad).reshape(Bn, nb + 2, BLOCK, Hkv, D)
    kw = jnp.concatenate([kp[:, :nb], kp[:, 1:nb + 1], kp[:, 2:]], axis=2)
    vw = jnp.concatenate([vp[:, :nb], vp[:, 1:nb + 1], vp[:, 2:]], axis=2)
    qb = q.reshape(Bn, nb, BLOCK, Hkv, G, D)
    s = jnp.einsum('bnqkgd,bnjkd->bnkgqj', qb, kw,
                   preferred_element_type=jnp.float32) * (D ** -0.5)
    r = jnp.arange(BLOCK)
    j = jnp.arange(3 * BLOCK)
    rel = j[None, :] - BLOCK - r[:, None]
    bias = t5_table.astype(jnp.float32)[t5_bucket(rel)]
    bias = bias.transpose(2, 0, 1).reshape(Hkv, G, BLOCK, 3 * BLOCK)
    kpos = (jnp.arange(nb)[:, None] - 1) * BLOCK + j[None, :]
    valid = (jnp.abs(rel) <= WINDOW)[None] & ((kpos >= 0) & (kpos < S))[:, None, :]
    s = jnp.where(valid[None, :, None, None], s + bias[None, None], NEG_INF)
    sink_l = sink.astype(jnp.float32).reshape(Hkv, G)[None, None, :, :, None, None]
    m = jnp.maximum(jnp.max(s, axis=-1, keepdims=True), sink_l)
    p = jnp.exp(s - m)
    p = p / (jnp.sum(p, axis=-1, keepdims=True) + jnp.exp(sink_l - m))
    o = jnp.einsum('bnkgqj,bnjkd->bnqkgd', p.astype(v.dtype), vw)
    return o.reshape(Bn, S, Hq * D)


def axial_attention(q, k, v):
    Bn, S, Hq, D = q.shape
    Hkv = k.shape[2]
    G = Hq // Hkv
    rows = S // GRID_W
    row = jnp.repeat(jnp.arange(rows), GRID_W)
    col = jnp.tile(jnp.arange(GRID_W), rows)
    half = D // 2
    ang_r = rope_angles(row, half, AX_THETA)
    ang_c = rope_angles(col, half, AX_THETA)
    q = jnp.concatenate([apply_rope(q[..., :half], ang_r), apply_rope(q[..., half:], ang_c)], axis=-1)
    k = jnp.concatenate([apply_rope(k[..., :half], ang_r), apply_rope(k[..., half:], ang_c)], axis=-1)
    nb = S // BLOCK
    qb = q.reshape(Bn, nb, BLOCK, Hkv, G, D).transpose(1, 0, 2, 3, 4, 5)
    scale = D ** -0.5

    def block(qi):
        s = jnp.einsum('bqkgd,bjkd->bkgqj', qi, k, preferred_element_type=jnp.float32) * scale
        p = jax.nn.softmax(s, axis=-1)
        return jnp.einsum('bkgqj,bjkd->bqkgd', p.astype(v.dtype), v)

    o = lax.map(block, qb)
    return o.transpose(1, 0, 2, 3, 4, 5).reshape(Bn, S, Hq * D)


def setup_inputs(seed: int = 0) -> dict:
    key = jax.random.key(seed)
    ks = jax.random.split(key, 20)
    f32 = jnp.float32

    def w(k, shape, fan_in):
        return jax.random.normal(k, shape, f32) * (fan_in ** -0.5)

    def gain(k, shape):
        return 1.0 + 0.05 * jax.random.normal(k, shape, f32)

    base_logit = jnp.log(2.0 ** (5.0 + jnp.arange(RET_HEADS, dtype=f32)) - 1.0)
    return {
        'x': jax.random.normal(ks[0], (BATCH, SEQ, D_MODEL), f32),
        'norm_mix': gain(ks[1], (DEPTH, D_MODEL)),
        'norm_mlp': gain(ks[2], (DEPTH, D_MODEL)),
        'w_in_even': w(ks[3], (N_EVEN, D_MODEL, EVEN_IN), D_MODEL),
        'w_out_even': w(ks[4], (N_EVEN, EVEN_OUT, D_MODEL), EVEN_OUT),
        'ret_decay_logit': base_logit[None, None, :] + 0.1 * jax.random.normal(ks[5], (N_EVEN, 2, RET_HEADS), f32),
        'ret_norm': gain(ks[6], (N_EVEN, RET_V)),
        'swa_q_norm': gain(ks[7], (N_EVEN, HEAD_DIM)),
        'swa_k_norm': gain(ks[8], (N_EVEN, HEAD_DIM)),
        'swa_sink': 0.5 * jax.random.normal(ks[9], (N_EVEN, SWA_HEADS), f32),
        't5_table': 0.5 * jax.random.normal(ks[10], (T5_BUCKETS, SWA_HEADS), f32),
        'w_in_odd': w(ks[11], (N_ODD, D_MODEL, ODD_IN), D_MODEL),
        'w_out_odd': w(ks[12], (N_ODD, AX_Q, D_MODEL), AX_Q),
        'ax_q_norm': gain(ks[13], (N_ODD, HEAD_DIM)),
        'ax_k_norm': gain(ks[14], (N_ODD, HEAD_DIM)),
        'w_mlp_up': w(ks[15], (DEPTH, D_MODEL, D_FF), D_MODEL),
        'w_mlp_down': w(ks[16], (DEPTH, D_FF, D_MODEL), D_FF),
    }


def reference(x, norm_mix, norm_mlp, w_in_even, w_out_even, ret_decay_logit, ret_norm,
              swa_q_norm, swa_k_norm, swa_sink, t5_table, w_in_odd, w_out_odd,
              ax_q_norm, ax_k_norm, w_mlp_up, w_mlp_down):
    Bn, S, _ = x.shape
    for layer in range(DEPTH):
        h = rms_norm(x, norm_mix[layer])
        if layer % 2 == 0:
            i = layer // 2
            proj = h @ w_in_even[i]
            qa, ka, va, ga, qb, kb, vb = split_cols(
                proj, [RET_Q, RET_Q, RET_V, RET_V, SWA_Q, SWA_KV, SWA_KV])
            ya = retention_mixer(qa, ka, va, ga, ret_decay_logit[i], ret_norm[i])
            qb = rms_norm(qb.reshape(Bn, S, SWA_HEADS, HEAD_DIM), swa_q_norm[i])
            kb = rms_norm(kb.reshape(Bn, S, SWA_KV_HEADS, HEAD_DIM), swa_k_norm[i])
            vb = vb.reshape(Bn, S, SWA_KV_HEADS, HEAD_DIM)
            yb = window_attention(qb, kb, vb, swa_sink[i], t5_table)
            y = jnp.concatenate([ya, yb], axis=-1) @ w_out_even[i]
        else:
            i = layer // 2
            proj = h @ w_in_odd[i]
            qc, kc, vc = split_cols(proj, [AX_Q, AX_KV, AX_KV])
            qc = rms_norm(qc.reshape(Bn, S, AX_HEADS, HEAD_DIM), ax_q_norm[i])
            kc = rms_norm(kc.reshape(Bn, S, AX_KV_HEADS, HEAD_DIM), ax_k_norm[i])
            vc = vc.reshape(Bn, S, AX_KV_HEADS, HEAD_DIM)
            y = axial_attention(qc, kc, vc) @ w_out_odd[i]
        x = x + y
        h = rms_norm(x, norm_mlp[layer])
        x = x + jnp.square(jax.nn.relu(h @ w_mlp_up[layer])) @ w_mlp_down[layer]
    return x
```

```python
import functools
import math

import jax
import jax.numpy as jnp
from jax import lax
from jax.experimental import pallas as pl
from jax.experimental.pallas import tpu as pltpu

F32 = jnp.float32
BF16 = jnp.bfloat16

EPS = 1e-6
NEG_INF = -1e30
HEAD_DIM = 128
BLOCK = 128
GRID_W = 64
RET_DK = 128
RET_DV = 256
RET_CHUNK = 128
RET_THETA = 10000.0
SWA_KV_HEADS = 2
WINDOW = 128
T5_BUCKETS = 32
T5_MAX_DIST = 128
AX_KV_HEADS = 2
AX_THETA = 10000.0

VMEM_LIMIT_BYTES = 56 * 1024 * 1024


def _params(semantics):
    return pltpu.CompilerParams(dimension_semantics=semantics, vmem_limit_bytes=VMEM_LIMIT_BYTES)


def _rms(x, gain):
    ms = jnp.mean(x * x, axis=-1, keepdims=True)
    return x * lax.rsqrt(ms + EPS) * gain


def _dot(a, b):
    return jnp.dot(a, b, preferred_element_type=F32)


def _dot_nt(a, b):
    return lax.dot_general(a, b, (((1,), (1,)), ((), ())), preferred_element_type=F32)


def _dot_tn(a, b):
    return lax.dot_general(a, b, (((0,), (0,)), ((), ())), preferred_element_type=F32)


def _norm_proj_kernel(x_ref, g_ref, w_ref, o_ref, h_ref):
    @pl.when(pl.program_id(1) == 0)
    def _():
        h_ref[...] = _rms(x_ref[...], g_ref[...]).astype(BF16)

    o_ref[...] = _dot(h_ref[...], w_ref[...]).astype(o_ref.dtype)


def norm_proj(x2, gain, w, *, tm, tn):
    t, d = x2.shape
    n = w.shape[1]
    return pl.pallas_call(
        _norm_proj_kernel,
        grid=(t // tm, n // tn),
        in_specs=[
            pl.BlockSpec((tm, d), lambda i, j: (i, 0)),
            pl.BlockSpec((1, d), lambda i, j: (0, 0)),
            pl.BlockSpec((d, tn), lambda i, j: (0, j)),
        ],
        out_specs=pl.BlockSpec((tm, tn), lambda i, j: (i, j)),
        out_shape=jax.ShapeDtypeStruct((t, n), BF16),
        scratch_shapes=[pltpu.VMEM((tm, d), BF16)],
        compiler_params=_params(("parallel", "arbitrary")),
        name="norm_proj",
    )(x2, gain.reshape(1, d), w)


def _out_proj_kernel(*refs):
    x_ref, o_ref = refs[0], refs[-1]
    n_in = (len(refs) - 2) // 2
    acc = x_ref[...]
    for i in range(n_in):
        acc = acc + _dot(refs[1 + i][...], refs[1 + n_in + i][...])
    o_ref[...] = acc


def out_proj(x2, acts, w, *, tm):
    t, d = x2.shape
    ks = [a.shape[1] for a in acts]
    assert len(set(ks)) == 1 and sum(ks) == w.shape[0]
    k = ks[0]
    in_specs = [pl.BlockSpec((tm, d), lambda i: (i, 0))]
    in_specs += [pl.BlockSpec((tm, k), lambda i: (i, 0)) for _ in acts]
    in_specs += [pl.BlockSpec((k, d), functools.partial(lambda i, s: (s, 0), s=s)) for s in range(len(acts))]
    return pl.pallas_call(
        _out_proj_kernel,
        grid=(t // tm,),
        in_specs=in_specs,
        out_specs=pl.BlockSpec((tm, d), lambda i: (i, 0)),
        out_shape=jax.ShapeDtypeStruct((t, d), F32),
        compiler_params=_params(("parallel",)),
        name="out_proj",
    )(x2, *acts, *([w] * len(acts)))


def _mlp_kernel(x_ref, g_ref, wu_ref, wd_ref, o_ref, h_ref):
    @pl.when(pl.program_id(1) == 0)
    def _():
        x = x_ref[...]
        h_ref[...] = _rms(x, g_ref[...]).astype(BF16)
        o_ref[...] = x

    u = _dot(h_ref[...], wu_ref[...])
    a = jnp.square(jnp.maximum(u, 0.0)).astype(BF16)
    o_ref[...] += _dot(a, wd_ref[...])


def mlp(x2, gain, wu, wd, *, tm, tf):
    t, d = x2.shape
    ff = wu.shape[1]
    return pl.pallas_call(
        _mlp_kernel,
        grid=(t // tm, ff // tf),
        in_specs=[
            pl.BlockSpec((tm, d), lambda i, j: (i, 0)),
            pl.BlockSpec((1, d), lambda i, j: (0, 0)),
            pl.BlockSpec((d, tf), lambda i, j: (0, j)),
            pl.BlockSpec((tf, d), lambda i, j: (j, 0)),
        ],
        out_specs=pl.BlockSpec((tm, d), lambda i, j: (i, 0)),
        out_shape=jax.ShapeDtypeStruct((t, d), F32),
        scratch_shapes=[pltpu.VMEM((tm, d), BF16)],
        compiler_params=_params(("parallel", "arbitrary")),
        name="mlp",
    )(x2, gain.reshape(1, d), wu, wd)


def _log_sigmoid(x):
    return -(jnp.maximum(-x, 0.0) + jnp.log1p(jnp.exp(-jnp.abs(x))))


def _ret_kernel(dl_ref, q_ref, k_ref, v_ref, g_ref, cos_ref, sin_ref, gn_ref, o_ref,
                sb_ref, sf_ref, cb_ref, *, nc):
    C = RET_CHUNK
    h = pl.program_id(1)
    phase = pl.program_id(2)
    t = pl.program_id(3)

    cos = cos_ref[...]
    sin = sin_ref[...]

    def rope(a):
        return a * cos + pltpu.roll(a, RET_DK // 2, 1) * sin

    qr = rope(q_ref[0].astype(F32))
    kr = rope(k_ref[0].astype(F32)) * (RET_DK ** -0.5)
    v = v_ref[0]

    row = lax.broadcasted_iota(jnp.int32, (C, C), 0).astype(F32)
    col = lax.broadcasted_iota(jnp.int32, (C, C), 1).astype(F32)
    lgf = _log_sigmoid(jnp.full((C, C), dl_ref[0, h], F32))
    lgb = _log_sigmoid(jnp.full((C, C), dl_ref[1, h], F32))
    lgf_w = _log_sigmoid(jnp.full((C, RET_DV), dl_ref[0, h], F32))
    lgb_w = _log_sigmoid(jnp.full((C, RET_DV), dl_ref[1, h], F32))

    @pl.when(phase == 0)
    def _():
        @pl.when(t == 0)
        def _():
            sb_ref[...] = jnp.zeros_like(sb_ref)

        c = nc - 1 - t
        sb = sb_ref[...]
        qd = qr * jnp.exp((C - row) * lgb)
        cb_ref[c] = _dot(qd.astype(BF16), sb.astype(BF16))
        kd = kr * jnp.exp(row * lgb)
        sb_ref[...] = jnp.exp(C * lgb_w) * sb + _dot_tn(kd.astype(BF16), v)

    @pl.when(phase == 1)
    def _():
        @pl.when(t == 0)
        def _():
            sf_ref[...] = jnp.zeros_like(sf_ref)

        sf = sf_ref[...]
        s = _dot_nt(qr.astype(BF16), kr.astype(BF16))
        diff = row - col
        decay = jnp.where(diff >= 0, jnp.exp(jnp.maximum(diff, 0.0) * lgf),
                          jnp.exp(jnp.maximum(-diff, 0.0) * lgb))
        inner = _dot((s * decay).astype(BF16), v)
        qd = qr * jnp.exp((row + 1.0) * lgf)
        cross = _dot(qd.astype(BF16), sf.astype(BF16))
        kd = kr * jnp.exp((C - 1.0 - row) * lgf)
        sf_ref[...] = jnp.exp(C * lgf_w) * sf + _dot_tn(kd.astype(BF16), v)

        tot = inner + cross + cb_ref[t]
        y = _rms(tot, gn_ref[...])
        g = g_ref[0].astype(F32)
        o_ref[0] = (g * jax.nn.sigmoid(g) * y).astype(o_ref.dtype)


def retention(proj, decay_logit, gn_gain, cos2, sin2, *, n_heads, q_off, k_off, v_off, g_off):
    b, s, _ = proj.shape
    C = RET_CHUNK
    nc = s // C
    ret_v = n_heads * RET_DV

    def chunk(p, t):
        return jnp.where(p == 0, nc - 1 - t, t)

    def spec(width, off):
        base = off // width
        return pl.BlockSpec((1, C, width), lambda bi, h, p, t: (bi, chunk(p, t), base + h))

    kern = functools.partial(_ret_kernel, nc=nc)
    return pl.pallas_call(
        kern,
        grid=(b, n_heads, 2, nc),
        in_specs=[
            pl.BlockSpec(memory_space=pltpu.SMEM),
            spec(RET_DK, q_off),
            spec(RET_DK, k_off),
            spec(RET_DV, v_off),
            spec(RET_DV, g_off),
            pl.BlockSpec((C, RET_DK), lambda bi, h, p, t: (chunk(p, t), 0)),
            pl.BlockSpec((C, RET_DK), lambda bi, h, p, t: (chunk(p, t), 0)),
            pl.BlockSpec((1, RET_DV), lambda bi, h, p, t: (0, h)),
        ],
        out_specs=pl.BlockSpec((1, C, RET_DV), lambda bi, h, p, t: (bi, jnp.where(p == 0, 0, t), h)),
        out_shape=jax.ShapeDtypeStruct((b, s, ret_v), BF16),
        scratch_shapes=[
            pltpu.VMEM((RET_DK, RET_DV), F32),
            pltpu.VMEM((RET_DK, RET_DV), F32),
            pltpu.VMEM((nc, C, RET_DV), F32),
        ],
        compiler_params=_params(("parallel", "parallel", "arbitrary", "arbitrary")),
        name="retention",
    )(decay_logit, proj, proj, proj, proj, cos2, sin2, gn_gain.reshape(1, ret_v))


def _swa_kernel(sink_ref, table_ref, q_ref, kp_ref, kc_ref, kn_ref, vp_ref, vc_ref, vn_ref,
                bucket_ref, qg_ref, kg_ref, o_ref, bias_ref, *, n_heads, n_kv, nb):
    i = pl.program_id(1)
    D = HEAD_DIM
    G = n_heads // n_kv

    @pl.when((pl.program_id(0) == 0) & (i == 0))
    def _():
        bucket = bucket_ref[...]
        for h in range(n_heads):
            def body(bk, acc):
                return jnp.where(bucket == bk, table_ref[bk, h], acc)
            bias_ref[h] = lax.fori_loop(0, T5_BUCKETS, body, jnp.zeros((BLOCK, 3 * BLOCK), F32))

    r = lax.broadcasted_iota(jnp.int32, (BLOCK, 3 * BLOCK), 0)
    j = lax.broadcasted_iota(jnp.int32, (BLOCK, 3 * BLOCK), 1)
    rel = j - BLOCK - r
    kpos = (i - 1) * BLOCK + j
    valid = (jnp.abs(rel) <= WINDOW) & (kpos >= 0) & (kpos < nb * BLOCK)

    qg = qg_ref[...]
    kg = kg_ref[...]
    for kv in range(n_kv):
        sl = slice(kv * D, (kv + 1) * D)
        k3 = jnp.concatenate([kp_ref[0][:, sl], kc_ref[0][:, sl], kn_ref[0][:, sl]], axis=0)
        k3 = _rms(k3.astype(F32), kg).astype(BF16)
        v3 = jnp.concatenate([vp_ref[0][:, sl], vc_ref[0][:, sl], vn_ref[0][:, sl]], axis=0)
        for g in range(G):
            h = kv * G + g
            hs = slice(h * D, (h + 1) * D)
            q = _rms(q_ref[0][:, hs].astype(F32), qg) * (D ** -0.5)
            s = _dot_nt(q.astype(BF16), k3) + bias_ref[h]
            s = jnp.where(valid, s, NEG_INF)
            sink = sink_ref[h]
            m = jnp.maximum(jnp.max(s, axis=-1, keepdims=True), sink)
            p = jnp.exp(s - m)
            denom = jnp.sum(p, axis=-1, keepdims=True) + jnp.exp(sink - m)
            o = _dot(p.astype(BF16), v3) / denom
            o_ref[0, :, hs] = o.astype(o_ref.dtype)


def _t5_bucket(rel):
    nb = T5_BUCKETS // 2
    max_exact = nb // 2
    ret = jnp.where(rel > 0, nb, 0)
    n = jnp.abs(rel)
    nf = jnp.maximum(n, 1).astype(jnp.float32)
    large = max_exact + (jnp.log(nf / max_exact) / math.log(T5_MAX_DIST / max_exact)
                         * (nb - max_exact)).astype(jnp.int32)
    large = jnp.minimum(large, nb - 1)
    return ret + jnp.where(n < max_exact, n, large)


def window_attention(proj, sink, t5_table, q_gain, k_gain, *, n_heads, n_kv, q_off, k_off, v_off):
    b, s, _ = proj.shape
    D = HEAD_DIM
    nb = s // BLOCK
    qw = n_heads * D
    kw = n_kv * D
    rr = jnp.arange(BLOCK)
    jj = jnp.arange(3 * BLOCK)
    bucket = _t5_bucket(jj[None, :] - BLOCK - rr[:, None]).astype(jnp.int32)

    def kv_spec(off, shift):
        base = off // kw
        return pl.BlockSpec((1, BLOCK, kw),
                            lambda bi, i: (bi, jnp.clip(i + shift, 0, nb - 1), base))

    kern = functools.partial(_swa_kernel, n_heads=n_heads, n_kv=n_kv, nb=nb)
    return pl.pallas_call(
        kern,
        grid=(b, nb),
        in_specs=[
            pl.BlockSpec(memory_space=pltpu.SMEM),
            pl.BlockSpec(memory_space=pltpu.SMEM),
            pl.BlockSpec((1, BLOCK, qw), lambda bi, i: (bi, i, q_off // qw)),
            kv_spec(k_off, -1), kv_spec(k_off, 0), kv_spec(k_off, 1),
            kv_spec(v_off, -1), kv_spec(v_off, 0), kv_spec(v_off, 1),
            pl.BlockSpec((BLOCK, 3 * BLOCK), lambda bi, i: (0, 0)),
            pl.BlockSpec((1, D), lambda bi, i: (0, 0)),
            pl.BlockSpec((1, D), lambda bi, i: (0, 0)),
        ],
        out_specs=pl.BlockSpec((1, BLOCK, qw), lambda bi, i: (bi, i, 0)),
        out_shape=jax.ShapeDtypeStruct((b, s, qw), BF16),
        scratch_shapes=[pltpu.VMEM((n_heads, BLOCK, 3 * BLOCK), F32)],
        compiler_params=_params(("arbitrary", "arbitrary")),
        name="window_attention",
    )(sink, t5_table, proj, proj, proj, proj, proj, proj, proj, bucket,
      q_gain.reshape(1, D), k_gain.reshape(1, D))


def _axial_rope(a, cc, sa, sb):
    q4 = HEAD_DIM // 4
    return a * cc + pltpu.roll(a, HEAD_DIM - q4, 1) * sa + pltpu.roll(a, q4, 1) * sb


def _ax_prep_kernel(q_ref, k_ref, cc_ref, sa_ref, sb_ref, qg_ref, kg_ref, qo_ref, ko_ref,
                    *, n_heads, n_kv):
    D = HEAD_DIM
    cc, sa, sb = cc_ref[...], sa_ref[...], sb_ref[...]
    qg = qg_ref[...] * (D ** -0.5)
    kg = kg_ref[...]
    for h in range(n_heads):
        hs = slice(h * D, (h + 1) * D)
        q = _rms(q_ref[0][:, hs].astype(F32), qg)
        qo_ref[0, :, hs] = _axial_rope(q, cc, sa, sb).astype(qo_ref.dtype)
    for h in range(n_kv):
        hs = slice(h * D, (h + 1) * D)
        k = _rms(k_ref[0][:, hs].astype(F32), kg)
        ko_ref[0, :, hs] = _axial_rope(k, cc, sa, sb).astype(ko_ref.dtype)


def axial_prep(proj, tables, q_gain, k_gain, *, n_heads, n_kv, ts):
    b, s, _ = proj.shape
    D = HEAD_DIM
    qw, kw = n_heads * D, n_kv * D
    kern = functools.partial(_ax_prep_kernel, n_heads=n_heads, n_kv=n_kv)
    tab = pl.BlockSpec((ts, D), lambda bi, i: (i, 0))
    return pl.pallas_call(
        kern,
        grid=(b, s // ts),
        in_specs=[
            pl.BlockSpec((1, ts, qw), lambda bi, i: (bi, i, 0)),
            pl.BlockSpec((1, ts, kw), lambda bi, i: (bi, i, qw // kw)),
            tab, tab, tab,
            pl.BlockSpec((1, D), lambda bi, i: (0, 0)),
            pl.BlockSpec((1, D), lambda bi, i: (0, 0)),
        ],
        out_specs=[
            pl.BlockSpec((1, ts, qw), lambda bi, i: (bi, i, 0)),
            pl.BlockSpec((1, ts, kw), lambda bi, i: (bi, i, 0)),
        ],
        out_shape=[jax.ShapeDtypeStruct((b, s, qw), BF16), jax.ShapeDtypeStruct((b, s, kw), BF16)],
        compiler_params=_params(("parallel", "parallel")),
        name="axial_prep",
    )(proj, proj, *tables, q_gain.reshape(1, D), k_gain.reshape(1, D))


def _flash_kernel(q_ref, k_ref, v_ref, o_ref, m_ref, l_ref, acc_ref, *, G, tq, tk, nk):
    D = HEAD_DIM
    qs = jnp.concatenate([q_ref[0][:, g * D:(g + 1) * D] for g in range(G)], axis=0)
    m_ref[...] = jnp.full_like(m_ref, -jnp.inf)
    l_ref[...] = jnp.zeros_like(l_ref)
    acc_ref[...] = jnp.zeros_like(acc_ref)

    def body(c, carry):
        start = pl.multiple_of(c * tk, tk)
        kc = k_ref[0, pl.ds(start, tk), :]
        vc = v_ref[0, pl.ds(start, tk), :]
        s = _dot_nt(qs, kc)
        m_old = m_ref[...]
        m_new = jnp.maximum(m_old, jnp.max(s, axis=-1, keepdims=True))
        alpha = jnp.exp(m_old - m_new)
        p = jnp.exp(s - m_new)
        l_ref[...] = alpha * l_ref[...] + jnp.sum(p, axis=-1, keepdims=True)
        acc_ref[...] = alpha * acc_ref[...] + _dot(p.astype(BF16), vc)
        m_ref[...] = m_new
        return carry

    lax.fori_loop(0, nk, body, 0)
    out = acc_ref[...] / l_ref[...]
    for g in range(G):
        o_ref[0, :, g * D:(g + 1) * D] = out[g * tq:(g + 1) * tq].astype(o_ref.dtype)


def flash_attention(q, k, v_src, *, n_heads, n_kv, v_off, tq, tk):
    b, s, _ = q.shape
    D = HEAD_DIM
    G = n_heads // n_kv
    kern = functools.partial(_flash_kernel, G=G, tq=tq, tk=tk, nk=s // tk)
    return pl.pallas_call(
        kern,
        grid=(b, n_kv, s // tq),
        in_specs=[
            pl.BlockSpec((1, tq, G * D), lambda bi, kv, qi: (bi, qi, kv)),
            pl.BlockSpec((1, s, D), lambda bi, kv, qi: (bi, 0, kv)),
            pl.BlockSpec((1, s, D), lambda bi, kv, qi: (bi, 0, v_off // D + kv)),
        ],
        out_specs=pl.BlockSpec((1, tq, G * D), lambda bi, kv, qi: (bi, qi, kv)),
        out_shape=jax.ShapeDtypeStruct((b, s, n_heads * D), BF16),
        scratch_shapes=[
            pltpu.VMEM((G * tq, 1), F32),
            pltpu.VMEM((G * tq, 1), F32),
            pltpu.VMEM((G * tq, D), F32),
        ],
        compiler_params=_params(("parallel", "parallel", "arbitrary")),
        name="flash_attention",
    )(q, k, v_src)


def _rope_angles(pos, dim, theta):
    inv = theta ** (-jnp.arange(0, dim, 2, dtype=jnp.float32) / dim)
    return pos.astype(jnp.float32)[:, None] * inv[None, :]


def _retention_tables(s):
    ang = _rope_angles(jnp.arange(s), RET_DK, RET_THETA)
    c, sn = jnp.cos(ang), jnp.sin(ang)
    return jnp.concatenate([c, c], axis=-1), jnp.concatenate([-sn, sn], axis=-1)


def _axial_tables(s):
    rows = s // GRID_W
    row = jnp.repeat(jnp.arange(rows), GRID_W)
    col = jnp.tile(jnp.arange(GRID_W), rows)
    half = HEAD_DIM // 2
    ar = _rope_angles(row, half, AX_THETA)
    ac = _rope_angles(col, half, AX_THETA)
    cr, sr, ccol, scol = jnp.cos(ar), jnp.sin(ar), jnp.cos(ac), jnp.sin(ac)
    z = jnp.zeros_like(sr)
    cc = jnp.concatenate([cr, cr, ccol, ccol], axis=-1)
    sa = jnp.concatenate([-sr, z, -scol, z], axis=-1)
    sb = jnp.concatenate([z, sr, z, scol], axis=-1)
    return cc, sa, sb


def kernel(x, norm_mix, norm_mlp, w_in_even, w_out_even, ret_decay_logit, ret_norm, swa_q_norm,
           swa_k_norm, swa_sink, t5_table, w_in_odd, w_out_odd, ax_q_norm, ax_k_norm, w_mlp_up,
           w_mlp_down):
    b, s, d = x.shape
    t = b * s
    depth = norm_mix.shape[0]
    ret_heads = ret_decay_logit.shape[-1]
    ret_q = ret_heads * RET_DK
    ret_v = ret_heads * RET_DV
    swa_heads = swa_sink.shape[-1]
    swa_q = swa_heads * HEAD_DIM
    swa_kv = SWA_KV_HEADS * HEAD_DIM
    ax_q = w_out_odd.shape[1]
    ax_heads = ax_q // HEAD_DIM
    ax_kv = AX_KV_HEADS * HEAD_DIM

    x2 = x.reshape(t, d)
    ret_tabs = _retention_tables(s)
    ax_tabs = _axial_tables(s)

    for layer in range(depth):
        i = layer // 2
        if layer % 2 == 0:
            proj = norm_proj(x2, norm_mix[layer], w_in_even[i].astype(BF16), tm=1024, tn=512)
            proj = proj.reshape(b, s, -1)
            ya = retention(proj, ret_decay_logit[i], ret_norm[i], *ret_tabs, n_heads=ret_heads,
                           q_off=0, k_off=ret_q, v_off=2 * ret_q, g_off=2 * ret_q + ret_v)
            off = 2 * ret_q + 2 * ret_v
            yb = window_attention(proj, swa_sink[i], t5_table, swa_q_norm[i], swa_k_norm[i],
                                  n_heads=swa_heads, n_kv=SWA_KV_HEADS, q_off=off,
                                  k_off=off + swa_q, v_off=off + swa_q + swa_kv)
            x2 = out_proj(x2, [ya.reshape(t, -1), yb.reshape(t, -1)], w_out_even[i].astype(BF16),
                          tm=1024)
        else:
            proj = norm_proj(x2, norm_mix[layer], w_in_odd[i].astype(BF16), tm=1024, tn=512)
            proj = proj.reshape(b, s, -1)
            qp, kp = axial_prep(proj, ax_tabs, ax_q_norm[i], ax_k_norm[i], n_heads=ax_heads,
                                n_kv=AX_KV_HEADS, ts=512)
            y = flash_attention(qp, kp, proj, n_heads=ax_heads, n_kv=AX_KV_HEADS,
                                v_off=ax_q + ax_kv, tq=256, tk=512)
            x2 = out_proj(x2, [y.reshape(t, -1)], w_out_odd[i].astype(BF16), tm=1024)
        x2 = mlp(x2, norm_mlp[layer], w_mlp_up[layer].astype(BF16), w_mlp_down[layer].astype(BF16),
                 tm=1024, tf=512)
    return x2.reshape(b, s, d)
```

```python
import functools
import math

import jax
import jax.numpy as jnp
from jax import lax
from jax.experimental import pallas as pl
from jax.experimental.pallas import tpu as pltpu

F32 = jnp.float32
BF16 = jnp.bfloat16

EPS = 1e-6
NEG_INF = -1e30
HEAD_DIM = 128
BLOCK = 128
GRID_W = 64
RET_DK = 128
RET_DV = 256
RET_CHUNK = 128
RET_THETA = 10000.0
SWA_KV_HEADS = 2
WINDOW = 128
T5_BUCKETS = 32
T5_MAX_DIST = 128
AX_KV_HEADS = 2
AX_THETA = 10000.0

VMEM_LIMIT_BYTES = 56 * 1024 * 1024


def _params(semantics):
    return pltpu.CompilerParams(dimension_semantics=semantics, vmem_limit_bytes=VMEM_LIMIT_BYTES)


def _rms(x, gain):
    ms = jnp.mean(x * x, axis=-1, keepdims=True)
    return x * lax.rsqrt(ms + EPS) * gain


def _dot(a, b):
    return jnp.dot(a, b, preferred_element_type=F32)


def _dot_nt(a, b):
    return lax.dot_general(a, b, (((1,), (1,)), ((), ())), preferred_element_type=F32)


def _dot_tn(a, b):
    return lax.dot_general(a, b, (((0,), (0,)), ((), ())), preferred_element_type=F32)


def _norm_proj_kernel(x_ref, g_ref, w_ref, o_ref, h_ref):
    @pl.when(pl.program_id(1) == 0)
    def _():
        h_ref[...] = _rms(x_ref[...], g_ref[...]).astype(BF16)

    o_ref[...] = _dot(h_ref[...], w_ref[...]).astype(o_ref.dtype)


def norm_proj(x2, gain, w, *, tm, tn):
    t, d = x2.shape
    n = w.shape[1]
    return pl.pallas_call(
        _norm_proj_kernel,
        grid=(t // tm, n // tn),
        in_specs=[
            pl.BlockSpec((tm, d), lambda i, j: (i, 0)),
            pl.BlockSpec((1, d), lambda i, j: (0, 0)),
            pl.BlockSpec((d, tn), lambda i, j: (0, j)),
        ],
        out_specs=pl.BlockSpec((tm, tn), lambda i, j: (i, j)),
        out_shape=jax.ShapeDtypeStruct((t, n), BF16),
        scratch_shapes=[pltpu.VMEM((tm, d), BF16)],
        compiler_params=_params(("parallel", "arbitrary")),
        name="norm_proj",
    )(x2, gain.reshape(1, d), w)


def _out_proj_kernel(*refs):
    x_ref, o_ref = refs[0], refs[-1]
    n_in = (len(refs) - 2) // 2
    acc = x_ref[...]
    for i in range(n_in):
        acc = acc + _dot(refs[1 + i][...], refs[1 + n_in + i][...])
    o_ref[...] = acc


def out_proj(x2, acts, w, *, tm):
    t, d = x2.shape
    ks = [a.shape[1] for a in acts]
    assert len(set(ks)) == 1 and sum(ks) == w.shape[0]
    k = ks[0]
    in_specs = [pl.BlockSpec((tm, d), lambda i: (i, 0))]
    in_specs += [pl.BlockSpec((tm, k), lambda i: (i, 0)) for _ in acts]
    in_specs += [pl.BlockSpec((k, d), functools.partial(lambda i, s: (s, 0), s=s)) for s in range(len(acts))]
    return pl.pallas_call(
        _out_proj_kernel,
        grid=(t // tm,),
        in_specs=in_specs,
        out_specs=pl.BlockSpec((tm, d), lambda i: (i, 0)),
        out_shape=jax.ShapeDtypeStruct((t, d), F32),
        compiler_params=_params(("parallel",)),
        name="out_proj",
    )(x2, *acts, *([w] * len(acts)))


def _mlp_kernel(x_ref, g_ref, wu_ref, wd_ref, o_ref, h_ref):
    @pl.when(pl.program_id(1) == 0)
    def _():
        x = x_ref[...]
        h_ref[...] = _rms(x, g_ref[...]).astype(BF16)
        o_ref[...] = x

    u = _dot(h_ref[...], wu_ref[...])
    a = jnp.square(jnp.maximum(u, 0.0)).astype(BF16)
    o_ref[...] += _dot(a, wd_ref[...])


def mlp(x2, gain, wu, wd, *, tm, tf):
    t, d = x2.shape
    ff = wu.shape[1]
    return pl.pallas_call(
        _mlp_kernel,
        grid=(t // tm, ff // tf),
        in_specs=[
            pl.BlockSpec((tm, d), lambda i, j: (i, 0)),
            pl.BlockSpec((1, d), lambda i, j: (0, 0)),
            pl.BlockSpec((d, tf), lambda i, j: (0, j)),
            pl.BlockSpec((tf, d), lambda i, j: (j, 0)),
        ],
        out_specs=pl.BlockSpec((tm, d), lambda i, j: (i, 0)),
        out_shape=jax.ShapeDtypeStruct((t, d), F32),
        scratch_shapes=[pltpu.VMEM((tm, d), BF16)],
        compiler_params=_params(("parallel", "arbitrary")),
        name="mlp",
    )(x2, gain.reshape(1, d), wu, wd)


def _log_sigmoid(x):
    return -(jnp.maximum(-x, 0.0) + jnp.log1p(jnp.exp(-jnp.abs(x))))


def _ret_kernel(dl_ref, q_ref, k_ref, v_ref, g_ref, cos_ref, sin_ref, gn_ref, o_ref,
                sb_ref, sf_ref, cb_ref, dec_ref, *, cs):
    C = RET_CHUNK
    h = pl.program_id(1)
    phase = pl.program_id(2)
    t = pl.program_id(3)
    nsteps = pl.num_programs(3)

    lgf_w = _log_sigmoid(jnp.full((1, RET_DV), dl_ref[0, h], F32))
    lgb_w = _log_sigmoid(jnp.full((1, RET_DV), dl_ref[1, h], F32))

    @pl.when((phase == 0) & (t == 0))
    def _():
        row = lax.broadcasted_iota(jnp.int32, (C, C), 0).astype(F32)
        col = lax.broadcasted_iota(jnp.int32, (C, C), 1).astype(F32)
        lgf = lgf_w[:, :C]
        lgb = lgb_w[:, :C]
        diff = row - col
        dec_ref[0] = jnp.where(diff >= 0, jnp.exp(jnp.maximum(diff, 0.0) * lgf),
                               jnp.exp(jnp.maximum(-diff, 0.0) * lgb))
        dec_ref[1] = jnp.exp((row + 1.0) * lgf)
        dec_ref[2] = jnp.exp((C - 1.0 - row) * lgf)
        dec_ref[3] = jnp.exp((C - row) * lgb)
        dec_ref[4] = jnp.exp(row * lgb)
        sb_ref[...] = jnp.zeros_like(sb_ref)
        sf_ref[...] = jnp.zeros_like(sf_ref)

    def rope(a, rows):
        return a * cos_ref[rows, :] + pltpu.roll(a, RET_DK // 2, 1) * sin_ref[rows, :]

    @pl.when(phase == 0)
    def _():
        sb = sb_ref[...]
        cdec = jnp.exp(C * lgb_w)
        for ci in reversed(range(cs)):
            rows = slice(ci * C, (ci + 1) * C)
            qr = rope(q_ref[0, rows, :].astype(F32), rows)
            kr = rope(k_ref[0, rows, :].astype(F32), rows) * (RET_DK ** -0.5)
            c = (nsteps - 1 - t) * cs + ci
            cb_ref[c] = _dot((qr * dec_ref[3]).astype(BF16), sb.astype(BF16))
            sb = cdec * sb + _dot_tn((kr * dec_ref[4]).astype(BF16), v_ref[0, rows, :])
        sb_ref[...] = sb

    @pl.when(phase == 1)
    def _():
        sf = sf_ref[...]
        cdec = jnp.exp(C * lgf_w)
        for ci in range(cs):
            rows = slice(ci * C, (ci + 1) * C)
            qr = rope(q_ref[0, rows, :].astype(F32), rows)
            kr = rope(k_ref[0, rows, :].astype(F32), rows) * (RET_DK ** -0.5)
            v = v_ref[0, rows, :]
            s = _dot_nt(qr.astype(BF16), kr.astype(BF16))
            inner = _dot((s * dec_ref[0]).astype(BF16), v)
            cross = _dot((qr * dec_ref[1]).astype(BF16), sf.astype(BF16))
            sf = cdec * sf + _dot_tn((kr * dec_ref[2]).astype(BF16), v)
            tot = inner + cross + cb_ref[t * cs + ci]
            y = _rms(tot, gn_ref[...])
            g = g_ref[0, rows, :].astype(F32)
            o_ref[0, rows, :] = (g * jax.nn.sigmoid(g) * y).astype(o_ref.dtype)
        sf_ref[...] = sf


def retention(proj, decay_logit, gn_gain, cos2, sin2, *, n_heads, q_off, k_off, v_off, g_off, cs):
    b, s, _ = proj.shape
    C = RET_CHUNK
    nc = s // C
    ts = cs * C
    nsteps = nc // cs
    ret_v = n_heads * RET_DV

    def step(p, t):
        return jnp.where(p == 0, nsteps - 1 - t, t)

    def spec(width, off):
        base = off // width
        return pl.BlockSpec((1, ts, width), lambda bi, h, p, t: (bi, step(p, t), base + h))

    kern = functools.partial(_ret_kernel, cs=cs)
    return pl.pallas_call(
        kern,
        grid=(b, n_heads, 2, nsteps),
        in_specs=[
            pl.BlockSpec(memory_space=pltpu.SMEM),
            spec(RET_DK, q_off),
            spec(RET_DK, k_off),
            spec(RET_DV, v_off),
            spec(RET_DV, g_off),
            pl.BlockSpec((ts, RET_DK), lambda bi, h, p, t: (step(p, t), 0)),
            pl.BlockSpec((ts, RET_DK), lambda bi, h, p, t: (step(p, t), 0)),
            pl.BlockSpec((1, RET_DV), lambda bi, h, p, t: (0, h)),
        ],
        out_specs=pl.BlockSpec((1, ts, RET_DV), lambda bi, h, p, t: (bi, jnp.where(p == 0, 0, t), h)),
        out_shape=jax.ShapeDtypeStruct((b, s, ret_v), BF16),
        scratch_shapes=[
            pltpu.VMEM((RET_DK, RET_DV), F32),
            pltpu.VMEM((RET_DK, RET_DV), F32),
            pltpu.VMEM((nc, C, RET_DV), F32),
            pltpu.VMEM((5, C, C), F32),
        ],
        compiler_params=_params(("parallel", "parallel", "arbitrary", "arbitrary")),
        name="retention",
    )(decay_logit, proj, proj, proj, proj, cos2, sin2, gn_gain.reshape(1, ret_v))


def _swa_kernel(sink_ref, table_ref, q_ref, kp_ref, kc_ref, kn_ref, vp_ref, vc_ref, vn_ref,
                bucket_ref, qg_ref, kg_ref, o_ref, bias_ref, *, n_heads, n_kv, nb):
    i = pl.program_id(1)
    D = HEAD_DIM
    G = n_heads // n_kv

    @pl.when((pl.program_id(0) == 0) & (i == 0))
    def _():
        bucket = bucket_ref[...]
        for h in range(n_heads):
            def body(bk, acc):
                return jnp.where(bucket == bk, table_ref[bk, h], acc)
            bias_ref[h] = lax.fori_loop(0, T5_BUCKETS, body, jnp.zeros((BLOCK, 3 * BLOCK), F32))

    r = lax.broadcasted_iota(jnp.int32, (BLOCK, 3 * BLOCK), 0)
    j = lax.broadcasted_iota(jnp.int32, (BLOCK, 3 * BLOCK), 1)
    rel = j - BLOCK - r
    kpos = (i - 1) * BLOCK + j
    valid = (jnp.abs(rel) <= WINDOW) & (kpos >= 0) & (kpos < nb * BLOCK)

    qg = qg_ref[...]
    kg = kg_ref[...]
    for kv in range(n_kv):
        sl = slice(kv * D, (kv + 1) * D)
        k3 = jnp.concatenate([kp_ref[0][:, sl], kc_ref[0][:, sl], kn_ref[0][:, sl]], axis=0)
        k3 = _rms(k3.astype(F32), kg).astype(BF16)
        v3 = jnp.concatenate([vp_ref[0][:, sl], vc_ref[0][:, sl], vn_ref[0][:, sl]], axis=0)
        for g in range(G):
            h = kv * G + g
            hs = slice(h * D, (h + 1) * D)
            q = _rms(q_ref[0][:, hs].astype(F32), qg) * (D ** -0.5)
            s = _dot_nt(q.astype(BF16), k3) + bias_ref[h]
            s = jnp.where(valid, s, NEG_INF)
            sink = sink_ref[h]
            m = jnp.maximum(jnp.max(s, axis=-1, keepdims=True), sink)
            p = jnp.exp(s - m)
            denom = jnp.sum(p, axis=-1, keepdims=True) + jnp.exp(sink - m)
            o = _dot(p.astype(BF16), v3) / denom
            o_ref[0, :, hs] = o.astype(o_ref.dtype)


def _t5_bucket(rel):
    nb = T5_BUCKETS // 2
    max_exact = nb // 2
    ret = jnp.where(rel > 0, nb, 0)
    n = jnp.abs(rel)
    nf = jnp.maximum(n, 1).astype(jnp.float32)
    large = max_exact + (jnp.log(nf / max_exact) / math.log(T5_MAX_DIST / max_exact)
                         * (nb - max_exact)).astype(jnp.int32)
    large = jnp.minimum(large, nb - 1)
    return ret + jnp.where(n < max_exact, n, large)


def window_attention(proj, sink, t5_table, q_gain, k_gain, *, n_heads, n_kv, q_off, k_off, v_off):
    b, s, _ = proj.shape
    D = HEAD_DIM
    nb = s // BLOCK
    qw = n_heads * D
    kw = n_kv * D
    rr = jnp.arange(BLOCK)
    jj = jnp.arange(3 * BLOCK)
    bucket = _t5_bucket(jj[None, :] - BLOCK - rr[:, None]).astype(jnp.int32)

    def kv_spec(off, shift):
        base = off // kw
        return pl.BlockSpec((1, BLOCK, kw),
                            lambda bi, i: (bi, jnp.clip(i + shift, 0, nb - 1), base))

    kern = functools.partial(_swa_kernel, n_heads=n_heads, n_kv=n_kv, nb=nb)
    return pl.pallas_call(
        kern,
        grid=(b, nb),
        in_specs=[
            pl.BlockSpec(memory_space=pltpu.SMEM),
            pl.BlockSpec(memory_space=pltpu.SMEM),
            pl.BlockSpec((1, BLOCK, qw), lambda bi, i: (bi, i, q_off // qw)),
            kv_spec(k_off, -1), kv_spec(k_off, 0), kv_spec(k_off, 1),
            kv_spec(v_off, -1), kv_spec(v_off, 0), kv_spec(v_off, 1),
            pl.BlockSpec((BLOCK, 3 * BLOCK), lambda bi, i: (0, 0)),
            pl.BlockSpec((1, D), lambda bi, i: (0, 0)),
            pl.BlockSpec((1, D), lambda bi, i: (0, 0)),
        ],
        out_specs=pl.BlockSpec((1, BLOCK, qw), lambda bi, i: (bi, i, 0)),
        out_shape=jax.ShapeDtypeStruct((b, s, qw), BF16),
        scratch_shapes=[pltpu.VMEM((n_heads, BLOCK, 3 * BLOCK), F32)],
        compiler_params=_params(("arbitrary", "arbitrary")),
        name="window_attention",
    )(sink, t5_table, proj, proj, proj, proj, proj, proj, proj, bucket,
      q_gain.reshape(1, D), k_gain.reshape(1, D))


def _axial_rope(a, cc, sa, sb):
    q4 = HEAD_DIM // 4
    return a * cc + pltpu.roll(a, HEAD_DIM - q4, 1) * sa + pltpu.roll(a, q4, 1) * sb


def _ax_prep_kernel(q_ref, k_ref, cc_ref, sa_ref, sb_ref, qg_ref, kg_ref, qo_ref, ko_ref,
                    *, n_heads, n_kv):
    D = HEAD_DIM
    cc, sa, sb = cc_ref[...], sa_ref[...], sb_ref[...]
    qg = qg_ref[...] * (D ** -0.5 * math.log2(math.e))
    kg = kg_ref[...]
    for h in range(n_heads):
        hs = slice(h * D, (h + 1) * D)
        q = _rms(q_ref[0][:, hs].astype(F32), qg)
        qo_ref[0, :, hs] = _axial_rope(q, cc, sa, sb).astype(qo_ref.dtype)
    for h in range(n_kv):
        hs = slice(h * D, (h + 1) * D)
        k = _rms(k_ref[0][:, hs].astype(F32), kg)
        ko_ref[0, :, hs] = _axial_rope(k, cc, sa, sb).astype(ko_ref.dtype)


def axial_prep(proj, tables, q_gain, k_gain, *, n_heads, n_kv, ts):
    b, s, _ = proj.shape
    D = HEAD_DIM
    qw, kw = n_heads * D, n_kv * D
    kern = functools.partial(_ax_prep_kernel, n_heads=n_heads, n_kv=n_kv)
    tab = pl.BlockSpec((ts, D), lambda bi, i: (i, 0))
    return pl.pallas_call(
        kern,
        grid=(b, s // ts),
        in_specs=[
            pl.BlockSpec((1, ts, qw), lambda bi, i: (bi, i, 0)),
            pl.BlockSpec((1, ts, kw), lambda bi, i: (bi, i, qw // kw)),
            tab, tab, tab,
            pl.BlockSpec((1, D), lambda bi, i: (0, 0)),
            pl.BlockSpec((1, D), lambda bi, i: (0, 0)),
        ],
        out_specs=[
            pl.BlockSpec((1, ts, qw), lambda bi, i: (bi, i, 0)),
            pl.BlockSpec((1, ts, kw), lambda bi, i: (bi, i, 0)),
        ],
        out_shape=[jax.ShapeDtypeStruct((b, s, qw), BF16), jax.ShapeDtypeStruct((b, s, kw), BF16)],
        compiler_params=_params(("parallel", "parallel")),
        name="axial_prep",
    )(proj, proj, *tables, q_gain.reshape(1, D), k_gain.reshape(1, D))


def _flash_kernel(q_ref, k_ref, v_ref, o_ref, acc_ref, *, G, tq, tk, nk):
    D = HEAD_DIM
    R = G * tq
    qs = jnp.concatenate([q_ref[0][:, g * D:(g + 1) * D] for g in range(G)], axis=0)
    acc_ref[...] = jnp.zeros_like(acc_ref)

    def body(c, carry):
        m_old, l_old = carry
        start = pl.multiple_of(c * tk, tk)
        kc = k_ref[0, pl.ds(start, tk), :]
        vc = v_ref[0, pl.ds(start, tk), :]
        st = _dot_nt(kc, qs)
        m_new = jnp.maximum(m_old, jnp.max(st, axis=0, keepdims=True))
        alpha = jnp.exp2(m_old - m_new)
        p = jnp.exp2(st - m_new)
        l_new = alpha * l_old + jnp.sum(p, axis=0, keepdims=True)
        acc_ref[...] = alpha * acc_ref[...] + _dot_tn(vc, p.astype(BF16))
        return m_new, l_new

    init = (jnp.full((1, R), -jnp.inf, F32), jnp.zeros((1, R), F32))
    _, l = lax.fori_loop(0, nk, body, init)
    out = (acc_ref[...] / l).T
    for g in range(G):
        o_ref[0, :, g * D:(g + 1) * D] = out[g * tq:(g + 1) * tq].astype(o_ref.dtype)


def flash_attention(q, k, v_src, *, n_heads, n_kv, v_off, tq, tk):
    b, s, _ = q.shape
    D = HEAD_DIM
    G = n_heads // n_kv
    kern = functools.partial(_flash_kernel, G=G, tq=tq, tk=tk, nk=s // tk)
    return pl.pallas_call(
        kern,
        grid=(b, n_kv, s // tq),
        in_specs=[
            pl.BlockSpec((1, tq, G * D), lambda bi, kv, qi: (bi, qi, kv)),
            pl.BlockSpec((1, s, D), lambda bi, kv, qi: (bi, 0, kv)),
            pl.BlockSpec((1, s, D), lambda bi, kv, qi: (bi, 0, v_off // D + kv)),
        ],
        out_specs=pl.BlockSpec((1, tq, G * D), lambda bi, kv, qi: (bi, qi, kv)),
        out_shape=jax.ShapeDtypeStruct((b, s, n_heads * D), BF16),
        scratch_shapes=[pltpu.VMEM((D, G * tq), F32)],
        compiler_params=_params(("parallel", "parallel", "arbitrary")),
        name="flash_attention",
    )(q, k, v_src)


def _rope_angles(pos, dim, theta):
    inv = theta ** (-jnp.arange(0, dim, 2, dtype=jnp.float32) / dim)
    return pos.astype(jnp.float32)[:, None] * inv[None, :]


def _retention_tables(s):
    ang = _rope_angles(jnp.arange(s), RET_DK, RET_THETA)
    c, sn = jnp.cos(ang), jnp.sin(ang)
    return jnp.concatenate([c, c], axis=-1), jnp.concatenate([-sn, sn], axis=-1)


def _axial_tables(s):
    rows = s // GRID_W
    row = jnp.repeat(jnp.arange(rows), GRID_W)
    col = jnp.tile(jnp.arange(GRID_W), rows)
    half = HEAD_DIM // 2
    ar = _rope_angles(row, half, AX_THETA)
    ac = _rope_angles(col, half, AX_THETA)
    cr, sr, ccol, scol = jnp.cos(ar), jnp.sin(ar), jnp.cos(ac), jnp.sin(ac)
    z = jnp.zeros_like(sr)
    cc = jnp.concatenate([cr, cr, ccol, ccol], axis=-1)
    sa = jnp.concatenate([-sr, z, -scol, z], axis=-1)
    sb = jnp.concatenate([z, sr, z, scol], axis=-1)
    return cc, sa, sb


def kernel(x, norm_mix, norm_mlp, w_in_even, w_out_even, ret_decay_logit, ret_norm, swa_q_norm,
           swa_k_norm, swa_sink, t5_table, w_in_odd, w_out_odd, ax_q_norm, ax_k_norm, w_mlp_up,
           w_mlp_down):
    b, s, d = x.shape
    t = b * s
    depth = norm_mix.shape[0]
    ret_heads = ret_decay_logit.shape[-1]
    ret_q = ret_heads * RET_DK
    ret_v = ret_heads * RET_DV
    swa_heads = swa_sink.shape[-1]
    swa_q = swa_heads * HEAD_DIM
    swa_kv = SWA_KV_HEADS * HEAD_DIM
    ax_q = w_out_odd.shape[1]
    ax_heads = ax_q // HEAD_DIM
    ax_kv = AX_KV_HEADS * HEAD_DIM

    x2 = x.reshape(t, d)
    ret_tabs = _retention_tables(s)
    ax_tabs = _axial_tables(s)

    for layer in range(depth):
        i = layer // 2
        if layer % 2 == 0:
            proj = norm_proj(x2, norm_mix[layer], w_in_even[i].astype(BF16), tm=1024, tn=512)
            proj = proj.reshape(b, s, -1)
            ya = retention(proj, ret_decay_logit[i], ret_norm[i], *ret_tabs, n_heads=ret_heads,
                           q_off=0, k_off=ret_q, v_off=2 * ret_q, g_off=2 * ret_q + ret_v, cs=4)
            off = 2 * ret_q + 2 * ret_v
            yb = window_attention(proj, swa_sink[i], t5_table, swa_q_norm[i], swa_k_norm[i],
                                  n_heads=swa_heads, n_kv=SWA_KV_HEADS, q_off=off,
                                  k_off=off + swa_q, v_off=off + swa_q + swa_kv)
            x2 = out_proj(x2, [ya.reshape(t, -1), yb.reshape(t, -1)], w_out_even[i].astype(BF16),
                          tm=1024)
        else:
            proj = norm_proj(x2, norm_mix[layer], w_in_odd[i].astype(BF16), tm=1024, tn=512)
            proj = proj.reshape(b, s, -1)
            qp, kp = axial_prep(proj, ax_tabs, ax_q_norm[i], ax_k_norm[i], n_heads=ax_heads,
                                n_kv=AX_KV_HEADS, ts=512)
            y = flash_attention(qp, kp, proj, n_heads=ax_heads, n_kv=AX_KV_HEADS,
                                v_off=ax_q + ax_kv, tq=256, tk=512)
            x2 = out_proj(x2, [y.reshape(t, -1)], w_out_odd[i].astype(BF16), tm=1024)
        x2 = mlp(x2, norm_mlp[layer], w_mlp_up[layer].astype(BF16), w_mlp_down[layer].astype(BF16),
                 tm=1024, tf=512)
    return x2.reshape(b, s, d)
```

```python
import functools
import math

import jax
import jax.numpy as jnp
from jax import lax
from jax.experimental import pallas as pl
from jax.experimental.pallas import tpu as pltpu

F32 = jnp.float32
BF16 = jnp.bfloat16

EPS = 1e-6
NEG_INF = -1e30
HEAD_DIM = 128
BLOCK = 128
GRID_W = 64
RET_DK = 128
RET_DV = 256
RET_CHUNK = 128
RET_THETA = 10000.0
SWA_KV_HEADS = 2
WINDOW = 128
T5_BUCKETS = 32
T5_MAX_DIST = 128
AX_KV_HEADS = 2
AX_THETA = 10000.0

VMEM_LIMIT_BYTES = 56 * 1024 * 1024


def _params(semantics):
    return pltpu.CompilerParams(dimension_semantics=semantics, vmem_limit_bytes=VMEM_LIMIT_BYTES)


def _rms(x, gain):
    ms = jnp.mean(x * x, axis=-1, keepdims=True)
    return x * lax.rsqrt(ms + EPS) * gain


def _dot(a, b):
    return jnp.dot(a, b, preferred_element_type=F32)


def _dot_nt(a, b):
    return lax.dot_general(a, b, (((1,), (1,)), ((), ())), preferred_element_type=F32)


def _dot_tn(a, b):
    return lax.dot_general(a, b, (((0,), (0,)), ((), ())), preferred_element_type=F32)


def _norm_proj_kernel(x_ref, g_ref, w_ref, o_ref, h_ref):
    @pl.when(pl.program_id(1) == 0)
    def _():
        h_ref[...] = _rms(x_ref[...], g_ref[...]).astype(BF16)

    o_ref[...] = _dot(h_ref[...], w_ref[...]).astype(o_ref.dtype)


def norm_proj(x2, gain, w, *, tm, tn):
    t, d = x2.shape
    n = w.shape[1]
    return pl.pallas_call(
        _norm_proj_kernel,
        grid=(t // tm, n // tn),
        in_specs=[
            pl.BlockSpec((tm, d), lambda i, j: (i, 0)),
            pl.BlockSpec((1, d), lambda i, j: (0, 0)),
            pl.BlockSpec((d, tn), lambda i, j: (0, j)),
        ],
        out_specs=pl.BlockSpec((tm, tn), lambda i, j: (i, j)),
        out_shape=jax.ShapeDtypeStruct((t, n), BF16),
        scratch_shapes=[pltpu.VMEM((tm, d), BF16)],
        compiler_params=_params(("parallel", "arbitrary")),
        name="norm_proj",
    )(x2, gain.reshape(1, d), w)


def _out_proj_kernel(*refs):
    x_ref, o_ref = refs[0], refs[-1]
    n_in = (len(refs) - 2) // 2
    acc = x_ref[...]
    for i in range(n_in):
        acc = acc + _dot(refs[1 + i][...], refs[1 + n_in + i][...])
    o_ref[...] = acc


def out_proj(x2, acts, w, *, tm):
    t, d = x2.shape
    ks = [a.shape[1] for a in acts]
    assert len(set(ks)) == 1 and sum(ks) == w.shape[0]
    k = ks[0]
    in_specs = [pl.BlockSpec((tm, d), lambda i: (i, 0))]
    in_specs += [pl.BlockSpec((tm, k), lambda i: (i, 0)) for _ in acts]
    in_specs += [pl.BlockSpec((k, d), functools.partial(lambda i, s: (s, 0), s=s)) for s in range(len(acts))]
    return pl.pallas_call(
        _out_proj_kernel,
        grid=(t // tm,),
        in_specs=in_specs,
        out_specs=pl.BlockSpec((tm, d), lambda i: (i, 0)),
        out_shape=jax.ShapeDtypeStruct((t, d), F32),
        compiler_params=_params(("parallel",)),
        name="out_proj",
    )(x2, *acts, *([w] * len(acts)))


def _mlp_kernel(x_ref, g_ref, wu_ref, wd_ref, o_ref, h_ref):
    @pl.when(pl.program_id(1) == 0)
    def _():
        x = x_ref[...]
        h_ref[...] = _rms(x, g_ref[...]).astype(BF16)
        o_ref[...] = x

    u = _dot(h_ref[...], wu_ref[...])
    a = jnp.square(jnp.maximum(u, 0.0)).astype(BF16)
    o_ref[...] += _dot(a, wd_ref[...])


def mlp(x2, gain, wu, wd, *, tm, tf):
    t, d = x2.shape
    ff = wu.shape[1]
    return pl.pallas_call(
        _mlp_kernel,
        grid=(t // tm, ff // tf),
        in_specs=[
            pl.BlockSpec((tm, d), lambda i, j: (i, 0)),
            pl.BlockSpec((1, d), lambda i, j: (0, 0)),
            pl.BlockSpec((d, tf), lambda i, j: (0, j)),
            pl.BlockSpec((tf, d), lambda i, j: (j, 0)),
        ],
        out_specs=pl.BlockSpec((tm, d), lambda i, j: (i, 0)),
        out_shape=jax.ShapeDtypeStruct((t, d), F32),
        scratch_shapes=[pltpu.VMEM((tm, d), BF16)],
        compiler_params=_params(("parallel", "arbitrary")),
        name="mlp",
    )(x2, gain.reshape(1, d), wu, wd)


def _log_sigmoid(x):
    return -(jnp.maximum(-x, 0.0) + jnp.log1p(jnp.exp(-jnp.abs(x))))


def _ret_kernel(dl_ref, q_ref, k_ref, v_ref, g_ref, cos_ref, sin_ref, gn_ref, o_ref,
                sb_ref, sf_ref, cb_ref, dec_ref, *, cs):
    C = RET_CHUNK
    h = pl.program_id(1)
    phase = pl.program_id(2)
    t = pl.program_id(3)
    nsteps = pl.num_programs(3)

    lgf_w = _log_sigmoid(jnp.full((1, RET_DV), dl_ref[0, h], F32))
    lgb_w = _log_sigmoid(jnp.full((1, RET_DV), dl_ref[1, h], F32))

    @pl.when((phase == 0) & (t == 0))
    def _():
        row = lax.broadcasted_iota(jnp.int32, (C, C), 0).astype(F32)
        col = lax.broadcasted_iota(jnp.int32, (C, C), 1).astype(F32)
        lgf = lgf_w[:, :C]
        lgb = lgb_w[:, :C]
        diff = row - col
        dec_ref[0] = jnp.where(diff >= 0, jnp.exp(jnp.maximum(diff, 0.0) * lgf),
                               jnp.exp(jnp.maximum(-diff, 0.0) * lgb))
        dec_ref[1] = jnp.exp((row + 1.0) * lgf)
        dec_ref[2] = jnp.exp((C - 1.0 - row) * lgf)
        dec_ref[3] = jnp.exp((C - row) * lgb)
        dec_ref[4] = jnp.exp(row * lgb)
        sb_ref[...] = jnp.zeros_like(sb_ref)
        sf_ref[...] = jnp.zeros_like(sf_ref)

    def rope(a, rows):
        return a * cos_ref[rows, :] + pltpu.roll(a, RET_DK // 2, 1) * sin_ref[rows, :]

    @pl.when(phase == 0)
    def _():
        sb = sb_ref[...]
        cdec = jnp.exp(C * lgb_w)
        for ci in reversed(range(cs)):
            rows = slice(ci * C, (ci + 1) * C)
            qr = rope(q_ref[0, rows, :].astype(F32), rows)
            kr = rope(k_ref[0, rows, :].astype(F32), rows) * (RET_DK ** -0.5)
            c = (nsteps - 1 - t) * cs + ci
            cb_ref[c] = _dot((qr * dec_ref[3]).astype(BF16), sb.astype(BF16))
            sb = cdec * sb + _dot_tn((kr * dec_ref[4]).astype(BF16), v_ref[0, rows, :])
        sb_ref[...] = sb

    @pl.when(phase == 1)
    def _():
        sf = sf_ref[...]
        cdec = jnp.exp(C * lgf_w)
        for ci in range(cs):
            rows = slice(ci * C, (ci + 1) * C)
            qr = rope(q_ref[0, rows, :].astype(F32), rows)
            kr = rope(k_ref[0, rows, :].astype(F32), rows) * (RET_DK ** -0.5)
            v = v_ref[0, rows, :]
            s = _dot_nt(qr.astype(BF16), kr.astype(BF16))
            inner = _dot((s * dec_ref[0]).astype(BF16), v)
            cross = _dot((qr * dec_ref[1]).astype(BF16), sf.astype(BF16))
            sf = cdec * sf + _dot_tn((kr * dec_ref[2]).astype(BF16), v)
            tot = inner + cross + cb_ref[t * cs + ci]
            y = _rms(tot, gn_ref[...])
            g = g_ref[0, rows, :].astype(F32)
            o_ref[0, rows, :] = (g * jax.nn.sigmoid(g) * y).astype(o_ref.dtype)
        sf_ref[...] = sf


def retention(proj, decay_logit, gn_gain, cos2, sin2, *, n_heads, q_off, k_off, v_off, g_off, cs):
    b, s, _ = proj.shape
    C = RET_CHUNK
    nc = s // C
    ts = cs * C
    nsteps = nc // cs
    ret_v = n_heads * RET_DV

    def step(p, t):
        return jnp.where(p == 0, nsteps - 1 - t, t)

    def spec(width, off):
        base = off // width
        return pl.BlockSpec((1, ts, width), lambda bi, h, p, t: (bi, step(p, t), base + h))

    kern = functools.partial(_ret_kernel, cs=cs)
    return pl.pallas_call(
        kern,
        grid=(b, n_heads, 2, nsteps),
        in_specs=[
            pl.BlockSpec(memory_space=pltpu.SMEM),
            spec(RET_DK, q_off),
            spec(RET_DK, k_off),
            spec(RET_DV, v_off),
            spec(RET_DV, g_off),
            pl.BlockSpec((ts, RET_DK), lambda bi, h, p, t: (step(p, t), 0)),
            pl.BlockSpec((ts, RET_DK), lambda bi, h, p, t: (step(p, t), 0)),
            pl.BlockSpec((1, RET_DV), lambda bi, h, p, t: (0, h)),
        ],
        out_specs=pl.BlockSpec((1, ts, RET_DV), lambda bi, h, p, t: (bi, jnp.where(p == 0, 0, t), h)),
        out_shape=jax.ShapeDtypeStruct((b, s, ret_v), BF16),
        scratch_shapes=[
            pltpu.VMEM((RET_DK, RET_DV), F32),
            pltpu.VMEM((RET_DK, RET_DV), F32),
            pltpu.VMEM((nc, C, RET_DV), F32),
            pltpu.VMEM((5, C, C), F32),
        ],
        compiler_params=_params(("parallel", "parallel", "arbitrary", "arbitrary")),
        name="retention",
    )(decay_logit, proj, proj, proj, proj, cos2, sin2, gn_gain.reshape(1, ret_v))


def _swa_kernel(sink_ref, table_ref, q_ref, kp_ref, kc_ref, kn_ref, vp_ref, vc_ref, vn_ref,
                bucket_ref, qg_ref, kg_ref, o_ref, bias_ref, *, n_heads, n_kv, nb):
    i = pl.program_id(1)
    D = HEAD_DIM
    G = n_heads // n_kv

    @pl.when((pl.program_id(0) == 0) & (i == 0))
    def _():
        bucket = bucket_ref[...]
        for h in range(n_heads):
            def body(bk, acc):
                return jnp.where(bucket == bk, table_ref[bk, h], acc)
            bias_ref[h] = lax.fori_loop(0, T5_BUCKETS, body, jnp.zeros((BLOCK, 3 * BLOCK), F32))

    r = lax.broadcasted_iota(jnp.int32, (BLOCK, 3 * BLOCK), 0)
    j = lax.broadcasted_iota(jnp.int32, (BLOCK, 3 * BLOCK), 1)
    rel = j - BLOCK - r
    kpos = (i - 1) * BLOCK + j
    valid = (jnp.abs(rel) <= WINDOW) & (kpos >= 0) & (kpos < nb * BLOCK)

    qg = qg_ref[...]
    kg = kg_ref[...]
    for kv in range(n_kv):
        sl = slice(kv * D, (kv + 1) * D)
        k3 = jnp.concatenate([kp_ref[0][:, sl], kc_ref[0][:, sl], kn_ref[0][:, sl]], axis=0)
        k3 = _rms(k3.astype(F32), kg).astype(BF16)
        v3 = jnp.concatenate([vp_ref[0][:, sl], vc_ref[0][:, sl], vn_ref[0][:, sl]], axis=0)
        for g in range(G):
            h = kv * G + g
            hs = slice(h * D, (h + 1) * D)
            q = _rms(q_ref[0][:, hs].astype(F32), qg) * (D ** -0.5)
            s = _dot_nt(q.astype(BF16), k3) + bias_ref[h]
            s = jnp.where(valid, s, NEG_INF)
            sink = sink_ref[h]
            m = jnp.maximum(jnp.max(s, axis=-1, keepdims=True), sink)
            p = jnp.exp(s - m)
            denom = jnp.sum(p, axis=-1, keepdims=True) + jnp.exp(sink - m)
            o = _dot(p.astype(BF16), v3) / denom
            o_ref[0, :, hs] = o.astype(o_ref.dtype)


def _t5_bucket(rel):
    nb = T5_BUCKETS // 2
    max_exact = nb // 2
    ret = jnp.where(rel > 0, nb, 0)
    n = jnp.abs(rel)
    nf = jnp.maximum(n, 1).astype(jnp.float32)
    large = max_exact + (jnp.log(nf / max_exact) / math.log(T5_MAX_DIST / max_exact)
                         * (nb - max_exact)).astype(jnp.int32)
    large = jnp.minimum(large, nb - 1)
    return ret + jnp.where(n < max_exact, n, large)


def window_attention(proj, sink, t5_table, q_gain, k_gain, *, n_heads, n_kv, q_off, k_off, v_off):
    b, s, _ = proj.shape
    D = HEAD_DIM
    nb = s // BLOCK
    qw = n_heads * D
    kw = n_kv * D
    rr = jnp.arange(BLOCK)
    jj = jnp.arange(3 * BLOCK)
    bucket = _t5_bucket(jj[None, :] - BLOCK - rr[:, None]).astype(jnp.int32)

    def kv_spec(off, shift):
        base = off // kw
        return pl.BlockSpec((1, BLOCK, kw),
                            lambda bi, i: (bi, jnp.clip(i + shift, 0, nb - 1), base))

    kern = functools.partial(_swa_kernel, n_heads=n_heads, n_kv=n_kv, nb=nb)
    return pl.pallas_call(
        kern,
        grid=(b, nb),
        in_specs=[
            pl.BlockSpec(memory_space=pltpu.SMEM),
            pl.BlockSpec(memory_space=pltpu.SMEM),
            pl.BlockSpec((1, BLOCK, qw), lambda bi, i: (bi, i, q_off // qw)),
            kv_spec(k_off, -1), kv_spec(k_off, 0), kv_spec(k_off, 1),
            kv_spec(v_off, -1), kv_spec(v_off, 0), kv_spec(v_off, 1),
            pl.BlockSpec((BLOCK, 3 * BLOCK), lambda bi, i: (0, 0)),
            pl.BlockSpec((1, D), lambda bi, i: (0, 0)),
            pl.BlockSpec((1, D), lambda bi, i: (0, 0)),
        ],
        out_specs=pl.BlockSpec((1, BLOCK, qw), lambda bi, i: (bi, i, 0)),
        out_shape=jax.ShapeDtypeStruct((b, s, qw), BF16),
        scratch_shapes=[pltpu.VMEM((n_heads, BLOCK, 3 * BLOCK), F32)],
        compiler_params=_params(("arbitrary", "arbitrary")),
        name="window_attention",
    )(sink, t5_table, proj, proj, proj, proj, proj, proj, proj, bucket,
      q_gain.reshape(1, D), k_gain.reshape(1, D))


def _axial_rope(a, cc, sa, sb):
    q4 = HEAD_DIM // 4
    return a * cc + pltpu.roll(a, HEAD_DIM - q4, 1) * sa + pltpu.roll(a, q4, 1) * sb


ONES_ROWS = 16


def _ax_prep_kernel(q_ref, k_ref, v_ref, cc_ref, sa_ref, sb_ref, qg_ref, kg_ref, qo_ref, ko_ref,
                    vt_ref, *, n_heads, n_kv):
    D = HEAD_DIM
    cc, sa, sb = cc_ref[...], sa_ref[...], sb_ref[...]
    qg = qg_ref[...] * (D ** -0.5 * math.log2(math.e))
    kg = kg_ref[...]
    for h in range(n_heads):
        hs = slice(h * D, (h + 1) * D)
        q = _rms(q_ref[0][:, hs].astype(F32), qg)
        qo_ref[0, :, hs] = _axial_rope(q, cc, sa, sb).astype(qo_ref.dtype)
    for h in range(n_kv):
        hs = slice(h * D, (h + 1) * D)
        k = _rms(k_ref[0][:, hs].astype(F32), kg)
        ko_ref[0, :, hs] = _axial_rope(k, cc, sa, sb).astype(ko_ref.dtype)
        vt_ref[0, h, :D, :] = v_ref[0][:, hs].astype(F32).T.astype(vt_ref.dtype)
        vt_ref[0, h, D:, :] = jnp.ones((ONES_ROWS, vt_ref.shape[-1]), vt_ref.dtype)


def axial_prep(proj, tables, q_gain, k_gain, *, n_heads, n_kv, ts):
    b, s, _ = proj.shape
    D = HEAD_DIM
    qw, kw = n_heads * D, n_kv * D
    kern = functools.partial(_ax_prep_kernel, n_heads=n_heads, n_kv=n_kv)
    tab = pl.BlockSpec((ts, D), lambda bi, i: (i, 0))
    return pl.pallas_call(
        kern,
        grid=(b, s // ts),
        in_specs=[
            pl.BlockSpec((1, ts, qw), lambda bi, i: (bi, i, 0)),
            pl.BlockSpec((1, ts, kw), lambda bi, i: (bi, i, qw // kw)),
            pl.BlockSpec((1, ts, kw), lambda bi, i: (bi, i, qw // kw + 1)),
            tab, tab, tab,
            pl.BlockSpec((1, D), lambda bi, i: (0, 0)),
            pl.BlockSpec((1, D), lambda bi, i: (0, 0)),
        ],
        out_specs=[
            pl.BlockSpec((1, ts, qw), lambda bi, i: (bi, i, 0)),
            pl.BlockSpec((1, ts, kw), lambda bi, i: (bi, i, 0)),
            pl.BlockSpec((1, n_kv, D + ONES_ROWS, ts), lambda bi, i: (bi, 0, 0, i)),
        ],
        out_shape=[jax.ShapeDtypeStruct((b, s, qw), BF16), jax.ShapeDtypeStruct((b, s, kw), BF16),
                   jax.ShapeDtypeStruct((b, n_kv, D + ONES_ROWS, s), BF16)],
        compiler_params=_params(("parallel", "parallel")),
        name="axial_prep",
    )(proj, proj, proj, *tables, q_gain.reshape(1, D), k_gain.reshape(1, D))


def _flash_kernel(q_ref, k_ref, vt_ref, o_ref, acc_ref, st_ref, *, G, tq, tk, nk):
    D = HEAD_DIM
    R = G * tq
    qs = jnp.concatenate([q_ref[0][:, g * D:(g + 1) * D] for g in range(G)], axis=0)
    acc_ref[...] = jnp.zeros_like(acc_ref)

    def scores(c, slot):
        start = pl.multiple_of(c * tk, tk)
        st_ref[slot] = _dot_nt(k_ref[0, pl.ds(start, tk), :], qs)

    def consume(c, slot, m_old):
        start = pl.multiple_of(c * tk, tk)
        vtc = vt_ref[0, 0, :, pl.ds(start, tk)]
        st = st_ref[slot]
        m_new = jnp.maximum(m_old, jnp.max(st, axis=0, keepdims=True))
        alpha = jnp.exp2(m_old - m_new)
        p = jnp.exp2((st - m_new).astype(BF16))
        acc_ref[...] = alpha * acc_ref[...] + _dot(vtc, p)
        return m_new

    def body(c2, m):
        c = 2 * c2
        scores(c + 1, 1)
        m = consume(c, 0, m)
        scores(jnp.minimum(c + 2, nk - 1), 0)
        return consume(c + 1, 1, m)

    scores(0, 0)
    lax.fori_loop(0, nk // 2, body, jnp.full((1, R), -jnp.inf, F32))
    out = (acc_ref[:D, :] / acc_ref[D:D + 1, :]).T
    for g in range(G):
        o_ref[0, :, g * D:(g + 1) * D] = out[g * tq:(g + 1) * tq].astype(o_ref.dtype)


def flash_attention(q, k, vt, *, n_heads, n_kv, tq, tk):
    b, s, _ = q.shape
    D = HEAD_DIM
    G = n_heads // n_kv
    kern = functools.partial(_flash_kernel, G=G, tq=tq, tk=tk, nk=s // tk)
    return pl.pallas_call(
        kern,
        grid=(b, n_kv, s // tq),
        in_specs=[
            pl.BlockSpec((1, tq, G * D), lambda bi, kv, qi: (bi, qi, kv)),
            pl.BlockSpec((1, s, D), lambda bi, kv, qi: (bi, 0, kv)),
            pl.BlockSpec((1, 1, D + ONES_ROWS, s), lambda bi, kv, qi: (bi, kv, 0, 0)),
        ],
        out_specs=pl.BlockSpec((1, tq, G * D), lambda bi, kv, qi: (bi, qi, kv)),
        out_shape=jax.ShapeDtypeStruct((b, s, n_heads * D), BF16),
        scratch_shapes=[pltpu.VMEM((D + ONES_ROWS, G * tq), F32), pltpu.VMEM((2, tk, G * tq), F32)],
        compiler_params=_params(("parallel", "parallel", "arbitrary")),
        name="flash_attention",
    )(q, k, vt)


def _rope_angles(pos, dim, theta):
    inv = theta ** (-jnp.arange(0, dim, 2, dtype=jnp.float32) / dim)
    return pos.astype(jnp.float32)[:, None] * inv[None, :]


def _retention_tables(s):
    ang = _rope_angles(jnp.arange(s), RET_DK, RET_THETA)
    c, sn = jnp.cos(ang), jnp.sin(ang)
    return jnp.concatenate([c, c], axis=-1), jnp.concatenate([-sn, sn], axis=-1)


def _axial_tables(s):
    rows = s // GRID_W
    row = jnp.repeat(jnp.arange(rows), GRID_W)
    col = jnp.tile(jnp.arange(GRID_W), rows)
    half = HEAD_DIM // 2
    ar = _rope_angles(row, half, AX_THETA)
    ac = _rope_angles(col, half, AX_THETA)
    cr, sr, ccol, scol = jnp.cos(ar), jnp.sin(ar), jnp.cos(ac), jnp.sin(ac)
    z = jnp.zeros_like(sr)
    cc = jnp.concatenate([cr, cr, ccol, ccol], axis=-1)
    sa = jnp.concatenate([-sr, z, -scol, z], axis=-1)
    sb = jnp.concatenate([z, sr, z, scol], axis=-1)
    return cc, sa, sb


def kernel(x, norm_mix, norm_mlp, w_in_even, w_out_even, ret_decay_logit, ret_norm, swa_q_norm,
           swa_k_norm, swa_sink, t5_table, w_in_odd, w_out_odd, ax_q_norm, ax_k_norm, w_mlp_up,
           w_mlp_down):
    b, s, d = x.shape
    t = b * s
    depth = norm_mix.shape[0]
    ret_heads = ret_decay_logit.shape[-1]
    ret_q = ret_heads * RET_DK
    ret_v = ret_heads * RET_DV
    swa_heads = swa_sink.shape[-1]
    swa_q = swa_heads * HEAD_DIM
    swa_kv = SWA_KV_HEADS * HEAD_DIM
    ax_q = w_out_odd.shape[1]
    ax_heads = ax_q // HEAD_DIM
    ax_kv = AX_KV_HEADS * HEAD_DIM

    x2 = x.reshape(t, d)
    ret_tabs = _retention_tables(s)
    ax_tabs = _axial_tables(s)

    for layer in range(depth):
        i = layer // 2
        if layer % 2 == 0:
            proj = norm_proj(x2, norm_mix[layer], w_in_even[i].astype(BF16), tm=1024, tn=512)
            proj = proj.reshape(b, s, -1)
            ya = retention(proj, ret_decay_logit[i], ret_norm[i], *ret_tabs, n_heads=ret_heads,
                           q_off=0, k_off=ret_q, v_off=2 * ret_q, g_off=2 * ret_q + ret_v, cs=4)
            off = 2 * ret_q + 2 * ret_v
            yb = window_attention(proj, swa_sink[i], t5_table, swa_q_norm[i], swa_k_norm[i],
                                  n_heads=swa_heads, n_kv=SWA_KV_HEADS, q_off=off,
                                  k_off=off + swa_q, v_off=off + swa_q + swa_kv)
            x2 = out_proj(x2, [ya.reshape(t, -1), yb.reshape(t, -1)], w_out_even[i].astype(BF16),
                          tm=1024)
        else:
            proj = norm_proj(x2, norm_mix[layer], w_in_odd[i].astype(BF16), tm=1024, tn=512)
            proj = proj.reshape(b, s, -1)
            qp, kp, vt = axial_prep(proj, ax_tabs, ax_q_norm[i], ax_k_norm[i], n_heads=ax_heads,
                                    n_kv=AX_KV_HEADS, ts=512)
            y = flash_attention(qp, kp, vt, n_heads=ax_heads, n_kv=AX_KV_HEADS, tq=256, tk=512)
            x2 = out_proj(x2, [y.reshape(t, -1)], w_out_odd[i].astype(BF16), tm=1024)
        x2 = mlp(x2, norm_mlp[layer], w_mlp_up[layer].astype(BF16), w_mlp_down[layer].astype(BF16),
                 tm=1024, tf=512)
    return x2.reshape(b, s, d)
```

```python
import functools
import math

import jax
import jax.numpy as jnp
from jax import lax
from jax.experimental import pallas as pl
from jax.experimental.pallas import tpu as pltpu

F32 = jnp.float32
BF16 = jnp.bfloat16

EPS = 1e-6
NEG_INF = -1e30
HEAD_DIM = 128
BLOCK = 128
GRID_W = 64
RET_DK = 128
RET_DV = 256
RET_CHUNK = 128
RET_THETA = 10000.0
SWA_KV_HEADS = 2
WINDOW = 128
T5_BUCKETS = 32
T5_MAX_DIST = 128
AX_KV_HEADS = 2
AX_THETA = 10000.0

VMEM_LIMIT_BYTES = 56 * 1024 * 1024


def _params(semantics):
    return pltpu.CompilerParams(dimension_semantics=semantics, vmem_limit_bytes=VMEM_LIMIT_BYTES)


def _rms(x, gain):
    ms = jnp.mean(x * x, axis=-1, keepdims=True)
    return x * lax.rsqrt(ms + EPS) * gain


def _dot(a, b):
    return jnp.dot(a, b, preferred_element_type=F32)


def _dot_nt(a, b):
    return lax.dot_general(a, b, (((1,), (1,)), ((), ())), preferred_element_type=F32)


def _dot_tn(a, b):
    return lax.dot_general(a, b, (((0,), (0,)), ((), ())), preferred_element_type=F32)


def _norm_proj_kernel(x_ref, g_ref, w_ref, o_ref, h_ref):
    @pl.when(pl.program_id(1) == 0)
    def _():
        h_ref[...] = _rms(x_ref[...], g_ref[...]).astype(BF16)

    o_ref[...] = _dot(h_ref[...], w_ref[...]).astype(o_ref.dtype)


def norm_proj(x2, gain, w, *, tm, tn):
    t, d = x2.shape
    n = w.shape[1]
    return pl.pallas_call(
        _norm_proj_kernel,
        grid=(t // tm, n // tn),
        in_specs=[
            pl.BlockSpec((tm, d), lambda i, j: (i, 0)),
            pl.BlockSpec((1, d), lambda i, j: (0, 0)),
            pl.BlockSpec((d, tn), lambda i, j: (0, j)),
        ],
        out_specs=pl.BlockSpec((tm, tn), lambda i, j: (i, j)),
        out_shape=jax.ShapeDtypeStruct((t, n), BF16),
        scratch_shapes=[pltpu.VMEM((tm, d), BF16)],
        compiler_params=_params(("parallel", "arbitrary")),
        name="norm_proj",
    )(x2, gain.reshape(1, d), w)


def _out_proj_kernel(*refs):
    x_ref, o_ref = refs[0], refs[-1]
    n_in = (len(refs) - 2) // 2
    acc = x_ref[...]
    for i in range(n_in):
        acc = acc + _dot(refs[1 + i][...], refs[1 + n_in + i][...])
    o_ref[...] = acc


def out_proj(x2, acts, w, *, tm):
    t, d = x2.shape
    ks = [a.shape[1] for a in acts]
    assert len(set(ks)) == 1 and sum(ks) == w.shape[0]
    k = ks[0]
    in_specs = [pl.BlockSpec((tm, d), lambda i: (i, 0))]
    in_specs += [pl.BlockSpec((tm, k), lambda i: (i, 0)) for _ in acts]
    in_specs += [pl.BlockSpec((k, d), functools.partial(lambda i, s: (s, 0), s=s)) for s in range(len(acts))]
    return pl.pallas_call(
        _out_proj_kernel,
        grid=(t // tm,),
        in_specs=in_specs,
        out_specs=pl.BlockSpec((tm, d), lambda i: (i, 0)),
        out_shape=jax.ShapeDtypeStruct((t, d), F32),
        compiler_params=_params(("parallel",)),
        name="out_proj",
    )(x2, *acts, *([w] * len(acts)))


def _mlp_kernel(x_ref, g_ref, wu_ref, wd_ref, o_ref, h_ref):
    @pl.when(pl.program_id(1) == 0)
    def _():
        x = x_ref[...]
        h_ref[...] = _rms(x, g_ref[...]).astype(BF16)
        o_ref[...] = x

    u = _dot(h_ref[...], wu_ref[...])
    a = jnp.square(jnp.maximum(u, 0.0)).astype(BF16)
    o_ref[...] += _dot(a, wd_ref[...])


def mlp(x2, gain, wu, wd, *, tm, tf):
    t, d = x2.shape
    ff = wu.shape[1]
    return pl.pallas_call(
        _mlp_kernel,
        grid=(t // tm, ff // tf),
        in_specs=[
            pl.BlockSpec((tm, d), lambda i, j: (i, 0)),
            pl.BlockSpec((1, d), lambda i, j: (0, 0)),
            pl.BlockSpec((d, tf), lambda i, j: (0, j)),
            pl.BlockSpec((tf, d), lambda i, j: (j, 0)),
        ],
        out_specs=pl.BlockSpec((tm, d), lambda i, j: (i, 0)),
        out_shape=jax.ShapeDtypeStruct((t, d), F32),
        scratch_shapes=[pltpu.VMEM((tm, d), BF16)],
        compiler_params=_params(("parallel", "arbitrary")),
        name="mlp",
    )(x2, gain.reshape(1, d), wu, wd)


def _log_sigmoid(x):
    return -(jnp.maximum(-x, 0.0) + jnp.log1p(jnp.exp(-jnp.abs(x))))


def _ret_kernel(dl_ref, q_ref, k_ref, v_ref, g_ref, cos_ref, sin_ref, gn_ref, o_ref,
                sb_ref, sf_ref, cb_ref, dec_ref, *, cs):
    C = RET_CHUNK
    h = pl.program_id(1)
    phase = pl.program_id(2)
    t = pl.program_id(3)
    nsteps = pl.num_programs(3)

    lgf_w = _log_sigmoid(jnp.full((1, RET_DV), dl_ref[0, h], F32))
    lgb_w = _log_sigmoid(jnp.full((1, RET_DV), dl_ref[1, h], F32))

    @pl.when((phase == 0) & (t == 0))
    def _():
        row = lax.broadcasted_iota(jnp.int32, (C, C), 0).astype(F32)
        col = lax.broadcasted_iota(jnp.int32, (C, C), 1).astype(F32)
        lgf = lgf_w[:, :C]
        lgb = lgb_w[:, :C]
        diff = row - col
        dec_ref[0] = jnp.where(diff >= 0, jnp.exp(jnp.maximum(diff, 0.0) * lgf),
                               jnp.exp(jnp.maximum(-diff, 0.0) * lgb))
        dec_ref[1] = jnp.exp((row + 1.0) * lgf)
        dec_ref[2] = jnp.exp((C - 1.0 - row) * lgf)
        dec_ref[3] = jnp.exp((C - row) * lgb)
        dec_ref[4] = jnp.exp(row * lgb)
        sb_ref[...] = jnp.zeros_like(sb_ref)
        sf_ref[...] = jnp.zeros_like(sf_ref)

    def rope(a, rows):
        return a * cos_ref[rows, :] + pltpu.roll(a, RET_DK // 2, 1) * sin_ref[rows, :]

    @pl.when(phase == 0)
    def _():
        sb = sb_ref[...]
        cdec = jnp.exp(C * lgb_w)
        for ci in reversed(range(cs)):
            rows = slice(ci * C, (ci + 1) * C)
            qr = rope(q_ref[0, rows, :].astype(F32), rows)
            kr = rope(k_ref[0, rows, :].astype(F32), rows) * (RET_DK ** -0.5)
            c = (nsteps - 1 - t) * cs + ci
            cb_ref[c] = _dot((qr * dec_ref[3]).astype(BF16), sb.astype(BF16))
            sb = cdec * sb + _dot_tn((kr * dec_ref[4]).astype(BF16), v_ref[0, rows, :])
        sb_ref[...] = sb

    @pl.when(phase == 1)
    def _():
        sf = sf_ref[...]
        cdec = jnp.exp(C * lgf_w)
        for ci in range(cs):
            rows = slice(ci * C, (ci + 1) * C)
            qr = rope(q_ref[0, rows, :].astype(F32), rows)
            kr = rope(k_ref[0, rows, :].astype(F32), rows) * (RET_DK ** -0.5)
            v = v_ref[0, rows, :]
            s = _dot_nt(qr.astype(BF16), kr.astype(BF16))
            inner = _dot((s * dec_ref[0]).astype(BF16), v)
            cross = _dot((qr * dec_ref[1]).astype(BF16), sf.astype(BF16))
            sf = cdec * sf + _dot_tn((kr * dec_ref[2]).astype(BF16), v)
            tot = inner + cross + cb_ref[t * cs + ci]
            y = _rms(tot, gn_ref[...])
            g = g_ref[0, rows, :].astype(F32)
            o_ref[0, rows, :] = (g * jax.nn.sigmoid(g) * y).astype(o_ref.dtype)
        sf_ref[...] = sf


def retention(proj, decay_logit, gn_gain, cos2, sin2, *, n_heads, q_off, k_off, v_off, g_off, cs):
    b, s, _ = proj.shape
    C = RET_CHUNK
    nc = s // C
    ts = cs * C
    nsteps = nc // cs
    ret_v = n_heads * RET_DV

    def step(p, t):
        return jnp.where(p == 0, nsteps - 1 - t, t)

    def spec(width, off):
        base = off // width
        return pl.BlockSpec((1, ts, width), lambda bi, h, p, t: (bi, step(p, t), base + h))

    kern = functools.partial(_ret_kernel, cs=cs)
    return pl.pallas_call(
        kern,
        grid=(b, n_heads, 2, nsteps),
        in_specs=[
            pl.BlockSpec(memory_space=pltpu.SMEM),
            spec(RET_DK, q_off),
            spec(RET_DK, k_off),
            spec(RET_DV, v_off),
            spec(RET_DV, g_off),
            pl.BlockSpec((ts, RET_DK), lambda bi, h, p, t: (step(p, t), 0)),
            pl.BlockSpec((ts, RET_DK), lambda bi, h, p, t: (step(p, t), 0)),
            pl.BlockSpec((1, RET_DV), lambda bi, h, p, t: (0, h)),
        ],
        out_specs=pl.BlockSpec((1, ts, RET_DV), lambda bi, h, p, t: (bi, jnp.where(p == 0, 0, t), h)),
        out_shape=jax.ShapeDtypeStruct((b, s, ret_v), BF16),
        scratch_shapes=[
            pltpu.VMEM((RET_DK, RET_DV), F32),
            pltpu.VMEM((RET_DK, RET_DV), F32),
            pltpu.VMEM((nc, C, RET_DV), F32),
            pltpu.VMEM((5, C, C), F32),
        ],
        compiler_params=_params(("parallel", "parallel", "arbitrary", "arbitrary")),
        name="retention",
    )(decay_logit, proj, proj, proj, proj, cos2, sin2, gn_gain.reshape(1, ret_v))


def _swa_kernel(sink_ref, table_ref, q_ref, kp_ref, kc_ref, kn_ref, vp_ref, vc_ref, vn_ref,
                bucket_ref, qg_ref, kg_ref, o_ref, bias_ref, *, n_heads, n_kv, nb):
    i = pl.program_id(1)
    D = HEAD_DIM
    G = n_heads // n_kv
    KW = 3 * BLOCK
    QW = G * BLOCK
    lane_head = lax.broadcasted_iota(jnp.int32, (1, QW), 1) // BLOCK

    def per_head_row(ref, kv, *idx):
        row = jnp.full((1, QW), ref[(*idx, kv * G)], F32)
        for g in range(1, G):
            row = jnp.where(lane_head == g, ref[(*idx, kv * G + g)], row)
        return row

    @pl.when((pl.program_id(0) == 0) & (i == 0))
    def _():
        bucket = bucket_ref[...]
        j = lax.broadcasted_iota(jnp.int32, (KW, QW), 0)
        r = lax.broadcasted_iota(jnp.int32, (KW, QW), 1) % BLOCK
        in_band = jnp.abs(j - BLOCK - r) <= WINDOW
        for kv in range(n_kv):
            def body(bk, acc):
                return jnp.where(bucket == bk, per_head_row(table_ref, kv, bk), acc)
            bias = lax.fori_loop(0, T5_BUCKETS, body, jnp.zeros((KW, QW), F32))
            bias_ref[kv] = jnp.where(in_band, bias, NEG_INF)

    off_lo = jnp.where(i == 0, NEG_INF, 0.0).astype(F32)
    off_hi = jnp.where(i == nb - 1, NEG_INF, 0.0).astype(F32)

    qg = qg_ref[...] * (D ** -0.5)
    kg = kg_ref[...]
    for kv in range(n_kv):
        sl = slice(kv * D, (kv + 1) * D)
        k3 = jnp.concatenate([kp_ref[0][:, sl], kc_ref[0][:, sl], kn_ref[0][:, sl]], axis=0)
        k3 = _rms(k3.astype(F32), kg).astype(BF16)
        v3 = jnp.concatenate([vp_ref[0][:, sl], vc_ref[0][:, sl], vn_ref[0][:, sl]], axis=0)
        qs = jnp.concatenate(
            [_rms(q_ref[0][:, (kv * G + g) * D:(kv * G + g + 1) * D].astype(F32), qg).astype(BF16)
             for g in range(G)], axis=0)
        st = _dot_nt(k3, qs) + bias_ref[kv]
        st = jnp.concatenate([st[:BLOCK] + off_lo, st[BLOCK:2 * BLOCK], st[2 * BLOCK:] + off_hi],
                             axis=0)
        sink = per_head_row(sink_ref, kv)
        m = jnp.maximum(jnp.max(st, axis=0, keepdims=True), sink)
        p = jnp.exp(st - m)
        denom = jnp.sum(p, axis=0, keepdims=True) + jnp.exp(sink - m)
        o = (_dot_tn(v3, p.astype(BF16)) / denom).T
        for g in range(G):
            h = kv * G + g
            o_ref[0, :, h * D:(h + 1) * D] = o[g * BLOCK:(g + 1) * BLOCK].astype(o_ref.dtype)


def _t5_bucket(rel):
    nb = T5_BUCKETS // 2
    max_exact = nb // 2
    ret = jnp.where(rel > 0, nb, 0)
    n = jnp.abs(rel)
    nf = jnp.maximum(n, 1).astype(jnp.float32)
    large = max_exact + (jnp.log(nf / max_exact) / math.log(T5_MAX_DIST / max_exact)
                         * (nb - max_exact)).astype(jnp.int32)
    large = jnp.minimum(large, nb - 1)
    return ret + jnp.where(n < max_exact, n, large)


def window_attention(proj, sink, t5_table, q_gain, k_gain, *, n_heads, n_kv, q_off, k_off, v_off):
    b, s, _ = proj.shape
    D = HEAD_DIM
    nb = s // BLOCK
    qw = n_heads * D
    kw = n_kv * D
    rr = jnp.arange(BLOCK)
    jj = jnp.arange(3 * BLOCK)
    bucket = _t5_bucket(jj[None, :] - BLOCK - rr[:, None]).astype(jnp.int32)
    G = n_heads // n_kv
    bucket = jnp.tile(bucket.T, (1, G))

    def kv_spec(off, shift):
        base = off // kw
        return pl.BlockSpec((1, BLOCK, kw),
                            lambda bi, i: (bi, jnp.clip(i + shift, 0, nb - 1), base))

    kern = functools.partial(_swa_kernel, n_heads=n_heads, n_kv=n_kv, nb=nb)
    return pl.pallas_call(
        kern,
        grid=(b, nb),
        in_specs=[
            pl.BlockSpec(memory_space=pltpu.SMEM),
            pl.BlockSpec(memory_space=pltpu.SMEM),
            pl.BlockSpec((1, BLOCK, qw), lambda bi, i: (bi, i, q_off // qw)),
            kv_spec(k_off, -1), kv_spec(k_off, 0), kv_spec(k_off, 1),
            kv_spec(v_off, -1), kv_spec(v_off, 0), kv_spec(v_off, 1),
            pl.BlockSpec((3 * BLOCK, G * BLOCK), lambda bi, i: (0, 0)),
            pl.BlockSpec((1, D), lambda bi, i: (0, 0)),
            pl.BlockSpec((1, D), lambda bi, i: (0, 0)),
        ],
        out_specs=pl.BlockSpec((1, BLOCK, qw), lambda bi, i: (bi, i, 0)),
        out_shape=jax.ShapeDtypeStruct((b, s, qw), BF16),
        scratch_shapes=[pltpu.VMEM((n_kv, 3 * BLOCK, G * BLOCK), F32)],
        compiler_params=_params(("arbitrary", "arbitrary")),
        name="window_attention",
    )(sink, t5_table, proj, proj, proj, proj, proj, proj, proj, bucket,
      q_gain.reshape(1, D), k_gain.reshape(1, D))


def _axial_rope(a, cc, sa, sb):
    q4 = HEAD_DIM // 4
    return a * cc + pltpu.roll(a, HEAD_DIM - q4, 1) * sa + pltpu.roll(a, q4, 1) * sb


ONES_ROWS = 16


def _ax_prep_kernel(q_ref, k_ref, v_ref, cc_ref, sa_ref, sb_ref, qg_ref, kg_ref, qo_ref, ko_ref,
                    vt_ref, *, n_heads, n_kv):
    D = HEAD_DIM
    cc, sa, sb = cc_ref[...], sa_ref[...], sb_ref[...]
    qg = qg_ref[...] * (D ** -0.5 * math.log2(math.e))
    kg = kg_ref[...]
    for h in range(n_heads):
        hs = slice(h * D, (h + 1) * D)
        q = _rms(q_ref[0][:, hs].astype(F32), qg)
        qo_ref[0, :, hs] = _axial_rope(q, cc, sa, sb).astype(qo_ref.dtype)
    for h in range(n_kv):
        hs = slice(h * D, (h + 1) * D)
        k = _rms(k_ref[0][:, hs].astype(F32), kg)
        ko_ref[0, :, hs] = _axial_rope(k, cc, sa, sb).astype(ko_ref.dtype)
        vt_ref[0, h, :D, :] = v_ref[0][:, hs].astype(F32).T.astype(vt_ref.dtype)
        vt_ref[0, h, D:, :] = jnp.ones((ONES_ROWS, vt_ref.shape[-1]), vt_ref.dtype)


def axial_prep(proj, tables, q_gain, k_gain, *, n_heads, n_kv, ts):
    b, s, _ = proj.shape
    D = HEAD_DIM
    qw, kw = n_heads * D, n_kv * D
    kern = functools.partial(_ax_prep_kernel, n_heads=n_heads, n_kv=n_kv)
    tab = pl.BlockSpec((ts, D), lambda bi, i: (i, 0))
    return pl.pallas_call(
        kern,
        grid=(b, s // ts),
        in_specs=[
            pl.BlockSpec((1, ts, qw), lambda bi, i: (bi, i, 0)),
            pl.BlockSpec((1, ts, kw), lambda bi, i: (bi, i, qw // kw)),
            pl.BlockSpec((1, ts, kw), lambda bi, i: (bi, i, qw // kw + 1)),
            tab, tab, tab,
            pl.BlockSpec((1, D), lambda bi, i: (0, 0)),
            pl.BlockSpec((1, D), lambda bi, i: (0, 0)),
        ],
        out_specs=[
            pl.BlockSpec((1, ts, qw), lambda bi, i: (bi, i, 0)),
            pl.BlockSpec((1, ts, kw), lambda bi, i: (bi, i, 0)),
            pl.BlockSpec((1, n_kv, D + ONES_ROWS, ts), lambda bi, i: (bi, 0, 0, i)),
        ],
        out_shape=[jax.ShapeDtypeStruct((b, s, qw), BF16), jax.ShapeDtypeStruct((b, s, kw), BF16),
                   jax.ShapeDtypeStruct((b, n_kv, D + ONES_ROWS, s), BF16)],
        compiler_params=_params(("parallel", "parallel")),
        name="axial_prep",
    )(proj, proj, proj, *tables, q_gain.reshape(1, D), k_gain.reshape(1, D))


def _flash_kernel(q_ref, k_ref, vt_ref, o_ref, acc_ref, st_ref, *, G, tq, tk, nk):
    D = HEAD_DIM
    R = G * tq
    qs = jnp.concatenate([q_ref[0][:, g * D:(g + 1) * D] for g in range(G)], axis=0)
    acc_ref[...] = jnp.zeros_like(acc_ref)

    def scores(c, slot):
        start = pl.multiple_of(c * tk, tk)
        st_ref[slot] = _dot_nt(k_ref[0, pl.ds(start, tk), :], qs)

    def consume(c, slot, m_old):
        start = pl.multiple_of(c * tk, tk)
        vtc = vt_ref[0, 0, :, pl.ds(start, tk)]
        st = st_ref[slot]
        m_new = jnp.maximum(m_old, jnp.max(st, axis=0, keepdims=True))
        alpha = jnp.exp2(m_old - m_new)
        p = jnp.exp2((st - m_new).astype(BF16))
        acc_ref[...] = alpha * acc_ref[...] + _dot(vtc, p)
        return m_new

    def body(c2, m):
        c = 2 * c2
        scores(c + 1, 1)
        m = consume(c, 0, m)
        scores(jnp.minimum(c + 2, nk - 1), 0)
        return consume(c + 1, 1, m)

    scores(0, 0)
    lax.fori_loop(0, nk // 2, body, jnp.full((1, R), -jnp.inf, F32))
    out = (acc_ref[:D, :] / acc_ref[D:D + 1, :]).T
    for g in range(G):
        o_ref[0, :, g * D:(g + 1) * D] = out[g * tq:(g + 1) * tq].astype(o_ref.dtype)


def flash_attention(q, k, vt, *, n_heads, n_kv, tq, tk):
    b, s, _ = q.shape
    D = HEAD_DIM
    G = n_heads // n_kv
    kern = functools.partial(_flash_kernel, G=G, tq=tq, tk=tk, nk=s // tk)
    return pl.pallas_call(
        kern,
        grid=(b, n_kv, s // tq),
        in_specs=[
            pl.BlockSpec((1, tq, G * D), lambda bi, kv, qi: (bi, qi, kv)),
            pl.BlockSpec((1, s, D), lambda bi, kv, qi: (bi, 0, kv)),
            pl.BlockSpec((1, 1, D + ONES_ROWS, s), lambda bi, kv, qi: (bi, kv, 0, 0)),
        ],
        out_specs=pl.BlockSpec((1, tq, G * D), lambda bi, kv, qi: (bi, qi, kv)),
        out_shape=jax.ShapeDtypeStruct((b, s, n_heads * D), BF16),
        scratch_shapes=[pltpu.VMEM((D + ONES_ROWS, G * tq), F32), pltpu.VMEM((2, tk, G * tq), F32)],
        compiler_params=_params(("parallel", "parallel", "arbitrary")),
        name="flash_attention",
    )(q, k, vt)


def _rope_angles(pos, dim, theta):
    inv = theta ** (-jnp.arange(0, dim, 2, dtype=jnp.float32) / dim)
    return pos.astype(jnp.float32)[:, None] * inv[None, :]


def _retention_tables(s):
    ang = _rope_angles(jnp.arange(s), RET_DK, RET_THETA)
    c, sn = jnp.cos(ang), jnp.sin(ang)
    return jnp.concatenate([c, c], axis=-1), jnp.concatenate([-sn, sn], axis=-1)


def _axial_tables(s):
    rows = s // GRID_W
    row = jnp.repeat(jnp.arange(rows), GRID_W)
    col = jnp.tile(jnp.arange(GRID_W), rows)
    half = HEAD_DIM // 2
    ar = _rope_angles(row, half, AX_THETA)
    ac = _rope_angles(col, half, AX_THETA)
    cr, sr, ccol, scol = jnp.cos(ar), jnp.sin(ar), jnp.cos(ac), jnp.sin(ac)
    z = jnp.zeros_like(sr)
    cc = jnp.concatenate([cr, cr, ccol, ccol], axis=-1)
    sa = jnp.concatenate([-sr, z, -scol, z], axis=-1)
    sb = jnp.concatenate([z, sr, z, scol], axis=-1)
    return cc, sa, sb


def kernel(x, norm_mix, norm_mlp, w_in_even, w_out_even, ret_decay_logit, ret_norm, swa_q_norm,
           swa_k_norm, swa_sink, t5_table, w_in_odd, w_out_odd, ax_q_norm, ax_k_norm, w_mlp_up,
           w_mlp_down):
    b, s, d = x.shape
    t = b * s
    depth = norm_mix.shape[0]
    ret_heads = ret_decay_logit.shape[-1]
    ret_q = ret_heads * RET_DK
    ret_v = ret_heads * RET_DV
    swa_heads = swa_sink.shape[-1]
    swa_q = swa_heads * HEAD_DIM
    swa_kv = SWA_KV_HEADS * HEAD_DIM
    ax_q = w_out_odd.shape[1]
    ax_heads = ax_q // HEAD_DIM
    ax_kv = AX_KV_HEADS * HEAD_DIM

    x2 = x.reshape(t, d)
    ret_tabs = _retention_tables(s)
    ax_tabs = _axial_tables(s)

    for layer in range(depth):
        i = layer // 2
        if layer % 2 == 0:
            proj = norm_proj(x2, norm_mix[layer], w_in_even[i].astype(BF16), tm=1024, tn=1536)
            proj = proj.reshape(b, s, -1)
            ya = retention(proj, ret_decay_logit[i], ret_norm[i], *ret_tabs, n_heads=ret_heads,
                           q_off=0, k_off=ret_q, v_off=2 * ret_q, g_off=2 * ret_q + ret_v, cs=4)
            off = 2 * ret_q + 2 * ret_v
            yb = window_attention(proj, swa_sink[i], t5_table, swa_q_norm[i], swa_k_norm[i],
                                  n_heads=swa_heads, n_kv=SWA_KV_HEADS, q_off=off,
                                  k_off=off + swa_q, v_off=off + swa_q + swa_kv)
            x2 = out_proj(x2, [ya.reshape(t, -1), yb.reshape(t, -1)], w_out_even[i].astype(BF16),
                          tm=1024)
        else:
            proj = norm_proj(x2, norm_mix[layer], w_in_odd[i].astype(BF16), tm=1024, tn=1536)
            proj = proj.reshape(b, s, -1)
            qp, kp, vt = axial_prep(proj, ax_tabs, ax_q_norm[i], ax_k_norm[i], n_heads=ax_heads,
                                    n_kv=AX_KV_HEADS, ts=512)
            y = flash_attention(qp, kp, vt, n_heads=ax_heads, n_kv=AX_KV_HEADS, tq=256, tk=512)
            x2 = out_proj(x2, [y.reshape(t, -1)], w_out_odd[i].astype(BF16), tm=1024)
        x2 = mlp(x2, norm_mlp[layer], w_mlp_up[layer].astype(BF16), w_mlp_down[layer].astype(BF16),
                 tm=1024, tf=1024)
    return x2.reshape(b, s, d)
```

```python
import functools
import math

import jax
import jax.numpy as jnp
from jax import lax
from jax.experimental import pallas as pl
from jax.experimental.pallas import tpu as pltpu

F32 = jnp.float32
BF16 = jnp.bfloat16

EPS = 1e-6
NEG_INF = -1e30
HEAD_DIM = 128
BLOCK = 128
GRID_W = 64
RET_DK = 128
RET_DV = 256
RET_CHUNK = 128
RET_THETA = 10000.0
SWA_KV_HEADS = 2
WINDOW = 128
T5_BUCKETS = 32
T5_MAX_DIST = 128
AX_KV_HEADS = 2
AX_THETA = 10000.0

VMEM_LIMIT_BYTES = 56 * 1024 * 1024


def _params(semantics):
    return pltpu.CompilerParams(dimension_semantics=semantics, vmem_limit_bytes=VMEM_LIMIT_BYTES)


def _rms(x, gain):
    ms = jnp.mean(x * x, axis=-1, keepdims=True)
    return x * lax.rsqrt(ms + EPS) * gain


def _dot(a, b):
    return jnp.dot(a, b, preferred_element_type=F32)


def _dot_nt(a, b):
    return lax.dot_general(a, b, (((1,), (1,)), ((), ())), preferred_element_type=F32)


def _dot_tn(a, b):
    return lax.dot_general(a, b, (((0,), (0,)), ((), ())), preferred_element_type=F32)


def _norm_proj_kernel(x_ref, g_ref, w_ref, o_ref, h_ref):
    @pl.when(pl.program_id(1) == 0)
    def _():
        h_ref[...] = _rms(x_ref[...], g_ref[...]).astype(BF16)

    o_ref[...] = _dot(h_ref[...], w_ref[...]).astype(o_ref.dtype)


def norm_proj(x2, gain, w, *, tm, tn):
    t, d = x2.shape
    n = w.shape[1]
    return pl.pallas_call(
        _norm_proj_kernel,
        grid=(t // tm, n // tn),
        in_specs=[
            pl.BlockSpec((tm, d), lambda i, j: (i, 0)),
            pl.BlockSpec((1, d), lambda i, j: (0, 0)),
            pl.BlockSpec((d, tn), lambda i, j: (0, j)),
        ],
        out_specs=pl.BlockSpec((tm, tn), lambda i, j: (i, j)),
        out_shape=jax.ShapeDtypeStruct((t, n), BF16),
        scratch_shapes=[pltpu.VMEM((tm, d), BF16)],
        compiler_params=_params(("parallel", "arbitrary")),
        name="norm_proj",
    )(x2, gain.reshape(1, d), w)


def _out_proj_kernel(*refs):
    x_ref, o_ref = refs[0], refs[-1]
    n_in = (len(refs) - 2) // 2
    acc = x_ref[...]
    for i in range(n_in):
        acc = acc + _dot(refs[1 + i][...], refs[1 + n_in + i][...])
    o_ref[...] = acc


def out_proj(x2, acts, w, *, tm):
    t, d = x2.shape
    ks = [a.shape[1] for a in acts]
    assert len(set(ks)) == 1 and sum(ks) == w.shape[0]
    k = ks[0]
    in_specs = [pl.BlockSpec((tm, d), lambda i: (i, 0))]
    in_specs += [pl.BlockSpec((tm, k), lambda i: (i, 0)) for _ in acts]
    in_specs += [pl.BlockSpec((k, d), functools.partial(lambda i, s: (s, 0), s=s)) for s in range(len(acts))]
    return pl.pallas_call(
        _out_proj_kernel,
        grid=(t // tm,),
        in_specs=in_specs,
        out_specs=pl.BlockSpec((tm, d), lambda i: (i, 0)),
        out_shape=jax.ShapeDtypeStruct((t, d), F32),
        compiler_params=_params(("parallel",)),
        name="out_proj",
    )(x2, *acts, *([w] * len(acts)))


def _mlp_kernel(x_ref, g_ref, wu_ref, wd_ref, o_ref, h_ref):
    @pl.when(pl.program_id(1) == 0)
    def _():
        x = x_ref[...]
        h_ref[...] = _rms(x, g_ref[...]).astype(BF16)
        o_ref[...] = x

    u = _dot(h_ref[...], wu_ref[...])
    a = jnp.square(jnp.maximum(u, 0.0)).astype(BF16)
    o_ref[...] += _dot(a, wd_ref[...])


def mlp(x2, gain, wu, wd, *, tm, tf):
    t, d = x2.shape
    ff = wu.shape[1]
    return pl.pallas_call(
        _mlp_kernel,
        grid=(t // tm, ff // tf),
        in_specs=[
            pl.BlockSpec((tm, d), lambda i, j: (i, 0)),
            pl.BlockSpec((1, d), lambda i, j: (0, 0)),
            pl.BlockSpec((d, tf), lambda i, j: (0, j)),
            pl.BlockSpec((tf, d), lambda i, j: (j, 0)),
        ],
        out_specs=pl.BlockSpec((tm, d), lambda i, j: (i, 0)),
        out_shape=jax.ShapeDtypeStruct((t, d), F32),
        scratch_shapes=[pltpu.VMEM((tm, d), BF16)],
        compiler_params=_params(("parallel", "arbitrary")),
        name="mlp",
    )(x2, gain.reshape(1, d), wu, wd)


def _log_sigmoid(x):
    return -(jnp.maximum(-x, 0.0) + jnp.log1p(jnp.exp(-jnp.abs(x))))


def _ret_kernel(dl_ref, q_ref, k_ref, v_ref, g_ref, cos_ref, sin_ref, gn_ref, o_ref,
                sb_ref, sf_ref, cb_ref, dec_ref, *, cs):
    C = RET_CHUNK
    h = pl.program_id(1)
    phase = pl.program_id(2)
    t = pl.program_id(3)
    nsteps = pl.num_programs(3)

    lgf_w = _log_sigmoid(jnp.full((1, RET_DV), dl_ref[0, h], F32))
    lgb_w = _log_sigmoid(jnp.full((1, RET_DV), dl_ref[1, h], F32))

    @pl.when((phase == 0) & (t == 0))
    def _():
        row = lax.broadcasted_iota(jnp.int32, (C, C), 0).astype(F32)
        col = lax.broadcasted_iota(jnp.int32, (C, C), 1).astype(F32)
        lgf = lgf_w[:, :C]
        lgb = lgb_w[:, :C]
        diff = row - col
        dec_ref[0] = jnp.where(diff >= 0, jnp.exp(jnp.maximum(diff, 0.0) * lgf),
                               jnp.exp(jnp.maximum(-diff, 0.0) * lgb))
        dec_ref[1] = jnp.exp((row + 1.0) * lgf)
        dec_ref[2] = jnp.exp((C - 1.0 - row) * lgf)
        dec_ref[3] = jnp.exp((C - row) * lgb)
        dec_ref[4] = jnp.exp(row * lgb)
        sb_ref[...] = jnp.zeros_like(sb_ref)
        sf_ref[...] = jnp.zeros_like(sf_ref)

    def rope(a, rows):
        return a * cos_ref[rows, :] + pltpu.roll(a, RET_DK // 2, 1) * sin_ref[rows, :]

    @pl.when(phase == 0)
    def _():
        sb = sb_ref[...]
        cdec = jnp.exp(C * lgb_w)
        for ci in reversed(range(cs)):
            rows = slice(ci * C, (ci + 1) * C)
            qr = rope(q_ref[0, rows, :].astype(F32), rows)
            kr = rope(k_ref[0, rows, :].astype(F32), rows) * (RET_DK ** -0.5)
            c = (nsteps - 1 - t) * cs + ci
            cb_ref[c] = _dot((qr * dec_ref[3]).astype(BF16), sb.astype(BF16))
            sb = cdec * sb + _dot_tn((kr * dec_ref[4]).astype(BF16), v_ref[0, rows, :])
        sb_ref[...] = sb

    @pl.when(phase == 1)
    def _():
        sf = sf_ref[...]
        cdec = jnp.exp(C * lgf_w)
        for ci in range(cs):
            rows = slice(ci * C, (ci + 1) * C)
            qr = rope(q_ref[0, rows, :].astype(F32), rows)
            kr = rope(k_ref[0, rows, :].astype(F32), rows) * (RET_DK ** -0.5)
            v = v_ref[0, rows, :]
            s = _dot_nt(qr.astype(BF16), kr.astype(BF16))
            inner = _dot((s * dec_ref[0]).astype(BF16), v)
            cross = _dot((qr * dec_ref[1]).astype(BF16), sf.astype(BF16))
            sf = cdec * sf + _dot_tn((kr * dec_ref[2]).astype(BF16), v)
            tot = inner + cross + cb_ref[t * cs + ci]
            y = _rms(tot, gn_ref[...])
            g = g_ref[0, rows, :].astype(F32)
            o_ref[0, rows, :] = (g * jax.nn.sigmoid(g) * y).astype(o_ref.dtype)
        sf_ref[...] = sf


def retention(proj, decay_logit, gn_gain, cos2, sin2, *, n_heads, q_off, k_off, v_off, g_off, cs):
    b, s, _ = proj.shape
    C = RET_CHUNK
    nc = s // C
    ts = cs * C
    nsteps = nc // cs
    ret_v = n_heads * RET_DV

    def step(p, t):
        return jnp.where(p == 0, nsteps - 1 - t, t)

    def spec(width, off):
        base = off // width
        return pl.BlockSpec((1, ts, width), lambda bi, h, p, t: (bi, step(p, t), base + h))

    kern = functools.partial(_ret_kernel, cs=cs)
    return pl.pallas_call(
        kern,
        grid=(b, n_heads, 2, nsteps),
        in_specs=[
            pl.BlockSpec(memory_space=pltpu.SMEM),
            spec(RET_DK, q_off),
            spec(RET_DK, k_off),
            spec(RET_DV, v_off),
            spec(RET_DV, g_off),
            pl.BlockSpec((ts, RET_DK), lambda bi, h, p, t: (step(p, t), 0)),
            pl.BlockSpec((ts, RET_DK), lambda bi, h, p, t: (step(p, t), 0)),
            pl.BlockSpec((1, RET_DV), lambda bi, h, p, t: (0, h)),
        ],
        out_specs=pl.BlockSpec((1, ts, RET_DV), lambda bi, h, p, t: (bi, jnp.where(p == 0, 0, t), h)),
        out_shape=jax.ShapeDtypeStruct((b, s, ret_v), BF16),
        scratch_shapes=[
            pltpu.VMEM((RET_DK, RET_DV), F32),
            pltpu.VMEM((RET_DK, RET_DV), F32),
            pltpu.VMEM((nc, C, RET_DV), F32),
            pltpu.VMEM((5, C, C), F32),
        ],
        compiler_params=_params(("parallel", "parallel", "arbitrary", "arbitrary")),
        name="retention",
    )(decay_logit, proj, proj, proj, proj, cos2, sin2, gn_gain.reshape(1, ret_v))


def _swa_kernel(sink_ref, table_ref, q_ref, kp_ref, kc_ref, kn_ref, vp_ref, vc_ref, vn_ref,
                bucket_ref, qg_ref, kg_ref, o_ref, bias_ref, *, n_heads, n_kv, nb):
    i = pl.program_id(1)
    D = HEAD_DIM
    G = n_heads // n_kv
    KW = 3 * BLOCK
    QW = G * BLOCK
    lane_head = lax.broadcasted_iota(jnp.int32, (1, QW), 1) // BLOCK

    def per_head_row(ref, kv, *idx):
        row = jnp.full((1, QW), ref[(*idx, kv * G)], F32)
        for g in range(1, G):
            row = jnp.where(lane_head == g, ref[(*idx, kv * G + g)], row)
        return row

    @pl.when((pl.program_id(0) == 0) & (i == 0))
    def _():
        bucket = bucket_ref[...]
        j = lax.broadcasted_iota(jnp.int32, (KW, QW), 0)
        r = lax.broadcasted_iota(jnp.int32, (KW, QW), 1) % BLOCK
        in_band = jnp.abs(j - BLOCK - r) <= WINDOW
        for kv in range(n_kv):
            def body(bk, acc):
                return jnp.where(bucket == bk, per_head_row(table_ref, kv, bk), acc)
            bias = lax.fori_loop(0, T5_BUCKETS, body, jnp.zeros((KW, QW), F32))
            bias_ref[kv] = jnp.where(in_band, bias, NEG_INF)

    off_lo = jnp.where(i == 0, NEG_INF, 0.0).astype(F32)
    off_hi = jnp.where(i == nb - 1, NEG_INF, 0.0).astype(F32)

    qg = qg_ref[...] * (D ** -0.5)
    kg = kg_ref[...]
    for kv in range(n_kv):
        sl = slice(kv * D, (kv + 1) * D)
        k3 = jnp.concatenate([kp_ref[0][:, sl], kc_ref[0][:, sl], kn_ref[0][:, sl]], axis=0)
        k3 = _rms(k3.astype(F32), kg).astype(BF16)
        v3 = jnp.concatenate([vp_ref[0][:, sl], vc_ref[0][:, sl], vn_ref[0][:, sl]], axis=0)
        qs = jnp.concatenate(
            [_rms(q_ref[0][:, (kv * G + g) * D:(kv * G + g + 1) * D].astype(F32), qg).astype(BF16)
             for g in range(G)], axis=0)
        st = _dot_nt(k3, qs) + bias_ref[kv]
        st = jnp.concatenate([st[:BLOCK] + off_lo, st[BLOCK:2 * BLOCK], st[2 * BLOCK:] + off_hi],
                             axis=0)
        sink = per_head_row(sink_ref, kv)
        m = jnp.maximum(jnp.max(st, axis=0, keepdims=True), sink)
        p = jnp.exp(st - m)
        denom = jnp.sum(p, axis=0, keepdims=True) + jnp.exp(sink - m)
        o = (_dot_tn(v3, p.astype(BF16)) / denom).T
        for g in range(G):
            h = kv * G + g
            o_ref[0, :, h * D:(h + 1) * D] = o[g * BLOCK:(g + 1) * BLOCK].astype(o_ref.dtype)


def _t5_bucket(rel):
    nb = T5_BUCKETS // 2
    max_exact = nb // 2
    ret = jnp.where(rel > 0, nb, 0)
    n = jnp.abs(rel)
    nf = jnp.maximum(n, 1).astype(jnp.float32)
    large = max_exact + (jnp.log(nf / max_exact) / math.log(T5_MAX_DIST / max_exact)
                         * (nb - max_exact)).astype(jnp.int32)
    large = jnp.minimum(large, nb - 1)
    return ret + jnp.where(n < max_exact, n, large)


def window_attention(proj, sink, t5_table, q_gain, k_gain, *, n_heads, n_kv, q_off, k_off, v_off):
    b, s, _ = proj.shape
    D = HEAD_DIM
    nb = s // BLOCK
    qw = n_heads * D
    kw = n_kv * D
    rr = jnp.arange(BLOCK)
    jj = jnp.arange(3 * BLOCK)
    bucket = _t5_bucket(jj[None, :] - BLOCK - rr[:, None]).astype(jnp.int32)
    G = n_heads // n_kv
    bucket = jnp.tile(bucket.T, (1, G))

    def kv_spec(off, shift):
        base = off // kw
        return pl.BlockSpec((1, BLOCK, kw),
                            lambda bi, i: (bi, jnp.clip(i + shift, 0, nb - 1), base))

    kern = functools.partial(_swa_kernel, n_heads=n_heads, n_kv=n_kv, nb=nb)
    return pl.pallas_call(
        kern,
        grid=(b, nb),
        in_specs=[
            pl.BlockSpec(memory_space=pltpu.SMEM),
            pl.BlockSpec(memory_space=pltpu.SMEM),
            pl.BlockSpec((1, BLOCK, qw), lambda bi, i: (bi, i, q_off // qw)),
            kv_spec(k_off, -1), kv_spec(k_off, 0), kv_spec(k_off, 1),
            kv_spec(v_off, -1), kv_spec(v_off, 0), kv_spec(v_off, 1),
            pl.BlockSpec((3 * BLOCK, G * BLOCK), lambda bi, i: (0, 0)),
            pl.BlockSpec((1, D), lambda bi, i: (0, 0)),
            pl.BlockSpec((1, D), lambda bi, i: (0, 0)),
        ],
        out_specs=pl.BlockSpec((1, BLOCK, qw), lambda bi, i: (bi, i, 0)),
        out_shape=jax.ShapeDtypeStruct((b, s, qw), BF16),
        scratch_shapes=[pltpu.VMEM((n_kv, 3 * BLOCK, G * BLOCK), F32)],
        compiler_params=_params(("arbitrary", "arbitrary")),
        name="window_attention",
    )(sink, t5_table, proj, proj, proj, proj, proj, proj, proj, bucket,
      q_gain.reshape(1, D), k_gain.reshape(1, D))


def _axial_rope(a, cc, sa, sb):
    q4 = HEAD_DIM // 4
    return a * cc + pltpu.roll(a, HEAD_DIM - q4, 1) * sa + pltpu.roll(a, q4, 1) * sb


ONES_ROWS = 16


def _ax_prep_kernel(q_ref, k_ref, v_ref, cc_ref, sa_ref, sb_ref, qg_ref, kg_ref, qo_ref, ko_ref,
                    vt_ref, *, n_heads, n_kv):
    D = HEAD_DIM
    cc, sa, sb = cc_ref[...], sa_ref[...], sb_ref[...]
    qg = qg_ref[...] * (D ** -0.5 * math.log2(math.e))
    kg = kg_ref[...]
    for h in range(n_heads):
        hs = slice(h * D, (h + 1) * D)
        q = _rms(q_ref[0][:, hs].astype(F32), qg)
        qo_ref[0, :, hs] = _axial_rope(q, cc, sa, sb).astype(qo_ref.dtype)
    for h in range(n_kv):
        hs = slice(h * D, (h + 1) * D)
        k = _rms(k_ref[0][:, hs].astype(F32), kg)
        ko_ref[0, :, hs] = _axial_rope(k, cc, sa, sb).astype(ko_ref.dtype)
        vt_ref[0, h, :D, :] = v_ref[0][:, hs].astype(F32).T.astype(vt_ref.dtype)
        vt_ref[0, h, D:, :] = jnp.ones((ONES_ROWS, vt_ref.shape[-1]), vt_ref.dtype)


def axial_prep(proj, tables, q_gain, k_gain, *, n_heads, n_kv, ts):
    b, s, _ = proj.shape
    D = HEAD_DIM
    qw, kw = n_heads * D, n_kv * D
    kern = functools.partial(_ax_prep_kernel, n_heads=n_heads, n_kv=n_kv)
    tab = pl.BlockSpec((ts, D), lambda bi, i: (i, 0))
    return pl.pallas_call(
        kern,
        grid=(b, s // ts),
        in_specs=[
            pl.BlockSpec((1, ts, qw), lambda bi, i: (bi, i, 0)),
            pl.BlockSpec((1, ts, kw), lambda bi, i: (bi, i, qw // kw)),
            pl.BlockSpec((1, ts, kw), lambda bi, i: (bi, i, qw // kw + 1)),
            tab, tab, tab,
            pl.BlockSpec((1, D), lambda bi, i: (0, 0)),
            pl.BlockSpec((1, D), lambda bi, i: (0, 0)),
        ],
        out_specs=[
            pl.BlockSpec((1, ts, qw), lambda bi, i: (bi, i, 0)),
            pl.BlockSpec((1, ts, kw), lambda bi, i: (bi, i, 0)),
            pl.BlockSpec((1, n_kv, D + ONES_ROWS, ts), lambda bi, i: (bi, 0, 0, i)),
        ],
        out_shape=[jax.ShapeDtypeStruct((b, s, qw), BF16), jax.ShapeDtypeStruct((b, s, kw), BF16),
                   jax.ShapeDtypeStruct((b, n_kv, D + ONES_ROWS, s), BF16)],
        compiler_params=_params(("parallel", "parallel")),
        name="axial_prep",
    )(proj, proj, proj, *tables, q_gain.reshape(1, D), k_gain.reshape(1, D))


FAST_SUM_MIN = 2.0 ** -80
FAST_SUM_MAX = 2.0 ** 100


def _flash_kernel(q_ref, k_ref, vt_ref, o_ref, acc_ref, st_ref, kn_ref, *, G, tq, tk, nk):
    D = HEAD_DIM
    R = G * tq
    qs = jnp.concatenate([q_ref[0][:, g * D:(g + 1) * D] for g in range(G)], axis=0)

    @pl.when(pl.program_id(2) == 0)
    def _():
        def kbody(c, mx):
            start = pl.multiple_of(c * tk, tk)
            kc = k_ref[0, pl.ds(start, tk), :].astype(F32)
            return jnp.maximum(mx, jnp.max(jnp.sum(kc * kc, axis=-1, keepdims=True), axis=0, keepdims=True))
        kn2 = lax.fori_loop(0, nk, kbody, jnp.zeros((1, 1), F32))
        kn_ref[...] = jnp.broadcast_to(kn2, kn_ref.shape)

    def scores(c, slot):
        start = pl.multiple_of(c * tk, tk)
        st_ref[slot] = _dot_nt(k_ref[0, pl.ds(start, tk), :], qs)

    def consume_fixed(c, slot, shift):
        start = pl.multiple_of(c * tk, tk)
        vtc = vt_ref[0, 0, :, pl.ds(start, tk)]
        p = jnp.exp2(st_ref[slot] - shift).astype(BF16)
        acc_ref[...] += _dot(vtc, p)
        return shift

    def consume_running(c, slot, m_old):
        start = pl.multiple_of(c * tk, tk)
        vtc = vt_ref[0, 0, :, pl.ds(start, tk)]
        st = st_ref[slot]
        m_new = jnp.maximum(m_old, jnp.max(st, axis=0, keepdims=True))
        alpha = jnp.exp2(m_old - m_new)
        p = jnp.exp2((st - m_new).astype(BF16))
        acc_ref[...] = alpha * acc_ref[...] + _dot(vtc, p)
        return m_new

    def run(consume, init):
        def body(c2, carry):
            c = 2 * c2
            scores(c + 1, 1)
            carry = consume(c, 0, carry)
            scores(jnp.minimum(c + 2, nk - 1), 0)
            return consume(c + 1, 1, carry)

        acc_ref[...] = jnp.zeros_like(acc_ref)
        scores(0, 0)
        lax.fori_loop(0, nk // 2, body, init)

    def write_out():
        out = (acc_ref[:D, :] / acc_ref[D:D + 1, :]).T
        for g in range(G):
            o_ref[0, :, g * D:(g + 1) * D] = out[g * tq:(g + 1) * tq].astype(o_ref.dtype)

    qf = qs.astype(F32)
    qn2 = _dot_nt(jnp.ones((8, D), BF16), (qf * qf).astype(BF16))[:1]
    run(consume_fixed, jnp.sqrt(qn2 * kn_ref[:1, :1]))
    sums = acc_ref[D:D + 1, :]
    trusted = (jnp.min(sums) >= FAST_SUM_MIN) & (jnp.max(sums) <= FAST_SUM_MAX)

    @pl.when(jnp.logical_not(trusted))
    def _():
        run(consume_running, jnp.full((1, R), -jnp.inf, F32))

    write_out()


def flash_attention(q, k, vt, *, n_heads, n_kv, tq, tk):
    b, s, _ = q.shape
    D = HEAD_DIM
    G = n_heads // n_kv
    kern = functools.partial(_flash_kernel, G=G, tq=tq, tk=tk, nk=s // tk)
    return pl.pallas_call(
        kern,
        grid=(b, n_kv, s // tq),
        in_specs=[
            pl.BlockSpec((1, tq, G * D), lambda bi, kv, qi: (bi, qi, kv)),
            pl.BlockSpec((1, s, D), lambda bi, kv, qi: (bi, 0, kv)),
            pl.BlockSpec((1, 1, D + ONES_ROWS, s), lambda bi, kv, qi: (bi, kv, 0, 0)),
        ],
        out_specs=pl.BlockSpec((1, tq, G * D), lambda bi, kv, qi: (bi, qi, kv)),
        out_shape=jax.ShapeDtypeStruct((b, s, n_heads * D), BF16),
        scratch_shapes=[pltpu.VMEM((D + ONES_ROWS, G * tq), F32), pltpu.VMEM((2, tk, G * tq), F32),
                        pltpu.VMEM((8, D), F32)],
        compiler_params=_params(("parallel", "parallel", "arbitrary")),
        name="flash_attention",
    )(q, k, vt)


def _rope_angles(pos, dim, theta):
    inv = theta ** (-jnp.arange(0, dim, 2, dtype=jnp.float32) / dim)
    return pos.astype(jnp.float32)[:, None] * inv[None, :]


def _retention_tables(s):
    ang = _rope_angles(jnp.arange(s), RET_DK, RET_THETA)
    c, sn = jnp.cos(ang), jnp.sin(ang)
    return jnp.concatenate([c, c], axis=-1), jnp.concatenate([-sn, sn], axis=-1)


def _axial_tables(s):
    rows = s // GRID_W
    row = jnp.repeat(jnp.arange(rows), GRID_W)
    col = jnp.tile(jnp.arange(GRID_W), rows)
    half = HEAD_DIM // 2
    ar = _rope_angles(row, half, AX_THETA)
    ac = _rope_angles(col, half, AX_THETA)
    cr, sr, ccol, scol = jnp.cos(ar), jnp.sin(ar), jnp.cos(ac), jnp.sin(ac)
    z = jnp.zeros_like(sr)
    cc = jnp.concatenate([cr, cr, ccol, ccol], axis=-1)
    sa = jnp.concatenate([-sr, z, -scol, z], axis=-1)
    sb = jnp.concatenate([z, sr, z, scol], axis=-1)
    return cc, sa, sb


def kernel(x, norm_mix, norm_mlp, w_in_even, w_out_even, ret_decay_logit, ret_norm, swa_q_norm,
           swa_k_norm, swa_sink, t5_table, w_in_odd, w_out_odd, ax_q_norm, ax_k_norm, w_mlp_up,
           w_mlp_down):
    b, s, d = x.shape
    t = b * s
    depth = norm_mix.shape[0]
    ret_heads = ret_decay_logit.shape[-1]
    ret_q = ret_heads * RET_DK
    ret_v = ret_heads * RET_DV
    swa_heads = swa_sink.shape[-1]
    swa_q = swa_heads * HEAD_DIM
    swa_kv = SWA_KV_HEADS * HEAD_DIM
    ax_q = w_out_odd.shape[1]
    ax_heads = ax_q // HEAD_DIM
    ax_kv = AX_KV_HEADS * HEAD_DIM

    x2 = x.reshape(t, d)
    ret_tabs = _retention_tables(s)
    ax_tabs = _axial_tables(s)

    for layer in range(depth):
        i = layer // 2
        if layer % 2 == 0:
            proj = norm_proj(x2, norm_mix[layer], w_in_even[i].astype(BF16), tm=1024, tn=1536)
            proj = proj.reshape(b, s, -1)
            ya = retention(proj, ret_decay_logit[i], ret_norm[i], *ret_tabs, n_heads=ret_heads,
                           q_off=0, k_off=ret_q, v_off=2 * ret_q, g_off=2 * ret_q + ret_v, cs=4)
            off = 2 * ret_q + 2 * ret_v
            yb = window_attention(proj, swa_sink[i], t5_table, swa_q_norm[i], swa_k_norm[i],
                                  n_heads=swa_heads, n_kv=SWA_KV_HEADS, q_off=off,
                                  k_off=off + swa_q, v_off=off + swa_q + swa_kv)
            x2 = out_proj(x2, [ya.reshape(t, -1), yb.reshape(t, -1)], w_out_even[i].astype(BF16),
                          tm=1024)
        else:
            proj = norm_proj(x2, norm_mix[layer], w_in_odd[i].astype(BF16), tm=1024, tn=1536)
            proj = proj.reshape(b, s, -1)
            qp, kp, vt = axial_prep(proj, ax_tabs, ax_q_norm[i], ax_k_norm[i], n_heads=ax_heads,
                                    n_kv=AX_KV_HEADS, ts=512)
            y = flash_attention(qp, kp, vt, n_heads=ax_heads, n_kv=AX_KV_HEADS, tq=256, tk=512)
            x2 = out_proj(x2, [y.reshape(t, -1)], w_out_odd[i].astype(BF16), tm=1024)
        x2 = mlp(x2, norm_mlp[layer], w_mlp_up[layer].astype(BF16), w_mlp_down[layer].astype(BF16),
                 tm=1024, tf=1024)
    return x2.reshape(b, s, d)
```

```python
import functools
import math

import jax
import jax.numpy as jnp
from jax import lax
from jax.experimental import pallas as pl
from jax.experimental.pallas import tpu as pltpu

F32 = jnp.float32
BF16 = jnp.bfloat16

EPS = 1e-6
NEG_INF = -1e30
LOG2E = math.log2(math.e)
HEAD_DIM = 128
BLOCK = 128
GRID_W = 64
RET_DK = 128
RET_DV = 256
RET_CHUNK = 128
RET_THETA = 10000.0
SWA_KV_HEADS = 2
WINDOW = 128
T5_BUCKETS = 32
T5_MAX_DIST = 128
AX_KV_HEADS = 2
AX_THETA = 10000.0

VMEM_LIMIT_BYTES = 56 * 1024 * 1024


def _params(semantics):
    return pltpu.CompilerParams(dimension_semantics=semantics, vmem_limit_bytes=VMEM_LIMIT_BYTES)


def _rms(x, gain):
    ms = jnp.mean(x * x, axis=-1, keepdims=True)
    return x * lax.rsqrt(ms + EPS) * gain


def _dot(a, b):
    return jnp.dot(a, b, preferred_element_type=F32)


def _dot_nt(a, b):
    return lax.dot_general(a, b, (((1,), (1,)), ((), ())), preferred_element_type=F32)


def _dot_tn(a, b):
    return lax.dot_general(a, b, (((0,), (0,)), ((), ())), preferred_element_type=F32)


def _norm_proj_kernel(x_ref, g_ref, w_ref, o_ref, h_ref):
    @pl.when(pl.program_id(1) == 0)
    def _():
        h_ref[...] = _rms(x_ref[...], g_ref[...]).astype(BF16)

    o_ref[...] = _dot(h_ref[...], w_ref[...]).astype(o_ref.dtype)


def norm_proj(x2, gain, w, *, tm, tn):
    t, d = x2.shape
    n = w.shape[1]
    return pl.pallas_call(
        _norm_proj_kernel,
        grid=(t // tm, n // tn),
        in_specs=[
            pl.BlockSpec((tm, d), lambda i, j: (i, 0)),
            pl.BlockSpec((1, d), lambda i, j: (0, 0)),
            pl.BlockSpec((d, tn), lambda i, j: (0, j)),
        ],
        out_specs=pl.BlockSpec((tm, tn), lambda i, j: (i, j)),
        out_shape=jax.ShapeDtypeStruct((t, n), BF16),
        scratch_shapes=[pltpu.VMEM((tm, d), BF16)],
        compiler_params=_params(("parallel", "arbitrary")),
        name="norm_proj",
    )(x2, gain.reshape(1, d), w)


def _out_proj_kernel(*refs):
    x_ref, o_ref = refs[0], refs[-1]
    n_in = (len(refs) - 2) // 2
    acc = x_ref[...]
    for i in range(n_in):
        acc = acc + _dot(refs[1 + i][...], refs[1 + n_in + i][...])
    o_ref[...] = acc


def out_proj(x2, acts, w, *, tm):
    t, d = x2.shape
    ks = [a.shape[1] for a in acts]
    assert len(set(ks)) == 1 and sum(ks) == w.shape[0]
    k = ks[0]
    in_specs = [pl.BlockSpec((tm, d), lambda i: (i, 0))]
    in_specs += [pl.BlockSpec((tm, k), lambda i: (i, 0)) for _ in acts]
    in_specs += [pl.BlockSpec((k, d), functools.partial(lambda i, s: (s, 0), s=s)) for s in range(len(acts))]
    return pl.pallas_call(
        _out_proj_kernel,
        grid=(t // tm,),
        in_specs=in_specs,
        out_specs=pl.BlockSpec((tm, d), lambda i: (i, 0)),
        out_shape=jax.ShapeDtypeStruct((t, d), F32),
        compiler_params=_params(("parallel",)),
        name="out_proj",
    )(x2, *acts, *([w] * len(acts)))


def _mlp_kernel(x_ref, g_ref, wu_ref, wd_ref, o_ref, h_ref):
    @pl.when(pl.program_id(1) == 0)
    def _():
        x = x_ref[...]
        h_ref[...] = _rms(x, g_ref[...]).astype(BF16)
        o_ref[...] = x

    u = _dot(h_ref[...], wu_ref[...])
    a = jnp.square(jnp.maximum(u, 0.0)).astype(BF16)
    o_ref[...] += _dot(a, wd_ref[...])


def mlp(x2, gain, wu, wd, *, tm, tf):
    t, d = x2.shape
    ff = wu.shape[1]
    return pl.pallas_call(
        _mlp_kernel,
        grid=(t // tm, ff // tf),
        in_specs=[
            pl.BlockSpec((tm, d), lambda i, j: (i, 0)),
            pl.BlockSpec((1, d), lambda i, j: (0, 0)),
            pl.BlockSpec((d, tf), lambda i, j: (0, j)),
            pl.BlockSpec((tf, d), lambda i, j: (j, 0)),
        ],
        out_specs=pl.BlockSpec((tm, d), lambda i, j: (i, 0)),
        out_shape=jax.ShapeDtypeStruct((t, d), F32),
        scratch_shapes=[pltpu.VMEM((tm, d), BF16)],
        compiler_params=_params(("parallel", "arbitrary")),
        name="mlp",
    )(x2, gain.reshape(1, d), wu, wd)


def _log_sigmoid(x):
    return -(jnp.maximum(-x, 0.0) + jnp.log1p(jnp.exp(-jnp.abs(x))))


def _ret_kernel(dl_ref, q_ref, k_ref, v_ref, g_ref, cos_ref, sin_ref, gn_ref, o_ref,
                sb_ref, sf_ref, cb_ref, dec_ref, *, cs):
    C = RET_CHUNK
    h = pl.program_id(1)
    phase = pl.program_id(2)
    t = pl.program_id(3)
    nsteps = pl.num_programs(3)

    lgf_w = _log_sigmoid(jnp.full((1, RET_DV), dl_ref[0, h], F32))
    lgb_w = _log_sigmoid(jnp.full((1, RET_DV), dl_ref[1, h], F32))

    @pl.when((phase == 0) & (t == 0))
    def _():
        row = lax.broadcasted_iota(jnp.int32, (C, C), 0).astype(F32)
        col = lax.broadcasted_iota(jnp.int32, (C, C), 1).astype(F32)
        lgf = lgf_w[:, :C]
        lgb = lgb_w[:, :C]
        diff = row - col
        dec_ref[0] = jnp.where(diff >= 0, jnp.exp(jnp.maximum(diff, 0.0) * lgf),
                               jnp.exp(jnp.maximum(-diff, 0.0) * lgb))
        dec_ref[1] = jnp.exp((row + 1.0) * lgf)
        dec_ref[2] = jnp.exp((C - 1.0 - row) * lgf)
        dec_ref[3] = jnp.exp((C - row) * lgb)
        dec_ref[4] = jnp.exp(row * lgb)
        sb_ref[...] = jnp.zeros_like(sb_ref)
        sf_ref[...] = jnp.zeros_like(sf_ref)

    def rope(a, rows):
        return a * cos_ref[rows, :] + pltpu.roll(a, RET_DK // 2, 1) * sin_ref[rows, :]

    @pl.when(phase == 0)
    def _():
        sb = sb_ref[...]
        cdec = jnp.exp(C * lgb_w)
        for ci in reversed(range(cs)):
            rows = slice(ci * C, (ci + 1) * C)
            qr = rope(q_ref[0, rows, :].astype(F32), rows)
            kr = rope(k_ref[0, rows, :].astype(F32), rows) * (RET_DK ** -0.5)
            c = (nsteps - 1 - t) * cs + ci
            cb_ref[c] = _dot((qr * dec_ref[3]).astype(BF16), sb.astype(BF16))
            sb = cdec * sb + _dot_tn((kr * dec_ref[4]).astype(BF16), v_ref[0, rows, :])
        sb_ref[...] = sb

    @pl.when(phase == 1)
    def _():
        sf = sf_ref[...]
        cdec = jnp.exp(C * lgf_w)
        for ci in range(cs):
            rows = slice(ci * C, (ci + 1) * C)
            qr = rope(q_ref[0, rows, :].astype(F32), rows)
            kr = rope(k_ref[0, rows, :].astype(F32), rows) * (RET_DK ** -0.5)
            v = v_ref[0, rows, :]
            s = _dot_nt(qr.astype(BF16), kr.astype(BF16))
            inner = _dot((s * dec_ref[0]).astype(BF16), v)
            cross = _dot((qr * dec_ref[1]).astype(BF16), sf.astype(BF16))
            sf = cdec * sf + _dot_tn((kr * dec_ref[2]).astype(BF16), v)
            tot = inner + cross + cb_ref[t * cs + ci]
            y = _rms(tot, gn_ref[...])
            g = g_ref[0, rows, :].astype(F32)
            o_ref[0, rows, :] = (g * jax.nn.sigmoid(g) * y).astype(o_ref.dtype)
        sf_ref[...] = sf


def retention(proj, decay_logit, gn_gain, cos2, sin2, *, n_heads, q_off, k_off, v_off, g_off, cs):
    b, s, _ = proj.shape
    C = RET_CHUNK
    nc = s // C
    ts = cs * C
    nsteps = nc // cs
    ret_v = n_heads * RET_DV

    def step(p, t):
        return jnp.where(p == 0, nsteps - 1 - t, t)

    def spec(width, off):
        base = off // width
        return pl.BlockSpec((1, ts, width), lambda bi, h, p, t: (bi, step(p, t), base + h))

    kern = functools.partial(_ret_kernel, cs=cs)
    return pl.pallas_call(
        kern,
        grid=(b, n_heads, 2, nsteps),
        in_specs=[
            pl.BlockSpec(memory_space=pltpu.SMEM),
            spec(RET_DK, q_off),
            spec(RET_DK, k_off),
            spec(RET_DV, v_off),
            spec(RET_DV, g_off),
            pl.BlockSpec((ts, RET_DK), lambda bi, h, p, t: (step(p, t), 0)),
            pl.BlockSpec((ts, RET_DK), lambda bi, h, p, t: (step(p, t), 0)),
            pl.BlockSpec((1, RET_DV), lambda bi, h, p, t: (0, h)),
        ],
        out_specs=pl.BlockSpec((1, ts, RET_DV), lambda bi, h, p, t: (bi, jnp.where(p == 0, 0, t), h)),
        out_shape=jax.ShapeDtypeStruct((b, s, ret_v), BF16),
        scratch_shapes=[
            pltpu.VMEM((RET_DK, RET_DV), F32),
            pltpu.VMEM((RET_DK, RET_DV), F32),
            pltpu.VMEM((nc, C, RET_DV), F32),
            pltpu.VMEM((5, C, C), F32),
        ],
        compiler_params=_params(("parallel", "parallel", "arbitrary", "arbitrary")),
        name="retention",
    )(decay_logit, proj, proj, proj, proj, cos2, sin2, gn_gain.reshape(1, ret_v))


def _swa_kernel(sink_ref, table_ref, q_ref, kp_ref, kc_ref, kn_ref, vp_ref, vc_ref, vn_ref,
                bucket_ref, qg_ref, kg_ref, o_ref, bias_ref, *, n_heads, n_kv, nb):
    i = pl.program_id(1)
    D = HEAD_DIM
    G = n_heads // n_kv
    KW = 3 * BLOCK
    QW = G * BLOCK
    lane_head = lax.broadcasted_iota(jnp.int32, (1, QW), 1) // BLOCK

    def per_head_row(ref, kv, *idx):
        row = jnp.full((1, QW), ref[(*idx, kv * G)], F32)
        for g in range(1, G):
            row = jnp.where(lane_head == g, ref[(*idx, kv * G + g)], row)
        return row

    @pl.when((pl.program_id(0) == 0) & (i == 0))
    def _():
        bucket = bucket_ref[...]
        j = lax.broadcasted_iota(jnp.int32, (KW, QW), 0)
        r = lax.broadcasted_iota(jnp.int32, (KW, QW), 1) % BLOCK
        in_band = jnp.abs(j - BLOCK - r) <= WINDOW
        for kv in range(n_kv):
            def body(bk, acc):
                return jnp.where(bucket == bk, per_head_row(table_ref, kv, bk), acc)
            bias = lax.fori_loop(0, T5_BUCKETS, body, jnp.zeros((KW, QW), F32))
            bias = jnp.where(in_band, bias * LOG2E, NEG_INF)
            for e in range(4):
                keep = (j >= BLOCK if e & 1 else True) & (j < 2 * BLOCK if e & 2 else True)
                bias_ref[e, kv] = bias if e == 0 else jnp.where(keep, bias, NEG_INF)

    edge = (i == 0).astype(jnp.int32) + 2 * (i == nb - 1).astype(jnp.int32)

    qg = qg_ref[...] * (D ** -0.5 * LOG2E)
    kg = kg_ref[...]
    for kv in range(n_kv):
        sl = slice(kv * D, (kv + 1) * D)
        k3 = jnp.concatenate([kp_ref[0][:, sl], kc_ref[0][:, sl], kn_ref[0][:, sl]], axis=0)
        k3 = _rms(k3.astype(F32), kg).astype(BF16)
        v3 = jnp.concatenate([vp_ref[0][:, sl], vc_ref[0][:, sl], vn_ref[0][:, sl]], axis=0)
        qs = jnp.concatenate(
            [_rms(q_ref[0][:, (kv * G + g) * D:(kv * G + g + 1) * D].astype(F32), qg).astype(BF16)
             for g in range(G)], axis=0)
        st = _dot_nt(k3, qs) + bias_ref[edge, kv]
        sink = per_head_row(sink_ref, kv) * LOG2E
        m = jnp.maximum(jnp.max(st, axis=0, keepdims=True), sink)
        p = jnp.exp2(st - m)
        denom = jnp.sum(p, axis=0, keepdims=True) + jnp.exp2(sink - m)
        o = (_dot_tn(v3, p.astype(BF16)) / denom).T
        for g in range(G):
            h = kv * G + g
            o_ref[0, :, h * D:(h + 1) * D] = o[g * BLOCK:(g + 1) * BLOCK].astype(o_ref.dtype)


def _t5_bucket(rel):
    nb = T5_BUCKETS // 2
    max_exact = nb // 2
    ret = jnp.where(rel > 0, nb, 0)
    n = jnp.abs(rel)
    nf = jnp.maximum(n, 1).astype(jnp.float32)
    large = max_exact + (jnp.log(nf / max_exact) / math.log(T5_MAX_DIST / max_exact)
                         * (nb - max_exact)).astype(jnp.int32)
    large = jnp.minimum(large, nb - 1)
    return ret + jnp.where(n < max_exact, n, large)


def window_attention(proj, sink, t5_table, q_gain, k_gain, *, n_heads, n_kv, q_off, k_off, v_off):
    b, s, _ = proj.shape
    D = HEAD_DIM
    nb = s // BLOCK
    qw = n_heads * D
    kw = n_kv * D
    rr = jnp.arange(BLOCK)
    jj = jnp.arange(3 * BLOCK)
    bucket = _t5_bucket(jj[None, :] - BLOCK - rr[:, None]).astype(jnp.int32)
    G = n_heads // n_kv
    bucket = jnp.tile(bucket.T, (1, G))

    def kv_spec(off, shift):
        base = off // kw
        return pl.BlockSpec((1, BLOCK, kw),
                            lambda bi, i: (bi, jnp.clip(i + shift, 0, nb - 1), base))

    kern = functools.partial(_swa_kernel, n_heads=n_heads, n_kv=n_kv, nb=nb)
    return pl.pallas_call(
        kern,
        grid=(b, nb),
        in_specs=[
            pl.BlockSpec(memory_space=pltpu.SMEM),
            pl.BlockSpec(memory_space=pltpu.SMEM),
            pl.BlockSpec((1, BLOCK, qw), lambda bi, i: (bi, i, q_off // qw)),
            kv_spec(k_off, -1), kv_spec(k_off, 0), kv_spec(k_off, 1),
            kv_spec(v_off, -1), kv_spec(v_off, 0), kv_spec(v_off, 1),
            pl.BlockSpec((3 * BLOCK, G * BLOCK), lambda bi, i: (0, 0)),
            pl.BlockSpec((1, D), lambda bi, i: (0, 0)),
            pl.BlockSpec((1, D), lambda bi, i: (0, 0)),
        ],
        out_specs=pl.BlockSpec((1, BLOCK, qw), lambda bi, i: (bi, i, 0)),
        out_shape=jax.ShapeDtypeStruct((b, s, qw), BF16),
        scratch_shapes=[pltpu.VMEM((4, n_kv, 3 * BLOCK, G * BLOCK), F32)],
        compiler_params=_params(("arbitrary", "arbitrary")),
        name="window_attention",
    )(sink, t5_table, proj, proj, proj, proj, proj, proj, proj, bucket,
      q_gain.reshape(1, D), k_gain.reshape(1, D))


def _axial_rope(a, cc, sa, sb):
    q4 = HEAD_DIM // 4
    return a * cc + pltpu.roll(a, HEAD_DIM - q4, 1) * sa + pltpu.roll(a, q4, 1) * sb


ONES_ROWS = 16


def _ax_prep_kernel(q_ref, k_ref, v_ref, cc_ref, sa_ref, sb_ref, qg_ref, kg_ref, qo_ref, ko_ref,
                    vt_ref, *, n_heads, n_kv):
    D = HEAD_DIM
    cc, sa, sb = cc_ref[...], sa_ref[...], sb_ref[...]
    qg = qg_ref[...] * (D ** -0.5 * LOG2E)
    kg = kg_ref[...]
    for h in range(n_heads):
        hs = slice(h * D, (h + 1) * D)
        q = _rms(q_ref[0][:, hs].astype(F32), qg)
        qo_ref[0, :, hs] = _axial_rope(q, cc, sa, sb).astype(qo_ref.dtype)
    for h in range(n_kv):
        hs = slice(h * D, (h + 1) * D)
        k = _rms(k_ref[0][:, hs].astype(F32), kg)
        ko_ref[0, :, hs] = _axial_rope(k, cc, sa, sb).astype(ko_ref.dtype)
        vt_ref[0, h, :D, :] = v_ref[0][:, hs].astype(F32).T.astype(vt_ref.dtype)
        vt_ref[0, h, D:, :] = jnp.ones((ONES_ROWS, vt_ref.shape[-1]), vt_ref.dtype)


def axial_prep(proj, tables, q_gain, k_gain, *, n_heads, n_kv, ts):
    b, s, _ = proj.shape
    D = HEAD_DIM
    qw, kw = n_heads * D, n_kv * D
    kern = functools.partial(_ax_prep_kernel, n_heads=n_heads, n_kv=n_kv)
    tab = pl.BlockSpec((ts, D), lambda bi, i: (i, 0))
    return pl.pallas_call(
        kern,
        grid=(b, s // ts),
        in_specs=[
            pl.BlockSpec((1, ts, qw), lambda bi, i: (bi, i, 0)),
            pl.BlockSpec((1, ts, kw), lambda bi, i: (bi, i, qw // kw)),
            pl.BlockSpec((1, ts, kw), lambda bi, i: (bi, i, qw // kw + 1)),
            tab, tab, tab,
            pl.BlockSpec((1, D), lambda bi, i: (0, 0)),
            pl.BlockSpec((1, D), lambda bi, i: (0, 0)),
        ],
        out_specs=[
            pl.BlockSpec((1, ts, qw), lambda bi, i: (bi, i, 0)),
            pl.BlockSpec((1, ts, kw), lambda bi, i: (bi, i, 0)),
            pl.BlockSpec((1, n_kv, D + ONES_ROWS, ts), lambda bi, i: (bi, 0, 0, i)),
        ],
        out_shape=[jax.ShapeDtypeStruct((b, s, qw), BF16), jax.ShapeDtypeStruct((b, s, kw), BF16),
                   jax.ShapeDtypeStruct((b, n_kv, D + ONES_ROWS, s), BF16)],
        compiler_params=_params(("parallel", "parallel")),
        name="axial_prep",
    )(proj, proj, proj, *tables, q_gain.reshape(1, D), k_gain.reshape(1, D))


FAST_SUM_MIN = 2.0 ** -80
FAST_SUM_MAX = 2.0 ** 100


def _flash_kernel(q_ref, k_ref, vt_ref, o_ref, acc_ref, st_ref, kn_ref, p_ref, *, G, tq, tk, nk):
    D = HEAD_DIM
    R = G * tq
    qs = jnp.concatenate([q_ref[0][:, g * D:(g + 1) * D] for g in range(G)], axis=0)

    @pl.when(pl.program_id(2) == 0)
    def _():
        def kbody(c, mx):
            start = pl.multiple_of(c * tk, tk)
            kc = k_ref[0, pl.ds(start, tk), :].astype(F32)
            return jnp.maximum(mx, jnp.max(jnp.sum(kc * kc, axis=-1, keepdims=True), axis=0, keepdims=True))
        kn2 = lax.fori_loop(0, nk, kbody, jnp.zeros((1, 1), F32))
        kn_ref[...] = jnp.broadcast_to(kn2, kn_ref.shape)

    def scores(c):
        start = pl.multiple_of(c * tk, tk)
        return _dot_nt(k_ref[0, pl.ds(start, tk), :], qs)

    def vt_chunk(c):
        return vt_ref[0, 0, :, pl.ds(pl.multiple_of(c * tk, tk), tk)]


    def produce_fixed(c, slot, shift):
        p_ref[slot] = jnp.exp2(scores(c) - shift).astype(BF16)

    def consume_fixed(c, slot, shift):
        acc_ref[...] += _dot(vt_chunk(c), p_ref[slot])
        return shift

    def produce_running(c, slot, m_old):
        st_ref[slot] = scores(c)

    def consume_running(c, slot, m_old):
        st = st_ref[slot]
        m_new = jnp.maximum(m_old, jnp.max(st, axis=0, keepdims=True))
        alpha = jnp.exp2(m_old - m_new)
        p = jnp.exp2((st - m_new).astype(BF16))
        acc_ref[...] = alpha * acc_ref[...] + _dot(vt_chunk(c), p)
        return m_new

    def run(produce, consume, init):
        def body(c2, carry):
            c = 2 * c2
            produce(c + 1, 1, carry)
            carry = consume(c, 0, carry)
            produce(jnp.minimum(c + 2, nk - 1), 0, carry)
            return consume(c + 1, 1, carry)

        acc_ref[...] = jnp.zeros_like(acc_ref)
        produce(0, 0, init)
        lax.fori_loop(0, nk // 2, body, init)

    def write_out():
        out = (acc_ref[:D, :] / acc_ref[D:D + 1, :]).T
        for g in range(G):
            o_ref[0, :, g * D:(g + 1) * D] = out[g * tq:(g + 1) * tq].astype(o_ref.dtype)

    qf = qs.astype(F32)
    qn2 = _dot_nt(jnp.ones((8, D), BF16), (qf * qf).astype(BF16))[:1]
    run(produce_fixed, consume_fixed, jnp.sqrt(qn2 * kn_ref[:1, :1]))
    sums = acc_ref[D:D + 1, :]
    trusted = (jnp.min(sums) >= FAST_SUM_MIN) & (jnp.max(sums) <= FAST_SUM_MAX)

    @pl.when(jnp.logical_not(trusted))
    def _():
        run(produce_running, consume_running, jnp.full((1, R), -jnp.inf, F32))

    write_out()


def flash_attention(q, k, vt, *, n_heads, n_kv, tq, tk):
    b, s, _ = q.shape
    D = HEAD_DIM
    G = n_heads // n_kv
    kern = functools.partial(_flash_kernel, G=G, tq=tq, tk=tk, nk=s // tk)
    return pl.pallas_call(
        kern,
        grid=(b, n_kv, s // tq),
        in_specs=[
            pl.BlockSpec((1, tq, G * D), lambda bi, kv, qi: (bi, qi, kv)),
            pl.BlockSpec((1, s, D), lambda bi, kv, qi: (bi, 0, kv)),
            pl.BlockSpec((1, 1, D + ONES_ROWS, s), lambda bi, kv, qi: (bi, kv, 0, 0)),
        ],
        out_specs=pl.BlockSpec((1, tq, G * D), lambda bi, kv, qi: (bi, qi, kv)),
        out_shape=jax.ShapeDtypeStruct((b, s, n_heads * D), BF16),
        scratch_shapes=[pltpu.VMEM((D + ONES_ROWS, G * tq), F32), pltpu.VMEM((2, tk, G * tq), F32),
                        pltpu.VMEM((8, D), F32), pltpu.VMEM((2, tk, G * tq), BF16)],
        compiler_params=_params(("parallel", "parallel", "arbitrary")),
        name="flash_attention",
    )(q, k, vt)


def _rope_angles(pos, dim, theta):
    inv = theta ** (-jnp.arange(0, dim, 2, dtype=jnp.float32) / dim)
    return pos.astype(jnp.float32)[:, None] * inv[None, :]


def _retention_tables(s):
    ang = _rope_angles(jnp.arange(s), RET_DK, RET_THETA)
    c, sn = jnp.cos(ang), jnp.sin(ang)
    return jnp.concatenate([c, c], axis=-1), jnp.concatenate([-sn, sn], axis=-1)


def _axial_tables(s):
    rows = s // GRID_W
    half = HEAD_DIM // 2
    ar = _rope_angles(jnp.arange(rows), half, AX_THETA)
    ac = _rope_angles(jnp.arange(GRID_W), half, AX_THETA)
    cr, sr = (jnp.repeat(f(ar), GRID_W, axis=0) for f in (jnp.cos, jnp.sin))
    ccol, scol = (jnp.tile(f(ac), (rows, 1)) for f in (jnp.cos, jnp.sin))
    z = jnp.zeros_like(sr)
    cc = jnp.concatenate([cr, cr, ccol, ccol], axis=-1)
    sa = jnp.concatenate([-sr, z, -scol, z], axis=-1)
    sb = jnp.concatenate([z, sr, z, scol], axis=-1)
    return cc, sa, sb


def kernel(x, norm_mix, norm_mlp, w_in_even, w_out_even, ret_decay_logit, ret_norm, swa_q_norm,
           swa_k_norm, swa_sink, t5_table, w_in_odd, w_out_odd, ax_q_norm, ax_k_norm, w_mlp_up,
           w_mlp_down):
    b, s, d = x.shape
    t = b * s
    depth = norm_mix.shape[0]
    ret_heads = ret_decay_logit.shape[-1]
    ret_q = ret_heads * RET_DK
    ret_v = ret_heads * RET_DV
    swa_heads = swa_sink.shape[-1]
    swa_q = swa_heads * HEAD_DIM
    swa_kv = SWA_KV_HEADS * HEAD_DIM
    ax_q = w_out_odd.shape[1]
    ax_heads = ax_q // HEAD_DIM
    ax_kv = AX_KV_HEADS * HEAD_DIM

    x2 = x.reshape(t, d)
    ret_tabs = _retention_tables(s)
    ax_tabs = _axial_tables(s)

    for layer in range(depth):
        i = layer // 2
        if layer % 2 == 0:
            proj = norm_proj(x2, norm_mix[layer], w_in_even[i].astype(BF16), tm=1024, tn=1536)
            proj = proj.reshape(b, s, -1)
            ya = retention(proj, ret_decay_logit[i], ret_norm[i], *ret_tabs, n_heads=ret_heads,
                           q_off=0, k_off=ret_q, v_off=2 * ret_q, g_off=2 * ret_q + ret_v, cs=8)
            off = 2 * ret_q + 2 * ret_v
            yb = window_attention(proj, swa_sink[i], t5_table, swa_q_norm[i], swa_k_norm[i],
                                  n_heads=swa_heads, n_kv=SWA_KV_HEADS, q_off=off,
                                  k_off=off + swa_q, v_off=off + swa_q + swa_kv)
            x2 = out_proj(x2, [ya.reshape(t, -1), yb.reshape(t, -1)], w_out_even[i].astype(BF16),
                          tm=1024)
        else:
            proj = norm_proj(x2, norm_mix[layer], w_in_odd[i].astype(BF16), tm=1024, tn=1536)
            proj = proj.reshape(b, s, -1)
            qp, kp, vt = axial_prep(proj, ax_tabs, ax_q_norm[i], ax_k_norm[i], n_heads=ax_heads,
                                    n_kv=AX_KV_HEADS, ts=512)
            y = flash_attention(qp, kp, vt, n_heads=ax_heads, n_kv=AX_KV_HEADS, tq=256, tk=512)
            x2 = out_proj(x2, [y.reshape(t, -1)], w_out_odd[i].astype(BF16), tm=1024)
        x2 = mlp(x2, norm_mlp[layer], w_mlp_up[layer].astype(BF16), w_mlp_down[layer].astype(BF16),
                 tm=1024, tf=1024)
    return x2.reshape(b, s, d)
```

```python
import functools
import math

import jax
import jax.numpy as jnp
from jax import lax
from jax.experimental import pallas as pl
from jax.experimental.pallas import tpu as pltpu

F32 = jnp.float32
BF16 = jnp.bfloat16

EPS = 1e-6
NEG_INF = -1e30
LOG2E = math.log2(math.e)
HEAD_DIM = 128
BLOCK = 128
GRID_W = 64
RET_DK = 128
RET_DV = 256
RET_CHUNK = 128
RET_THETA = 10000.0
SWA_KV_HEADS = 2
WINDOW = 128
T5_BUCKETS = 32
T5_MAX_DIST = 128
AX_KV_HEADS = 2
AX_THETA = 10000.0

VMEM_LIMIT_BYTES = 56 * 1024 * 1024


def _params(semantics):
    return pltpu.CompilerParams(dimension_semantics=semantics, vmem_limit_bytes=VMEM_LIMIT_BYTES)


def _rms(x, gain):
    ms = jnp.mean(x * x, axis=-1, keepdims=True)
    return x * lax.rsqrt(ms + EPS) * gain


def _dot(a, b):
    return jnp.dot(a, b, preferred_element_type=F32)


def _dot_nt(a, b):
    return lax.dot_general(a, b, (((1,), (1,)), ((), ())), preferred_element_type=F32)


def _dot_tn(a, b):
    return lax.dot_general(a, b, (((0,), (0,)), ((), ())), preferred_element_type=F32)


def _norm_proj_kernel(x_ref, g_ref, w_ref, o_ref, h_ref):
    @pl.when(pl.program_id(1) == 0)
    def _():
        h_ref[...] = _rms(x_ref[...], g_ref[...]).astype(BF16)

    o_ref[...] = _dot(h_ref[...], w_ref[...]).astype(o_ref.dtype)


def norm_proj(x2, gain, w, *, tm, tn):
    t, d = x2.shape
    n = w.shape[1]
    return pl.pallas_call(
        _norm_proj_kernel,
        grid=(t // tm, n // tn),
        in_specs=[
            pl.BlockSpec((tm, d), lambda i, j: (i, 0)),
            pl.BlockSpec((1, d), lambda i, j: (0, 0)),
            pl.BlockSpec((d, tn), lambda i, j: (0, j)),
        ],
        out_specs=pl.BlockSpec((tm, tn), lambda i, j: (i, j)),
        out_shape=jax.ShapeDtypeStruct((t, n), BF16),
        scratch_shapes=[pltpu.VMEM((tm, d), BF16)],
        compiler_params=_params(("parallel", "arbitrary")),
        name="norm_proj",
    )(x2, gain.reshape(1, d), w)


def _out_proj_kernel(*refs):
    x_ref, o_ref = refs[0], refs[-1]
    n_in = (len(refs) - 2) // 2
    acc = x_ref[...]
    for i in range(n_in):
        acc = acc + _dot(refs[1 + i][...], refs[1 + n_in + i][...])
    o_ref[...] = acc


def out_proj(x2, acts, w, *, tm):
    t, d = x2.shape
    ks = [a.shape[1] for a in acts]
    assert len(set(ks)) == 1 and sum(ks) == w.shape[0]
    k = ks[0]
    in_specs = [pl.BlockSpec((tm, d), lambda i: (i, 0))]
    in_specs += [pl.BlockSpec((tm, k), lambda i: (i, 0)) for _ in acts]
    in_specs += [pl.BlockSpec((k, d), functools.partial(lambda i, s: (s, 0), s=s)) for s in range(len(acts))]
    return pl.pallas_call(
        _out_proj_kernel,
        grid=(t // tm,),
        in_specs=in_specs,
        out_specs=pl.BlockSpec((tm, d), lambda i: (i, 0)),
        out_shape=jax.ShapeDtypeStruct((t, d), F32),
        compiler_params=_params(("parallel",)),
        name="out_proj",
    )(x2, *acts, *([w] * len(acts)))


def _mlp_kernel(x_ref, g_ref, wu_ref, wd_ref, o_ref, h_ref):
    @pl.when(pl.program_id(1) == 0)
    def _():
        x = x_ref[...]
        h_ref[...] = _rms(x, g_ref[...]).astype(BF16)
        o_ref[...] = x

    u = _dot(h_ref[...], wu_ref[...])
    a = jnp.square(jnp.maximum(u, 0.0)).astype(BF16)
    o_ref[...] += _dot(a, wd_ref[...])


def mlp(x2, gain, wu, wd, *, tm, tf):
    t, d = x2.shape
    ff = wu.shape[1]
    return pl.pallas_call(
        _mlp_kernel,
        grid=(t // tm, ff // tf),
        in_specs=[
            pl.BlockSpec((tm, d), lambda i, j: (i, 0)),
            pl.BlockSpec((1, d), lambda i, j: (0, 0)),
            pl.BlockSpec((d, tf), lambda i, j: (0, j)),
            pl.BlockSpec((tf, d), lambda i, j: (j, 0)),
        ],
        out_specs=pl.BlockSpec((tm, d), lambda i, j: (i, 0)),
        out_shape=jax.ShapeDtypeStruct((t, d), F32),
        scratch_shapes=[pltpu.VMEM((tm, d), BF16)],
        compiler_params=_params(("parallel", "arbitrary")),
        name="mlp",
    )(x2, gain.reshape(1, d), wu, wd)


def _log_sigmoid(x):
    return -(jnp.maximum(-x, 0.0) + jnp.log1p(jnp.exp(-jnp.abs(x))))


def _ret_kernel(dl_ref, q_ref, k_ref, v_ref, g_ref, cos_ref, sin_ref, gn_ref, o_ref,
                sb_ref, sf_ref, cb_ref, dec_ref, *, cs):
    C = RET_CHUNK
    h = pl.program_id(1)
    phase = pl.program_id(2)
    t = pl.program_id(3)
    nsteps = pl.num_programs(3)

    lgf_w = _log_sigmoid(jnp.full((1, RET_DV), dl_ref[0, h], F32))
    lgb_w = _log_sigmoid(jnp.full((1, RET_DV), dl_ref[1, h], F32))

    @pl.when((phase == 0) & (t == 0))
    def _():
        row = lax.broadcasted_iota(jnp.int32, (C, C), 0).astype(F32)
        col = lax.broadcasted_iota(jnp.int32, (C, C), 1).astype(F32)
        lgf = lgf_w[:, :C]
        lgb = lgb_w[:, :C]
        diff = row - col
        dec_ref[0] = jnp.where(diff >= 0, jnp.exp(jnp.maximum(diff, 0.0) * lgf),
                               jnp.exp(jnp.maximum(-diff, 0.0) * lgb))
        dec_ref[1] = jnp.exp((row + 1.0) * lgf)
        dec_ref[2] = jnp.exp((C - 1.0 - row) * lgf)
        dec_ref[3] = jnp.exp((C - row) * lgb)
        dec_ref[4] = jnp.exp(row * lgb)
        sb_ref[...] = jnp.zeros_like(sb_ref)
        sf_ref[...] = jnp.zeros_like(sf_ref)

    def rope(a, rows):
        return a * cos_ref[rows, :] + pltpu.roll(a, RET_DK // 2, 1) * sin_ref[rows, :]

    @pl.when(phase == 0)
    def _():
        sb = sb_ref[...]
        cdec = jnp.exp(C * lgb_w)
        for ci in reversed(range(cs)):
            rows = slice(ci * C, (ci + 1) * C)
            qr = rope(q_ref[0, rows, :].astype(F32), rows)
            kr = rope(k_ref[0, rows, :].astype(F32), rows) * (RET_DK ** -0.5)
            c = (nsteps - 1 - t) * cs + ci
            cb_ref[c] = _dot((qr * dec_ref[3]).astype(BF16), sb.astype(BF16))
            sb = cdec * sb + _dot_tn((kr * dec_ref[4]).astype(BF16), v_ref[0, rows, :])
        sb_ref[...] = sb

    @pl.when(phase == 1)
    def _():
        sf = sf_ref[...]
        cdec = jnp.exp(C * lgf_w)
        for ci in range(cs):
            rows = slice(ci * C, (ci + 1) * C)
            qr = rope(q_ref[0, rows, :].astype(F32), rows)
            kr = rope(k_ref[0, rows, :].astype(F32), rows) * (RET_DK ** -0.5)
            v = v_ref[0, rows, :]
            s = _dot_nt(qr.astype(BF16), kr.astype(BF16))
            inner = _dot((s * dec_ref[0]).astype(BF16), v)
            cross = _dot((qr * dec_ref[1]).astype(BF16), sf.astype(BF16))
            sf = cdec * sf + _dot_tn((kr * dec_ref[2]).astype(BF16), v)
            tot = inner + cross + cb_ref[t * cs + ci]
            y = _rms(tot, gn_ref[...])
            g = g_ref[0, rows, :].astype(F32)
            o_ref[0, rows, :] = (g * jax.nn.sigmoid(g) * y).astype(o_ref.dtype)
        sf_ref[...] = sf


def retention(proj, decay_logit, gn_gain, cos2, sin2, *, n_heads, q_off, k_off, v_off, g_off, cs):
    b, s, _ = proj.shape
    C = RET_CHUNK
    nc = s // C
    ts = cs * C
    nsteps = nc // cs
    ret_v = n_heads * RET_DV

    def step(p, t):
        return jnp.where(p == 0, nsteps - 1 - t, t)

    def spec(width, off):
        base = off // width
        return pl.BlockSpec((1, ts, width), lambda bi, h, p, t: (bi, step(p, t), base + h))

    kern = functools.partial(_ret_kernel, cs=cs)
    return pl.pallas_call(
        kern,
        grid=(b, n_heads, 2, nsteps),
        in_specs=[
            pl.BlockSpec(memory_space=pltpu.SMEM),
            spec(RET_DK, q_off),
            spec(RET_DK, k_off),
            spec(RET_DV, v_off),
            spec(RET_DV, g_off),
            pl.BlockSpec((ts, RET_DK), lambda bi, h, p, t: (step(p, t), 0)),
            pl.BlockSpec((ts, RET_DK), lambda bi, h, p, t: (step(p, t), 0)),
            pl.BlockSpec((1, RET_DV), lambda bi, h, p, t: (0, h)),
        ],
        out_specs=pl.BlockSpec((1, ts, RET_DV), lambda bi, h, p, t: (bi, jnp.where(p == 0, 0, t), h)),
        out_shape=jax.ShapeDtypeStruct((b, s, ret_v), BF16),
        scratch_shapes=[
            pltpu.VMEM((RET_DK, RET_DV), F32),
            pltpu.VMEM((RET_DK, RET_DV), F32),
            pltpu.VMEM((nc, C, RET_DV), F32),
            pltpu.VMEM((5, C, C), F32),
        ],
        compiler_params=_params(("parallel", "parallel", "arbitrary", "arbitrary")),
        name="retention",
    )(decay_logit, proj, proj, proj, proj, cos2, sin2, gn_gain.reshape(1, ret_v))


def _swa_kernel(sink_ref, table_ref, q_ref, kp_ref, kc_ref, kn_ref, vp_ref, vc_ref, vn_ref,
                bucket_ref, qg_ref, kg_ref, o_ref, bias_ref, *, n_heads, n_kv, nb):
    i = pl.program_id(1)
    D = HEAD_DIM
    G = n_heads // n_kv
    KW = 3 * BLOCK
    QW = G * BLOCK
    lane_head = lax.broadcasted_iota(jnp.int32, (1, QW), 1) // BLOCK

    def per_head_row(ref, kv, *idx):
        row = jnp.full((1, QW), ref[(*idx, kv * G)], F32)
        for g in range(1, G):
            row = jnp.where(lane_head == g, ref[(*idx, kv * G + g)], row)
        return row

    @pl.when((pl.program_id(0) == 0) & (i == 0))
    def _():
        bucket = bucket_ref[...]
        j = lax.broadcasted_iota(jnp.int32, (KW, QW), 0)
        r = lax.broadcasted_iota(jnp.int32, (KW, QW), 1) % BLOCK
        in_band = jnp.abs(j - BLOCK - r) <= WINDOW
        for kv in range(n_kv):
            def body(bk, acc):
                return jnp.where(bucket == bk, per_head_row(table_ref, kv, bk), acc)
            bias = lax.fori_loop(0, T5_BUCKETS, body, jnp.zeros((KW, QW), F32))
            bias = jnp.where(in_band, bias * LOG2E, NEG_INF)
            for e in range(4):
                keep = (j >= BLOCK if e & 1 else True) & (j < 2 * BLOCK if e & 2 else True)
                bias_ref[e, kv] = bias if e == 0 else jnp.where(keep, bias, NEG_INF)

    edge = (i == 0).astype(jnp.int32) + 2 * (i == nb - 1).astype(jnp.int32)

    qg = qg_ref[...] * (D ** -0.5 * LOG2E)
    kg = kg_ref[...]
    for kv in range(n_kv):
        sl = slice(kv * D, (kv + 1) * D)
        k3 = jnp.concatenate([kp_ref[0][:, sl], kc_ref[0][:, sl], kn_ref[0][:, sl]], axis=0)
        k3 = _rms(k3.astype(F32), kg).astype(BF16)
        v3 = jnp.concatenate([vp_ref[0][:, sl], vc_ref[0][:, sl], vn_ref[0][:, sl]], axis=0)
        qs = jnp.concatenate(
            [_rms(q_ref[0][:, (kv * G + g) * D:(kv * G + g + 1) * D].astype(F32), qg).astype(BF16)
             for g in range(G)], axis=0)
        st = _dot_nt(k3, qs) + bias_ref[edge, kv]
        sink = per_head_row(sink_ref, kv) * LOG2E
        m = jnp.maximum(jnp.max(st, axis=0, keepdims=True), sink)
        p = jnp.exp2(st - m)
        denom = jnp.sum(p, axis=0, keepdims=True) + jnp.exp2(sink - m)
        o = (_dot_tn(v3, p.astype(BF16)) / denom).T
        for g in range(G):
            h = kv * G + g
            o_ref[0, :, h * D:(h + 1) * D] = o[g * BLOCK:(g + 1) * BLOCK].astype(o_ref.dtype)


def _t5_bucket(rel):
    nb = T5_BUCKETS // 2
    max_exact = nb // 2
    ret = jnp.where(rel > 0, nb, 0)
    n = jnp.abs(rel)
    nf = jnp.maximum(n, 1).astype(jnp.float32)
    large = max_exact + (jnp.log(nf / max_exact) / math.log(T5_MAX_DIST / max_exact)
                         * (nb - max_exact)).astype(jnp.int32)
    large = jnp.minimum(large, nb - 1)
    return ret + jnp.where(n < max_exact, n, large)


def window_attention(proj, sink, t5_table, q_gain, k_gain, *, n_heads, n_kv, q_off, k_off, v_off):
    b, s, _ = proj.shape
    D = HEAD_DIM
    nb = s // BLOCK
    qw = n_heads * D
    kw = n_kv * D
    rr = jnp.arange(BLOCK)
    jj = jnp.arange(3 * BLOCK)
    bucket = _t5_bucket(jj[None, :] - BLOCK - rr[:, None]).astype(jnp.int32)
    G = n_heads // n_kv
    bucket = jnp.tile(bucket.T, (1, G))

    def kv_spec(off, shift):
        base = off // kw
        return pl.BlockSpec((1, BLOCK, kw),
                            lambda bi, i: (bi, jnp.clip(i + shift, 0, nb - 1), base))

    kern = functools.partial(_swa_kernel, n_heads=n_heads, n_kv=n_kv, nb=nb)
    return pl.pallas_call(
        kern,
        grid=(b, nb),
        in_specs=[
            pl.BlockSpec(memory_space=pltpu.SMEM),
            pl.BlockSpec(memory_space=pltpu.SMEM),
            pl.BlockSpec((1, BLOCK, qw), lambda bi, i: (bi, i, q_off // qw)),
            kv_spec(k_off, -1), kv_spec(k_off, 0), kv_spec(k_off, 1),
            kv_spec(v_off, -1), kv_spec(v_off, 0), kv_spec(v_off, 1),
            pl.BlockSpec((3 * BLOCK, G * BLOCK), lambda bi, i: (0, 0)),
            pl.BlockSpec((1, D), lambda bi, i: (0, 0)),
            pl.BlockSpec((1, D), lambda bi, i: (0, 0)),
        ],
        out_specs=pl.BlockSpec((1, BLOCK, qw), lambda bi, i: (bi, i, 0)),
        out_shape=jax.ShapeDtypeStruct((b, s, qw), BF16),
        scratch_shapes=[pltpu.VMEM((4, n_kv, 3 * BLOCK, G * BLOCK), F32)],
        compiler_params=_params(("arbitrary", "arbitrary")),
        name="window_attention",
    )(sink, t5_table, proj, proj, proj, proj, proj, proj, proj, bucket,
      q_gain.reshape(1, D), k_gain.reshape(1, D))


def _axial_rope(a, cc, sa, sb):
    q4 = HEAD_DIM // 4
    return a * cc + pltpu.roll(a, HEAD_DIM - q4, 1) * sa + pltpu.roll(a, q4, 1) * sb


ONES_ROWS = 16


def _ax_prep_kernel(q_ref, k_ref, v_ref, cc_ref, sa_ref, sb_ref, qg_ref, kg_ref, qo_ref, ko_ref,
                    vt_ref, *, n_heads, n_kv):
    D = HEAD_DIM
    cc, sa, sb = cc_ref[...], sa_ref[...], sb_ref[...]
    qg = qg_ref[...] * (D ** -0.5 * LOG2E)
    kg = kg_ref[...]
    for h in range(n_heads):
        hs = slice(h * D, (h + 1) * D)
        q = _rms(q_ref[0][:, hs].astype(F32), qg)
        qo_ref[0, :, hs] = _axial_rope(q, cc, sa, sb).astype(qo_ref.dtype)
    for h in range(n_kv):
        hs = slice(h * D, (h + 1) * D)
        k = _rms(k_ref[0][:, hs].astype(F32), kg)
        ko_ref[0, :, hs] = _axial_rope(k, cc, sa, sb).astype(ko_ref.dtype)
        vt_ref[0, h, :D, :] = v_ref[0][:, hs].astype(F32).T.astype(vt_ref.dtype)
        vt_ref[0, h, D:, :] = jnp.ones((ONES_ROWS, vt_ref.shape[-1]), vt_ref.dtype)


def axial_prep(proj, tables, q_gain, k_gain, *, n_heads, n_kv, ts):
    b, s, _ = proj.shape
    D = HEAD_DIM
    qw, kw = n_heads * D, n_kv * D
    kern = functools.partial(_ax_prep_kernel, n_heads=n_heads, n_kv=n_kv)
    tab = pl.BlockSpec((ts, D), lambda bi, i: (i, 0))
    return pl.pallas_call(
        kern,
        grid=(b, s // ts),
        in_specs=[
            pl.BlockSpec((1, ts, qw), lambda bi, i: (bi, i, 0)),
            pl.BlockSpec((1, ts, kw), lambda bi, i: (bi, i, qw // kw)),
            pl.BlockSpec((1, ts, kw), lambda bi, i: (bi, i, qw // kw + 1)),
            tab, tab, tab,
            pl.BlockSpec((1, D), lambda bi, i: (0, 0)),
            pl.BlockSpec((1, D), lambda bi, i: (0, 0)),
        ],
        out_specs=[
            pl.BlockSpec((1, ts, qw), lambda bi, i: (bi, i, 0)),
            pl.BlockSpec((1, ts, kw), lambda bi, i: (bi, i, 0)),
            pl.BlockSpec((1, n_kv, D + ONES_ROWS, ts), lambda bi, i: (bi, 0, 0, i)),
        ],
        out_shape=[jax.ShapeDtypeStruct((b, s, qw), BF16), jax.ShapeDtypeStruct((b, s, kw), BF16),
                   jax.ShapeDtypeStruct((b, n_kv, D + ONES_ROWS, s), BF16)],
        compiler_params=_params(("parallel", "parallel")),
        name="axial_prep",
    )(proj, proj, proj, *tables, q_gain.reshape(1, D), k_gain.reshape(1, D))


FAST_SUM_MIN = 2.0 ** -80
FAST_SUM_MAX = 2.0 ** 100


def _flash_kernel(q_ref, k_ref, vt_ref, o_ref, acc_ref, st_ref, kn_ref, p_ref, *, G, tq, tk, nk):
    D = HEAD_DIM
    R = G * tq
    qs = jnp.concatenate([q_ref[0][:, g * D:(g + 1) * D] for g in range(G)], axis=0)

    @pl.when(pl.program_id(2) == 0)
    def _():
        def kbody(c, mx):
            start = pl.multiple_of(c * tk, tk)
            kc = k_ref[0, pl.ds(start, tk), :].astype(F32)
            return jnp.maximum(mx, jnp.max(jnp.sum(kc * kc, axis=-1, keepdims=True), axis=0, keepdims=True))
        kn2 = lax.fori_loop(0, nk, kbody, jnp.zeros((1, 1), F32))
        kn_ref[...] = jnp.broadcast_to(kn2, kn_ref.shape)

    def scores(c):
        start = pl.multiple_of(c * tk, tk)
        return _dot_nt(k_ref[0, pl.ds(start, tk), :], qs)

    def vt_chunk(c):
        return vt_ref[0, 0, :, pl.ds(pl.multiple_of(c * tk, tk), tk)]


    def produce_fixed(c, slot, shift):
        p_ref[slot] = jnp.exp2(scores(c) - shift).astype(BF16)

    def consume_fixed(c, slot, shift):
        acc_ref[...] += _dot(vt_chunk(c), p_ref[slot])
        return shift

    def produce_running(c, slot, m_old):
        st_ref[slot] = scores(c)

    def consume_running(c, slot, m_old):
        st = st_ref[slot]
        m_new = jnp.maximum(m_old, jnp.max(st, axis=0, keepdims=True))
        alpha = jnp.exp2(m_old - m_new)
        p = jnp.exp2((st - m_new).astype(BF16))
        acc_ref[...] = alpha * acc_ref[...] + _dot(vt_chunk(c), p)
        return m_new

    def run(produce, consume, init):
        def body(c2, carry):
            c = 2 * c2
            produce(c + 1, 1, carry)
            carry = consume(c, 0, carry)
            produce(c + 2, 0, carry)
            return consume(c + 1, 1, carry)

        acc_ref[...] = jnp.zeros_like(acc_ref)
        produce(0, 0, init)
        carry = lax.fori_loop(0, nk // 2 - 1, body, init)
        produce(nk - 1, 1, carry)
        carry = consume(nk - 2, 0, carry)
        consume(nk - 1, 1, carry)

    def write_out():
        out = (acc_ref[:D, :] / acc_ref[D:D + 1, :]).T
        for g in range(G):
            o_ref[0, :, g * D:(g + 1) * D] = out[g * tq:(g + 1) * tq].astype(o_ref.dtype)

    qf = qs.astype(F32)
    qn2 = _dot_nt(jnp.ones((8, D), BF16), (qf * qf).astype(BF16))[:1]
    run(produce_fixed, consume_fixed, jnp.sqrt(qn2 * kn_ref[:1, :1]))
    sums = acc_ref[D:D + 1, :]
    trusted = (jnp.min(sums) >= FAST_SUM_MIN) & (jnp.max(sums) <= FAST_SUM_MAX)

    @pl.when(jnp.logical_not(trusted))
    def _():
        run(produce_running, consume_running, jnp.full((1, R), -jnp.inf, F32))

    write_out()


def flash_attention(q, k, vt, *, n_heads, n_kv, tq, tk):
    b, s, _ = q.shape
    D = HEAD_DIM
    G = n_heads // n_kv
    assert s % (2 * tk) == 0 and s // tk >= 2
    kern = functools.partial(_flash_kernel, G=G, tq=tq, tk=tk, nk=s // tk)
    return pl.pallas_call(
        kern,
        grid=(b, n_kv, s // tq),
        in_specs=[
            pl.BlockSpec((1, tq, G * D), lambda bi, kv, qi: (bi, qi, kv)),
            pl.BlockSpec((1, s, D), lambda bi, kv, qi: (bi, 0, kv)),
            pl.BlockSpec((1, 1, D + ONES_ROWS, s), lambda bi, kv, qi: (bi, kv, 0, 0)),
        ],
        out_specs=pl.BlockSpec((1, tq, G * D), lambda bi, kv, qi: (bi, qi, kv)),
        out_shape=jax.ShapeDtypeStruct((b, s, n_heads * D), BF16),
        scratch_shapes=[pltpu.VMEM((D + ONES_ROWS, G * tq), F32), pltpu.VMEM((2, tk, G * tq), F32),
                        pltpu.VMEM((8, D), F32), pltpu.VMEM((2, tk, G * tq), BF16)],
        compiler_params=_params(("parallel", "parallel", "arbitrary")),
        name="flash_attention",
    )(q, k, vt)


def _rope_angles(pos, dim, theta):
    inv = theta ** (-jnp.arange(0, dim, 2, dtype=jnp.float32) / dim)
    return pos.astype(jnp.float32)[:, None] * inv[None, :]


def _retention_tables(s):
    ang = _rope_angles(jnp.arange(s), RET_DK, RET_THETA)
    c, sn = jnp.cos(ang), jnp.sin(ang)
    return jnp.concatenate([c, c], axis=-1), jnp.concatenate([-sn, sn], axis=-1)


def _axial_tables(s):
    rows = s // GRID_W
    half = HEAD_DIM // 2
    ar = _rope_angles(jnp.arange(rows), half, AX_THETA)
    ac = _rope_angles(jnp.arange(GRID_W), half, AX_THETA)
    cr, sr = (jnp.repeat(f(ar), GRID_W, axis=0) for f in (jnp.cos, jnp.sin))
    ccol, scol = (jnp.tile(f(ac), (rows, 1)) for f in (jnp.cos, jnp.sin))
    z = jnp.zeros_like(sr)
    cc = jnp.concatenate([cr, cr, ccol, ccol], axis=-1)
    sa = jnp.concatenate([-sr, z, -scol, z], axis=-1)
    sb = jnp.concatenate([z, sr, z, scol], axis=-1)
    return cc, sa, sb


def kernel(x, norm_mix, norm_mlp, w_in_even, w_out_even, ret_decay_logit, ret_norm, swa_q_norm,
           swa_k_norm, swa_sink, t5_table, w_in_odd, w_out_odd, ax_q_norm, ax_k_norm, w_mlp_up,
           w_mlp_down):
    b, s, d = x.shape
    t = b * s
    depth = norm_mix.shape[0]
    ret_heads = ret_decay_logit.shape[-1]
    ret_q = ret_heads * RET_DK
    ret_v = ret_heads * RET_DV
    swa_heads = swa_sink.shape[-1]
    swa_q = swa_heads * HEAD_DIM
    swa_kv = SWA_KV_HEADS * HEAD_DIM
    ax_q = w_out_odd.shape[1]
    ax_heads = ax_q // HEAD_DIM
    ax_kv = AX_KV_HEADS * HEAD_DIM

    x2 = x.reshape(t, d)
    ret_tabs = _retention_tables(s)
    ax_tabs = _axial_tables(s)

    for layer in range(depth):
        i = layer // 2
        if layer % 2 == 0:
            proj = norm_proj(x2, norm_mix[layer], w_in_even[i].astype(BF16), tm=1024, tn=1536)
            proj = proj.reshape(b, s, -1)
            ya = retention(proj, ret_decay_logit[i], ret_norm[i], *ret_tabs, n_heads=ret_heads,
                           q_off=0, k_off=ret_q, v_off=2 * ret_q, g_off=2 * ret_q + ret_v, cs=8)
            off = 2 * ret_q + 2 * ret_v
            yb = window_attention(proj, swa_sink[i], t5_table, swa_q_norm[i], swa_k_norm[i],
                                  n_heads=swa_heads, n_kv=SWA_KV_HEADS, q_off=off,
                                  k_off=off + swa_q, v_off=off + swa_q + swa_kv)
            x2 = out_proj(x2, [ya.reshape(t, -1), yb.reshape(t, -1)], w_out_even[i].astype(BF16),
                          tm=1024)
        else:
            proj = norm_proj(x2, norm_mix[layer], w_in_odd[i].astype(BF16), tm=1024, tn=1536)
            proj = proj.reshape(b, s, -1)
            qp, kp, vt = axial_prep(proj, ax_tabs, ax_q_norm[i], ax_k_norm[i], n_heads=ax_heads,
                                    n_kv=AX_KV_HEADS, ts=512)
            y = flash_attention(qp, kp, vt, n_heads=ax_heads, n_kv=AX_KV_HEADS, tq=512, tk=512)
            x2 = out_proj(x2, [y.reshape(t, -1)], w_out_odd[i].astype(BF16), tm=1024)
        x2 = mlp(x2, norm_mlp[layer], w_mlp_up[layer].astype(BF16), w_mlp_down[layer].astype(BF16),
                 tm=1024, tf=1024)
    return x2.reshape(b, s, d)
```

```python
import functools
import math

import jax
import jax.numpy as jnp
from jax import lax
from jax.experimental import pallas as pl
from jax.experimental.pallas import tpu as pltpu

F32 = jnp.float32
BF16 = jnp.bfloat16

EPS = 1e-6
NEG_INF = -1e30
LOG2E = math.log2(math.e)
HEAD_DIM = 128
BLOCK = 128
GRID_W = 64
RET_DK = 128
RET_DV = 256
RET_CHUNK = 128
RET_THETA = 10000.0
SWA_KV_HEADS = 2
WINDOW = 128
T5_BUCKETS = 32
T5_MAX_DIST = 128
AX_KV_HEADS = 2
AX_THETA = 10000.0

VMEM_LIMIT_BYTES = 56 * 1024 * 1024
DENSE_ROWS = 512


def _params(semantics):
    return pltpu.CompilerParams(dimension_semantics=semantics, vmem_limit_bytes=VMEM_LIMIT_BYTES)


def _rms(x, gain):
    ms = jnp.mean(x * x, axis=-1, keepdims=True)
    return x * lax.rsqrt(ms + EPS) * gain


def _dot(a, b):
    return jnp.dot(a, b, preferred_element_type=F32)


def _dot_nt(a, b):
    return lax.dot_general(a, b, (((1,), (1,)), ((), ())), preferred_element_type=F32)


def _dot_tn(a, b):
    return lax.dot_general(a, b, (((0,), (0,)), ((), ())), preferred_element_type=F32)


def _resident(shape, row=0):
    idx = (row,) + (0,) * (len(shape) - 1)
    return pl.BlockSpec(shape, lambda i: idx, pipeline_mode=pl.Buffered(1))


def _norm_proj_kernel(x_ref, g_ref, w_ref, o_ref):
    h = _rms(x_ref[...], g_ref[...]).astype(BF16)
    o_ref[...] = _dot(h, w_ref[...]).astype(o_ref.dtype)


def norm_proj(x2, gain, w, *, tm):
    t, d = x2.shape
    n = w.shape[1]
    return pl.pallas_call(
        _norm_proj_kernel,
        grid=(t // tm,),
        in_specs=[pl.BlockSpec((tm, d), lambda i: (i, 0)), _resident((1, d)), _resident((d, n))],
        out_specs=pl.BlockSpec((tm, n), lambda i: (i, 0)),
        out_shape=jax.ShapeDtypeStruct((t, n), BF16),
        compiler_params=_params(("parallel",)),
        name="norm_proj",
    )(x2, gain.reshape(1, d), w)


def _out_mlp_kernel(*refs, n_act):
    x_ref = refs[0]
    act_refs = refs[1:1 + n_act]
    wo_refs = refs[1 + n_act:1 + 2 * n_act]
    g_ref, wu_ref, wd_ref, o_ref = refs[1 + 2 * n_act:]
    y = x_ref[...]
    for a_ref, wo_ref in zip(act_refs, wo_refs):
        y = y + _dot(a_ref[...], wo_ref[...])
    h = _rms(y, g_ref[...]).astype(BF16)
    a = jnp.square(jnp.maximum(_dot(h, wu_ref[...]), 0.0)).astype(BF16)
    o_ref[...] = y + _dot(a, wd_ref[...])


def out_mlp(x2, acts, w_out, gain, wu, wd, *, tm):
    t, d = x2.shape
    ff = wu.shape[1]
    k = acts[0].shape[1]
    assert all(a.shape[1] == k for a in acts) and len(acts) * k == w_out.shape[0]

    def row_tile(width):
        return pl.BlockSpec((tm, width), lambda i: (i, 0))

    in_specs = [row_tile(d)] + [row_tile(k) for _ in acts]
    in_specs += [_resident((k, d), row=s) for s in range(len(acts))]
    in_specs += [_resident((1, d)), _resident((d, ff)), _resident((ff, d))]
    return pl.pallas_call(
        functools.partial(_out_mlp_kernel, n_act=len(acts)),
        grid=(t // tm,),
        in_specs=in_specs,
        out_specs=row_tile(d),
        out_shape=jax.ShapeDtypeStruct((t, d), F32),
        compiler_params=_params(("parallel",)),
        name="out_mlp",
    )(x2, *acts, *([w_out] * len(acts)), gain.reshape(1, d), wu, wd)


def _log_sigmoid(x):
    return -(jnp.maximum(-x, 0.0) + jnp.log1p(jnp.exp(-jnp.abs(x))))


def _ret_kernel(dl_ref, q_ref, k_ref, v_ref, g_ref, cos_ref, sin_ref, gn_ref, o_ref,
                sb_ref, sf_ref, cb_ref, dec_ref, *, cs):
    C = RET_CHUNK
    h = pl.program_id(1)
    phase = pl.program_id(2)
    t = pl.program_id(3)
    nsteps = pl.num_programs(3)

    lgf_w = _log_sigmoid(jnp.full((1, RET_DV), dl_ref[0, h], F32))
    lgb_w = _log_sigmoid(jnp.full((1, RET_DV), dl_ref[1, h], F32))

    @pl.when((phase == 0) & (t == 0))
    def _():
        row = lax.broadcasted_iota(jnp.int32, (C, C), 0).astype(F32)
        col = lax.broadcasted_iota(jnp.int32, (C, C), 1).astype(F32)
        lgf = lgf_w[:, :C]
        lgb = lgb_w[:, :C]
        diff = row - col
        dec_ref[0] = jnp.where(diff >= 0, jnp.exp(jnp.maximum(diff, 0.0) * lgf),
                               jnp.exp(jnp.maximum(-diff, 0.0) * lgb))
        dec_ref[1] = jnp.exp((row + 1.0) * lgf)
        dec_ref[2] = jnp.exp((C - 1.0 - row) * lgf)
        dec_ref[3] = jnp.exp((C - row) * lgb)
        dec_ref[4] = jnp.exp(row * lgb)
        sb_ref[...] = jnp.zeros_like(sb_ref)
        sf_ref[...] = jnp.zeros_like(sf_ref)

    def rope(a, rows):
        return a * cos_ref[rows, :] + pltpu.roll(a, RET_DK // 2, 1) * sin_ref[rows, :]

    @pl.when(phase == 0)
    def _():
        sb = sb_ref[...]
        cdec = jnp.exp(C * lgb_w)
        for ci in reversed(range(cs)):
            rows = slice(ci * C, (ci + 1) * C)
            qr = rope(q_ref[0, rows, :].astype(F32), rows)
            kr = rope(k_ref[0, rows, :].astype(F32), rows) * (RET_DK ** -0.5)
            c = (nsteps - 1 - t) * cs + ci
            cb_ref[c] = _dot((qr * dec_ref[3]).astype(BF16), sb.astype(BF16))
            sb = cdec * sb + _dot_tn((kr * dec_ref[4]).astype(BF16), v_ref[0, rows, :])
        sb_ref[...] = sb

    @pl.when(phase == 1)
    def _():
        sf = sf_ref[...]
        cdec = jnp.exp(C * lgf_w)
        for ci in range(cs):
            rows = slice(ci * C, (ci + 1) * C)
            qr = rope(q_ref[0, rows, :].astype(F32), rows)
            kr = rope(k_ref[0, rows, :].astype(F32), rows) * (RET_DK ** -0.5)
            v = v_ref[0, rows, :]
            s = _dot_nt(qr.astype(BF16), kr.astype(BF16))
            inner = _dot((s * dec_ref[0]).astype(BF16), v)
            cross = _dot((qr * dec_ref[1]).astype(BF16), sf.astype(BF16))
            sf = cdec * sf + _dot_tn((kr * dec_ref[2]).astype(BF16), v)
            tot = inner + cross + cb_ref[t * cs + ci]
            y = _rms(tot, gn_ref[...])
            g = g_ref[0, rows, :].astype(F32)
            o_ref[0, rows, :] = (g * jax.nn.sigmoid(g) * y).astype(o_ref.dtype)
        sf_ref[...] = sf


def retention(proj, decay_logit, gn_gain, cos2, sin2, *, n_heads, q_off, k_off, v_off, g_off, cs):
    b, s, _ = proj.shape
    C = RET_CHUNK
    nc = s // C
    ts = cs * C
    nsteps = nc // cs
    ret_v = n_heads * RET_DV

    def step(p, t):
        return jnp.where(p == 0, nsteps - 1 - t, t)

    def spec(width, off):
        base = off // width
        return pl.BlockSpec((1, ts, width), lambda bi, h, p, t: (bi, step(p, t), base + h))

    kern = functools.partial(_ret_kernel, cs=cs)
    return pl.pallas_call(
        kern,
        grid=(b, n_heads, 2, nsteps),
        in_specs=[
            pl.BlockSpec(memory_space=pltpu.SMEM),
            spec(RET_DK, q_off),
            spec(RET_DK, k_off),
            spec(RET_DV, v_off),
            spec(RET_DV, g_off),
            pl.BlockSpec((ts, RET_DK), lambda bi, h, p, t: (step(p, t), 0)),
            pl.BlockSpec((ts, RET_DK), lambda bi, h, p, t: (step(p, t), 0)),
            pl.BlockSpec((1, RET_DV), lambda bi, h, p, t: (0, h)),
        ],
        out_specs=pl.BlockSpec((1, ts, RET_DV), lambda bi, h, p, t: (bi, jnp.where(p == 0, 0, t), h)),
        out_shape=jax.ShapeDtypeStruct((b, s, ret_v), BF16),
        scratch_shapes=[
            pltpu.VMEM((RET_DK, RET_DV), F32),
            pltpu.VMEM((RET_DK, RET_DV), F32),
            pltpu.VMEM((nc, C, RET_DV), F32),
            pltpu.VMEM((5, C, C), F32),
        ],
        compiler_params=_params(("parallel", "parallel", "arbitrary", "arbitrary")),
        name="retention",
    )(decay_logit, proj, proj, proj, proj, cos2, sin2, gn_gain.reshape(1, ret_v))


def _swa_kernel(sink_ref, table_ref, q_ref, kp_ref, kc_ref, kn_ref, vp_ref, vc_ref, vn_ref,
                bucket_ref, qg_ref, kg_ref, o_ref, bias_ref, *, n_heads, n_kv, nb):
    i = pl.program_id(1)
    D = HEAD_DIM
    G = n_heads // n_kv
    KW = 3 * BLOCK
    QW = G * BLOCK
    lane_head = lax.broadcasted_iota(jnp.int32, (1, QW), 1) // BLOCK

    def per_head_row(ref, kv, *idx):
        row = jnp.full((1, QW), ref[(*idx, kv * G)], F32)
        for g in range(1, G):
            row = jnp.where(lane_head == g, ref[(*idx, kv * G + g)], row)
        return row

    @pl.when((pl.program_id(0) == 0) & (i == 0))
    def _():
        bucket = bucket_ref[...]
        j = lax.broadcasted_iota(jnp.int32, (KW, QW), 0)
        r = lax.broadcasted_iota(jnp.int32, (KW, QW), 1) % BLOCK
        in_band = jnp.abs(j - BLOCK - r) <= WINDOW
        for kv in range(n_kv):
            def body(bk, acc):
                return jnp.where(bucket == bk, per_head_row(table_ref, kv, bk), acc)
            bias = lax.fori_loop(0, T5_BUCKETS, body, jnp.zeros((KW, QW), F32))
            bias = jnp.where(in_band, bias * LOG2E, NEG_INF)
            for e in range(4):
                keep = (j >= BLOCK if e & 1 else True) & (j < 2 * BLOCK if e & 2 else True)
                bias_ref[e, kv] = bias if e == 0 else jnp.where(keep, bias, NEG_INF)

    edge = (i == 0).astype(jnp.int32) + 2 * (i == nb - 1).astype(jnp.int32)

    qg = qg_ref[...] * (D ** -0.5 * LOG2E)
    kg = kg_ref[...]
    for kv in range(n_kv):
        sl = slice(kv * D, (kv + 1) * D)
        k3 = jnp.concatenate([kp_ref[0][:, sl], kc_ref[0][:, sl], kn_ref[0][:, sl]], axis=0)
        k3 = _rms(k3.astype(F32), kg).astype(BF16)
        v3 = jnp.concatenate([vp_ref[0][:, sl], vc_ref[0][:, sl], vn_ref[0][:, sl]], axis=0)
        qs = jnp.concatenate(
            [_rms(q_ref[0][:, (kv * G + g) * D:(kv * G + g + 1) * D].astype(F32), qg).astype(BF16)
             for g in range(G)], axis=0)
        st = _dot_nt(k3, qs) + bias_ref[edge, kv]
        sink = per_head_row(sink_ref, kv) * LOG2E
        m = jnp.maximum(jnp.max(st, axis=0, keepdims=True), sink)
        p = jnp.exp2(st - m)
        denom = jnp.sum(p, axis=0, keepdims=True) + jnp.exp2(sink - m)
        o = (_dot_tn(v3, p.astype(BF16)) / denom).T
        for g in range(G):
            h = kv * G + g
            o_ref[0, :, h * D:(h + 1) * D] = o[g * BLOCK:(g + 1) * BLOCK].astype(o_ref.dtype)


def _t5_bucket(rel):
    nb = T5_BUCKETS // 2
    max_exact = nb // 2
    ret = jnp.where(rel > 0, nb, 0)
    n = jnp.abs(rel)
    nf = jnp.maximum(n, 1).astype(jnp.float32)
    large = max_exact + (jnp.log(nf / max_exact) / math.log(T5_MAX_DIST / max_exact)
                         * (nb - max_exact)).astype(jnp.int32)
    large = jnp.minimum(large, nb - 1)
    return ret + jnp.where(n < max_exact, n, large)


def window_attention(proj, sink, t5_table, q_gain, k_gain, *, n_heads, n_kv, q_off, k_off, v_off):
    b, s, _ = proj.shape
    D = HEAD_DIM
    nb = s // BLOCK
    qw = n_heads * D
    kw = n_kv * D
    rr = jnp.arange(BLOCK)
    jj = jnp.arange(3 * BLOCK)
    bucket = _t5_bucket(jj[None, :] - BLOCK - rr[:, None]).astype(jnp.int32)
    G = n_heads // n_kv
    bucket = jnp.tile(bucket.T, (1, G))

    def kv_spec(off, shift):
        base = off // kw
        return pl.BlockSpec((1, BLOCK, kw),
                            lambda bi, i: (bi, jnp.clip(i + shift, 0, nb - 1), base))

    kern = functools.partial(_swa_kernel, n_heads=n_heads, n_kv=n_kv, nb=nb)
    return pl.pallas_call(
        kern,
        grid=(b, nb),
        in_specs=[
            pl.BlockSpec(memory_space=pltpu.SMEM),
            pl.BlockSpec(memory_space=pltpu.SMEM),
            pl.BlockSpec((1, BLOCK, qw), lambda bi, i: (bi, i, q_off // qw)),
            kv_spec(k_off, -1), kv_spec(k_off, 0), kv_spec(k_off, 1),
            kv_spec(v_off, -1), kv_spec(v_off, 0), kv_spec(v_off, 1),
            pl.BlockSpec((3 * BLOCK, G * BLOCK), lambda bi, i: (0, 0)),
            pl.BlockSpec((1, D), lambda bi, i: (0, 0)),
            pl.BlockSpec((1, D), lambda bi, i: (0, 0)),
        ],
        out_specs=pl.BlockSpec((1, BLOCK, qw), lambda bi, i: (bi, i, 0)),
        out_shape=jax.ShapeDtypeStruct((b, s, qw), BF16),
        scratch_shapes=[pltpu.VMEM((4, n_kv, 3 * BLOCK, G * BLOCK), F32)],
        compiler_params=_params(("arbitrary", "arbitrary")),
        name="window_attention",
    )(sink, t5_table, proj, proj, proj, proj, proj, proj, proj, bucket,
      q_gain.reshape(1, D), k_gain.reshape(1, D))


def _axial_rope(a, cc, sa, sb):
    q4 = HEAD_DIM // 4
    return a * cc + pltpu.roll(a, HEAD_DIM - q4, 1) * sa + pltpu.roll(a, q4, 1) * sb


ONES_ROWS = 16


def _ax_prep_kernel(q_ref, k_ref, v_ref, cc_ref, sa_ref, sb_ref, qg_ref, kg_ref, qo_ref, ko_ref,
                    vt_ref, *, n_heads, n_kv):
    D = HEAD_DIM
    cc, sa, sb = cc_ref[...], sa_ref[...], sb_ref[...]
    qg = qg_ref[...] * (D ** -0.5 * LOG2E)
    kg = kg_ref[...]
    for h in range(n_heads):
        hs = slice(h * D, (h + 1) * D)
        q = _rms(q_ref[0][:, hs].astype(F32), qg)
        qo_ref[0, :, hs] = _axial_rope(q, cc, sa, sb).astype(qo_ref.dtype)
    for h in range(n_kv):
        hs = slice(h * D, (h + 1) * D)
        k = _rms(k_ref[0][:, hs].astype(F32), kg)
        ko_ref[0, :, hs] = _axial_rope(k, cc, sa, sb).astype(ko_ref.dtype)
        vt_ref[0, h, :D, :] = v_ref[0][:, hs].astype(F32).T.astype(vt_ref.dtype)
        vt_ref[0, h, D:, :] = jnp.ones((ONES_ROWS, vt_ref.shape[-1]), vt_ref.dtype)


def axial_prep(proj, tables, q_gain, k_gain, *, n_heads, n_kv, ts):
    b, s, _ = proj.shape
    D = HEAD_DIM
    qw, kw = n_heads * D, n_kv * D
    kern = functools.partial(_ax_prep_kernel, n_heads=n_heads, n_kv=n_kv)
    tab = pl.BlockSpec((ts, D), lambda bi, i: (i, 0))
    return pl.pallas_call(
        kern,
        grid=(b, s // ts),
        in_specs=[
            pl.BlockSpec((1, ts, qw), lambda bi, i: (bi, i, 0)),
            pl.BlockSpec((1, ts, kw), lambda bi, i: (bi, i, qw // kw)),
            pl.BlockSpec((1, ts, kw), lambda bi, i: (bi, i, qw // kw + 1)),
            tab, tab, tab,
            pl.BlockSpec((1, D), lambda bi, i: (0, 0)),
            pl.BlockSpec((1, D), lambda bi, i: (0, 0)),
        ],
        out_specs=[
            pl.BlockSpec((1, ts, qw), lambda bi, i: (bi, i, 0)),
            pl.BlockSpec((1, ts, kw), lambda bi, i: (bi, i, 0)),
            pl.BlockSpec((1, n_kv, D + ONES_ROWS, ts), lambda bi, i: (bi, 0, 0, i)),
        ],
        out_shape=[jax.ShapeDtypeStruct((b, s, qw), BF16), jax.ShapeDtypeStruct((b, s, kw), BF16),
                   jax.ShapeDtypeStruct((b, n_kv, D + ONES_ROWS, s), BF16)],
        compiler_params=_params(("parallel", "parallel")),
        name="axial_prep",
    )(proj, proj, proj, *tables, q_gain.reshape(1, D), k_gain.reshape(1, D))


FAST_SUM_MIN = 2.0 ** -80
FAST_SUM_MAX = 2.0 ** 100


def _flash_kernel(q_ref, k_ref, vt_ref, o_ref, acc_ref, st_ref, kn_ref, p_ref, *, G, tq, tk, nk):
    D = HEAD_DIM
    R = G * tq
    qs = jnp.concatenate([q_ref[0][:, g * D:(g + 1) * D] for g in range(G)], axis=0)

    @pl.when(pl.program_id(2) == 0)
    def _():
        def kbody(c, mx):
            start = pl.multiple_of(c * tk, tk)
            kc = k_ref[0, pl.ds(start, tk), :].astype(F32)
            return jnp.maximum(mx, jnp.max(jnp.sum(kc * kc, axis=-1, keepdims=True), axis=0, keepdims=True))
        kn2 = lax.fori_loop(0, nk, kbody, jnp.zeros((1, 1), F32))
        kn_ref[...] = jnp.broadcast_to(kn2, kn_ref.shape)

    def scores(c):
        start = pl.multiple_of(c * tk, tk)
        return _dot_nt(k_ref[0, pl.ds(start, tk), :], qs)

    def vt_chunk(c):
        return vt_ref[0, 0, :, pl.ds(pl.multiple_of(c * tk, tk), tk)]


    def produce_fixed(c, slot, shift):
        p_ref[slot] = jnp.exp2(scores(c) - shift).astype(BF16)

    def consume_fixed(c, slot, shift):
        acc_ref[...] += _dot(vt_chunk(c), p_ref[slot])
        return shift

    def produce_running(c, slot, m_old):
        st_ref[slot] = scores(c)

    def consume_running(c, slot, m_old):
        st = st_ref[slot]
        m_new = jnp.maximum(m_old, jnp.max(st, axis=0, keepdims=True))
        alpha = jnp.exp2(m_old - m_new)
        p = jnp.exp2((st - m_new).astype(BF16))
        acc_ref[...] = alpha * acc_ref[...] + _dot(vt_chunk(c), p)
        return m_new

    def run(produce, consume, init):
        def body(c2, carry):
            c = 2 * c2
            produce(c + 1, 1, carry)
            carry = consume(c, 0, carry)
            produce(c + 2, 0, carry)
            return consume(c + 1, 1, carry)

        acc_ref[...] = jnp.zeros_like(acc_ref)
        produce(0, 0, init)
        carry = lax.fori_loop(0, nk // 2 - 1, body, init)
        produce(nk - 1, 1, carry)
        carry = consume(nk - 2, 0, carry)
        consume(nk - 1, 1, carry)

    def write_out():
        out = (acc_ref[:D, :] / acc_ref[D:D + 1, :]).T
        for g in range(G):
            o_ref[0, :, g * D:(g + 1) * D] = out[g * tq:(g + 1) * tq].astype(o_ref.dtype)

    qf = qs.astype(F32)
    qn2 = _dot_nt(jnp.ones((8, D), BF16), (qf * qf).astype(BF16))[:1]
    run(produce_fixed, consume_fixed, jnp.sqrt(qn2 * kn_ref[:1, :1]))
    sums = acc_ref[D:D + 1, :]
    trusted = (jnp.min(sums) >= FAST_SUM_MIN) & (jnp.max(sums) <= FAST_SUM_MAX)

    @pl.when(jnp.logical_not(trusted))
    def _():
        run(produce_running, consume_running, jnp.full((1, R), -jnp.inf, F32))

    write_out()


def flash_attention(q, k, vt, *, n_heads, n_kv, tq, tk):
    b, s, _ = q.shape
    D = HEAD_DIM
    G = n_heads // n_kv
    assert s % (2 * tk) == 0 and s // tk >= 2
    kern = functools.partial(_flash_kernel, G=G, tq=tq, tk=tk, nk=s // tk)
    return pl.pallas_call(
        kern,
        grid=(b, n_kv, s // tq),
        in_specs=[
            pl.BlockSpec((1, tq, G * D), lambda bi, kv, qi: (bi, qi, kv)),
            pl.BlockSpec((1, s, D), lambda bi, kv, qi: (bi, 0, kv)),
            pl.BlockSpec((1, 1, D + ONES_ROWS, s), lambda bi, kv, qi: (bi, kv, 0, 0)),
        ],
        out_specs=pl.BlockSpec((1, tq, G * D), lambda bi, kv, qi: (bi, qi, kv)),
        out_shape=jax.ShapeDtypeStruct((b, s, n_heads * D), BF16),
        scratch_shapes=[pltpu.VMEM((D + ONES_ROWS, G * tq), F32), pltpu.VMEM((2, tk, G * tq), F32),
                        pltpu.VMEM((8, D), F32), pltpu.VMEM((2, tk, G * tq), BF16)],
        compiler_params=_params(("parallel", "parallel", "arbitrary")),
        name="flash_attention",
    )(q, k, vt)


def _rope_angles(pos, dim, theta):
    inv = theta ** (-jnp.arange(0, dim, 2, dtype=jnp.float32) / dim)
    return pos.astype(jnp.float32)[:, None] * inv[None, :]


def _retention_tables(s):
    ang = _rope_angles(jnp.arange(s), RET_DK, RET_THETA)
    c, sn = jnp.cos(ang), jnp.sin(ang)
    return jnp.concatenate([c, c], axis=-1), jnp.concatenate([-sn, sn], axis=-1)


def _axial_tables(s):
    rows = s // GRID_W
    half = HEAD_DIM // 2
    ar = _rope_angles(jnp.arange(rows), half, AX_THETA)
    ac = _rope_angles(jnp.arange(GRID_W), half, AX_THETA)
    cr, sr = (jnp.repeat(f(ar), GRID_W, axis=0) for f in (jnp.cos, jnp.sin))
    ccol, scol = (jnp.tile(f(ac), (rows, 1)) for f in (jnp.cos, jnp.sin))
    z = jnp.zeros_like(sr)
    cc = jnp.concatenate([cr, cr, ccol, ccol], axis=-1)
    sa = jnp.concatenate([-sr, z, -scol, z], axis=-1)
    sb = jnp.concatenate([z, sr, z, scol], axis=-1)
    return cc, sa, sb


def kernel(x, norm_mix, norm_mlp, w_in_even, w_out_even, ret_decay_logit, ret_norm, swa_q_norm,
           swa_k_norm, swa_sink, t5_table, w_in_odd, w_out_odd, ax_q_norm, ax_k_norm, w_mlp_up,
           w_mlp_down):
    b, s, d = x.shape
    t = b * s
    depth = norm_mix.shape[0]
    ret_heads = ret_decay_logit.shape[-1]
    ret_q = ret_heads * RET_DK
    ret_v = ret_heads * RET_DV
    swa_heads = swa_sink.shape[-1]
    swa_q = swa_heads * HEAD_DIM
    swa_kv = SWA_KV_HEADS * HEAD_DIM
    ax_q = w_out_odd.shape[1]
    ax_heads = ax_q // HEAD_DIM
    ax_kv = AX_KV_HEADS * HEAD_DIM

    x2 = x.reshape(t, d)
    ret_tabs = _retention_tables(s)
    ax_tabs = _axial_tables(s)

    for layer in range(depth):
        i = layer // 2
        if layer % 2 == 0:
            proj = norm_proj(x2, norm_mix[layer], w_in_even[i].astype(BF16), tm=DENSE_ROWS)
            proj = proj.reshape(b, s, -1)
            ya = retention(proj, ret_decay_logit[i], ret_norm[i], *ret_tabs, n_heads=ret_heads,
                           q_off=0, k_off=ret_q, v_off=2 * ret_q, g_off=2 * ret_q + ret_v, cs=8)
            off = 2 * ret_q + 2 * ret_v
            yb = window_attention(proj, swa_sink[i], t5_table, swa_q_norm[i], swa_k_norm[i],
                                  n_heads=swa_heads, n_kv=SWA_KV_HEADS, q_off=off,
                                  k_off=off + swa_q, v_off=off + swa_q + swa_kv)
            acts, w_out = [ya.reshape(t, -1), yb.reshape(t, -1)], w_out_even[i]
        else:
            proj = norm_proj(x2, norm_mix[layer], w_in_odd[i].astype(BF16), tm=DENSE_ROWS)
            proj = proj.reshape(b, s, -1)
            qp, kp, vt = axial_prep(proj, ax_tabs, ax_q_norm[i], ax_k_norm[i], n_heads=ax_heads,
                                    n_kv=AX_KV_HEADS, ts=512)
            y = flash_attention(qp, kp, vt, n_heads=ax_heads, n_kv=AX_KV_HEADS, tq=512, tk=512)
            acts, w_out = [y.reshape(t, -1)], w_out_odd[i]
        x2 = out_mlp(x2, acts, w_out.astype(BF16), norm_mlp[layer], w_mlp_up[layer].astype(BF16),
                     w_mlp_down[layer].astype(BF16), tm=DENSE_ROWS)
    return x2.reshape(b, s, d)
```

```python
import functools
import math

import jax
import jax.numpy as jnp
from jax import lax
from jax.experimental import pallas as pl
from jax.experimental.pallas import tpu as pltpu

F32 = jnp.float32
BF16 = jnp.bfloat16

EPS = 1e-6
NEG_INF = -1e30
LOG2E = math.log2(math.e)
HEAD_DIM = 128
BLOCK = 128
GRID_W = 64
RET_DK = 128
RET_DV = 256
RET_CHUNK = 128
RET_THETA = 10000.0
SWA_KV_HEADS = 2
WINDOW = 128
T5_BUCKETS = 32
T5_MAX_DIST = 128
AX_KV_HEADS = 2
AX_THETA = 10000.0

VMEM_LIMIT_BYTES = 56 * 1024 * 1024
DENSE_ROWS = 512


def _params(semantics):
    return pltpu.CompilerParams(dimension_semantics=semantics, vmem_limit_bytes=VMEM_LIMIT_BYTES)


def _rms(x, gain):
    ms = jnp.mean(x * x, axis=-1, keepdims=True)
    return x * lax.rsqrt(ms + EPS) * gain


def _dot(a, b):
    return jnp.dot(a, b, preferred_element_type=F32)


def _dot_nt(a, b):
    return lax.dot_general(a, b, (((1,), (1,)), ((), ())), preferred_element_type=F32)


def _dot_tn(a, b):
    return lax.dot_general(a, b, (((0,), (0,)), ((), ())), preferred_element_type=F32)


def _resident(shape, row=0):
    idx = (row,) + (0,) * (len(shape) - 1)
    return pl.BlockSpec(shape, lambda i: idx, pipeline_mode=pl.Buffered(1))


def _norm_proj_kernel(x_ref, g_ref, w_ref, o_ref):
    h = _rms(x_ref[...], g_ref[...]).astype(BF16)
    o_ref[...] = _dot(h, w_ref[...]).astype(o_ref.dtype)


def norm_proj(x2, gain, w, *, tm):
    t, d = x2.shape
    n = w.shape[1]
    return pl.pallas_call(
        _norm_proj_kernel,
        grid=(t // tm,),
        in_specs=[pl.BlockSpec((tm, d), lambda i: (i, 0)), _resident((1, d)), _resident((d, n))],
        out_specs=pl.BlockSpec((tm, n), lambda i: (i, 0)),
        out_shape=jax.ShapeDtypeStruct((t, n), BF16),
        compiler_params=_params(("parallel",)),
        name="norm_proj",
    )(x2, gain.reshape(1, d), w)


def _out_mlp_kernel(*refs, n_act):
    x_ref = refs[0]
    act_refs = refs[1:1 + n_act]
    wo_refs = refs[1 + n_act:1 + 2 * n_act]
    g_ref, wu_ref, wd_ref, o_ref = refs[1 + 2 * n_act:]
    y = x_ref[...]
    for a_ref, wo_ref in zip(act_refs, wo_refs):
        y = y + _dot(a_ref[...], wo_ref[...])
    h = _rms(y, g_ref[...]).astype(BF16)
    a = jnp.square(jnp.maximum(_dot(h, wu_ref[...]), 0.0)).astype(BF16)
    o_ref[...] = y + _dot(a, wd_ref[...])


def out_mlp(x2, acts, w_out, gain, wu, wd, *, tm):
    t, d = x2.shape
    ff = wu.shape[1]
    k = acts[0].shape[1]
    assert all(a.shape[1] == k for a in acts) and len(acts) * k == w_out.shape[0]

    def row_tile(width):
        return pl.BlockSpec((tm, width), lambda i: (i, 0))

    in_specs = [row_tile(d)] + [row_tile(k) for _ in acts]
    in_specs += [_resident((k, d), row=s) for s in range(len(acts))]
    in_specs += [_resident((1, d)), _resident((d, ff)), _resident((ff, d))]
    return pl.pallas_call(
        functools.partial(_out_mlp_kernel, n_act=len(acts)),
        grid=(t // tm,),
        in_specs=in_specs,
        out_specs=row_tile(d),
        out_shape=jax.ShapeDtypeStruct((t, d), F32),
        compiler_params=_params(("parallel",)),
        name="out_mlp",
    )(x2, *acts, *([w_out] * len(acts)), gain.reshape(1, d), wu, wd)


def _log_sigmoid(x):
    return -(jnp.maximum(-x, 0.0) + jnp.log1p(jnp.exp(-jnp.abs(x))))


def _ret_kernel(dl_ref, q_ref, k_ref, v_ref, g_ref, cos_ref, sin_ref, gn_ref, o_ref,
                sb_ref, sf_ref, cb_ref, dec_ref, *, cs):
    C = RET_CHUNK
    h = pl.program_id(1)
    phase = pl.program_id(2)
    t = pl.program_id(3)
    nsteps = pl.num_programs(3)

    lgf_w = _log_sigmoid(jnp.full((1, RET_DV), dl_ref[0, h], F32))
    lgb_w = _log_sigmoid(jnp.full((1, RET_DV), dl_ref[1, h], F32))

    @pl.when((phase == 0) & (t == 0))
    def _():
        row = lax.broadcasted_iota(jnp.int32, (C, C), 0).astype(F32)
        col = lax.broadcasted_iota(jnp.int32, (C, C), 1).astype(F32)
        lgf = lgf_w[:, :C]
        lgb = lgb_w[:, :C]
        diff = row - col
        dec_ref[0] = jnp.where(diff >= 0, jnp.exp(jnp.maximum(diff, 0.0) * lgf),
                               jnp.exp(jnp.maximum(-diff, 0.0) * lgb))
        dec_ref[1] = jnp.exp((row + 1.0) * lgf)
        dec_ref[2] = jnp.exp((C - 1.0 - row) * lgf)
        dec_ref[3] = jnp.exp((C - row) * lgb)
        dec_ref[4] = jnp.exp(row * lgb)
        sb_ref[...] = jnp.zeros_like(sb_ref)
        sf_ref[...] = jnp.zeros_like(sf_ref)

    def rope(a, rows):
        return a * cos_ref[rows, :] + pltpu.roll(a, RET_DK // 2, 1) * sin_ref[rows, :]

    @pl.when(phase == 0)
    def _():
        sb = sb_ref[...]
        cdec = jnp.exp(C * lgb_w)
        for ci in reversed(range(cs)):
            rows = slice(ci * C, (ci + 1) * C)
            qr = rope(q_ref[0, rows, :].astype(F32), rows)
            kr = rope(k_ref[0, rows, :].astype(F32), rows) * (RET_DK ** -0.5)
            c = (nsteps - 1 - t) * cs + ci
            cb_ref[c] = _dot((qr * dec_ref[3]).astype(BF16), sb.astype(BF16))
            sb = cdec * sb + _dot_tn((kr * dec_ref[4]).astype(BF16), v_ref[0, rows, :])
        sb_ref[...] = sb

    @pl.when(phase == 1)
    def _():
        sf = sf_ref[...]
        cdec = jnp.exp(C * lgf_w)
        for ci in range(cs):
            rows = slice(ci * C, (ci + 1) * C)
            qr = rope(q_ref[0, rows, :].astype(F32), rows)
            kr = rope(k_ref[0, rows, :].astype(F32), rows) * (RET_DK ** -0.5)
            v = v_ref[0, rows, :]
            s = _dot_nt(qr.astype(BF16), kr.astype(BF16))
            inner = _dot((s * dec_ref[0]).astype(BF16), v)
            cross = _dot((qr * dec_ref[1]).astype(BF16), sf.astype(BF16))
            sf = cdec * sf + _dot_tn((kr * dec_ref[2]).astype(BF16), v)
            tot = inner + cross + cb_ref[t * cs + ci]
            y = _rms(tot, gn_ref[...])
            g = g_ref[0, rows, :].astype(F32)
            o_ref[0, rows, :] = (g * jax.nn.sigmoid(g) * y).astype(o_ref.dtype)
        sf_ref[...] = sf


def retention(proj, decay_logit, gn_gain, cos2, sin2, *, n_heads, q_off, k_off, v_off, g_off, cs):
    b, s, _ = proj.shape
    C = RET_CHUNK
    nc = s // C
    ts = cs * C
    nsteps = nc // cs
    ret_v = n_heads * RET_DV

    def step(p, t):
        return jnp.where(p == 0, nsteps - 1 - t, t)

    def spec(width, off):
        base = off // width
        return pl.BlockSpec((1, ts, width), lambda bi, h, p, t: (bi, step(p, t), base + h))

    kern = functools.partial(_ret_kernel, cs=cs)
    return pl.pallas_call(
        kern,
        grid=(b, n_heads, 2, nsteps),
        in_specs=[
            pl.BlockSpec(memory_space=pltpu.SMEM),
            spec(RET_DK, q_off),
            spec(RET_DK, k_off),
            spec(RET_DV, v_off),
            spec(RET_DV, g_off),
            pl.BlockSpec((ts, RET_DK), lambda bi, h, p, t: (step(p, t), 0)),
            pl.BlockSpec((ts, RET_DK), lambda bi, h, p, t: (step(p, t), 0)),
            pl.BlockSpec((1, RET_DV), lambda bi, h, p, t: (0, h)),
        ],
        out_specs=pl.BlockSpec((1, ts, RET_DV), lambda bi, h, p, t: (bi, jnp.where(p == 0, 0, t), h)),
        out_shape=jax.ShapeDtypeStruct((b, s, ret_v), BF16),
        scratch_shapes=[
            pltpu.VMEM((RET_DK, RET_DV), F32),
            pltpu.VMEM((RET_DK, RET_DV), F32),
            pltpu.VMEM((nc, C, RET_DV), F32),
            pltpu.VMEM((5, C, C), F32),
        ],
        compiler_params=_params(("parallel", "parallel", "arbitrary", "arbitrary")),
        name="retention",
    )(decay_logit, proj, proj, proj, proj, cos2, sin2, gn_gain.reshape(1, ret_v))


def _swa_kernel(sink_ref, table_ref, q_ref, kp_ref, kc_ref, kn_ref, vp_ref, vc_ref, vn_ref,
                bucket_ref, qg_ref, kg_ref, o_ref, bias_ref, *, n_heads, n_kv, nb):
    i = pl.program_id(1)
    D = HEAD_DIM
    G = n_heads // n_kv
    KW = 3 * BLOCK
    QW = G * BLOCK
    lane_head = lax.broadcasted_iota(jnp.int32, (1, QW), 1) // BLOCK

    def per_head_row(ref, kv, *idx):
        row = jnp.full((1, QW), ref[(*idx, kv * G)], F32)
        for g in range(1, G):
            row = jnp.where(lane_head == g, ref[(*idx, kv * G + g)], row)
        return row

    @pl.when((pl.program_id(0) == 0) & (i == 0))
    def _():
        bucket = bucket_ref[...]
        j = lax.broadcasted_iota(jnp.int32, (KW, QW), 0)
        r = lax.broadcasted_iota(jnp.int32, (KW, QW), 1) % BLOCK
        in_band = jnp.abs(j - BLOCK - r) <= WINDOW
        for kv in range(n_kv):
            def body(bk, acc):
                return jnp.where(bucket == bk, per_head_row(table_ref, kv, bk), acc)
            bias = lax.fori_loop(0, T5_BUCKETS, body, jnp.zeros((KW, QW), F32))
            bias = jnp.where(in_band, bias * LOG2E, NEG_INF)
            for e in range(4):
                keep = (j >= BLOCK if e & 1 else True) & (j < 2 * BLOCK if e & 2 else True)
                bias_ref[e, kv] = bias if e == 0 else jnp.where(keep, bias, NEG_INF)

    edge = (i == 0).astype(jnp.int32) + 2 * (i == nb - 1).astype(jnp.int32)

    qg = qg_ref[...] * (D ** -0.5 * LOG2E)
    kg = kg_ref[...]
    for kv in range(n_kv):
        sl = slice(kv * D, (kv + 1) * D)
        k3 = jnp.concatenate([kp_ref[0][:, sl], kc_ref[0][:, sl], kn_ref[0][:, sl]], axis=0)
        k3 = _rms(k3.astype(F32), kg).astype(BF16)
        v3 = jnp.concatenate([vp_ref[0][:, sl], vc_ref[0][:, sl], vn_ref[0][:, sl]], axis=0)
        qs = jnp.concatenate(
            [_rms(q_ref[0][:, (kv * G + g) * D:(kv * G + g + 1) * D].astype(F32), qg).astype(BF16)
             for g in range(G)], axis=0)
        st = _dot_nt(k3, qs) + bias_ref[edge, kv]
        sink = per_head_row(sink_ref, kv) * LOG2E
        m = jnp.maximum(jnp.max(st, axis=0, keepdims=True), sink)
        p = jnp.exp2(st - m)
        denom = jnp.sum(p, axis=0, keepdims=True) + jnp.exp2(sink - m)
        o = (_dot_tn(v3, p.astype(BF16)) / denom).T
        for g in range(G):
            h = kv * G + g
            o_ref[0, :, h * D:(h + 1) * D] = o[g * BLOCK:(g + 1) * BLOCK].astype(o_ref.dtype)


def _t5_bucket(rel):
    nb = T5_BUCKETS // 2
    max_exact = nb // 2
    ret = jnp.where(rel > 0, nb, 0)
    n = jnp.abs(rel)
    nf = jnp.maximum(n, 1).astype(jnp.float32)
    large = max_exact + (jnp.log(nf / max_exact) / math.log(T5_MAX_DIST / max_exact)
                         * (nb - max_exact)).astype(jnp.int32)
    large = jnp.minimum(large, nb - 1)
    return ret + jnp.where(n < max_exact, n, large)


def window_attention(proj, sink, t5_table, q_gain, k_gain, *, n_heads, n_kv, q_off, k_off, v_off):
    b, s, _ = proj.shape
    D = HEAD_DIM
    nb = s // BLOCK
    qw = n_heads * D
    kw = n_kv * D
    rr = jnp.arange(BLOCK)
    jj = jnp.arange(3 * BLOCK)
    bucket = _t5_bucket(jj[None, :] - BLOCK - rr[:, None]).astype(jnp.int32)
    G = n_heads // n_kv
    bucket = jnp.tile(bucket.T, (1, G))

    def kv_spec(off, shift):
        base = off // kw
        return pl.BlockSpec((1, BLOCK, kw),
                            lambda bi, i: (bi, jnp.clip(i + shift, 0, nb - 1), base))

    kern = functools.partial(_swa_kernel, n_heads=n_heads, n_kv=n_kv, nb=nb)
    return pl.pallas_call(
        kern,
        grid=(b, nb),
        in_specs=[
            pl.BlockSpec(memory_space=pltpu.SMEM),
            pl.BlockSpec(memory_space=pltpu.SMEM),
            pl.BlockSpec((1, BLOCK, qw), lambda bi, i: (bi, i, q_off // qw)),
            kv_spec(k_off, -1), kv_spec(k_off, 0), kv_spec(k_off, 1),
            kv_spec(v_off, -1), kv_spec(v_off, 0), kv_spec(v_off, 1),
            pl.BlockSpec((3 * BLOCK, G * BLOCK), lambda bi, i: (0, 0)),
            pl.BlockSpec((1, D), lambda bi, i: (0, 0)),
            pl.BlockSpec((1, D), lambda bi, i: (0, 0)),
        ],
        out_specs=pl.BlockSpec((1, BLOCK, qw), lambda bi, i: (bi, i, 0)),
        out_shape=jax.ShapeDtypeStruct((b, s, qw), BF16),
        scratch_shapes=[pltpu.VMEM((4, n_kv, 3 * BLOCK, G * BLOCK), F32)],
        compiler_params=_params(("arbitrary", "arbitrary")),
        name="window_attention",
    )(sink, t5_table, proj, proj, proj, proj, proj, proj, proj, bucket,
      q_gain.reshape(1, D), k_gain.reshape(1, D))


def _axial_head_perm():
    q4 = HEAD_DIM // 4
    return [blk * q4 + j for blk in (0, 2, 1, 3) for j in range(q4)]


def _axial_rope(a, cc, ss):
    return a * cc + pltpu.roll(a, HEAD_DIM // 2, 1) * ss


ONES_ROWS = 16


def _axial_proj_kernel(x_ref, g_ref, w_ref, cc_ref, ss_ref, qg_ref, kg_ref, qo_ref, ko_ref,
                       vt_ref, proj_ref, *, n_heads, n_kv):
    D = HEAD_DIM
    i = pl.program_id(0)

    @pl.when(i == 0)
    def _():
        proj_ref[1] = jnp.zeros(proj_ref.shape[1:], proj_ref.dtype)

    proj = proj_ref[(i + 1) % 2]
    cc, ss = cc_ref[...], ss_ref[...]
    qg = qg_ref[...] * (D ** -0.5 * LOG2E)
    kg = kg_ref[...]
    for hh in range(n_heads):
        hs = slice(hh * D, (hh + 1) * D)
        qo_ref[:, hs] = _axial_rope(_rms(proj[:, hs], qg), cc, ss).astype(qo_ref.dtype)
    k_off, v_off = n_heads * D, (n_heads + n_kv) * D
    for hh in range(n_kv):
        hs = slice(hh * D, (hh + 1) * D)
        k = _rms(proj[:, k_off + hh * D:k_off + (hh + 1) * D], kg)
        ko_ref[:, hs] = _axial_rope(k, cc, ss).astype(ko_ref.dtype)
        vt_ref[0, hh, :D, :] = proj[:, v_off + hh * D:v_off + (hh + 1) * D].T.astype(vt_ref.dtype)
        vt_ref[0, hh, D:, :] = jnp.ones((ONES_ROWS, vt_ref.shape[-1]), vt_ref.dtype)

    h = _rms(x_ref[...], g_ref[...]).astype(BF16)
    proj_ref[i % 2] = _dot(h, w_ref[...])


def axial_proj(x2, gain, w, tables, q_gain, k_gain, *, seq, n_heads, n_kv, tm):
    t, d = x2.shape
    D = HEAD_DIM
    qw, kw = n_heads * D, n_kv * D
    assert w.shape[1] == qw + 2 * kw and seq % tm == 0
    perm = jnp.asarray(_axial_head_perm())
    cols = jnp.concatenate([(hh * D + perm) for hh in range(n_heads + n_kv)]
                           + [jnp.arange(qw + kw, qw + 2 * kw)])
    w = w[:, cols]
    q_gain, k_gain = q_gain[perm], k_gain[perm]
    spb = seq // tm
    n = t // tm
    kern = functools.partial(_axial_proj_kernel, n_heads=n_heads, n_kv=n_kv)

    def done(i):
        return jnp.maximum(i - 1, 0)

    tab = pl.BlockSpec((tm, D), lambda i: (done(i) % spb, 0))
    return pl.pallas_call(
        kern,
        grid=(n + 1,),
        in_specs=[pl.BlockSpec((tm, d), lambda i: (jnp.minimum(i, n - 1), 0)), _resident((1, d)),
                  _resident((d, w.shape[1])), tab, tab, _resident((1, D)), _resident((1, D))],
        out_specs=[
            pl.BlockSpec((tm, qw), lambda i: (done(i), 0)),
            pl.BlockSpec((tm, kw), lambda i: (done(i), 0)),
            pl.BlockSpec((1, n_kv, D + ONES_ROWS, tm), lambda i: (done(i) // spb, 0, 0, done(i) % spb)),
        ],
        out_shape=[jax.ShapeDtypeStruct((t, qw), BF16), jax.ShapeDtypeStruct((t, kw), BF16),
                   jax.ShapeDtypeStruct((t // seq, n_kv, D + ONES_ROWS, seq), BF16)],
        scratch_shapes=[pltpu.VMEM((2, tm, w.shape[1]), F32)],
        compiler_params=_params(("arbitrary",)),
        name="axial_proj",
    )(x2, gain.reshape(1, d), w, *tables, q_gain.reshape(1, D), k_gain.reshape(1, D))


FAST_SUM_MIN = 2.0 ** -80
FAST_SUM_MAX = 2.0 ** 100


def _flash_kernel(q_ref, k_ref, vt_ref, o_ref, acc_ref, st_ref, kn_ref, p_ref, *, G, tq, tk, nk):
    D = HEAD_DIM
    R = G * tq
    qs = jnp.concatenate([q_ref[0][:, g * D:(g + 1) * D] for g in range(G)], axis=0)

    @pl.when(pl.program_id(2) == 0)
    def _():
        def kbody(c, mx):
            start = pl.multiple_of(c * tk, tk)
            kc = k_ref[0, pl.ds(start, tk), :].astype(F32)
            return jnp.maximum(mx, jnp.max(jnp.sum(kc * kc, axis=-1, keepdims=True), axis=0, keepdims=True))
        kn2 = lax.fori_loop(0, nk, kbody, jnp.zeros((1, 1), F32))
        kn_ref[...] = jnp.broadcast_to(kn2, kn_ref.shape)

    def scores(c):
        start = pl.multiple_of(c * tk, tk)
        return _dot_nt(k_ref[0, pl.ds(start, tk), :], qs)

    def vt_chunk(c):
        return vt_ref[0, 0, :, pl.ds(pl.multiple_of(c * tk, tk), tk)]


    def produce_fixed(c, slot, shift):
        p_ref[slot] = jnp.exp2(scores(c) - shift).astype(BF16)

    def consume_fixed(c, slot, shift):
        acc_ref[...] += _dot(vt_chunk(c), p_ref[slot])
        return shift

    def produce_running(c, slot, m_old):
        st_ref[slot] = scores(c)

    def consume_running(c, slot, m_old):
        st = st_ref[slot]
        m_new = jnp.maximum(m_old, jnp.max(st, axis=0, keepdims=True))
        alpha = jnp.exp2(m_old - m_new)
        p = jnp.exp2((st - m_new).astype(BF16))
        acc_ref[...] = alpha * acc_ref[...] + _dot(vt_chunk(c), p)
        return m_new

    def run(produce, consume, init):
        def body(c2, carry):
            c = 2 * c2
            produce(c + 1, 1, carry)
            carry = consume(c, 0, carry)
            produce(c + 2, 0, carry)
            return consume(c + 1, 1, carry)

        acc_ref[...] = jnp.zeros_like(acc_ref)
        produce(0, 0, init)
        carry = lax.fori_loop(0, nk // 2 - 1, body, init)
        produce(nk - 1, 1, carry)
        carry = consume(nk - 2, 0, carry)
        consume(nk - 1, 1, carry)

    def write_out():
        out = (acc_ref[:D, :] / acc_ref[D:D + 1, :]).T
        for g in range(G):
            o_ref[0, :, g * D:(g + 1) * D] = out[g * tq:(g + 1) * tq].astype(o_ref.dtype)

    qf = qs.astype(F32)
    qn2 = _dot_nt(jnp.ones((8, D), BF16), (qf * qf).astype(BF16))[:1]
    run(produce_fixed, consume_fixed, jnp.sqrt(qn2 * kn_ref[:1, :1]))
    sums = acc_ref[D:D + 1, :]
    trusted = (jnp.min(sums) >= FAST_SUM_MIN) & (jnp.max(sums) <= FAST_SUM_MAX)

    @pl.when(jnp.logical_not(trusted))
    def _():
        run(produce_running, consume_running, jnp.full((1, R), -jnp.inf, F32))

    write_out()


def flash_attention(q, k, vt, *, n_heads, n_kv, tq, tk):
    b, s, _ = q.shape
    D = HEAD_DIM
    G = n_heads // n_kv
    assert s % (2 * tk) == 0 and s // tk >= 2
    kern = functools.partial(_flash_kernel, G=G, tq=tq, tk=tk, nk=s // tk)
    return pl.pallas_call(
        kern,
        grid=(b, n_kv, s // tq),
        in_specs=[
            pl.BlockSpec((1, tq, G * D), lambda bi, kv, qi: (bi, qi, kv)),
            pl.BlockSpec((1, s, D), lambda bi, kv, qi: (bi, 0, kv)),
            pl.BlockSpec((1, 1, D + ONES_ROWS, s), lambda bi, kv, qi: (bi, kv, 0, 0)),
        ],
        out_specs=pl.BlockSpec((1, tq, G * D), lambda bi, kv, qi: (bi, qi, kv)),
        out_shape=jax.ShapeDtypeStruct((b, s, n_heads * D), BF16),
        scratch_shapes=[pltpu.VMEM((D + ONES_ROWS, G * tq), F32), pltpu.VMEM((2, tk, G * tq), F32),
                        pltpu.VMEM((8, D), F32), pltpu.VMEM((2, tk, G * tq), BF16)],
        compiler_params=_params(("parallel", "parallel", "arbitrary")),
        name="flash_attention",
    )(q, k, vt)


def _rope_angles(pos, dim, theta):
    inv = theta ** (-jnp.arange(0, dim, 2, dtype=jnp.float32) / dim)
    return pos.astype(jnp.float32)[:, None] * inv[None, :]


def _retention_tables(s):
    ang = _rope_angles(jnp.arange(s), RET_DK, RET_THETA)
    c, sn = jnp.cos(ang), jnp.sin(ang)
    return jnp.concatenate([c, c], axis=-1), jnp.concatenate([-sn, sn], axis=-1)


def _axial_tables(s):
    rows = s // GRID_W
    half = HEAD_DIM // 2
    ar = _rope_angles(jnp.arange(rows), half, AX_THETA)
    ac = _rope_angles(jnp.arange(GRID_W), half, AX_THETA)
    cr, sr = (jnp.repeat(f(ar), GRID_W, axis=0) for f in (jnp.cos, jnp.sin))
    ccol, scol = (jnp.tile(f(ac), (rows, 1)) for f in (jnp.cos, jnp.sin))
    cc = jnp.concatenate([cr, ccol, cr, ccol], axis=-1)
    ss = jnp.concatenate([-sr, -scol, sr, scol], axis=-1)
    return cc, ss


def kernel(x, norm_mix, norm_mlp, w_in_even, w_out_even, ret_decay_logit, ret_norm, swa_q_norm,
           swa_k_norm, swa_sink, t5_table, w_in_odd, w_out_odd, ax_q_norm, ax_k_norm, w_mlp_up,
           w_mlp_down):
    b, s, d = x.shape
    t = b * s
    depth = norm_mix.shape[0]
    ret_heads = ret_decay_logit.shape[-1]
    ret_q = ret_heads * RET_DK
    ret_v = ret_heads * RET_DV
    swa_heads = swa_sink.shape[-1]
    swa_q = swa_heads * HEAD_DIM
    swa_kv = SWA_KV_HEADS * HEAD_DIM
    ax_q = w_out_odd.shape[1]
    ax_heads = ax_q // HEAD_DIM
    ax_kv = AX_KV_HEADS * HEAD_DIM

    x2 = x.reshape(t, d)
    ret_tabs = _retention_tables(s)
    ax_tabs = _axial_tables(s)

    for layer in range(depth):
        i = layer // 2
        if layer % 2 == 0:
            proj = norm_proj(x2, norm_mix[layer], w_in_even[i].astype(BF16), tm=DENSE_ROWS)
            proj = proj.reshape(b, s, -1)
            ya = retention(proj, ret_decay_logit[i], ret_norm[i], *ret_tabs, n_heads=ret_heads,
                           q_off=0, k_off=ret_q, v_off=2 * ret_q, g_off=2 * ret_q + ret_v, cs=8)
            off = 2 * ret_q + 2 * ret_v
            yb = window_attention(proj, swa_sink[i], t5_table, swa_q_norm[i], swa_k_norm[i],
                                  n_heads=swa_heads, n_kv=SWA_KV_HEADS, q_off=off,
                                  k_off=off + swa_q, v_off=off + swa_q + swa_kv)
            acts, w_out = [ya.reshape(t, -1), yb.reshape(t, -1)], w_out_even[i]
        else:
            qp, kp, vt = axial_proj(x2, norm_mix[layer], w_in_odd[i].astype(BF16), ax_tabs,
                                    ax_q_norm[i], ax_k_norm[i], seq=s, n_heads=ax_heads,
                                    n_kv=AX_KV_HEADS, tm=DENSE_ROWS)
            y = flash_attention(qp.reshape(b, s, -1), kp.reshape(b, s, -1), vt, n_heads=ax_heads,
                                n_kv=AX_KV_HEADS, tq=512, tk=512)
            acts, w_out = [y.reshape(t, -1)], w_out_odd[i]
        x2 = out_mlp(x2, acts, w_out.astype(BF16), norm_mlp[layer], w_mlp_up[layer].astype(BF16),
                     w_mlp_down[layer].astype(BF16), tm=DENSE_ROWS)
    return x2.reshape(b, s, d)
```

```python
import functools
import math

import jax
import jax.numpy as jnp
from jax import lax
from jax.experimental import pallas as pl
from jax.experimental.pallas import tpu as pltpu

F32 = jnp.float32
BF16 = jnp.bfloat16

EPS = 1e-6
NEG_INF = -1e30
LOG2E = math.log2(math.e)
HEAD_DIM = 128
BLOCK = 128
GRID_W = 64
RET_DK = 128
RET_DV = 256
RET_CHUNK = 128
RET_THETA = 10000.0
SWA_KV_HEADS = 2
SWA_QBLOCKS = 2
WINDOW = 128
T5_BUCKETS = 32
T5_MAX_DIST = 128
AX_KV_HEADS = 2
AX_THETA = 10000.0

VMEM_LIMIT_BYTES = 56 * 1024 * 1024
DENSE_ROWS = 512


def _params(semantics):
    return pltpu.CompilerParams(dimension_semantics=semantics, vmem_limit_bytes=VMEM_LIMIT_BYTES)


def _rms(x, gain):
    ms = jnp.mean(x * x, axis=-1, keepdims=True)
    return x * lax.rsqrt(ms + EPS) * gain


def _dot(a, b):
    return jnp.dot(a, b, preferred_element_type=F32)


def _dot_nt(a, b):
    return lax.dot_general(a, b, (((1,), (1,)), ((), ())), preferred_element_type=F32)


def _dot_tn(a, b):
    return lax.dot_general(a, b, (((0,), (0,)), ((), ())), preferred_element_type=F32)


def _resident(shape, row=0):
    idx = (row,) + (0,) * (len(shape) - 1)
    return pl.BlockSpec(shape, lambda i: idx, pipeline_mode=pl.Buffered(1))


def _norm_proj_kernel(x_ref, g_ref, w_ref, o_ref):
    h = _rms(x_ref[...], g_ref[...]).astype(BF16)
    o_ref[...] = _dot(h, w_ref[...]).astype(o_ref.dtype)


def norm_proj(x2, gain, w, *, tm):
    t, d = x2.shape
    n = w.shape[1]
    return pl.pallas_call(
        _norm_proj_kernel,
        grid=(t // tm,),
        in_specs=[pl.BlockSpec((tm, d), lambda i: (i, 0)), _resident((1, d)), _resident((d, n))],
        out_specs=pl.BlockSpec((tm, n), lambda i: (i, 0)),
        out_shape=jax.ShapeDtypeStruct((t, n), BF16),
        compiler_params=_params(("parallel",)),
        name="norm_proj",
    )(x2, gain.reshape(1, d), w)


def _out_mlp_kernel(*refs, n_act):
    x_ref = refs[0]
    act_refs = refs[1:1 + n_act]
    wo_refs = refs[1 + n_act:1 + 2 * n_act]
    g_ref, wu_ref, wd_ref, o_ref = refs[1 + 2 * n_act:]
    y = x_ref[...]
    for a_ref, wo_ref in zip(act_refs, wo_refs):
        y = y + _dot(a_ref[...], wo_ref[...])
    h = _rms(y, g_ref[...]).astype(BF16)
    a = jnp.square(jnp.maximum(_dot(h, wu_ref[...]), 0.0)).astype(BF16)
    o_ref[...] = y + _dot(a, wd_ref[...])


def out_mlp(x2, acts, w_out, gain, wu, wd, *, tm):
    t, d = x2.shape
    ff = wu.shape[1]
    k = acts[0].shape[1]
    assert all(a.shape[1] == k for a in acts) and len(acts) * k == w_out.shape[0]

    def row_tile(width):
        return pl.BlockSpec((tm, width), lambda i: (i, 0))

    in_specs = [row_tile(d)] + [row_tile(k) for _ in acts]
    in_specs += [_resident((k, d), row=s) for s in range(len(acts))]
    in_specs += [_resident((1, d)), _resident((d, ff)), _resident((ff, d))]
    return pl.pallas_call(
        functools.partial(_out_mlp_kernel, n_act=len(acts)),
        grid=(t // tm,),
        in_specs=in_specs,
        out_specs=row_tile(d),
        out_shape=jax.ShapeDtypeStruct((t, d), F32),
        compiler_params=_params(("parallel",)),
        name="out_mlp",
    )(x2, *acts, *([w_out] * len(acts)), gain.reshape(1, d), wu, wd)


def _log_sigmoid(x):
    return -(jnp.maximum(-x, 0.0) + jnp.log1p(jnp.exp(-jnp.abs(x))))


def _ret_kernel(dl_ref, q_ref, k_ref, v_ref, g_ref, cos_ref, sin_ref, gn_ref, o_ref,
                sb_ref, sf_ref, cb_ref, dec_ref, *, cs):
    C = RET_CHUNK
    h = pl.program_id(1)
    phase = pl.program_id(2)
    t = pl.program_id(3)
    nsteps = pl.num_programs(3)

    lgf_w = _log_sigmoid(jnp.full((1, RET_DV), dl_ref[0, h], F32))
    lgb_w = _log_sigmoid(jnp.full((1, RET_DV), dl_ref[1, h], F32))

    @pl.when((phase == 0) & (t == 0))
    def _():
        row = lax.broadcasted_iota(jnp.int32, (C, C), 0).astype(F32)
        col = lax.broadcasted_iota(jnp.int32, (C, C), 1).astype(F32)
        lgf = lgf_w[:, :C]
        lgb = lgb_w[:, :C]
        diff = row - col
        dec_ref[0] = jnp.where(diff >= 0, jnp.exp(jnp.maximum(diff, 0.0) * lgf),
                               jnp.exp(jnp.maximum(-diff, 0.0) * lgb))
        dec_ref[1] = jnp.exp((row + 1.0) * lgf)
        dec_ref[2] = jnp.exp((C - 1.0 - row) * lgf)
        dec_ref[3] = jnp.exp((C - row) * lgb)
        dec_ref[4] = jnp.exp(row * lgb)
        sb_ref[...] = jnp.zeros_like(sb_ref)
        sf_ref[...] = jnp.zeros_like(sf_ref)

    def rope(a, rows):
        return a * cos_ref[rows, :] + pltpu.roll(a, RET_DK // 2, 1) * sin_ref[rows, :]

    @pl.when(phase == 0)
    def _():
        sb = sb_ref[...]
        cdec = jnp.exp(C * lgb_w)
        for ci in reversed(range(cs)):
            rows = slice(ci * C, (ci + 1) * C)
            qr = rope(q_ref[0, rows, :].astype(F32), rows)
            kr = rope(k_ref[0, rows, :].astype(F32), rows) * (RET_DK ** -0.5)
            c = (nsteps - 1 - t) * cs + ci
            cb_ref[c] = _dot((qr * dec_ref[3]).astype(BF16), sb.astype(BF16))
            sb = cdec * sb + _dot_tn((kr * dec_ref[4]).astype(BF16), v_ref[0, rows, :])
        sb_ref[...] = sb

    @pl.when(phase == 1)
    def _():
        sf = sf_ref[...]
        cdec = jnp.exp(C * lgf_w)
        for ci in range(cs):
            rows = slice(ci * C, (ci + 1) * C)
            qr = rope(q_ref[0, rows, :].astype(F32), rows)
            kr = rope(k_ref[0, rows, :].astype(F32), rows) * (RET_DK ** -0.5)
            v = v_ref[0, rows, :]
            s = _dot_nt(qr.astype(BF16), kr.astype(BF16))
            inner = _dot((s * dec_ref[0]).astype(BF16), v)
            cross = _dot((qr * dec_ref[1]).astype(BF16), sf.astype(BF16))
            sf = cdec * sf + _dot_tn((kr * dec_ref[2]).astype(BF16), v)
            tot = inner + cross + cb_ref[t * cs + ci]
            y = _rms(tot, gn_ref[...])
            g = g_ref[0, rows, :].astype(F32)
            o_ref[0, rows, :] = (g * jax.nn.sigmoid(g) * y).astype(o_ref.dtype)
        sf_ref[...] = sf


def retention(proj, decay_logit, gn_gain, cos2, sin2, *, n_heads, q_off, k_off, v_off, g_off, cs):
    b, s, _ = proj.shape
    C = RET_CHUNK
    nc = s // C
    ts = cs * C
    nsteps = nc // cs
    ret_v = n_heads * RET_DV

    def step(p, t):
        return jnp.where(p == 0, nsteps - 1 - t, t)

    def spec(width, off):
        base = off // width
        return pl.BlockSpec((1, ts, width), lambda bi, h, p, t: (bi, step(p, t), base + h))

    kern = functools.partial(_ret_kernel, cs=cs)
    return pl.pallas_call(
        kern,
        grid=(b, n_heads, 2, nsteps),
        in_specs=[
            pl.BlockSpec(memory_space=pltpu.SMEM),
            spec(RET_DK, q_off),
            spec(RET_DK, k_off),
            spec(RET_DV, v_off),
            spec(RET_DV, g_off),
            pl.BlockSpec((ts, RET_DK), lambda bi, h, p, t: (step(p, t), 0)),
            pl.BlockSpec((ts, RET_DK), lambda bi, h, p, t: (step(p, t), 0)),
            pl.BlockSpec((1, RET_DV), lambda bi, h, p, t: (0, h)),
        ],
        out_specs=pl.BlockSpec((1, ts, RET_DV), lambda bi, h, p, t: (bi, jnp.where(p == 0, 0, t), h)),
        out_shape=jax.ShapeDtypeStruct((b, s, ret_v), BF16),
        scratch_shapes=[
            pltpu.VMEM((RET_DK, RET_DV), F32),
            pltpu.VMEM((RET_DK, RET_DV), F32),
            pltpu.VMEM((nc, C, RET_DV), F32),
            pltpu.VMEM((5, C, C), F32),
        ],
        compiler_params=_params(("parallel", "parallel", "arbitrary", "arbitrary")),
        name="retention",
    )(decay_logit, proj, proj, proj, proj, cos2, sin2, gn_gain.reshape(1, ret_v))


def _swa_kernel(sink_ref, table_ref, q_ref, kp_ref, km_ref, kn_ref, vp_ref, vm_ref, vn_ref,
                bucket_ref, qg_ref, kg_ref, o_ref, bias_ref, *, n_heads, n_kv, nb):
    step = pl.program_id(1)
    D = HEAD_DIM
    G = n_heads // n_kv
    KW = 3 * BLOCK
    QW = G * BLOCK
    lane_head = lax.broadcasted_iota(jnp.int32, (1, QW), 1) // BLOCK

    def per_head_row(ref, kv, *idx):
        row = jnp.full((1, QW), ref[(*idx, kv * G)], F32)
        for g in range(1, G):
            row = jnp.where(lane_head == g, ref[(*idx, kv * G + g)], row)
        return row

    @pl.when((pl.program_id(0) == 0) & (step == 0))
    def _():
        bucket = bucket_ref[...]
        j = lax.broadcasted_iota(jnp.int32, (KW, QW), 0)
        r = lax.broadcasted_iota(jnp.int32, (KW, QW), 1) % BLOCK
        in_band = jnp.abs(j - BLOCK - r) <= WINDOW
        for kv in range(n_kv):
            def body(bk, acc):
                return jnp.where(bucket == bk, per_head_row(table_ref, kv, bk), acc)
            bias = lax.fori_loop(0, T5_BUCKETS, body, jnp.zeros((KW, QW), F32))
            bias = jnp.where(in_band, bias * LOG2E, NEG_INF)
            for e in range(4):
                keep = (j >= BLOCK if e & 1 else True) & (j < 2 * BLOCK if e & 2 else True)
                bias_ref[e, kv] = bias if e == 0 else jnp.where(keep, bias, NEG_INF)

    qg = qg_ref[...] * (D ** -0.5 * LOG2E)
    kg = kg_ref[...]
    for kv in range(n_kv):
        sl = slice(kv * D, (kv + 1) * D)
        kall = jnp.concatenate([kp_ref[0][:, sl], km_ref[0][:, sl], kn_ref[0][:, sl]], axis=0)
        kall = _rms(kall.astype(F32), kg).astype(BF16)
        vall = jnp.concatenate([vp_ref[0][:, sl], vm_ref[0][:, sl], vn_ref[0][:, sl]], axis=0)
        sink = per_head_row(sink_ref, kv) * LOG2E
        for qb in range(SWA_QBLOCKS):
            i = step * SWA_QBLOCKS + qb
            edge = (i == 0).astype(jnp.int32) + 2 * (i == nb - 1).astype(jnp.int32)
            rows = slice(qb * BLOCK, (qb + 1) * BLOCK)
            k3 = kall[qb * BLOCK:qb * BLOCK + KW]
            v3 = vall[qb * BLOCK:qb * BLOCK + KW]
            qs = jnp.concatenate(
                [_rms(q_ref[0, rows, (kv * G + g) * D:(kv * G + g + 1) * D].astype(F32), qg).astype(BF16)
                 for g in range(G)], axis=0)
            st = _dot_nt(k3, qs) + bias_ref[edge, kv]
            m = jnp.maximum(jnp.max(st, axis=0, keepdims=True), sink)
            p = jnp.exp2(st - m)
            denom = jnp.sum(p, axis=0, keepdims=True) + jnp.exp2(sink - m)
            o = (_dot_tn(v3, p.astype(BF16)) / denom).T
            for g in range(G):
                h = kv * G + g
                o_ref[0, rows, h * D:(h + 1) * D] = o[g * BLOCK:(g + 1) * BLOCK].astype(o_ref.dtype)


def _t5_bucket(rel):
    nb = T5_BUCKETS // 2
    max_exact = nb // 2
    ret = jnp.where(rel > 0, nb, 0)
    n = jnp.abs(rel)
    nf = jnp.maximum(n, 1).astype(jnp.float32)
    large = max_exact + (jnp.log(nf / max_exact) / math.log(T5_MAX_DIST / max_exact)
                         * (nb - max_exact)).astype(jnp.int32)
    large = jnp.minimum(large, nb - 1)
    return ret + jnp.where(n < max_exact, n, large)


def window_attention(proj, sink, t5_table, q_gain, k_gain, *, n_heads, n_kv, q_off, k_off, v_off):
    b, s, _ = proj.shape
    D = HEAD_DIM
    nb = s // BLOCK
    qw = n_heads * D
    kw = n_kv * D
    rr = jnp.arange(BLOCK)
    jj = jnp.arange(3 * BLOCK)
    bucket = _t5_bucket(jj[None, :] - BLOCK - rr[:, None]).astype(jnp.int32)
    G = n_heads // n_kv
    bucket = jnp.tile(bucket.T, (1, G))

    QB = SWA_QBLOCKS
    assert nb % QB == 0

    def kv_specs(off):
        base = off // kw
        return [pl.BlockSpec((1, BLOCK, kw), lambda bi, j: (bi, jnp.maximum(j * QB - 1, 0), base)),
                pl.BlockSpec((1, QB * BLOCK, kw), lambda bi, j: (bi, j, base)),
                pl.BlockSpec((1, BLOCK, kw), lambda bi, j: (bi, jnp.minimum(j * QB + QB, nb - 1), base))]

    kern = functools.partial(_swa_kernel, n_heads=n_heads, n_kv=n_kv, nb=nb)
    return pl.pallas_call(
        kern,
        grid=(b, nb // QB),
        in_specs=[
            pl.BlockSpec(memory_space=pltpu.SMEM),
            pl.BlockSpec(memory_space=pltpu.SMEM),
            pl.BlockSpec((1, QB * BLOCK, qw), lambda bi, j: (bi, j, q_off // qw)),
            *kv_specs(k_off), *kv_specs(v_off),
            pl.BlockSpec((3 * BLOCK, G * BLOCK), lambda bi, j: (0, 0)),
            pl.BlockSpec((1, D), lambda bi, j: (0, 0)),
            pl.BlockSpec((1, D), lambda bi, j: (0, 0)),
        ],
        out_specs=pl.BlockSpec((1, QB * BLOCK, qw), lambda bi, j: (bi, j, 0)),
        out_shape=jax.ShapeDtypeStruct((b, s, qw), BF16),
        scratch_shapes=[pltpu.VMEM((4, n_kv, 3 * BLOCK, G * BLOCK), F32)],
        compiler_params=_params(("arbitrary", "arbitrary")),
        name="window_attention",
    )(sink, t5_table, proj, proj, proj, proj, proj, proj, proj, bucket,
      q_gain.reshape(1, D), k_gain.reshape(1, D))


def _axial_head_perm():
    q4 = HEAD_DIM // 4
    return [blk * q4 + j for blk in (0, 2, 1, 3) for j in range(q4)]


def _axial_rope(a, cc, ss):
    return a * cc + pltpu.roll(a, HEAD_DIM // 2, 1) * ss


ONES_ROWS = 16


def _axial_proj_kernel(x_ref, g_ref, w_ref, cc_ref, ss_ref, qg_ref, kg_ref, qo_ref, ko_ref,
                       vt_ref, proj_ref, *, n_heads, n_kv):
    D = HEAD_DIM
    i = pl.program_id(0)

    @pl.when(i == 0)
    def _():
        proj_ref[1] = jnp.zeros(proj_ref.shape[1:], proj_ref.dtype)

    proj = proj_ref[(i + 1) % 2]
    cc, ss = cc_ref[...], ss_ref[...]
    qg = qg_ref[...] * (D ** -0.5 * LOG2E)
    kg = kg_ref[...]
    for hh in range(n_heads):
        hs = slice(hh * D, (hh + 1) * D)
        qo_ref[:, hs] = _axial_rope(_rms(proj[:, hs], qg), cc, ss).astype(qo_ref.dtype)
    k_off, v_off = n_heads * D, (n_heads + n_kv) * D
    for hh in range(n_kv):
        hs = slice(hh * D, (hh + 1) * D)
        k = _rms(proj[:, k_off + hh * D:k_off + (hh + 1) * D], kg)
        ko_ref[:, hs] = _axial_rope(k, cc, ss).astype(ko_ref.dtype)
        vt_ref[0, hh, :D, :] = proj[:, v_off + hh * D:v_off + (hh + 1) * D].T.astype(vt_ref.dtype)
        vt_ref[0, hh, D:, :] = jnp.ones((ONES_ROWS, vt_ref.shape[-1]), vt_ref.dtype)

    h = _rms(x_ref[...], g_ref[...]).astype(BF16)
    proj_ref[i % 2] = _dot(h, w_ref[...])


def axial_proj(x2, gain, w, tables, q_gain, k_gain, *, seq, n_heads, n_kv, tm):
    t, d = x2.shape
    D = HEAD_DIM
    qw, kw = n_heads * D, n_kv * D
    assert w.shape[1] == qw + 2 * kw and seq % tm == 0
    perm = jnp.asarray(_axial_head_perm())
    cols = jnp.concatenate([(hh * D + perm) for hh in range(n_heads + n_kv)]
                           + [jnp.arange(qw + kw, qw + 2 * kw)])
    w = w[:, cols]
    q_gain, k_gain = q_gain[perm], k_gain[perm]
    spb = seq // tm
    n = t // tm
    kern = functools.partial(_axial_proj_kernel, n_heads=n_heads, n_kv=n_kv)

    def done(i):
        return jnp.maximum(i - 1, 0)

    tab = pl.BlockSpec((tm, D), lambda i: (done(i) % spb, 0))
    return pl.pallas_call(
        kern,
        grid=(n + 1,),
        in_specs=[pl.BlockSpec((tm, d), lambda i: (jnp.minimum(i, n - 1), 0)), _resident((1, d)),
                  _resident((d, w.shape[1])), tab, tab, _resident((1, D)), _resident((1, D))],
        out_specs=[
            pl.BlockSpec((tm, qw), lambda i: (done(i), 0)),
            pl.BlockSpec((tm, kw), lambda i: (done(i), 0)),
            pl.BlockSpec((1, n_kv, D + ONES_ROWS, tm), lambda i: (done(i) // spb, 0, 0, done(i) % spb)),
        ],
        out_shape=[jax.ShapeDtypeStruct((t, qw), BF16), jax.ShapeDtypeStruct((t, kw), BF16),
                   jax.ShapeDtypeStruct((t // seq, n_kv, D + ONES_ROWS, seq), BF16)],
        scratch_shapes=[pltpu.VMEM((2, tm, w.shape[1]), F32)],
        compiler_params=_params(("arbitrary",)),
        name="axial_proj",
    )(x2, gain.reshape(1, d), w, *tables, q_gain.reshape(1, D), k_gain.reshape(1, D))


FAST_SUM_MIN = 2.0 ** -80
FAST_SUM_MAX = 2.0 ** 100


def _flash_kernel(q_ref, k_ref, vt_ref, o_ref, acc_ref, st_ref, kn_ref, p_ref, l_ref, *, G, tq, tk, nk):
    D = HEAD_DIM
    R = G * tq
    qs = jnp.concatenate([q_ref[0][:, g * D:(g + 1) * D] for g in range(G)], axis=0)

    @pl.when(pl.program_id(2) == 0)
    def _():
        def kbody(c, mx):
            start = pl.multiple_of(c * tk, tk)
            kc = k_ref[0, pl.ds(start, tk), :].astype(F32)
            return jnp.maximum(mx, jnp.max(jnp.sum(kc * kc, axis=-1, keepdims=True), axis=0, keepdims=True))
        kn2 = lax.fori_loop(0, nk, kbody, jnp.zeros((1, 1), F32))
        kn_ref[...] = jnp.broadcast_to(kn2, kn_ref.shape)

    def scores(c):
        start = pl.multiple_of(c * tk, tk)
        return _dot_nt(k_ref[0, pl.ds(start, tk), :], qs)

    def vt_chunk(c, rows=D + ONES_ROWS):
        return vt_ref[0, 0, :rows, pl.ds(pl.multiple_of(c * tk, tk), tk)]


    def produce_fixed(c, slot, shift):
        p = jnp.exp2(scores(c) - shift)
        l_ref[...] += jnp.sum(p, axis=0, keepdims=True)
        p_ref[slot] = p.astype(BF16)

    def consume_fixed(c, slot, shift):
        acc_ref[:D, :] += _dot(vt_chunk(c, D), p_ref[slot])
        return shift

    def produce_running(c, slot, m_old):
        st_ref[slot] = scores(c)

    def consume_running(c, slot, m_old):
        st = st_ref[slot]
        m_new = jnp.maximum(m_old, jnp.max(st, axis=0, keepdims=True))
        alpha = jnp.exp2(m_old - m_new)
        p = jnp.exp2((st - m_new).astype(BF16))
        acc_ref[...] = alpha * acc_ref[...] + _dot(vt_chunk(c), p)
        return m_new

    def run(produce, consume, init):
        def body(c2, carry):
            c = 2 * c2
            produce(c + 1, 1, carry)
            carry = consume(c, 0, carry)
            produce(c + 2, 0, carry)
            return consume(c + 1, 1, carry)

        acc_ref[...] = jnp.zeros_like(acc_ref)
        l_ref[...] = jnp.zeros_like(l_ref)
        produce(0, 0, init)
        carry = lax.fori_loop(0, nk // 2 - 1, body, init)
        produce(nk - 1, 1, carry)
        carry = consume(nk - 2, 0, carry)
        consume(nk - 1, 1, carry)

    def write_out(sums):
        out = (acc_ref[:D, :] / sums).T
        for g in range(G):
            o_ref[0, :, g * D:(g + 1) * D] = out[g * tq:(g + 1) * tq].astype(o_ref.dtype)

    qf = qs.astype(F32)
    qn2 = _dot_nt(jnp.ones((8, D), BF16), (qf * qf).astype(BF16))[:1]
    run(produce_fixed, consume_fixed, jnp.sqrt(qn2 * kn_ref[:1, :1]))
    sums = l_ref[...]
    trusted = (jnp.min(sums) >= FAST_SUM_MIN) & (jnp.max(sums) <= FAST_SUM_MAX)

    @pl.when(trusted)
    def _():
        write_out(sums)

    @pl.when(jnp.logical_not(trusted))
    def _():
        run(produce_running, consume_running, jnp.full((1, R), -jnp.inf, F32))
        write_out(acc_ref[D:D + 1, :])


def flash_attention(q, k, vt, *, n_heads, n_kv, tq, tk):
    b, s, _ = q.shape
    D = HEAD_DIM
    G = n_heads // n_kv
    assert s % (2 * tk) == 0 and s // tk >= 2
    kern = functools.partial(_flash_kernel, G=G, tq=tq, tk=tk, nk=s // tk)
    return pl.pallas_call(
        kern,
        grid=(b, n_kv, s // tq),
        in_specs=[
            pl.BlockSpec((1, tq, G * D), lambda bi, kv, qi: (bi, qi, kv)),
            pl.BlockSpec((1, s, D), lambda bi, kv, qi: (bi, 0, kv)),
            pl.BlockSpec((1, 1, D + ONES_ROWS, s), lambda bi, kv, qi: (bi, kv, 0, 0)),
        ],
        out_specs=pl.BlockSpec((1, tq, G * D), lambda bi, kv, qi: (bi, qi, kv)),
        out_shape=jax.ShapeDtypeStruct((b, s, n_heads * D), BF16),
        scratch_shapes=[pltpu.VMEM((D + ONES_ROWS, G * tq), F32), pltpu.VMEM((2, tk, G * tq), F32),
                        pltpu.VMEM((8, D), F32), pltpu.VMEM((2, tk, G * tq), BF16),
                        pltpu.VMEM((1, G * tq), F32)],
        compiler_params=_params(("parallel", "parallel", "arbitrary")),
        name="flash_attention",
    )(q, k, vt)


def _rope_angles(pos, dim, theta):
    inv = theta ** (-jnp.arange(0, dim, 2, dtype=jnp.float32) / dim)
    return pos.astype(jnp.float32)[:, None] * inv[None, :]


def _retention_tables(s):
    ang = _rope_angles(jnp.arange(s), RET_DK, RET_THETA)
    c, sn = jnp.cos(ang), jnp.sin(ang)
    return jnp.concatenate([c, c], axis=-1), jnp.concatenate([-sn, sn], axis=-1)


def _axial_tables(s):
    rows = s // GRID_W
    half = HEAD_DIM // 2
    ar = _rope_angles(jnp.arange(rows), half, AX_THETA)
    ac = _rope_angles(jnp.arange(GRID_W), half, AX_THETA)
    cr, sr = (jnp.repeat(f(ar), GRID_W, axis=0) for f in (jnp.cos, jnp.sin))
    ccol, scol = (jnp.tile(f(ac), (rows, 1)) for f in (jnp.cos, jnp.sin))
    cc = jnp.concatenate([cr, ccol, cr, ccol], axis=-1)
    ss = jnp.concatenate([-sr, -scol, sr, scol], axis=-1)
    return cc, ss


def kernel(x, norm_mix, norm_mlp, w_in_even, w_out_even, ret_decay_logit, ret_norm, swa_q_norm,
           swa_k_norm, swa_sink, t5_table, w_in_odd, w_out_odd, ax_q_norm, ax_k_norm, w_mlp_up,
           w_mlp_down):
    b, s, d = x.shape
    t = b * s
    depth = norm_mix.shape[0]
    ret_heads = ret_decay_logit.shape[-1]
    ret_q = ret_heads * RET_DK
    ret_v = ret_heads * RET_DV
    swa_heads = swa_sink.shape[-1]
    swa_q = swa_heads * HEAD_DIM
    swa_kv = SWA_KV_HEADS * HEAD_DIM
    ax_q = w_out_odd.shape[1]
    ax_heads = ax_q // HEAD_DIM
    ax_kv = AX_KV_HEADS * HEAD_DIM

    x2 = x.reshape(t, d)
    ret_tabs = _retention_tables(s)
    ax_tabs = _axial_tables(s)

    for layer in range(depth):
        i = layer // 2
        if layer % 2 == 0:
            proj = norm_proj(x2, norm_mix[layer], w_in_even[i].astype(BF16), tm=DENSE_ROWS)
            proj = proj.reshape(b, s, -1)
            ya = retention(proj, ret_decay_logit[i], ret_norm[i], *ret_tabs, n_heads=ret_heads,
                           q_off=0, k_off=ret_q, v_off=2 * ret_q, g_off=2 * ret_q + ret_v, cs=16)
            off = 2 * ret_q + 2 * ret_v
            yb = window_attention(proj, swa_sink[i], t5_table, swa_q_norm[i], swa_k_norm[i],
                                  n_heads=swa_heads, n_kv=SWA_KV_HEADS, q_off=off,
                                  k_off=off + swa_q, v_off=off + swa_q + swa_kv)
            acts, w_out = [ya.reshape(t, -1), yb.reshape(t, -1)], w_out_even[i]
        else:
            qp, kp, vt = axial_proj(x2, norm_mix[layer], w_in_odd[i].astype(BF16), ax_tabs,
                                    ax_q_norm[i], ax_k_norm[i], seq=s, n_heads=ax_heads,
                                    n_kv=AX_KV_HEADS, tm=DENSE_ROWS)
            y = flash_attention(qp.reshape(b, s, -1), kp.reshape(b, s, -1), vt, n_heads=ax_heads,
                                n_kv=AX_KV_HEADS, tq=512, tk=512)
            acts, w_out = [y.reshape(t, -1)], w_out_odd[i]
        x2 = out_mlp(x2, acts, w_out.astype(BF16), norm_mlp[layer], w_mlp_up[layer].astype(BF16),
                     w_mlp_down[layer].astype(BF16), tm=DENSE_ROWS)
    return x2.reshape(b, s, d)
```

```python
import functools
import math

import jax
import jax.numpy as jnp
from jax import lax
from jax.experimental import pallas as pl
from jax.experimental.pallas import tpu as pltpu

F32 = jnp.float32
BF16 = jnp.bfloat16

EPS = 1e-6
NEG_INF = -1e30
LOG2E = math.log2(math.e)
HEAD_DIM = 128
BLOCK = 128
GRID_W = 64
RET_DK = 128
RET_DV = 256
RET_CHUNK = 128
RET_THETA = 10000.0
SWA_KV_HEADS = 2
SWA_QBLOCKS = 2
WINDOW = 128
T5_BUCKETS = 32
T5_MAX_DIST = 128
AX_KV_HEADS = 2
AX_THETA = 10000.0

VMEM_LIMIT_BYTES = 56 * 1024 * 1024
DENSE_ROWS = 512


def _params(semantics):
    return pltpu.CompilerParams(dimension_semantics=semantics, vmem_limit_bytes=VMEM_LIMIT_BYTES)


def _rms(x, gain):
    ms = jnp.mean(x * x, axis=-1, keepdims=True)
    return x * lax.rsqrt(ms + EPS) * gain


def _dot(a, b):
    return jnp.dot(a, b, preferred_element_type=F32)


def _dot_nt(a, b):
    return lax.dot_general(a, b, (((1,), (1,)), ((), ())), preferred_element_type=F32)


def _dot_tn(a, b):
    return lax.dot_general(a, b, (((0,), (0,)), ((), ())), preferred_element_type=F32)


def _resident(shape, row=0):
    idx = (row,) + (0,) * (len(shape) - 1)
    return pl.BlockSpec(shape, lambda i: idx, pipeline_mode=pl.Buffered(1))


def _norm_proj_kernel(x_ref, g_ref, w_ref, o_ref):
    h = _rms(x_ref[...], g_ref[...]).astype(BF16)
    o_ref[...] = _dot(h, w_ref[...]).astype(o_ref.dtype)


def norm_proj(x2, gain, w, *, tm):
    t, d = x2.shape
    n = w.shape[1]
    return pl.pallas_call(
        _norm_proj_kernel,
        grid=(t // tm,),
        in_specs=[pl.BlockSpec((tm, d), lambda i: (i, 0)), _resident((1, d)), _resident((d, n))],
        out_specs=pl.BlockSpec((tm, n), lambda i: (i, 0)),
        out_shape=jax.ShapeDtypeStruct((t, n), BF16),
        compiler_params=_params(("parallel",)),
        name="norm_proj",
    )(x2, gain.reshape(1, d), w)


def _out_mlp_kernel(*refs, n_act):
    x_ref = refs[0]
    act_refs = refs[1:1 + n_act]
    wo_refs = refs[1 + n_act:1 + 2 * n_act]
    g_ref, wu_ref, wd_ref, o_ref = refs[1 + 2 * n_act:]
    y = x_ref[...]
    for a_ref, wo_ref in zip(act_refs, wo_refs):
        y = y + _dot(a_ref[...], wo_ref[...])
    h = _rms(y, g_ref[...]).astype(BF16)
    a = jnp.square(jnp.maximum(_dot(h, wu_ref[...]), 0.0)).astype(BF16)
    o_ref[...] = y + _dot(a, wd_ref[...])


def out_mlp(x2, acts, w_out, gain, wu, wd, *, tm):
    t, d = x2.shape
    ff = wu.shape[1]
    k = acts[0].shape[1]
    assert all(a.shape[1] == k for a in acts) and len(acts) * k == w_out.shape[0]

    def row_tile(width):
        return pl.BlockSpec((tm, width), lambda i: (i, 0))

    in_specs = [row_tile(d)] + [row_tile(k) for _ in acts]
    in_specs += [_resident((k, d), row=s) for s in range(len(acts))]
    in_specs += [_resident((1, d)), _resident((d, ff)), _resident((ff, d))]
    return pl.pallas_call(
        functools.partial(_out_mlp_kernel, n_act=len(acts)),
        grid=(t // tm,),
        in_specs=in_specs,
        out_specs=row_tile(d),
        out_shape=jax.ShapeDtypeStruct((t, d), F32),
        compiler_params=_params(("parallel",)),
        name="out_mlp",
    )(x2, *acts, *([w_out] * len(acts)), gain.reshape(1, d), wu, wd)


def _log_sigmoid(x):
    return -(jnp.maximum(-x, 0.0) + jnp.log1p(jnp.exp(-jnp.abs(x))))


def _ret_kernel(dl_ref, q_ref, k_ref, v_ref, g_ref, cos_ref, sin_ref, gn_ref, o_ref,
                sb_ref, sf_ref, cb_ref, dec_ref, *, cs):
    C = RET_CHUNK
    h = pl.program_id(1)
    phase = pl.program_id(2)
    t = pl.program_id(3)
    nsteps = pl.num_programs(3)

    lgf_w = _log_sigmoid(jnp.full((1, RET_DV), dl_ref[0, h], F32))
    lgb_w = _log_sigmoid(jnp.full((1, RET_DV), dl_ref[1, h], F32))

    @pl.when((phase == 0) & (t == 0))
    def _():
        row = lax.broadcasted_iota(jnp.int32, (C, C), 0).astype(F32)
        col = lax.broadcasted_iota(jnp.int32, (C, C), 1).astype(F32)
        lgf = lgf_w[:, :C]
        lgb = lgb_w[:, :C]
        diff = row - col
        dec_ref[0] = jnp.where(diff >= 0, jnp.exp(jnp.maximum(diff, 0.0) * lgf),
                               jnp.exp(jnp.maximum(-diff, 0.0) * lgb))
        dec_ref[1] = jnp.exp((row + 1.0) * lgf)
        dec_ref[2] = jnp.exp((C - 1.0 - row) * lgf)
        dec_ref[3] = jnp.exp((C - row) * lgb)
        dec_ref[4] = jnp.exp(row * lgb)
        sb_ref[...] = jnp.zeros_like(sb_ref)
        sf_ref[...] = jnp.zeros_like(sf_ref)

    def rope(a, rows):
        return a * cos_ref[rows, :] + pltpu.roll(a, RET_DK // 2, 1) * sin_ref[rows, :]

    @pl.when(phase == 0)
    def _():
        sb = sb_ref[...]
        cdec = jnp.exp(C * lgb_w)
        for ci in reversed(range(cs)):
            rows = slice(ci * C, (ci + 1) * C)
            qr = rope(q_ref[0, rows, :].astype(F32), rows)
            kr = rope(k_ref[0, rows, :].astype(F32), rows) * (RET_DK ** -0.5)
            c = (nsteps - 1 - t) * cs + ci
            cb_ref[c] = _dot((qr * dec_ref[3]).astype(BF16), sb.astype(BF16))
            sb = cdec * sb + _dot_tn((kr * dec_ref[4]).astype(BF16), v_ref[0, rows, :])
        sb_ref[...] = sb

    @pl.when(phase == 1)
    def _():
        sf = sf_ref[...]
        cdec = jnp.exp(C * lgf_w)
        for ci in range(cs):
            rows = slice(ci * C, (ci + 1) * C)
            qr = rope(q_ref[0, rows, :].astype(F32), rows)
            kr = rope(k_ref[0, rows, :].astype(F32), rows) * (RET_DK ** -0.5)
            v = v_ref[0, rows, :]
            s = _dot_nt(qr.astype(BF16), kr.astype(BF16))
            inner = _dot((s * dec_ref[0]).astype(BF16), v)
            cross = _dot((qr * dec_ref[1]).astype(BF16), sf.astype(BF16))
            sf = cdec * sf + _dot_tn((kr * dec_ref[2]).astype(BF16), v)
            tot = inner + cross + cb_ref[t * cs + ci]
            y = _rms(tot, gn_ref[...])
            g = g_ref[0, rows, :].astype(F32)
            o_ref[0, rows, :] = (g * jax.nn.sigmoid(g) * y).astype(o_ref.dtype)
        sf_ref[...] = sf


def retention(proj, decay_logit, gn_gain, cos2, sin2, *, n_heads, q_off, k_off, v_off, g_off, cs):
    b, s, _ = proj.shape
    C = RET_CHUNK
    nc = s // C
    ts = cs * C
    nsteps = nc // cs
    ret_v = n_heads * RET_DV

    def step(p, t):
        return jnp.where(p == 0, nsteps - 1 - t, t)

    def spec(width, off):
        base = off // width
        return pl.BlockSpec((1, ts, width), lambda bi, h, p, t: (bi, step(p, t), base + h))

    kern = functools.partial(_ret_kernel, cs=cs)
    return pl.pallas_call(
        kern,
        grid=(b, n_heads, 2, nsteps),
        in_specs=[
            pl.BlockSpec(memory_space=pltpu.SMEM),
            spec(RET_DK, q_off),
            spec(RET_DK, k_off),
            spec(RET_DV, v_off),
            spec(RET_DV, g_off),
            pl.BlockSpec((ts, RET_DK), lambda bi, h, p, t: (step(p, t), 0)),
            pl.BlockSpec((ts, RET_DK), lambda bi, h, p, t: (step(p, t), 0)),
            pl.BlockSpec((1, RET_DV), lambda bi, h, p, t: (0, h)),
        ],
        out_specs=pl.BlockSpec((1, ts, RET_DV), lambda bi, h, p, t: (bi, jnp.where(p == 0, 0, t), h)),
        out_shape=jax.ShapeDtypeStruct((b, s, ret_v), BF16),
        scratch_shapes=[
            pltpu.VMEM((RET_DK, RET_DV), F32),
            pltpu.VMEM((RET_DK, RET_DV), F32),
            pltpu.VMEM((nc, C, RET_DV), F32),
            pltpu.VMEM((5, C, C), F32),
        ],
        compiler_params=_params(("parallel", "parallel", "arbitrary", "arbitrary")),
        name="retention",
    )(decay_logit, proj, proj, proj, proj, cos2, sin2, gn_gain.reshape(1, ret_v))


def _swa_kernel(sink_ref, table_ref, q_ref, kp_ref, km_ref, kn_ref, vp_ref, vm_ref, vn_ref,
                bucket_ref, qg_ref, kg_ref, o_ref, bias_ref, *, n_heads, n_kv, nb):
    step = pl.program_id(1)
    D = HEAD_DIM
    G = n_heads // n_kv
    KW = 3 * BLOCK
    QW = G * BLOCK
    lane_head = lax.broadcasted_iota(jnp.int32, (1, QW), 1) // BLOCK

    def per_head_row(ref, kv, *idx):
        row = jnp.full((1, QW), ref[(*idx, kv * G)], F32)
        for g in range(1, G):
            row = jnp.where(lane_head == g, ref[(*idx, kv * G + g)], row)
        return row

    @pl.when((pl.program_id(0) == 0) & (step == 0))
    def _():
        bucket = bucket_ref[...]
        j = lax.broadcasted_iota(jnp.int32, (KW, QW), 0)
        r = lax.broadcasted_iota(jnp.int32, (KW, QW), 1) % BLOCK
        in_band = jnp.abs(j - BLOCK - r) <= WINDOW
        for kv in range(n_kv):
            def body(bk, acc):
                return jnp.where(bucket == bk, per_head_row(table_ref, kv, bk), acc)
            bias = lax.fori_loop(0, T5_BUCKETS, body, jnp.zeros((KW, QW), F32))
            bias = jnp.where(in_band, bias * LOG2E, NEG_INF)
            for e in range(4):
                keep = (j >= BLOCK if e & 1 else True) & (j < 2 * BLOCK if e & 2 else True)
                bias_ref[e, kv] = bias if e == 0 else jnp.where(keep, bias, NEG_INF)

    qg = qg_ref[...] * (D ** -0.5 * LOG2E)
    kg = kg_ref[...]
    for kv in range(n_kv):
        sl = slice(kv * D, (kv + 1) * D)
        kall = jnp.concatenate([kp_ref[0][:, sl], km_ref[0][:, sl], kn_ref[0][:, sl]], axis=0)
        kall = _rms(kall.astype(F32), kg).astype(BF16)
        vall = jnp.concatenate([vp_ref[0][:, sl], vm_ref[0][:, sl], vn_ref[0][:, sl]], axis=0)
        sink = per_head_row(sink_ref, kv) * LOG2E
        for qb in range(SWA_QBLOCKS):
            i = step * SWA_QBLOCKS + qb
            edge = (i == 0).astype(jnp.int32) + 2 * (i == nb - 1).astype(jnp.int32)
            rows = slice(qb * BLOCK, (qb + 1) * BLOCK)
            k3 = kall[qb * BLOCK:qb * BLOCK + KW]
            v3 = vall[qb * BLOCK:qb * BLOCK + KW]
            qs = jnp.concatenate(
                [_rms(q_ref[0, rows, (kv * G + g) * D:(kv * G + g + 1) * D].astype(F32), qg).astype(BF16)
                 for g in range(G)], axis=0)
            st = _dot_nt(k3, qs) + bias_ref[edge, kv]
            m = jnp.maximum(jnp.max(st, axis=0, keepdims=True), sink)
            p = jnp.exp2(st - m)
            denom = jnp.sum(p, axis=0, keepdims=True) + jnp.exp2(sink - m)
            o = (_dot_tn(v3, p.astype(BF16)) / denom).T
            for g in range(G):
                h = kv * G + g
                o_ref[0, rows, h * D:(h + 1) * D] = o[g * BLOCK:(g + 1) * BLOCK].astype(o_ref.dtype)


def _t5_bucket(rel):
    nb = T5_BUCKETS // 2
    max_exact = nb // 2
    ret = jnp.where(rel > 0, nb, 0)
    n = jnp.abs(rel)
    nf = jnp.maximum(n, 1).astype(jnp.float32)
    large = max_exact + (jnp.log(nf / max_exact) / math.log(T5_MAX_DIST / max_exact)
                         * (nb - max_exact)).astype(jnp.int32)
    large = jnp.minimum(large, nb - 1)
    return ret + jnp.where(n < max_exact, n, large)


def window_attention(proj, sink, t5_table, q_gain, k_gain, *, n_heads, n_kv, q_off, k_off, v_off):
    b, s, _ = proj.shape
    D = HEAD_DIM
    nb = s // BLOCK
    qw = n_heads * D
    kw = n_kv * D
    rr = jnp.arange(BLOCK)
    jj = jnp.arange(3 * BLOCK)
    bucket = _t5_bucket(jj[None, :] - BLOCK - rr[:, None]).astype(jnp.int32)
    G = n_heads // n_kv
    bucket = jnp.tile(bucket.T, (1, G))

    QB = SWA_QBLOCKS
    assert nb % QB == 0

    def kv_specs(off):
        base = off // kw
        return [pl.BlockSpec((1, BLOCK, kw), lambda bi, j: (bi, jnp.maximum(j * QB - 1, 0), base)),
                pl.BlockSpec((1, QB * BLOCK, kw), lambda bi, j: (bi, j, base)),
                pl.BlockSpec((1, BLOCK, kw), lambda bi, j: (bi, jnp.minimum(j * QB + QB, nb - 1), base))]

    kern = functools.partial(_swa_kernel, n_heads=n_heads, n_kv=n_kv, nb=nb)
    return pl.pallas_call(
        kern,
        grid=(b, nb // QB),
        in_specs=[
            pl.BlockSpec(memory_space=pltpu.SMEM),
            pl.BlockSpec(memory_space=pltpu.SMEM),
            pl.BlockSpec((1, QB * BLOCK, qw), lambda bi, j: (bi, j, q_off // qw)),
            *kv_specs(k_off), *kv_specs(v_off),
            pl.BlockSpec((3 * BLOCK, G * BLOCK), lambda bi, j: (0, 0)),
            pl.BlockSpec((1, D), lambda bi, j: (0, 0)),
            pl.BlockSpec((1, D), lambda bi, j: (0, 0)),
        ],
        out_specs=pl.BlockSpec((1, QB * BLOCK, qw), lambda bi, j: (bi, j, 0)),
        out_shape=jax.ShapeDtypeStruct((b, s, qw), BF16),
        scratch_shapes=[pltpu.VMEM((4, n_kv, 3 * BLOCK, G * BLOCK), F32)],
        compiler_params=_params(("arbitrary", "arbitrary")),
        name="window_attention",
    )(sink, t5_table, proj, proj, proj, proj, proj, proj, proj, bucket,
      q_gain.reshape(1, D), k_gain.reshape(1, D))


def _axial_head_perm():
    q4 = HEAD_DIM // 4
    return [blk * q4 + j for blk in (0, 2, 1, 3) for j in range(q4)]


def _axial_rope(a, cc, ss):
    return a * cc + pltpu.roll(a, HEAD_DIM // 2, 1) * ss


ONES_ROWS = 16


def _axial_proj_kernel(x_ref, g_ref, w_ref, cc_ref, ss_ref, qg_ref, kg_ref, qo_ref, ko_ref,
                       vt_ref, proj_ref, *, n_heads, n_kv):
    D = HEAD_DIM
    i = pl.program_id(0)

    @pl.when(i == 0)
    def _():
        proj_ref[1] = jnp.zeros(proj_ref.shape[1:], proj_ref.dtype)

    proj = proj_ref[(i + 1) % 2]
    cc, ss = cc_ref[...], ss_ref[...]
    qg = qg_ref[...] * (D ** -0.5 * LOG2E)
    kg = kg_ref[...]
    for hh in range(n_heads):
        hs = slice(hh * D, (hh + 1) * D)
        qo_ref[:, hs] = _axial_rope(_rms(proj[:, hs], qg), cc, ss).astype(qo_ref.dtype)
    k_off, v_off = n_heads * D, (n_heads + n_kv) * D
    for hh in range(n_kv):
        hs = slice(hh * D, (hh + 1) * D)
        k = _rms(proj[:, k_off + hh * D:k_off + (hh + 1) * D], kg)
        ko_ref[:, hs] = _axial_rope(k, cc, ss).astype(ko_ref.dtype)
        vt_ref[0, hh, :D, :] = proj[:, v_off + hh * D:v_off + (hh + 1) * D].T.astype(vt_ref.dtype)
        vt_ref[0, hh, D:, :] = jnp.ones((ONES_ROWS, vt_ref.shape[-1]), vt_ref.dtype)

    h = _rms(x_ref[...], g_ref[...]).astype(BF16)
    proj_ref[i % 2] = _dot(h, w_ref[...])


def axial_proj(x2, gain, w, tables, q_gain, k_gain, *, seq, n_heads, n_kv, tm):
    t, d = x2.shape
    D = HEAD_DIM
    qw, kw = n_heads * D, n_kv * D
    assert w.shape[1] == qw + 2 * kw and seq % tm == 0
    perm = jnp.asarray(_axial_head_perm())
    cols = jnp.concatenate([(hh * D + perm) for hh in range(n_heads + n_kv)]
                           + [jnp.arange(qw + kw, qw + 2 * kw)])
    w = w[:, cols]
    q_gain, k_gain = q_gain[perm], k_gain[perm]
    spb = seq // tm
    n = t // tm
    kern = functools.partial(_axial_proj_kernel, n_heads=n_heads, n_kv=n_kv)

    def done(i):
        return jnp.maximum(i - 1, 0)

    tab = pl.BlockSpec((tm, D), lambda i: (done(i) % spb, 0))
    return pl.pallas_call(
        kern,
        grid=(n + 1,),
        in_specs=[pl.BlockSpec((tm, d), lambda i: (jnp.minimum(i, n - 1), 0)), _resident((1, d)),
                  _resident((d, w.shape[1])), tab, tab, _resident((1, D)), _resident((1, D))],
        out_specs=[
            pl.BlockSpec((tm, qw), lambda i: (done(i), 0)),
            pl.BlockSpec((tm, kw), lambda i: (done(i), 0)),
            pl.BlockSpec((1, n_kv, D + ONES_ROWS, tm), lambda i: (done(i) // spb, 0, 0, done(i) % spb)),
        ],
        out_shape=[jax.ShapeDtypeStruct((t, qw), BF16), jax.ShapeDtypeStruct((t, kw), BF16),
                   jax.ShapeDtypeStruct((t // seq, n_kv, D + ONES_ROWS, seq), BF16)],
        scratch_shapes=[pltpu.VMEM((2, tm, w.shape[1]), F32)],
        compiler_params=_params(("arbitrary",)),
        name="axial_proj",
    )(x2, gain.reshape(1, d), w, *tables, q_gain.reshape(1, D), k_gain.reshape(1, D))


FAST_SUM_MIN = 2.0 ** -80
FAST_SUM_MAX = 2.0 ** 100


def _flash_kernel(q_ref, k_ref, vt_ref, o_ref, acc_ref, kn_ref, p_ref, l_ref, *, G, tq, tk, nk):
    D = HEAD_DIM
    R = G * tq
    qs = jnp.concatenate([q_ref[0][:, g * D:(g + 1) * D] for g in range(G)], axis=0)

    @pl.when(pl.program_id(2) == 0)
    def _():
        def kbody(c, mx):
            start = pl.multiple_of(c * tk, tk)
            kc = k_ref[0, pl.ds(start, tk), :].astype(F32)
            return jnp.maximum(mx, jnp.max(jnp.sum(kc * kc, axis=-1, keepdims=True), axis=0, keepdims=True))
        kn2 = lax.fori_loop(0, nk, kbody, jnp.zeros((1, 1), F32))
        kn_ref[...] = jnp.broadcast_to(kn2, kn_ref.shape)

    def scores(c):
        start = pl.multiple_of(c * tk, tk)
        return _dot_nt(k_ref[0, pl.ds(start, tk), :], qs)

    def vt_chunk(c, rows=D + ONES_ROWS):
        return vt_ref[0, 0, :rows, pl.ds(pl.multiple_of(c * tk, tk), tk)]


    def produce_fixed(c, slot, shift):
        p = jnp.exp2(scores(c) - shift)
        l_ref[...] += jnp.sum(p, axis=0, keepdims=True)
        p_ref[slot] = p.astype(BF16)

    def consume_fixed(c, slot, shift):
        acc_ref[:D, :] += _dot(vt_chunk(c, D), p_ref[slot])

    def first_pass(shift):
        def body(c2, carry):
            c = 2 * c2
            produce_fixed(c + 1, 1, shift)
            consume_fixed(c, 0, shift)
            produce_fixed(c + 2, 0, shift)
            consume_fixed(c + 1, 1, shift)
            return carry

        acc_ref[...] = jnp.zeros_like(acc_ref)
        l_ref[...] = jnp.zeros_like(l_ref)
        produce_fixed(0, 0, shift)
        lax.fori_loop(0, nk // 2 - 1, body, 0)
        produce_fixed(nk - 1, 1, shift)
        consume_fixed(nk - 2, 0, shift)
        consume_fixed(nk - 1, 1, shift)

    def running_max_pass():
        def body(c, m_old):
            st = scores(c)
            m_new = jnp.maximum(m_old, jnp.max(st, axis=0, keepdims=True))
            alpha = jnp.exp2(m_old - m_new)
            p = jnp.exp2((st - m_new).astype(BF16))
            acc_ref[...] = alpha * acc_ref[...] + _dot(vt_chunk(c), p)
            return m_new

        acc_ref[...] = jnp.zeros_like(acc_ref)
        lax.fori_loop(0, nk, body, jnp.full((1, R), -jnp.inf, F32))

    def write_out(sums):
        out = (acc_ref[:D, :] / sums).T
        for g in range(G):
            o_ref[0, :, g * D:(g + 1) * D] = out[g * tq:(g + 1) * tq].astype(o_ref.dtype)

    qf = qs.astype(F32)
    qn2 = _dot_nt(jnp.ones((8, D), BF16), (qf * qf).astype(BF16))[:1]
    first_pass(jnp.sqrt(qn2 * kn_ref[:1, :1]))
    sums = l_ref[...]
    trusted = (jnp.min(sums) >= FAST_SUM_MIN) & (jnp.max(sums) <= FAST_SUM_MAX)

    @pl.when(trusted)
    def _():
        write_out(sums)

    @pl.when(jnp.logical_not(trusted))
    def _():
        running_max_pass()
        write_out(acc_ref[D:D + 1, :])


def flash_attention(q, k, vt, *, n_heads, n_kv, tq, tk):
    b, s, _ = q.shape
    D = HEAD_DIM
    G = n_heads // n_kv
    assert s % (2 * tk) == 0 and s // tk >= 2
    kern = functools.partial(_flash_kernel, G=G, tq=tq, tk=tk, nk=s // tk)
    return pl.pallas_call(
        kern,
        grid=(b, n_kv, s // tq),
        in_specs=[
            pl.BlockSpec((1, tq, G * D), lambda bi, kv, qi: (bi, qi, kv)),
            pl.BlockSpec((1, s, D), lambda bi, kv, qi: (bi, 0, kv)),
            pl.BlockSpec((1, 1, D + ONES_ROWS, s), lambda bi, kv, qi: (bi, kv, 0, 0)),
        ],
        out_specs=pl.BlockSpec((1, tq, G * D), lambda bi, kv, qi: (bi, qi, kv)),
        out_shape=jax.ShapeDtypeStruct((b, s, n_heads * D), BF16),
        scratch_shapes=[pltpu.VMEM((D + ONES_ROWS, G * tq), F32), pltpu.VMEM((8, D), F32), pltpu.VMEM((2, tk, G * tq), BF16),
                        pltpu.VMEM((1, G * tq), F32)],
        compiler_params=_params(("parallel", "parallel", "arbitrary")),
        name="flash_attention",
    )(q, k, vt)


def _rope_angles(pos, dim, theta):
    inv = theta ** (-jnp.arange(0, dim, 2, dtype=jnp.float32) / dim)
    return pos.astype(jnp.float32)[:, None] * inv[None, :]


def _retention_tables(s):
    ang = _rope_angles(jnp.arange(s), RET_DK, RET_THETA)
    c, sn = jnp.cos(ang), jnp.sin(ang)
    return jnp.concatenate([c, c], axis=-1), jnp.concatenate([-sn, sn], axis=-1)


def _axial_tables(s):
    rows = s // GRID_W
    half = HEAD_DIM // 2
    ar = _rope_angles(jnp.arange(rows), half, AX_THETA)
    ac = _rope_angles(jnp.arange(GRID_W), half, AX_THETA)
    cr, sr = (jnp.repeat(f(ar), GRID_W, axis=0) for f in (jnp.cos, jnp.sin))
    ccol, scol = (jnp.tile(f(ac), (rows, 1)) for f in (jnp.cos, jnp.sin))
    cc = jnp.concatenate([cr, ccol, cr, ccol], axis=-1)
    ss = jnp.concatenate([-sr, -scol, sr, scol], axis=-1)
    return cc, ss


def kernel(x, norm_mix, norm_mlp, w_in_even, w_out_even, ret_decay_logit, ret_norm, swa_q_norm,
           swa_k_norm, swa_sink, t5_table, w_in_odd, w_out_odd, ax_q_norm, ax_k_norm, w_mlp_up,
           w_mlp_down):
    b, s, d = x.shape
    t = b * s
    depth = norm_mix.shape[0]
    ret_heads = ret_decay_logit.shape[-1]
    ret_q = ret_heads * RET_DK
    ret_v = ret_heads * RET_DV
    swa_heads = swa_sink.shape[-1]
    swa_q = swa_heads * HEAD_DIM
    swa_kv = SWA_KV_HEADS * HEAD_DIM
    ax_q = w_out_odd.shape[1]
    ax_heads = ax_q // HEAD_DIM
    ax_kv = AX_KV_HEADS * HEAD_DIM

    x2 = x.reshape(t, d)
    ret_tabs = _retention_tables(s)
    ax_tabs = _axial_tables(s)

    for layer in range(depth):
        i = layer // 2
        if layer % 2 == 0:
            proj = norm_proj(x2, norm_mix[layer], w_in_even[i].astype(BF16), tm=DENSE_ROWS)
            proj = proj.reshape(b, s, -1)
            ya = retention(proj, ret_decay_logit[i], ret_norm[i], *ret_tabs, n_heads=ret_heads,
                           q_off=0, k_off=ret_q, v_off=2 * ret_q, g_off=2 * ret_q + ret_v, cs=16)
            off = 2 * ret_q + 2 * ret_v
            yb = window_attention(proj, swa_sink[i], t5_table, swa_q_norm[i], swa_k_norm[i],
                                  n_heads=swa_heads, n_kv=SWA_KV_HEADS, q_off=off,
                                  k_off=off + swa_q, v_off=off + swa_q + swa_kv)
            acts, w_out = [ya.reshape(t, -1), yb.reshape(t, -1)], w_out_even[i]
        else:
            qp, kp, vt = axial_proj(x2, norm_mix[layer], w_in_odd[i].astype(BF16), ax_tabs,
                                    ax_q_norm[i], ax_k_norm[i], seq=s, n_heads=ax_heads,
                                    n_kv=AX_KV_HEADS, tm=DENSE_ROWS)
            y = flash_attention(qp.reshape(b, s, -1), kp.reshape(b, s, -1), vt, n_heads=ax_heads,
                                n_kv=AX_KV_HEADS, tq=1024, tk=512)
            acts, w_out = [y.reshape(t, -1)], w_out_odd[i]
        x2 = out_mlp(x2, acts, w_out.astype(BF16), norm_mlp[layer], w_mlp_up[layer].astype(BF16),
                     w_mlp_down[layer].astype(BF16), tm=DENSE_ROWS)
    return x2.reshape(b, s, d)
```

```python
import functools
import math

import jax
import jax.numpy as jnp
from jax import lax
from jax.experimental import pallas as pl
from jax.experimental.pallas import tpu as pltpu

F32 = jnp.float32
BF16 = jnp.bfloat16

EPS = 1e-6
NEG_INF = -1e30
LOG2E = math.log2(math.e)
HEAD_DIM = 128
BLOCK = 128
GRID_W = 64
RET_DK = 128
RET_DV = 256
RET_CHUNK = 128
RET_THETA = 10000.0
SWA_KV_HEADS = 2
SWA_QBLOCKS = 2
WINDOW = 128
T5_BUCKETS = 32
T5_MAX_DIST = 128
AX_KV_HEADS = 2
AX_THETA = 10000.0

VMEM_LIMIT_BYTES = 56 * 1024 * 1024
DENSE_ROWS = 512


def _params(semantics):
    return pltpu.CompilerParams(dimension_semantics=semantics, vmem_limit_bytes=VMEM_LIMIT_BYTES)


def _rms(x, gain):
    ms = jnp.mean(x * x, axis=-1, keepdims=True)
    return x * lax.rsqrt(ms + EPS) * gain


def _dot(a, b):
    return jnp.dot(a, b, preferred_element_type=F32)


def _dot_nt(a, b):
    return lax.dot_general(a, b, (((1,), (1,)), ((), ())), preferred_element_type=F32)


def _dot_tn(a, b):
    return lax.dot_general(a, b, (((0,), (0,)), ((), ())), preferred_element_type=F32)


def _resident(shape, row=0):
    idx = (row,) + (0,) * (len(shape) - 1)
    return pl.BlockSpec(shape, lambda i: idx, pipeline_mode=pl.Buffered(1))


def _norm_proj_kernel(x_ref, g_ref, w_ref, o_ref):
    h = _rms(x_ref[...], g_ref[...]).astype(BF16)
    o_ref[...] = _dot(h, w_ref[...]).astype(o_ref.dtype)


def norm_proj(x2, gain, w, *, tm):
    t, d = x2.shape
    n = w.shape[1]
    return pl.pallas_call(
        _norm_proj_kernel,
        grid=(t // tm,),
        in_specs=[pl.BlockSpec((tm, d), lambda i: (i, 0)), _resident((1, d)), _resident((d, n))],
        out_specs=pl.BlockSpec((tm, n), lambda i: (i, 0)),
        out_shape=jax.ShapeDtypeStruct((t, n), BF16),
        compiler_params=_params(("parallel",)),
        name="norm_proj",
    )(x2, gain.reshape(1, d), w)


STAGE_BYTES = 2 * 1024 * 1024


def _stream_cast(src_hbm, dst_ref, stage_ref, sem_ref):
    rows = stage_ref.shape[1]
    n = src_hbm.shape[0] // rows

    def copy(c):
        return pltpu.make_async_copy(src_hbm.at[pl.ds(c * rows, rows)], stage_ref.at[c % 2],
                                     sem_ref.at[c % 2])

    copy(0).start()
    for c in range(n):
        if c + 1 < n:
            copy(c + 1).start()
        copy(c).wait()
        dst_ref[c * rows:(c + 1) * rows, :] = stage_ref[c % 2].astype(dst_ref.dtype)


def _out_mlp_kernel(x_ref, *refs, n_act):
    act_refs = refs[:n_act]
    (wo_hbm, g_ref, wu_hbm, wd_hbm, o_ref,
     wo_ref, wu_ref, wd_ref, stage_wide, stage_narrow, sem_wide, sem_narrow) = refs[n_act:]

    @pl.when(pl.program_id(0) == 0)
    def _():
        _stream_cast(wo_hbm, wo_ref, stage_narrow, sem_narrow)
        _stream_cast(wu_hbm, wu_ref, stage_wide, sem_wide)
        _stream_cast(wd_hbm, wd_ref, stage_narrow, sem_narrow)

    y = x_ref[...]
    k = act_refs[0].shape[1]
    for s, a_ref in enumerate(act_refs):
        y = y + _dot(a_ref[...], wo_ref[s * k:(s + 1) * k, :])
    h = _rms(y, g_ref[...]).astype(BF16)
    a = jnp.square(jnp.maximum(_dot(h, wu_ref[...]), 0.0)).astype(BF16)
    o_ref[...] = y + _dot(a, wd_ref[...])


def out_mlp(x2, acts, w_out, gain, wu, wd, *, tm):
    t, d = x2.shape
    ff = wu.shape[1]
    k = acts[0].shape[1]
    assert all(a.shape[1] == k for a in acts) and len(acts) * k == w_out.shape[0]
    rows_wide, rows_narrow = STAGE_BYTES // (4 * ff), STAGE_BYTES // (4 * d)
    assert d % rows_wide == 0 and ff % rows_narrow == 0 and w_out.shape[0] % rows_narrow == 0

    def row_tile(width):
        return pl.BlockSpec((tm, width), lambda i: (i, 0))

    hbm = pl.BlockSpec(memory_space=pl.ANY)
    return pl.pallas_call(
        functools.partial(_out_mlp_kernel, n_act=len(acts)),
        grid=(t // tm,),
        in_specs=[row_tile(d)] + [row_tile(k) for _ in acts] + [hbm, _resident((1, d)), hbm, hbm],
        out_specs=row_tile(d),
        out_shape=jax.ShapeDtypeStruct((t, d), F32),
        scratch_shapes=[
            pltpu.VMEM(w_out.shape, BF16), pltpu.VMEM((d, ff), BF16), pltpu.VMEM((ff, d), BF16),
            pltpu.VMEM((2, rows_wide, ff), F32), pltpu.VMEM((2, rows_narrow, d), F32),
            pltpu.SemaphoreType.DMA((2,)), pltpu.SemaphoreType.DMA((2,)),
        ],
        compiler_params=_params(("arbitrary",)),
        name="out_mlp",
    )(x2, *acts, w_out, gain.reshape(1, d), wu, wd)


def _log_sigmoid(x):
    return -(jnp.maximum(-x, 0.0) + jnp.log1p(jnp.exp(-jnp.abs(x))))


def _ret_kernel(dl_ref, q_ref, k_ref, v_ref, g_ref, cos_ref, sin_ref, gn_ref, o_ref,
                sb_ref, sf_ref, cb_ref, dec_ref, *, cs):
    C = RET_CHUNK
    h = pl.program_id(1)
    phase = pl.program_id(2)
    t = pl.program_id(3)
    nsteps = pl.num_programs(3)

    lgf_w = _log_sigmoid(jnp.full((1, RET_DV), dl_ref[0, h], F32))
    lgb_w = _log_sigmoid(jnp.full((1, RET_DV), dl_ref[1, h], F32))

    @pl.when((phase == 0) & (t == 0))
    def _():
        row = lax.broadcasted_iota(jnp.int32, (C, C), 0).astype(F32)
        col = lax.broadcasted_iota(jnp.int32, (C, C), 1).astype(F32)
        lgf = lgf_w[:, :C]
        lgb = lgb_w[:, :C]
        diff = row - col
        dec_ref[0] = jnp.where(diff >= 0, jnp.exp(jnp.maximum(diff, 0.0) * lgf),
                               jnp.exp(jnp.maximum(-diff, 0.0) * lgb))
        dec_ref[1] = jnp.exp((row + 1.0) * lgf)
        dec_ref[2] = jnp.exp((C - 1.0 - row) * lgf)
        dec_ref[3] = jnp.exp((C - row) * lgb)
        dec_ref[4] = jnp.exp(row * lgb)
        sb_ref[...] = jnp.zeros_like(sb_ref)
        sf_ref[...] = jnp.zeros_like(sf_ref)

    def rope(a, rows):
        return a * cos_ref[rows, :] + pltpu.roll(a, RET_DK // 2, 1) * sin_ref[rows, :]

    @pl.when(phase == 0)
    def _():
        sb = sb_ref[...]
        cdec = jnp.exp(C * lgb_w)
        for ci in reversed(range(cs)):
            rows = slice(ci * C, (ci + 1) * C)
            qr = rope(q_ref[0, rows, :].astype(F32), rows)
            kr = rope(k_ref[0, rows, :].astype(F32), rows) * (RET_DK ** -0.5)
            c = (nsteps - 1 - t) * cs + ci
            cb_ref[c] = _dot((qr * dec_ref[3]).astype(BF16), sb.astype(BF16))
            sb = cdec * sb + _dot_tn((kr * dec_ref[4]).astype(BF16), v_ref[0, rows, :])
        sb_ref[...] = sb

    @pl.when(phase == 1)
    def _():
        sf = sf_ref[...]
        cdec = jnp.exp(C * lgf_w)
        for ci in range(cs):
            rows = slice(ci * C, (ci + 1) * C)
            qr = rope(q_ref[0, rows, :].astype(F32), rows)
            kr = rope(k_ref[0, rows, :].astype(F32), rows) * (RET_DK ** -0.5)
            v = v_ref[0, rows, :]
            s = _dot_nt(qr.astype(BF16), kr.astype(BF16))
            inner = _dot((s * dec_ref[0]).astype(BF16), v)
            cross = _dot((qr * dec_ref[1]).astype(BF16), sf.astype(BF16))
            sf = cdec * sf + _dot_tn((kr * dec_ref[2]).astype(BF16), v)
            tot = inner + cross + cb_ref[t * cs + ci]
            y = _rms(tot, gn_ref[...])
            g = g_ref[0, rows, :].astype(F32)
            o_ref[0, rows, :] = (g * jax.nn.sigmoid(g) * y).astype(o_ref.dtype)
        sf_ref[...] = sf


def retention(proj, decay_logit, gn_gain, cos2, sin2, *, n_heads, q_off, k_off, v_off, g_off, cs):
    b, s, _ = proj.shape
    C = RET_CHUNK
    nc = s // C
    ts = cs * C
    nsteps = nc // cs
    ret_v = n_heads * RET_DV

    def step(p, t):
        return jnp.where(p == 0, nsteps - 1 - t, t)

    def spec(width, off):
        base = off // width
        return pl.BlockSpec((1, ts, width), lambda bi, h, p, t: (bi, step(p, t), base + h))

    kern = functools.partial(_ret_kernel, cs=cs)
    return pl.pallas_call(
        kern,
        grid=(b, n_heads, 2, nsteps),
        in_specs=[
            pl.BlockSpec(memory_space=pltpu.SMEM),
            spec(RET_DK, q_off),
            spec(RET_DK, k_off),
            spec(RET_DV, v_off),
            spec(RET_DV, g_off),
            pl.BlockSpec((ts, RET_DK), lambda bi, h, p, t: (step(p, t), 0)),
            pl.BlockSpec((ts, RET_DK), lambda bi, h, p, t: (step(p, t), 0)),
            pl.BlockSpec((1, RET_DV), lambda bi, h, p, t: (0, h)),
        ],
        out_specs=pl.BlockSpec((1, ts, RET_DV), lambda bi, h, p, t: (bi, jnp.where(p == 0, 0, t), h)),
        out_shape=jax.ShapeDtypeStruct((b, s, ret_v), BF16),
        scratch_shapes=[
            pltpu.VMEM((RET_DK, RET_DV), F32),
            pltpu.VMEM((RET_DK, RET_DV), F32),
            pltpu.VMEM((nc, C, RET_DV), F32),
            pltpu.VMEM((5, C, C), F32),
        ],
        compiler_params=_params(("parallel", "parallel", "arbitrary", "arbitrary")),
        name="retention",
    )(decay_logit, proj, proj, proj, proj, cos2, sin2, gn_gain.reshape(1, ret_v))


def _swa_kernel(sink_ref, table_ref, q_ref, kp_ref, km_ref, kn_ref, vp_ref, vm_ref, vn_ref,
                bucket_ref, qg_ref, kg_ref, o_ref, bias_ref, *, n_heads, n_kv, nb):
    step = pl.program_id(1)
    D = HEAD_DIM
    G = n_heads // n_kv
    KW = 3 * BLOCK
    QW = G * BLOCK
    lane_head = lax.broadcasted_iota(jnp.int32, (1, QW), 1) // BLOCK

    def per_head_row(ref, kv, *idx):
        row = jnp.full((1, QW), ref[(*idx, kv * G)], F32)
        for g in range(1, G):
            row = jnp.where(lane_head == g, ref[(*idx, kv * G + g)], row)
        return row

    @pl.when((pl.program_id(0) == 0) & (step == 0))
    def _():
        bucket = bucket_ref[...]
        j = lax.broadcasted_iota(jnp.int32, (KW, QW), 0)
        r = lax.broadcasted_iota(jnp.int32, (KW, QW), 1) % BLOCK
        in_band = jnp.abs(j - BLOCK - r) <= WINDOW
        for kv in range(n_kv):
            def body(bk, acc):
                return jnp.where(bucket == bk, per_head_row(table_ref, kv, bk), acc)
            bias = lax.fori_loop(0, T5_BUCKETS, body, jnp.zeros((KW, QW), F32))
            bias = jnp.where(in_band, bias * LOG2E, NEG_INF)
            for e in range(4):
                keep = (j >= BLOCK if e & 1 else True) & (j < 2 * BLOCK if e & 2 else True)
                bias_ref[e, kv] = bias if e == 0 else jnp.where(keep, bias, NEG_INF)

    qg = qg_ref[...] * (D ** -0.5 * LOG2E)
    kg = kg_ref[...]
    for kv in range(n_kv):
        sl = slice(kv * D, (kv + 1) * D)
        kall = jnp.concatenate([kp_ref[0][:, sl], km_ref[0][:, sl], kn_ref[0][:, sl]], axis=0)
        kall = _rms(kall.astype(F32), kg).astype(BF16)
        vall = jnp.concatenate([vp_ref[0][:, sl], vm_ref[0][:, sl], vn_ref[0][:, sl]], axis=0)
        sink = per_head_row(sink_ref, kv) * LOG2E
        for qb in range(SWA_QBLOCKS):
            i = step * SWA_QBLOCKS + qb
            edge = (i == 0).astype(jnp.int32) + 2 * (i == nb - 1).astype(jnp.int32)
            rows = slice(qb * BLOCK, (qb + 1) * BLOCK)
            k3 = kall[qb * BLOCK:qb * BLOCK + KW]
            v3 = vall[qb * BLOCK:qb * BLOCK + KW]
            qs = jnp.concatenate(
                [_rms(q_ref[0, rows, (kv * G + g) * D:(kv * G + g + 1) * D].astype(F32), qg).astype(BF16)
                 for g in range(G)], axis=0)
            st = _dot_nt(k3, qs) + bias_ref[edge, kv]
            m = jnp.maximum(jnp.max(st, axis=0, keepdims=True), sink)
            p = jnp.exp2(st - m)
            denom = jnp.sum(p, axis=0, keepdims=True) + jnp.exp2(sink - m)
            o = (_dot_tn(v3, p.astype(BF16)) / denom).T
            for g in range(G):
                h = kv * G + g
                o_ref[0, rows, h * D:(h + 1) * D] = o[g * BLOCK:(g + 1) * BLOCK].astype(o_ref.dtype)


def _t5_bucket(rel):
    nb = T5_BUCKETS // 2
    max_exact = nb // 2
    ret = jnp.where(rel > 0, nb, 0)
    n = jnp.abs(rel)
    nf = jnp.maximum(n, 1).astype(jnp.float32)
    large = max_exact + (jnp.log(nf / max_exact) / math.log(T5_MAX_DIST / max_exact)
                         * (nb - max_exact)).astype(jnp.int32)
    large = jnp.minimum(large, nb - 1)
    return ret + jnp.where(n < max_exact, n, large)


def window_attention(proj, sink, t5_table, q_gain, k_gain, *, n_heads, n_kv, q_off, k_off, v_off):
    b, s, _ = proj.shape
    D = HEAD_DIM
    nb = s // BLOCK
    qw = n_heads * D
    kw = n_kv * D
    rr = jnp.arange(BLOCK)
    jj = jnp.arange(3 * BLOCK)
    bucket = _t5_bucket(jj[None, :] - BLOCK - rr[:, None]).astype(jnp.int32)
    G = n_heads // n_kv
    bucket = jnp.tile(bucket.T, (1, G))

    QB = SWA_QBLOCKS
    assert nb % QB == 0

    def kv_specs(off):
        base = off // kw
        return [pl.BlockSpec((1, BLOCK, kw), lambda bi, j: (bi, jnp.maximum(j * QB - 1, 0), base)),
                pl.BlockSpec((1, QB * BLOCK, kw), lambda bi, j: (bi, j, base)),
                pl.BlockSpec((1, BLOCK, kw), lambda bi, j: (bi, jnp.minimum(j * QB + QB, nb - 1), base))]

    kern = functools.partial(_swa_kernel, n_heads=n_heads, n_kv=n_kv, nb=nb)
    return pl.pallas_call(
        kern,
        grid=(b, nb // QB),
        in_specs=[
            pl.BlockSpec(memory_space=pltpu.SMEM),
            pl.BlockSpec(memory_space=pltpu.SMEM),
            pl.BlockSpec((1, QB * BLOCK, qw), lambda bi, j: (bi, j, q_off // qw)),
            *kv_specs(k_off), *kv_specs(v_off),
            pl.BlockSpec((3 * BLOCK, G * BLOCK), lambda bi, j: (0, 0)),
            pl.BlockSpec((1, D), lambda bi, j: (0, 0)),
            pl.BlockSpec((1, D), lambda bi, j: (0, 0)),
        ],
        out_specs=pl.BlockSpec((1, QB * BLOCK, qw), lambda bi, j: (bi, j, 0)),
        out_shape=jax.ShapeDtypeStruct((b, s, qw), BF16),
        scratch_shapes=[pltpu.VMEM((4, n_kv, 3 * BLOCK, G * BLOCK), F32)],
        compiler_params=_params(("arbitrary", "arbitrary")),
        name="window_attention",
    )(sink, t5_table, proj, proj, proj, proj, proj, proj, proj, bucket,
      q_gain.reshape(1, D), k_gain.reshape(1, D))


def _axial_head_perm():
    q4 = HEAD_DIM // 4
    return [blk * q4 + j for blk in (0, 2, 1, 3) for j in range(q4)]


def _axial_rope(a, cc, ss):
    return a * cc + pltpu.roll(a, HEAD_DIM // 2, 1) * ss


ONES_ROWS = 16


def _axial_proj_kernel(x_ref, g_ref, w_ref, cc_ref, ss_ref, qg_ref, kg_ref, qo_ref, ko_ref,
                       vt_ref, proj_ref, *, n_heads, n_kv):
    D = HEAD_DIM
    i = pl.program_id(0)

    @pl.when(i == 0)
    def _():
        proj_ref[1] = jnp.zeros(proj_ref.shape[1:], proj_ref.dtype)

    proj = proj_ref[(i + 1) % 2]
    cc, ss = cc_ref[...], ss_ref[...]
    qg = qg_ref[...] * (D ** -0.5 * LOG2E)
    kg = kg_ref[...]
    for hh in range(n_heads):
        hs = slice(hh * D, (hh + 1) * D)
        qo_ref[:, hs] = _axial_rope(_rms(proj[:, hs], qg), cc, ss).astype(qo_ref.dtype)
    k_off, v_off = n_heads * D, (n_heads + n_kv) * D
    for hh in range(n_kv):
        hs = slice(hh * D, (hh + 1) * D)
        k = _rms(proj[:, k_off + hh * D:k_off + (hh + 1) * D], kg)
        ko_ref[:, hs] = _axial_rope(k, cc, ss).astype(ko_ref.dtype)
        vt_ref[0, hh, :D, :] = proj[:, v_off + hh * D:v_off + (hh + 1) * D].T.astype(vt_ref.dtype)
        vt_ref[0, hh, D:, :] = jnp.ones((ONES_ROWS, vt_ref.shape[-1]), vt_ref.dtype)

    h = _rms(x_ref[...], g_ref[...]).astype(BF16)
    proj_ref[i % 2] = _dot(h, w_ref[...])


def axial_proj(x2, gain, w, tables, q_gain, k_gain, *, seq, n_heads, n_kv, tm):
    t, d = x2.shape
    D = HEAD_DIM
    qw, kw = n_heads * D, n_kv * D
    assert w.shape[1] == qw + 2 * kw and seq % tm == 0
    perm = jnp.asarray(_axial_head_perm())
    cols = jnp.concatenate([(hh * D + perm) for hh in range(n_heads + n_kv)]
                           + [jnp.arange(qw + kw, qw + 2 * kw)])
    w = w[:, cols]
    q_gain, k_gain = q_gain[perm], k_gain[perm]
    spb = seq // tm
    n = t // tm
    kern = functools.partial(_axial_proj_kernel, n_heads=n_heads, n_kv=n_kv)

    def done(i):
        return jnp.maximum(i - 1, 0)

    tab = pl.BlockSpec((tm, D), lambda i: (done(i) % spb, 0))
    return pl.pallas_call(
        kern,
        grid=(n + 1,),
        in_specs=[pl.BlockSpec((tm, d), lambda i: (jnp.minimum(i, n - 1), 0)), _resident((1, d)),
                  _resident((d, w.shape[1])), tab, tab, _resident((1, D)), _resident((1, D))],
        out_specs=[
            pl.BlockSpec((tm, qw), lambda i: (done(i), 0)),
            pl.BlockSpec((tm, kw), lambda i: (done(i), 0)),
            pl.BlockSpec((1, n_kv, D + ONES_ROWS, tm), lambda i: (done(i) // spb, 0, 0, done(i) % spb)),
        ],
        out_shape=[jax.ShapeDtypeStruct((t, qw), BF16), jax.ShapeDtypeStruct((t, kw), BF16),
                   jax.ShapeDtypeStruct((t // seq, n_kv, D + ONES_ROWS, seq), BF16)],
        scratch_shapes=[pltpu.VMEM((2, tm, w.shape[1]), F32)],
        compiler_params=_params(("arbitrary",)),
        name="axial_proj",
    )(x2, gain.reshape(1, d), w, *tables, q_gain.reshape(1, D), k_gain.reshape(1, D))


FAST_SUM_MIN = 2.0 ** -80
FAST_SUM_MAX = 2.0 ** 100


def _flash_kernel(q_ref, k_ref, vt_ref, o_ref, acc_ref, kn_ref, p_ref, l_ref, *, G, tq, tk, nk):
    D = HEAD_DIM
    R = G * tq
    qs = jnp.concatenate([q_ref[0][:, g * D:(g + 1) * D] for g in range(G)], axis=0)

    @pl.when(pl.program_id(2) == 0)
    def _():
        def kbody(c, mx):
            start = pl.multiple_of(c * tk, tk)
            kc = k_ref[0, pl.ds(start, tk), :].astype(F32)
            return jnp.maximum(mx, jnp.max(jnp.sum(kc * kc, axis=-1, keepdims=True), axis=0, keepdims=True))
        kn2 = lax.fori_loop(0, nk, kbody, jnp.zeros((1, 1), F32))
        kn_ref[...] = jnp.broadcast_to(kn2, kn_ref.shape)

    def scores(c):
        start = pl.multiple_of(c * tk, tk)
        return _dot_nt(k_ref[0, pl.ds(start, tk), :], qs)

    def vt_chunk(c, rows=D + ONES_ROWS):
        return vt_ref[0, 0, :rows, pl.ds(pl.multiple_of(c * tk, tk), tk)]


    def produce_fixed(c, slot, shift):
        p = jnp.exp2(scores(c) - shift)
        l_ref[...] += jnp.sum(p, axis=0, keepdims=True)
        p_ref[slot] = p.astype(BF16)

    def consume_fixed(c, slot, shift):
        acc_ref[:D, :] += _dot(vt_chunk(c, D), p_ref[slot])

    def first_pass(shift):
        def body(c2, carry):
            c = 2 * c2
            produce_fixed(c + 1, 1, shift)
            consume_fixed(c, 0, shift)
            produce_fixed(c + 2, 0, shift)
            consume_fixed(c + 1, 1, shift)
            return carry

        acc_ref[...] = jnp.zeros_like(acc_ref)
        l_ref[...] = jnp.zeros_like(l_ref)
        produce_fixed(0, 0, shift)
        lax.fori_loop(0, nk // 2 - 1, body, 0)
        produce_fixed(nk - 1, 1, shift)
        consume_fixed(nk - 2, 0, shift)
        consume_fixed(nk - 1, 1, shift)

    def running_max_pass():
        def body(c, m_old):
            st = scores(c)
            m_new = jnp.maximum(m_old, jnp.max(st, axis=0, keepdims=True))
            alpha = jnp.exp2(m_old - m_new)
            p = jnp.exp2((st - m_new).astype(BF16))
            acc_ref[...] = alpha * acc_ref[...] + _dot(vt_chunk(c), p)
            return m_new

        acc_ref[...] = jnp.zeros_like(acc_ref)
        lax.fori_loop(0, nk, body, jnp.full((1, R), -jnp.inf, F32))

    def write_out(sums):
        out = (acc_ref[:D, :] / sums).T
        for g in range(G):
            o_ref[0, :, g * D:(g + 1) * D] = out[g * tq:(g + 1) * tq].astype(o_ref.dtype)

    qf = qs.astype(F32)
    qn2 = _dot_nt(jnp.ones((8, D), BF16), (qf * qf).astype(BF16))[:1]
    first_pass(jnp.sqrt(qn2 * kn_ref[:1, :1]))
    sums = l_ref[...]
    trusted = (jnp.min(sums) >= FAST_SUM_MIN) & (jnp.max(sums) <= FAST_SUM_MAX)

    @pl.when(trusted)
    def _():
        write_out(sums)

    @pl.when(jnp.logical_not(trusted))
    def _():
        running_max_pass()
        write_out(acc_ref[D:D + 1, :])


def flash_attention(q, k, vt, *, n_heads, n_kv, tq, tk):
    b, s, _ = q.shape
    D = HEAD_DIM
    G = n_heads // n_kv
    assert s % (2 * tk) == 0 and s // tk >= 2
    kern = functools.partial(_flash_kernel, G=G, tq=tq, tk=tk, nk=s // tk)
    return pl.pallas_call(
        kern,
        grid=(b, n_kv, s // tq),
        in_specs=[
            pl.BlockSpec((1, tq, G * D), lambda bi, kv, qi: (bi, qi, kv)),
            pl.BlockSpec((1, s, D), lambda bi, kv, qi: (bi, 0, kv)),
            pl.BlockSpec((1, 1, D + ONES_ROWS, s), lambda bi, kv, qi: (bi, kv, 0, 0)),
        ],
        out_specs=pl.BlockSpec((1, tq, G * D), lambda bi, kv, qi: (bi, qi, kv)),
        out_shape=jax.ShapeDtypeStruct((b, s, n_heads * D), BF16),
        scratch_shapes=[pltpu.VMEM((D + ONES_ROWS, G * tq), F32), pltpu.VMEM((8, D), F32), pltpu.VMEM((2, tk, G * tq), BF16),
                        pltpu.VMEM((1, G * tq), F32)],
        compiler_params=_params(("parallel", "parallel", "arbitrary")),
        name="flash_attention",
    )(q, k, vt)


def _rope_angles(pos, dim, theta):
    inv = theta ** (-jnp.arange(0, dim, 2, dtype=jnp.float32) / dim)
    return pos.astype(jnp.float32)[:, None] * inv[None, :]


def _retention_tables(s):
    ang = _rope_angles(jnp.arange(s), RET_DK, RET_THETA)
    c, sn = jnp.cos(ang), jnp.sin(ang)
    return jnp.concatenate([c, c], axis=-1), jnp.concatenate([-sn, sn], axis=-1)


def _axial_tables(s):
    rows = s // GRID_W
    half = HEAD_DIM // 2
    ar = _rope_angles(jnp.arange(rows), half, AX_THETA)
    ac = _rope_angles(jnp.arange(GRID_W), half, AX_THETA)
    cr, sr = (jnp.repeat(f(ar), GRID_W, axis=0) for f in (jnp.cos, jnp.sin))
    ccol, scol = (jnp.tile(f(ac), (rows, 1)) for f in (jnp.cos, jnp.sin))
    cc = jnp.concatenate([cr, ccol, cr, ccol], axis=-1)
    ss = jnp.concatenate([-sr, -scol, sr, scol], axis=-1)
    return cc, ss


def kernel(x, norm_mix, norm_mlp, w_in_even, w_out_even, ret_decay_logit, ret_norm, swa_q_norm,
           swa_k_norm, swa_sink, t5_table, w_in_odd, w_out_odd, ax_q_norm, ax_k_norm, w_mlp_up,
           w_mlp_down):
    b, s, d = x.shape
    t = b * s
    depth = norm_mix.shape[0]
    ret_heads = ret_decay_logit.shape[-1]
    ret_q = ret_heads * RET_DK
    ret_v = ret_heads * RET_DV
    swa_heads = swa_sink.shape[-1]
    swa_q = swa_heads * HEAD_DIM
    swa_kv = SWA_KV_HEADS * HEAD_DIM
    ax_q = w_out_odd.shape[1]
    ax_heads = ax_q // HEAD_DIM
    ax_kv = AX_KV_HEADS * HEAD_DIM

    x2 = x.reshape(t, d)
    ret_tabs = _retention_tables(s)
    ax_tabs = _axial_tables(s)

    for layer in range(depth):
        i = layer // 2
        if layer % 2 == 0:
            proj = norm_proj(x2, norm_mix[layer], w_in_even[i].astype(BF16), tm=DENSE_ROWS)
            proj = proj.reshape(b, s, -1)
            ya = retention(proj, ret_decay_logit[i], ret_norm[i], *ret_tabs, n_heads=ret_heads,
                           q_off=0, k_off=ret_q, v_off=2 * ret_q, g_off=2 * ret_q + ret_v, cs=16)
            off = 2 * ret_q + 2 * ret_v
            yb = window_attention(proj, swa_sink[i], t5_table, swa_q_norm[i], swa_k_norm[i],
                                  n_heads=swa_heads, n_kv=SWA_KV_HEADS, q_off=off,
                                  k_off=off + swa_q, v_off=off + swa_q + swa_kv)
            acts, w_out = [ya.reshape(t, -1), yb.reshape(t, -1)], w_out_even[i]
        else:
            qp, kp, vt = axial_proj(x2, norm_mix[layer], w_in_odd[i].astype(BF16), ax_tabs,
                                    ax_q_norm[i], ax_k_norm[i], seq=s, n_heads=ax_heads,
                                    n_kv=AX_KV_HEADS, tm=DENSE_ROWS)
            y = flash_attention(qp.reshape(b, s, -1), kp.reshape(b, s, -1), vt, n_heads=ax_heads,
                                n_kv=AX_KV_HEADS, tq=1024, tk=512)
            acts, w_out = [y.reshape(t, -1)], w_out_odd[i]
        x2 = out_mlp(x2, acts, w_out, norm_mlp[layer], w_mlp_up[layer], w_mlp_down[layer],
                     tm=DENSE_ROWS)
    return x2.reshape(b, s, d)
```

```python
import functools
import math

import jax
import jax.numpy as jnp
from jax import lax
from jax.experimental import pallas as pl
from jax.experimental.pallas import tpu as pltpu

F32 = jnp.float32
BF16 = jnp.bfloat16

EPS = 1e-6
NEG_INF = -1e30
LOG2E = math.log2(math.e)
HEAD_DIM = 128
BLOCK = 128
GRID_W = 64
RET_DK = 128
RET_DV = 256
RET_CHUNK = 128
RET_THETA = 10000.0
SWA_KV_HEADS = 2
SWA_QBLOCKS = 2
WINDOW = 128
T5_BUCKETS = 32
T5_MAX_DIST = 128
AX_KV_HEADS = 2
AX_THETA = 10000.0

VMEM_LIMIT_BYTES = 56 * 1024 * 1024
DENSE_ROWS = 512


def _params(semantics):
    return pltpu.CompilerParams(dimension_semantics=semantics, vmem_limit_bytes=VMEM_LIMIT_BYTES)


def _rms(x, gain):
    ms = jnp.mean(x * x, axis=-1, keepdims=True)
    return x * lax.rsqrt(ms + EPS) * gain


def _dot(a, b):
    return jnp.dot(a, b, preferred_element_type=F32)


def _dot_nt(a, b):
    return lax.dot_general(a, b, (((1,), (1,)), ((), ())), preferred_element_type=F32)


def _dot_tn(a, b):
    return lax.dot_general(a, b, (((0,), (0,)), ((), ())), preferred_element_type=F32)


def _resident(shape, row=0):
    idx = (row,) + (0,) * (len(shape) - 1)
    return pl.BlockSpec(shape, lambda i: idx, pipeline_mode=pl.Buffered(1))


def _norm_proj_kernel(x_ref, g_ref, w_ref, o_ref):
    h = _rms(x_ref[...], g_ref[...]).astype(BF16)
    o_ref[...] = _dot(h, w_ref[...]).astype(o_ref.dtype)


def norm_proj(x2, gain, w, *, tm):
    t, d = x2.shape
    n = w.shape[1]
    return pl.pallas_call(
        _norm_proj_kernel,
        grid=(t // tm,),
        in_specs=[pl.BlockSpec((tm, d), lambda i: (i, 0)), _resident((1, d)), _resident((d, n))],
        out_specs=pl.BlockSpec((tm, n), lambda i: (i, 0)),
        out_shape=jax.ShapeDtypeStruct((t, n), BF16),
        compiler_params=_params(("parallel",)),
        name="norm_proj",
    )(x2, gain.reshape(1, d), w)


STAGE_BYTES = 2 * 1024 * 1024


def _stream_cast(src_hbm, index, dst_ref, stage_ref, sem_ref):
    rows = stage_ref.shape[1]
    n = src_hbm.shape[1] // rows

    def copy(c):
        return pltpu.make_async_copy(src_hbm.at[index, pl.ds(c * rows, rows)], stage_ref.at[c % 2],
                                     sem_ref.at[c % 2])

    copy(0).start()
    for c in range(n):
        if c + 1 < n:
            copy(c + 1).start()
        copy(c).wait()
        dst_ref[c * rows:(c + 1) * rows, :] = stage_ref[c % 2].astype(dst_ref.dtype)


def _out_mlp_kernel(x_ref, *refs, n_act, wo_index, mlp_index):
    act_refs = refs[:n_act]
    (wo_hbm, g_ref, wu_hbm, wd_hbm, o_ref,
     wo_ref, wu_ref, wd_ref, stage_wide, stage_narrow, sem_wide, sem_narrow) = refs[n_act:]

    @pl.when(pl.program_id(0) == 0)
    def _():
        _stream_cast(wo_hbm, wo_index, wo_ref, stage_narrow, sem_narrow)
        _stream_cast(wu_hbm, mlp_index, wu_ref, stage_wide, sem_wide)
        _stream_cast(wd_hbm, mlp_index, wd_ref, stage_narrow, sem_narrow)

    y = x_ref[...]
    k = act_refs[0].shape[1]
    for s, a_ref in enumerate(act_refs):
        y = y + _dot(a_ref[...], wo_ref[s * k:(s + 1) * k, :])
    h = _rms(y, g_ref[...]).astype(BF16)
    a = jnp.square(jnp.maximum(_dot(h, wu_ref[...]), 0.0)).astype(BF16)
    o_ref[...] = y + _dot(a, wd_ref[...])


def out_mlp(x2, acts, w_out, wo_index, gain, wu, wd, mlp_index, *, tm):
    t, d = x2.shape
    ff = wu.shape[2]
    k = acts[0].shape[1]
    assert all(a.shape[1] == k for a in acts) and len(acts) * k == w_out.shape[1]
    rows_wide, rows_narrow = STAGE_BYTES // (4 * ff), STAGE_BYTES // (4 * d)
    assert d % rows_wide == 0 and ff % rows_narrow == 0 and w_out.shape[1] % rows_narrow == 0

    def row_tile(width):
        return pl.BlockSpec((tm, width), lambda i: (i, 0))

    hbm = pl.BlockSpec(memory_space=pl.ANY)
    return pl.pallas_call(
        functools.partial(_out_mlp_kernel, n_act=len(acts), wo_index=wo_index, mlp_index=mlp_index),
        grid=(t // tm,),
        in_specs=[row_tile(d)] + [row_tile(k) for _ in acts] + [hbm, _resident((1, d)), hbm, hbm],
        out_specs=row_tile(d),
        out_shape=jax.ShapeDtypeStruct((t, d), F32),
        scratch_shapes=[
            pltpu.VMEM(w_out.shape[1:], BF16), pltpu.VMEM((d, ff), BF16), pltpu.VMEM((ff, d), BF16),
            pltpu.VMEM((2, rows_wide, ff), F32), pltpu.VMEM((2, rows_narrow, d), F32),
            pltpu.SemaphoreType.DMA((2,)), pltpu.SemaphoreType.DMA((2,)),
        ],
        compiler_params=_params(("arbitrary",)),
        name="out_mlp",
    )(x2, *acts, w_out, gain.reshape(1, d), wu, wd)


def _log_sigmoid(x):
    return -(jnp.maximum(-x, 0.0) + jnp.log1p(jnp.exp(-jnp.abs(x))))


def _ret_kernel(dl_ref, q_ref, k_ref, v_ref, g_ref, cos_ref, sin_ref, gn_ref, o_ref,
                sb_ref, sf_ref, cb_ref, dec_ref, *, cs):
    C = RET_CHUNK
    h = pl.program_id(1)
    phase = pl.program_id(2)
    t = pl.program_id(3)
    nsteps = pl.num_programs(3)

    lgf_w = _log_sigmoid(jnp.full((1, RET_DV), dl_ref[0, h], F32))
    lgb_w = _log_sigmoid(jnp.full((1, RET_DV), dl_ref[1, h], F32))

    @pl.when((phase == 0) & (t == 0))
    def _():
        row = lax.broadcasted_iota(jnp.int32, (C, C), 0).astype(F32)
        col = lax.broadcasted_iota(jnp.int32, (C, C), 1).astype(F32)
        lgf = lgf_w[:, :C]
        lgb = lgb_w[:, :C]
        diff = row - col
        dec_ref[0] = jnp.where(diff >= 0, jnp.exp(jnp.maximum(diff, 0.0) * lgf),
                               jnp.exp(jnp.maximum(-diff, 0.0) * lgb))
        dec_ref[1] = jnp.exp((row + 1.0) * lgf)
        dec_ref[2] = jnp.exp((C - 1.0 - row) * lgf)
        dec_ref[3] = jnp.exp((C - row) * lgb)
        dec_ref[4] = jnp.exp(row * lgb)
        sb_ref[...] = jnp.zeros_like(sb_ref)
        sf_ref[...] = jnp.zeros_like(sf_ref)

    def rope(a, rows):
        return a * cos_ref[rows, :] + pltpu.roll(a, RET_DK // 2, 1) * sin_ref[rows, :]

    @pl.when(phase == 0)
    def _():
        sb = sb_ref[...]
        cdec = jnp.exp(C * lgb_w)
        for ci in reversed(range(cs)):
            rows = slice(ci * C, (ci + 1) * C)
            qr = rope(q_ref[0, rows, :].astype(F32), rows)
            kr = rope(k_ref[0, rows, :].astype(F32), rows) * (RET_DK ** -0.5)
            c = (nsteps - 1 - t) * cs + ci
            cb_ref[c] = _dot((qr * dec_ref[3]).astype(BF16), sb.astype(BF16))
            sb = cdec * sb + _dot_tn((kr * dec_ref[4]).astype(BF16), v_ref[0, rows, :])
        sb_ref[...] = sb

    @pl.when(phase == 1)
    def _():
        sf = sf_ref[...]
        cdec = jnp.exp(C * lgf_w)
        for ci in range(cs):
            rows = slice(ci * C, (ci + 1) * C)
            qr = rope(q_ref[0, rows, :].astype(F32), rows)
            kr = rope(k_ref[0, rows, :].astype(F32), rows) * (RET_DK ** -0.5)
            v = v_ref[0, rows, :]
            s = _dot_nt(qr.astype(BF16), kr.astype(BF16))
            inner = _dot((s * dec_ref[0]).astype(BF16), v)
            cross = _dot((qr * dec_ref[1]).astype(BF16), sf.astype(BF16))
            sf = cdec * sf + _dot_tn((kr * dec_ref[2]).astype(BF16), v)
            tot = inner + cross + cb_ref[t * cs + ci]
            y = _rms(tot, gn_ref[...])
            g = g_ref[0, rows, :].astype(F32)
            o_ref[0, rows, :] = (g * jax.nn.sigmoid(g) * y).astype(o_ref.dtype)
        sf_ref[...] = sf


def retention(proj, decay_logit, gn_gain, cos2, sin2, *, n_heads, q_off, k_off, v_off, g_off, cs):
    b, s, _ = proj.shape
    C = RET_CHUNK
    nc = s // C
    ts = cs * C
    nsteps = nc // cs
    ret_v = n_heads * RET_DV

    def step(p, t):
        return jnp.where(p == 0, nsteps - 1 - t, t)

    def spec(width, off):
        base = off // width
        return pl.BlockSpec((1, ts, width), lambda bi, h, p, t: (bi, step(p, t), base + h))

    kern = functools.partial(_ret_kernel, cs=cs)
    return pl.pallas_call(
        kern,
        grid=(b, n_heads, 2, nsteps),
        in_specs=[
            pl.BlockSpec(memory_space=pltpu.SMEM),
            spec(RET_DK, q_off),
            spec(RET_DK, k_off),
            spec(RET_DV, v_off),
            spec(RET_DV, g_off),
            pl.BlockSpec((ts, RET_DK), lambda bi, h, p, t: (step(p, t), 0)),
            pl.BlockSpec((ts, RET_DK), lambda bi, h, p, t: (step(p, t), 0)),
            pl.BlockSpec((1, RET_DV), lambda bi, h, p, t: (0, h)),
        ],
        out_specs=pl.BlockSpec((1, ts, RET_DV), lambda bi, h, p, t: (bi, jnp.where(p == 0, 0, t), h)),
        out_shape=jax.ShapeDtypeStruct((b, s, ret_v), BF16),
        scratch_shapes=[
            pltpu.VMEM((RET_DK, RET_DV), F32),
            pltpu.VMEM((RET_DK, RET_DV), F32),
            pltpu.VMEM((nc, C, RET_DV), F32),
            pltpu.VMEM((5, C, C), F32),
        ],
        compiler_params=_params(("parallel", "parallel", "arbitrary", "arbitrary")),
        name="retention",
    )(decay_logit, proj, proj, proj, proj, cos2, sin2, gn_gain.reshape(1, ret_v))


def _swa_kernel(sink_ref, table_ref, q_ref, kp_ref, km_ref, kn_ref, vp_ref, vm_ref, vn_ref,
                bucket_ref, qg_ref, kg_ref, o_ref, bias_ref, *, n_heads, n_kv, nb):
    step = pl.program_id(1)
    D = HEAD_DIM
    G = n_heads // n_kv
    KW = 3 * BLOCK
    QW = G * BLOCK
    lane_head = lax.broadcasted_iota(jnp.int32, (1, QW), 1) // BLOCK

    def per_head_row(ref, kv, *idx):
        row = jnp.full((1, QW), ref[(*idx, kv * G)], F32)
        for g in range(1, G):
            row = jnp.where(lane_head == g, ref[(*idx, kv * G + g)], row)
        return row

    @pl.when((pl.program_id(0) == 0) & (step == 0))
    def _():
        bucket = bucket_ref[...]
        j = lax.broadcasted_iota(jnp.int32, (KW, QW), 0)
        r = lax.broadcasted_iota(jnp.int32, (KW, QW), 1) % BLOCK
        in_band = jnp.abs(j - BLOCK - r) <= WINDOW
        for kv in range(n_kv):
            def body(bk, acc):
                return jnp.where(bucket == bk, per_head_row(table_ref, kv, bk), acc)
            bias = lax.fori_loop(0, T5_BUCKETS, body, jnp.zeros((KW, QW), F32))
            bias = jnp.where(in_band, bias * LOG2E, NEG_INF)
            for e in range(4):
                keep = (j >= BLOCK if e & 1 else True) & (j < 2 * BLOCK if e & 2 else True)
                bias_ref[e, kv] = bias if e == 0 else jnp.where(keep, bias, NEG_INF)

    qg = qg_ref[...] * (D ** -0.5 * LOG2E)
    kg = kg_ref[...]
    for kv in range(n_kv):
        sl = slice(kv * D, (kv + 1) * D)
        kall = jnp.concatenate([kp_ref[0][:, sl], km_ref[0][:, sl], kn_ref[0][:, sl]], axis=0)
        kall = _rms(kall.astype(F32), kg).astype(BF16)
        vall = jnp.concatenate([vp_ref[0][:, sl], vm_ref[0][:, sl], vn_ref[0][:, sl]], axis=0)
        sink = per_head_row(sink_ref, kv) * LOG2E
        for qb in range(SWA_QBLOCKS):
            i = step * SWA_QBLOCKS + qb
            edge = (i == 0).astype(jnp.int32) + 2 * (i == nb - 1).astype(jnp.int32)
            rows = slice(qb * BLOCK, (qb + 1) * BLOCK)
            k3 = kall[qb * BLOCK:qb * BLOCK + KW]
            v3 = vall[qb * BLOCK:qb * BLOCK + KW]
            qs = jnp.concatenate(
                [_rms(q_ref[0, rows, (kv * G + g) * D:(kv * G + g + 1) * D].astype(F32), qg).astype(BF16)
                 for g in range(G)], axis=0)
            st = _dot_nt(k3, qs) + bias_ref[edge, kv]
            m = jnp.maximum(jnp.max(st, axis=0, keepdims=True), sink)
            p = jnp.exp2(st - m)
            denom = jnp.sum(p, axis=0, keepdims=True) + jnp.exp2(sink - m)
            o = (_dot_tn(v3, p.astype(BF16)) / denom).T
            for g in range(G):
                h = kv * G + g
                o_ref[0, rows, h * D:(h + 1) * D] = o[g * BLOCK:(g + 1) * BLOCK].astype(o_ref.dtype)


def _t5_bucket(rel):
    nb = T5_BUCKETS // 2
    max_exact = nb // 2
    ret = jnp.where(rel > 0, nb, 0)
    n = jnp.abs(rel)
    nf = jnp.maximum(n, 1).astype(jnp.float32)
    large = max_exact + (jnp.log(nf / max_exact) / math.log(T5_MAX_DIST / max_exact)
                         * (nb - max_exact)).astype(jnp.int32)
    large = jnp.minimum(large, nb - 1)
    return ret + jnp.where(n < max_exact, n, large)


def window_attention(proj, sink, t5_table, q_gain, k_gain, *, n_heads, n_kv, q_off, k_off, v_off):
    b, s, _ = proj.shape
    D = HEAD_DIM
    nb = s // BLOCK
    qw = n_heads * D
    kw = n_kv * D
    rr = jnp.arange(BLOCK)
    jj = jnp.arange(3 * BLOCK)
    bucket = _t5_bucket(jj[None, :] - BLOCK - rr[:, None]).astype(jnp.int32)
    G = n_heads // n_kv
    bucket = jnp.tile(bucket.T, (1, G))

    QB = SWA_QBLOCKS
    assert nb % QB == 0

    def kv_specs(off):
        base = off // kw
        return [pl.BlockSpec((1, BLOCK, kw), lambda bi, j: (bi, jnp.maximum(j * QB - 1, 0), base)),
                pl.BlockSpec((1, QB * BLOCK, kw), lambda bi, j: (bi, j, base)),
                pl.BlockSpec((1, BLOCK, kw), lambda bi, j: (bi, jnp.minimum(j * QB + QB, nb - 1), base))]

    kern = functools.partial(_swa_kernel, n_heads=n_heads, n_kv=n_kv, nb=nb)
    return pl.pallas_call(
        kern,
        grid=(b, nb // QB),
        in_specs=[
            pl.BlockSpec(memory_space=pltpu.SMEM),
            pl.BlockSpec(memory_space=pltpu.SMEM),
            pl.BlockSpec((1, QB * BLOCK, qw), lambda bi, j: (bi, j, q_off // qw)),
            *kv_specs(k_off), *kv_specs(v_off),
            pl.BlockSpec((3 * BLOCK, G * BLOCK), lambda bi, j: (0, 0)),
            pl.BlockSpec((1, D), lambda bi, j: (0, 0)),
            pl.BlockSpec((1, D), lambda bi, j: (0, 0)),
        ],
        out_specs=pl.BlockSpec((1, QB * BLOCK, qw), lambda bi, j: (bi, j, 0)),
        out_shape=jax.ShapeDtypeStruct((b, s, qw), BF16),
        scratch_shapes=[pltpu.VMEM((4, n_kv, 3 * BLOCK, G * BLOCK), F32)],
        compiler_params=_params(("arbitrary", "arbitrary")),
        name="window_attention",
    )(sink, t5_table, proj, proj, proj, proj, proj, proj, proj, bucket,
      q_gain.reshape(1, D), k_gain.reshape(1, D))


def _axial_head_perm():
    q4 = HEAD_DIM // 4
    return [blk * q4 + j for blk in (0, 2, 1, 3) for j in range(q4)]


def _axial_rope(a, cc, ss):
    return a * cc + pltpu.roll(a, HEAD_DIM // 2, 1) * ss


ONES_ROWS = 16


def _axial_proj_kernel(x_ref, g_ref, w_ref, cc_ref, ss_ref, qg_ref, kg_ref, qo_ref, ko_ref,
                       vt_ref, proj_ref, *, n_heads, n_kv):
    D = HEAD_DIM
    i = pl.program_id(0)

    @pl.when(i == 0)
    def _():
        proj_ref[1] = jnp.zeros(proj_ref.shape[1:], proj_ref.dtype)

    proj = proj_ref[(i + 1) % 2]
    cc, ss = cc_ref[...], ss_ref[...]
    qg = qg_ref[...] * (D ** -0.5 * LOG2E)
    kg = kg_ref[...]
    for hh in range(n_heads):
        hs = slice(hh * D, (hh + 1) * D)
        qo_ref[:, hs] = _axial_rope(_rms(proj[:, hs], qg), cc, ss).astype(qo_ref.dtype)
    k_off, v_off = n_heads * D, (n_heads + n_kv) * D
    for hh in range(n_kv):
        hs = slice(hh * D, (hh + 1) * D)
        k = _rms(proj[:, k_off + hh * D:k_off + (hh + 1) * D], kg)
        ko_ref[:, hs] = _axial_rope(k, cc, ss).astype(ko_ref.dtype)
        vt_ref[0, hh, :D, :] = proj[:, v_off + hh * D:v_off + (hh + 1) * D].T.astype(vt_ref.dtype)
        vt_ref[0, hh, D:, :] = jnp.ones((ONES_ROWS, vt_ref.shape[-1]), vt_ref.dtype)

    h = _rms(x_ref[...], g_ref[...]).astype(BF16)
    proj_ref[i % 2] = _dot(h, w_ref[...])


def axial_proj(x2, gain, w, tables, q_gain, k_gain, *, seq, n_heads, n_kv, tm):
    t, d = x2.shape
    D = HEAD_DIM
    qw, kw = n_heads * D, n_kv * D
    assert w.shape[1] == qw + 2 * kw and seq % tm == 0
    perm = jnp.asarray(_axial_head_perm())
    cols = jnp.concatenate([(hh * D + perm) for hh in range(n_heads + n_kv)]
                           + [jnp.arange(qw + kw, qw + 2 * kw)])
    w = w[:, cols]
    q_gain, k_gain = q_gain[perm], k_gain[perm]
    spb = seq // tm
    n = t // tm
    kern = functools.partial(_axial_proj_kernel, n_heads=n_heads, n_kv=n_kv)

    def done(i):
        return jnp.maximum(i - 1, 0)

    tab = pl.BlockSpec((tm, D), lambda i: (done(i) % spb, 0))
    return pl.pallas_call(
        kern,
        grid=(n + 1,),
        in_specs=[pl.BlockSpec((tm, d), lambda i: (jnp.minimum(i, n - 1), 0)), _resident((1, d)),
                  _resident((d, w.shape[1])), tab, tab, _resident((1, D)), _resident((1, D))],
        out_specs=[
            pl.BlockSpec((tm, qw), lambda i: (done(i), 0)),
            pl.BlockSpec((tm, kw), lambda i: (done(i), 0)),
            pl.BlockSpec((1, n_kv, D + ONES_ROWS, tm), lambda i: (done(i) // spb, 0, 0, done(i) % spb)),
        ],
        out_shape=[jax.ShapeDtypeStruct((t, qw), BF16), jax.ShapeDtypeStruct((t, kw), BF16),
                   jax.ShapeDtypeStruct((t // seq, n_kv, D + ONES_ROWS, seq), BF16)],
        scratch_shapes=[pltpu.VMEM((2, tm, w.shape[1]), F32)],
        compiler_params=_params(("arbitrary",)),
        name="axial_proj",
    )(x2, gain.reshape(1, d), w, *tables, q_gain.reshape(1, D), k_gain.reshape(1, D))


FAST_SUM_MIN = 2.0 ** -80
FAST_SUM_MAX = 2.0 ** 100


def _flash_kernel(q_ref, k_ref, vt_ref, o_ref, acc_ref, kn_ref, p_ref, l_ref, *, G, tq, tk, nk):
    D = HEAD_DIM
    R = G * tq
    qs = jnp.concatenate([q_ref[0][:, g * D:(g + 1) * D] for g in range(G)], axis=0)

    @pl.when(pl.program_id(2) == 0)
    def _():
        def kbody(c, mx):
            start = pl.multiple_of(c * tk, tk)
            kc = k_ref[0, pl.ds(start, tk), :].astype(F32)
            return jnp.maximum(mx, jnp.max(jnp.sum(kc * kc, axis=-1, keepdims=True), axis=0, keepdims=True))
        kn2 = lax.fori_loop(0, nk, kbody, jnp.zeros((1, 1), F32))
        kn_ref[...] = jnp.broadcast_to(kn2, kn_ref.shape)

    def scores(c):
        start = pl.multiple_of(c * tk, tk)
        return _dot_nt(k_ref[0, pl.ds(start, tk), :], qs)

    def vt_chunk(c, rows=D + ONES_ROWS):
        return vt_ref[0, 0, :rows, pl.ds(pl.multiple_of(c * tk, tk), tk)]


    def produce_fixed(c, slot, shift):
        p = jnp.exp2(scores(c) - shift)
        l_ref[...] += jnp.sum(p, axis=0, keepdims=True)
        p_ref[slot] = p.astype(BF16)

    def consume_fixed(c, slot, shift):
        acc_ref[:D, :] += _dot(vt_chunk(c, D), p_ref[slot])

    def first_pass(shift):
        def body(c2, carry):
            c = 2 * c2
            produce_fixed(c + 1, 1, shift)
            consume_fixed(c, 0, shift)
            produce_fixed(c + 2, 0, shift)
            consume_fixed(c + 1, 1, shift)
            return carry

        acc_ref[...] = jnp.zeros_like(acc_ref)
        l_ref[...] = jnp.zeros_like(l_ref)
        produce_fixed(0, 0, shift)
        lax.fori_loop(0, nk // 2 - 1, body, 0)
        produce_fixed(nk - 1, 1, shift)
        consume_fixed(nk - 2, 0, shift)
        consume_fixed(nk - 1, 1, shift)

    def running_max_pass():
        def body(c, m_old):
            st = scores(c)
            m_new = jnp.maximum(m_old, jnp.max(st, axis=0, keepdims=True))
            alpha = jnp.exp2(m_old - m_new)
            p = jnp.exp2((st - m_new).astype(BF16))
            acc_ref[...] = alpha * acc_ref[...] + _dot(vt_chunk(c), p)
            return m_new

        acc_ref[...] = jnp.zeros_like(acc_ref)
        lax.fori_loop(0, nk, body, jnp.full((1, R), -jnp.inf, F32))

    def write_out(sums):
        out = (acc_ref[:D, :] / sums).T
        for g in range(G):
            o_ref[0, :, g * D:(g + 1) * D] = out[g * tq:(g + 1) * tq].astype(o_ref.dtype)

    qf = qs.astype(F32)
    qn2 = _dot_nt(jnp.ones((8, D), BF16), (qf * qf).astype(BF16))[:1]
    first_pass(jnp.sqrt(qn2 * kn_ref[:1, :1]))
    sums = l_ref[...]
    trusted = (jnp.min(sums) >= FAST_SUM_MIN) & (jnp.max(sums) <= FAST_SUM_MAX)

    @pl.when(trusted)
    def _():
        write_out(sums)

    @pl.when(jnp.logical_not(trusted))
    def _():
        running_max_pass()
        write_out(acc_ref[D:D + 1, :])


def flash_attention(q, k, vt, *, n_heads, n_kv, tq, tk):
    b, s, _ = q.shape
    D = HEAD_DIM
    G = n_heads // n_kv
    assert s % (2 * tk) == 0 and s // tk >= 2
    kern = functools.partial(_flash_kernel, G=G, tq=tq, tk=tk, nk=s // tk)
    return pl.pallas_call(
        kern,
        grid=(b, n_kv, s // tq),
        in_specs=[
            pl.BlockSpec((1, tq, G * D), lambda bi, kv, qi: (bi, qi, kv)),
            pl.BlockSpec((1, s, D), lambda bi, kv, qi: (bi, 0, kv)),
            pl.BlockSpec((1, 1, D + ONES_ROWS, s), lambda bi, kv, qi: (bi, kv, 0, 0)),
        ],
        out_specs=pl.BlockSpec((1, tq, G * D), lambda bi, kv, qi: (bi, qi, kv)),
        out_shape=jax.ShapeDtypeStruct((b, s, n_heads * D), BF16),
        scratch_shapes=[pltpu.VMEM((D + ONES_ROWS, G * tq), F32), pltpu.VMEM((8, D), F32), pltpu.VMEM((2, tk, G * tq), BF16),
                        pltpu.VMEM((1, G * tq), F32)],
        compiler_params=_params(("parallel", "parallel", "arbitrary")),
        name="flash_attention",
    )(q, k, vt)


def _rope_angles(pos, dim, theta):
    inv = theta ** (-jnp.arange(0, dim, 2, dtype=jnp.float32) / dim)
    return pos.astype(jnp.float32)[:, None] * inv[None, :]


def _retention_tables(s):
    ang = _rope_angles(jnp.arange(s), RET_DK, RET_THETA)
    c, sn = jnp.cos(ang), jnp.sin(ang)
    return jnp.concatenate([c, c], axis=-1), jnp.concatenate([-sn, sn], axis=-1)


def _axial_tables(s):
    rows = s // GRID_W
    half = HEAD_DIM // 2
    ar = _rope_angles(jnp.arange(rows), half, AX_THETA)
    ac = _rope_angles(jnp.arange(GRID_W), half, AX_THETA)
    cr, sr = (jnp.repeat(f(ar), GRID_W, axis=0) for f in (jnp.cos, jnp.sin))
    ccol, scol = (jnp.tile(f(ac), (rows, 1)) for f in (jnp.cos, jnp.sin))
    cc = jnp.concatenate([cr, ccol, cr, ccol], axis=-1)
    ss = jnp.concatenate([-sr, -scol, sr, scol], axis=-1)
    return cc, ss


def kernel(x, norm_mix, norm_mlp, w_in_even, w_out_even, ret_decay_logit, ret_norm, swa_q_norm,
           swa_k_norm, swa_sink, t5_table, w_in_odd, w_out_odd, ax_q_norm, ax_k_norm, w_mlp_up,
           w_mlp_down):
    b, s, d = x.shape
    t = b * s
    depth = norm_mix.shape[0]
    ret_heads = ret_decay_logit.shape[-1]
    ret_q = ret_heads * RET_DK
    ret_v = ret_heads * RET_DV
    swa_heads = swa_sink.shape[-1]
    swa_q = swa_heads * HEAD_DIM
    swa_kv = SWA_KV_HEADS * HEAD_DIM
    ax_q = w_out_odd.shape[1]
    ax_heads = ax_q // HEAD_DIM
    ax_kv = AX_KV_HEADS * HEAD_DIM

    x2 = x.reshape(t, d)
    ret_tabs = _retention_tables(s)
    ax_tabs = _axial_tables(s)

    for layer in range(depth):
        i = layer // 2
        if layer % 2 == 0:
            proj = norm_proj(x2, norm_mix[layer], w_in_even[i].astype(BF16), tm=DENSE_ROWS)
            proj = proj.reshape(b, s, -1)
            ya = retention(proj, ret_decay_logit[i], ret_norm[i], *ret_tabs, n_heads=ret_heads,
                           q_off=0, k_off=ret_q, v_off=2 * ret_q, g_off=2 * ret_q + ret_v, cs=16)
            off = 2 * ret_q + 2 * ret_v
            yb = window_attention(proj, swa_sink[i], t5_table, swa_q_norm[i], swa_k_norm[i],
                                  n_heads=swa_heads, n_kv=SWA_KV_HEADS, q_off=off,
                                  k_off=off + swa_q, v_off=off + swa_q + swa_kv)
            acts, w_out = [ya.reshape(t, -1), yb.reshape(t, -1)], w_out_even
        else:
            qp, kp, vt = axial_proj(x2, norm_mix[layer], w_in_odd[i].astype(BF16), ax_tabs,
                                    ax_q_norm[i], ax_k_norm[i], seq=s, n_heads=ax_heads,
                                    n_kv=AX_KV_HEADS, tm=DENSE_ROWS)
            y = flash_attention(qp.reshape(b, s, -1), kp.reshape(b, s, -1), vt, n_heads=ax_heads,
                                n_kv=AX_KV_HEADS, tq=1024, tk=512)
            acts, w_out = [y.reshape(t, -1)], w_out_odd
        x2 = out_mlp(x2, acts, w_out, i, norm_mlp[layer], w_mlp_up, w_mlp_down, layer,
                     tm=DENSE_ROWS)
    return x2.reshape(b, s, d)
```

```python
import functools
import math

import jax
import jax.numpy as jnp
from jax import lax
from jax.experimental import pallas as pl
from jax.experimental.pallas import tpu as pltpu

F32 = jnp.float32
BF16 = jnp.bfloat16

EPS = 1e-6
NEG_INF = -1e30
LOG2E = math.log2(math.e)
HEAD_DIM = 128
BLOCK = 128
GRID_W = 64
RET_DK = 128
RET_DV = 256
RET_CHUNK = 128
RET_THETA = 10000.0
SWA_KV_HEADS = 2
SWA_QBLOCKS = 2
WINDOW = 128
T5_BUCKETS = 32
T5_MAX_DIST = 128
AX_KV_HEADS = 2
AX_THETA = 10000.0

VMEM_LIMIT_BYTES = 56 * 1024 * 1024
DENSE_ROWS = 512


def _params(semantics):
    return pltpu.CompilerParams(dimension_semantics=semantics, vmem_limit_bytes=VMEM_LIMIT_BYTES)


def _rms(x, gain):
    ms = jnp.mean(x * x, axis=-1, keepdims=True)
    return x * lax.rsqrt(ms + EPS) * gain


def _dot(a, b):
    return jnp.dot(a, b, preferred_element_type=F32)


def _dot_nt(a, b):
    return lax.dot_general(a, b, (((1,), (1,)), ((), ())), preferred_element_type=F32)


def _dot_tn(a, b):
    return lax.dot_general(a, b, (((0,), (0,)), ((), ())), preferred_element_type=F32)


def _resident(shape, row=0):
    idx = (row,) + (0,) * (len(shape) - 1)
    return pl.BlockSpec(shape, lambda i: idx, pipeline_mode=pl.Buffered(1))


STAGE_BYTES = 2 * 1024 * 1024


def _stage_rows(rows, cols):
    r = 1
    while 2 * r * cols * 4 <= STAGE_BYTES and rows % (2 * r) == 0:
        r *= 2
    return r


def _stream_cast(src_hbm, index, dst_ref, stage_ref, sem_ref):
    rows = stage_ref.shape[1]
    n = src_hbm.shape[1] // rows

    def copy(c):
        return pltpu.make_async_copy(src_hbm.at[index, pl.ds(c * rows, rows)], stage_ref.at[c % 2],
                                     sem_ref.at[c % 2])

    copy(0).start()
    for c in range(n):
        if c + 1 < n:
            copy(c + 1).start()
        copy(c).wait()
        dst_ref[c * rows:(c + 1) * rows, :] = stage_ref[c % 2].astype(dst_ref.dtype)


def _norm_proj_kernel(x_ref, g_ref, w_hbm, o_ref, w_ref, stage_ref, sem_ref, *, w_index):
    @pl.when(pl.program_id(0) == 0)
    def _():
        _stream_cast(w_hbm, w_index, w_ref, stage_ref, sem_ref)

    h = _rms(x_ref[...], g_ref[...]).astype(BF16)
    o_ref[...] = _dot(h, w_ref[...]).astype(o_ref.dtype)


def norm_proj(x2, gain, w, w_index, *, tm):
    t, d = x2.shape
    n = w.shape[2]
    return pl.pallas_call(
        functools.partial(_norm_proj_kernel, w_index=w_index),
        grid=(t // tm,),
        in_specs=[pl.BlockSpec((tm, d), lambda i: (i, 0)), _resident((1, d)),
                  pl.BlockSpec(memory_space=pl.ANY)],
        out_specs=pl.BlockSpec((tm, n), lambda i: (i, 0)),
        out_shape=jax.ShapeDtypeStruct((t, n), BF16),
        scratch_shapes=[pltpu.VMEM((d, n), BF16), pltpu.VMEM((2, _stage_rows(d, n), n), F32),
                        pltpu.SemaphoreType.DMA((2,))],
        compiler_params=_params(("arbitrary",)),
        name="norm_proj",
    )(x2, gain.reshape(1, d), w)


def _out_mlp_kernel(x_ref, *refs, n_act, wo_index, mlp_index):
    act_refs = refs[:n_act]
    (wo_hbm, g_ref, wu_hbm, wd_hbm, o_ref,
     wo_ref, wu_ref, wd_ref, stage_wide, stage_narrow, sem_wide, sem_narrow) = refs[n_act:]

    @pl.when(pl.program_id(0) == 0)
    def _():
        _stream_cast(wo_hbm, wo_index, wo_ref, stage_narrow, sem_narrow)
        _stream_cast(wu_hbm, mlp_index, wu_ref, stage_wide, sem_wide)
        _stream_cast(wd_hbm, mlp_index, wd_ref, stage_narrow, sem_narrow)

    y = x_ref[...]
    k = act_refs[0].shape[1]
    for s, a_ref in enumerate(act_refs):
        y = y + _dot(a_ref[...], wo_ref[s * k:(s + 1) * k, :])
    h = _rms(y, g_ref[...]).astype(BF16)
    a = jnp.square(jnp.maximum(_dot(h, wu_ref[...]), 0.0)).astype(BF16)
    o_ref[...] = y + _dot(a, wd_ref[...])


def out_mlp(x2, acts, w_out, wo_index, gain, wu, wd, mlp_index, *, tm):
    t, d = x2.shape
    ff = wu.shape[2]
    k = acts[0].shape[1]
    assert all(a.shape[1] == k for a in acts) and len(acts) * k == w_out.shape[1]
    rows_wide, rows_narrow = _stage_rows(d, ff), _stage_rows(math.gcd(ff, w_out.shape[1]), d)

    def row_tile(width):
        return pl.BlockSpec((tm, width), lambda i: (i, 0))

    hbm = pl.BlockSpec(memory_space=pl.ANY)
    return pl.pallas_call(
        functools.partial(_out_mlp_kernel, n_act=len(acts), wo_index=wo_index, mlp_index=mlp_index),
        grid=(t // tm,),
        in_specs=[row_tile(d)] + [row_tile(k) for _ in acts] + [hbm, _resident((1, d)), hbm, hbm],
        out_specs=row_tile(d),
        out_shape=jax.ShapeDtypeStruct((t, d), F32),
        scratch_shapes=[
            pltpu.VMEM(w_out.shape[1:], BF16), pltpu.VMEM((d, ff), BF16), pltpu.VMEM((ff, d), BF16),
            pltpu.VMEM((2, rows_wide, ff), F32), pltpu.VMEM((2, rows_narrow, d), F32),
            pltpu.SemaphoreType.DMA((2,)), pltpu.SemaphoreType.DMA((2,)),
        ],
        compiler_params=_params(("arbitrary",)),
        name="out_mlp",
    )(x2, *acts, w_out, gain.reshape(1, d), wu, wd)


def _log_sigmoid(x):
    return -(jnp.maximum(-x, 0.0) + jnp.log1p(jnp.exp(-jnp.abs(x))))


def _ret_kernel(dl_ref, q_ref, k_ref, v_ref, g_ref, cos_ref, sin_ref, gn_ref, o_ref,
                sb_ref, sf_ref, cb_ref, dec_ref, *, cs):
    C = RET_CHUNK
    h = pl.program_id(1)
    phase = pl.program_id(2)
    t = pl.program_id(3)
    nsteps = pl.num_programs(3)

    lgf_w = _log_sigmoid(jnp.full((1, RET_DV), dl_ref[0, h], F32))
    lgb_w = _log_sigmoid(jnp.full((1, RET_DV), dl_ref[1, h], F32))

    @pl.when((phase == 0) & (t == 0))
    def _():
        row = lax.broadcasted_iota(jnp.int32, (C, C), 0).astype(F32)
        col = lax.broadcasted_iota(jnp.int32, (C, C), 1).astype(F32)
        lgf = lgf_w[:, :C]
        lgb = lgb_w[:, :C]
        diff = row - col
        dec_ref[0] = jnp.where(diff >= 0, jnp.exp(jnp.maximum(diff, 0.0) * lgf),
                               jnp.exp(jnp.maximum(-diff, 0.0) * lgb))
        dec_ref[1] = jnp.exp((row + 1.0) * lgf)
        dec_ref[2] = jnp.exp((C - 1.0 - row) * lgf)
        dec_ref[3] = jnp.exp((C - row) * lgb)
        dec_ref[4] = jnp.exp(row * lgb)
        sb_ref[...] = jnp.zeros_like(sb_ref)
        sf_ref[...] = jnp.zeros_like(sf_ref)

    def rope(a, rows):
        return a * cos_ref[rows, :] + pltpu.roll(a, RET_DK // 2, 1) * sin_ref[rows, :]

    @pl.when(phase == 0)
    def _():
        sb = sb_ref[...]
        cdec = jnp.exp(C * lgb_w)
        for ci in reversed(range(cs)):
            rows = slice(ci * C, (ci + 1) * C)
            qr = rope(q_ref[0, rows, :].astype(F32), rows)
            kr = rope(k_ref[0, rows, :].astype(F32), rows) * (RET_DK ** -0.5)
            c = (nsteps - 1 - t) * cs + ci
            cb_ref[c] = _dot((qr * dec_ref[3]).astype(BF16), sb.astype(BF16))
            sb = cdec * sb + _dot_tn((kr * dec_ref[4]).astype(BF16), v_ref[0, rows, :])
        sb_ref[...] = sb

    @pl.when(phase == 1)
    def _():
        sf = sf_ref[...]
        cdec = jnp.exp(C * lgf_w)
        for ci in range(cs):
            rows = slice(ci * C, (ci + 1) * C)
            qr = rope(q_ref[0, rows, :].astype(F32), rows)
            kr = rope(k_ref[0, rows, :].astype(F32), rows) * (RET_DK ** -0.5)
            v = v_ref[0, rows, :]
            s = _dot_nt(qr.astype(BF16), kr.astype(BF16))
            inner = _dot((s * dec_ref[0]).astype(BF16), v)
            cross = _dot((qr * dec_ref[1]).astype(BF16), sf.astype(BF16))
            sf = cdec * sf + _dot_tn((kr * dec_ref[2]).astype(BF16), v)
            tot = inner + cross + cb_ref[t * cs + ci]
            y = _rms(tot, gn_ref[...])
            g = g_ref[0, rows, :].astype(F32)
            o_ref[0, rows, :] = (g * jax.nn.sigmoid(g) * y).astype(o_ref.dtype)
        sf_ref[...] = sf


def retention(proj, decay_logit, gn_gain, cos2, sin2, *, n_heads, q_off, k_off, v_off, g_off, cs):
    b, s, _ = proj.shape
    C = RET_CHUNK
    nc = s // C
    ts = cs * C
    nsteps = nc // cs
    ret_v = n_heads * RET_DV

    def step(p, t):
        return jnp.where(p == 0, nsteps - 1 - t, t)

    def spec(width, off):
        base = off // width
        return pl.BlockSpec((1, ts, width), lambda bi, h, p, t: (bi, step(p, t), base + h))

    kern = functools.partial(_ret_kernel, cs=cs)
    return pl.pallas_call(
        kern,
        grid=(b, n_heads, 2, nsteps),
        in_specs=[
            pl.BlockSpec(memory_space=pltpu.SMEM),
            spec(RET_DK, q_off),
            spec(RET_DK, k_off),
            spec(RET_DV, v_off),
            spec(RET_DV, g_off),
            pl.BlockSpec((ts, RET_DK), lambda bi, h, p, t: (step(p, t), 0)),
            pl.BlockSpec((ts, RET_DK), lambda bi, h, p, t: (step(p, t), 0)),
            pl.BlockSpec((1, RET_DV), lambda bi, h, p, t: (0, h)),
        ],
        out_specs=pl.BlockSpec((1, ts, RET_DV), lambda bi, h, p, t: (bi, jnp.where(p == 0, 0, t), h)),
        out_shape=jax.ShapeDtypeStruct((b, s, ret_v), BF16),
        scratch_shapes=[
            pltpu.VMEM((RET_DK, RET_DV), F32),
            pltpu.VMEM((RET_DK, RET_DV), F32),
            pltpu.VMEM((nc, C, RET_DV), F32),
            pltpu.VMEM((5, C, C), F32),
        ],
        compiler_params=_params(("parallel", "parallel", "arbitrary", "arbitrary")),
        name="retention",
    )(decay_logit, proj, proj, proj, proj, cos2, sin2, gn_gain.reshape(1, ret_v))


def _swa_kernel(sink_ref, q_ref, kp_ref, km_ref, kn_ref, vp_ref, vm_ref, vn_ref,
                t5_ref, qg_ref, kg_ref, o_ref, bias_ref, *, n_heads, n_kv, nb):
    step = pl.program_id(1)
    D = HEAD_DIM
    G = n_heads // n_kv
    KW = 3 * BLOCK
    QW = G * BLOCK
    lane_head = lax.broadcasted_iota(jnp.int32, (1, QW), 1) // BLOCK

    def per_head_row(ref, kv, *idx):
        row = jnp.full((1, QW), ref[(*idx, kv * G)], F32)
        for g in range(1, G):
            row = jnp.where(lane_head == g, ref[(*idx, kv * G + g)], row)
        return row

    @pl.when((pl.program_id(0) == 0) & (step == 0))
    def _():
        j = lax.broadcasted_iota(jnp.int32, (KW, QW), 0)
        r = lax.broadcasted_iota(jnp.int32, (KW, QW), 1) % BLOCK
        in_band = jnp.abs(j - BLOCK - r) <= WINDOW
        for kv in range(n_kv):
            bias = jnp.where(in_band, t5_ref[kv] * LOG2E, NEG_INF)
            for e in range(4):
                keep = (j >= BLOCK if e & 1 else True) & (j < 2 * BLOCK if e & 2 else True)
                bias_ref[e, kv] = bias if e == 0 else jnp.where(keep, bias, NEG_INF)

    qg = qg_ref[...] * (D ** -0.5 * LOG2E)
    kg = kg_ref[...]
    for kv in range(n_kv):
        sl = slice(kv * D, (kv + 1) * D)
        kall = jnp.concatenate([kp_ref[0][:, sl], km_ref[0][:, sl], kn_ref[0][:, sl]], axis=0)
        kall = _rms(kall.astype(F32), kg).astype(BF16)
        vall = jnp.concatenate([vp_ref[0][:, sl], vm_ref[0][:, sl], vn_ref[0][:, sl]], axis=0)
        sink = per_head_row(sink_ref, kv) * LOG2E
        for qb in range(SWA_QBLOCKS):
            i = step * SWA_QBLOCKS + qb
            edge = (i == 0).astype(jnp.int32) + 2 * (i == nb - 1).astype(jnp.int32)
            rows = slice(qb * BLOCK, (qb + 1) * BLOCK)
            k3 = kall[qb * BLOCK:qb * BLOCK + KW]
            v3 = vall[qb * BLOCK:qb * BLOCK + KW]
            qs = jnp.concatenate(
                [_rms(q_ref[0, rows, (kv * G + g) * D:(kv * G + g + 1) * D].astype(F32), qg).astype(BF16)
                 for g in range(G)], axis=0)
            st = _dot_nt(k3, qs) + bias_ref[edge, kv]
            m = jnp.maximum(jnp.max(st, axis=0, keepdims=True), sink)
            p = jnp.exp2(st - m)
            denom = jnp.sum(p, axis=0, keepdims=True) + jnp.exp2(sink - m)
            o = (_dot_tn(v3, p.astype(BF16)) / denom).T
            for g in range(G):
                h = kv * G + g
                o_ref[0, rows, h * D:(h + 1) * D] = o[g * BLOCK:(g + 1) * BLOCK].astype(o_ref.dtype)


def _t5_bucket(rel):
    nb = T5_BUCKETS // 2
    max_exact = nb // 2
    ret = jnp.where(rel > 0, nb, 0)
    n = jnp.abs(rel)
    nf = jnp.maximum(n, 1).astype(jnp.float32)
    large = max_exact + (jnp.log(nf / max_exact) / math.log(T5_MAX_DIST / max_exact)
                         * (nb - max_exact)).astype(jnp.int32)
    large = jnp.minimum(large, nb - 1)
    return ret + jnp.where(n < max_exact, n, large)


def window_attention(proj, sink, t5_table, q_gain, k_gain, *, n_heads, n_kv, q_off, k_off, v_off):
    b, s, _ = proj.shape
    D = HEAD_DIM
    nb = s // BLOCK
    qw = n_heads * D
    kw = n_kv * D
    G = n_heads // n_kv
    rr = jnp.arange(BLOCK)
    jj = jnp.arange(3 * BLOCK)
    t5 = t5_table.astype(F32)[_t5_bucket(jj[None, :] - BLOCK - rr[:, None])]
    t5 = t5.transpose(2, 1, 0).reshape(n_kv, G, 3 * BLOCK, BLOCK)
    t5 = t5.transpose(0, 2, 1, 3).reshape(n_kv, 3 * BLOCK, G * BLOCK)

    QB = SWA_QBLOCKS
    assert nb % QB == 0

    def kv_specs(off):
        base = off // kw
        return [pl.BlockSpec((1, BLOCK, kw), lambda bi, j: (bi, jnp.maximum(j * QB - 1, 0), base)),
                pl.BlockSpec((1, QB * BLOCK, kw), lambda bi, j: (bi, j, base)),
                pl.BlockSpec((1, BLOCK, kw), lambda bi, j: (bi, jnp.minimum(j * QB + QB, nb - 1), base))]

    kern = functools.partial(_swa_kernel, n_heads=n_heads, n_kv=n_kv, nb=nb)
    return pl.pallas_call(
        kern,
        grid=(b, nb // QB),
        in_specs=[
            pl.BlockSpec(memory_space=pltpu.SMEM),
            pl.BlockSpec((1, QB * BLOCK, qw), lambda bi, j: (bi, j, q_off // qw)),
            *kv_specs(k_off), *kv_specs(v_off),
            pl.BlockSpec((n_kv, 3 * BLOCK, G * BLOCK), lambda bi, j: (0, 0, 0)),
            pl.BlockSpec((1, D), lambda bi, j: (0, 0)),
            pl.BlockSpec((1, D), lambda bi, j: (0, 0)),
        ],
        out_specs=pl.BlockSpec((1, QB * BLOCK, qw), lambda bi, j: (bi, j, 0)),
        out_shape=jax.ShapeDtypeStruct((b, s, qw), BF16),
        scratch_shapes=[pltpu.VMEM((4, n_kv, 3 * BLOCK, G * BLOCK), F32)],
        compiler_params=_params(("arbitrary", "arbitrary")),
        name="window_attention",
    )(sink, proj, proj, proj, proj, proj, proj, proj, t5,
      q_gain.reshape(1, D), k_gain.reshape(1, D))


def _axial_head_perm():
    q4 = HEAD_DIM // 4
    return [blk * q4 + j for blk in (0, 2, 1, 3) for j in range(q4)]


def _axial_rope(a, cc, ss):
    return a * cc + pltpu.roll(a, HEAD_DIM // 2, 1) * ss


ONES_ROWS = 16


def _axial_proj_kernel(x_ref, g_ref, w_ref, cc_ref, ss_ref, qg_ref, kg_ref, qo_ref, ko_ref,
                       vt_ref, proj_ref, *, n_heads, n_kv):
    D = HEAD_DIM
    i = pl.program_id(0)

    @pl.when(i == 0)
    def _():
        proj_ref[1] = jnp.zeros(proj_ref.shape[1:], proj_ref.dtype)

    proj = proj_ref[(i + 1) % 2]
    cc, ss = cc_ref[...], ss_ref[...]
    qg = qg_ref[...] * (D ** -0.5 * LOG2E)
    kg = kg_ref[...]
    for hh in range(n_heads):
        hs = slice(hh * D, (hh + 1) * D)
        qo_ref[:, hs] = _axial_rope(_rms(proj[:, hs], qg), cc, ss).astype(qo_ref.dtype)
    k_off, v_off = n_heads * D, (n_heads + n_kv) * D
    for hh in range(n_kv):
        hs = slice(hh * D, (hh + 1) * D)
        k = _rms(proj[:, k_off + hh * D:k_off + (hh + 1) * D], kg)
        ko_ref[:, hs] = _axial_rope(k, cc, ss).astype(ko_ref.dtype)
        vt_ref[0, hh, :D, :] = proj[:, v_off + hh * D:v_off + (hh + 1) * D].T.astype(vt_ref.dtype)
        vt_ref[0, hh, D:, :] = jnp.ones((ONES_ROWS, vt_ref.shape[-1]), vt_ref.dtype)

    h = _rms(x_ref[...], g_ref[...]).astype(BF16)
    proj_ref[i % 2] = _dot(h, w_ref[...])


def axial_proj(x2, gain, w, tables, q_gain, k_gain, *, seq, n_heads, n_kv, tm):
    t, d = x2.shape
    D = HEAD_DIM
    qw, kw = n_heads * D, n_kv * D
    assert w.shape[1] == qw + 2 * kw and seq % tm == 0
    perm = jnp.asarray(_axial_head_perm())
    q4 = D // 4
    w4 = w[:, :qw + kw].reshape(d, n_heads + n_kv, 4, q4)
    wqk = jnp.stack([w4[:, :, blk] for blk in (0, 2, 1, 3)], axis=2)
    w = jnp.concatenate([wqk.reshape(d, qw + kw), w[:, qw + kw:]], axis=1).astype(BF16)
    q_gain, k_gain = q_gain[perm], k_gain[perm]
    spb = seq // tm
    n = t // tm
    kern = functools.partial(_axial_proj_kernel, n_heads=n_heads, n_kv=n_kv)

    def done(i):
        return jnp.maximum(i - 1, 0)

    tab = pl.BlockSpec((tm, D), lambda i: (done(i) % spb, 0))
    return pl.pallas_call(
        kern,
        grid=(n + 1,),
        in_specs=[pl.BlockSpec((tm, d), lambda i: (jnp.minimum(i, n - 1), 0)), _resident((1, d)),
                  _resident((d, w.shape[1])), tab, tab, _resident((1, D)), _resident((1, D))],
        out_specs=[
            pl.BlockSpec((tm, qw), lambda i: (done(i), 0)),
            pl.BlockSpec((tm, kw), lambda i: (done(i), 0)),
            pl.BlockSpec((1, n_kv, D + ONES_ROWS, tm), lambda i: (done(i) // spb, 0, 0, done(i) % spb)),
        ],
        out_shape=[jax.ShapeDtypeStruct((t, qw), BF16), jax.ShapeDtypeStruct((t, kw), BF16),
                   jax.ShapeDtypeStruct((t // seq, n_kv, D + ONES_ROWS, seq), BF16)],
        scratch_shapes=[pltpu.VMEM((2, tm, w.shape[1]), F32)],
        compiler_params=_params(("arbitrary",)),
        name="axial_proj",
    )(x2, gain.reshape(1, d), w, *tables, q_gain.reshape(1, D), k_gain.reshape(1, D))


FAST_SUM_MIN = 2.0 ** -80
FAST_SUM_MAX = 2.0 ** 100


def _flash_kernel(q_ref, k_ref, vt_ref, o_ref, acc_ref, kn_ref, p_ref, l_ref, *, G, tq, tk, nk):
    D = HEAD_DIM
    R = G * tq
    qs = jnp.concatenate([q_ref[0][:, g * D:(g + 1) * D] for g in range(G)], axis=0)

    @pl.when(pl.program_id(2) == 0)
    def _():
        def kbody(c, mx):
            start = pl.multiple_of(c * tk, tk)
            kc = k_ref[0, pl.ds(start, tk), :].astype(F32)
            return jnp.maximum(mx, jnp.max(jnp.sum(kc * kc, axis=-1, keepdims=True), axis=0, keepdims=True))
        kn2 = lax.fori_loop(0, nk, kbody, jnp.zeros((1, 1), F32))
        kn_ref[...] = jnp.broadcast_to(kn2, kn_ref.shape)

    def scores(c):
        start = pl.multiple_of(c * tk, tk)
        return _dot_nt(k_ref[0, pl.ds(start, tk), :], qs)

    def vt_chunk(c, rows=D + ONES_ROWS):
        return vt_ref[0, 0, :rows, pl.ds(pl.multiple_of(c * tk, tk), tk)]


    def produce_fixed(c, slot, shift):
        p = jnp.exp2(scores(c) - shift)
        l_ref[...] += jnp.sum(p, axis=0, keepdims=True)
        p_ref[slot] = p.astype(BF16)

    def consume_fixed(c, slot, shift):
        acc_ref[:D, :] += _dot(vt_chunk(c, D), p_ref[slot])

    def first_pass(shift):
        def body(c2, carry):
            c = 2 * c2
            produce_fixed(c + 1, 1, shift)
            consume_fixed(c, 0, shift)
            produce_fixed(c + 2, 0, shift)
            consume_fixed(c + 1, 1, shift)
            return carry

        acc_ref[...] = jnp.zeros_like(acc_ref)
        l_ref[...] = jnp.zeros_like(l_ref)
        produce_fixed(0, 0, shift)
        lax.fori_loop(0, nk // 2 - 1, body, 0)
        produce_fixed(nk - 1, 1, shift)
        consume_fixed(nk - 2, 0, shift)
        consume_fixed(nk - 1, 1, shift)

    def running_max_pass():
        def body(c, m_old):
            st = scores(c)
            m_new = jnp.maximum(m_old, jnp.max(st, axis=0, keepdims=True))
            alpha = jnp.exp2(m_old - m_new)
            p = jnp.exp2((st - m_new).astype(BF16))
            acc_ref[...] = alpha * acc_ref[...] + _dot(vt_chunk(c), p)
            return m_new

        acc_ref[...] = jnp.zeros_like(acc_ref)
        lax.fori_loop(0, nk, body, jnp.full((1, R), -jnp.inf, F32))

    def write_out(sums):
        out = (acc_ref[:D, :] / sums).T
        for g in range(G):
            o_ref[0, :, g * D:(g + 1) * D] = out[g * tq:(g + 1) * tq].astype(o_ref.dtype)

    qf = qs.astype(F32)
    qn2 = _dot_nt(jnp.ones((8, D), BF16), (qf * qf).astype(BF16))[:1]
    first_pass(jnp.sqrt(qn2 * kn_ref[:1, :1]))
    sums = l_ref[...]
    trusted = (jnp.min(sums) >= FAST_SUM_MIN) & (jnp.max(sums) <= FAST_SUM_MAX)

    @pl.when(trusted)
    def _():
        write_out(sums)

    @pl.when(jnp.logical_not(trusted))
    def _():
        running_max_pass()
        write_out(acc_ref[D:D + 1, :])


def flash_attention(q, k, vt, *, n_heads, n_kv, tq, tk):
    b, s, _ = q.shape
    D = HEAD_DIM
    G = n_heads // n_kv
    assert s % (2 * tk) == 0 and s // tk >= 2
    kern = functools.partial(_flash_kernel, G=G, tq=tq, tk=tk, nk=s // tk)
    return pl.pallas_call(
        kern,
        grid=(b, n_kv, s // tq),
        in_specs=[
            pl.BlockSpec((1, tq, G * D), lambda bi, kv, qi: (bi, qi, kv)),
            pl.BlockSpec((1, s, D), lambda bi, kv, qi: (bi, 0, kv)),
            pl.BlockSpec((1, 1, D + ONES_ROWS, s), lambda bi, kv, qi: (bi, kv, 0, 0)),
        ],
        out_specs=pl.BlockSpec((1, tq, G * D), lambda bi, kv, qi: (bi, qi, kv)),
        out_shape=jax.ShapeDtypeStruct((b, s, n_heads * D), BF16),
        scratch_shapes=[pltpu.VMEM((D + ONES_ROWS, G * tq), F32), pltpu.VMEM((8, D), F32), pltpu.VMEM((2, tk, G * tq), BF16),
                        pltpu.VMEM((1, G * tq), F32)],
        compiler_params=_params(("parallel", "parallel", "arbitrary")),
        name="flash_attention",
    )(q, k, vt)


def _rope_angles(pos, dim, theta):
    inv = theta ** (-jnp.arange(0, dim, 2, dtype=jnp.float32) / dim)
    return pos.astype(jnp.float32)[:, None] * inv[None, :]


def _retention_tables(s):
    ang = _rope_angles(jnp.arange(s), RET_DK, RET_THETA)
    c, sn = jnp.cos(ang), jnp.sin(ang)
    sign = jnp.where(jnp.arange(RET_DK) < RET_DK // 2, -1.0, 1.0).astype(F32)
    return jnp.tile(c, (1, 2)), jnp.tile(sn, (1, 2)) * sign


def _axial_tables(s):
    rows = s // GRID_W
    half = HEAD_DIM // 2
    ar = _rope_angles(jnp.arange(rows), half, AX_THETA)
    ac = _rope_angles(jnp.arange(GRID_W), half, AX_THETA)
    cr, sr = (jnp.repeat(f(ar), GRID_W, axis=0) for f in (jnp.cos, jnp.sin))
    ccol, scol = (jnp.tile(f(ac), (rows, 1)) for f in (jnp.cos, jnp.sin))
    sign = jnp.where(jnp.arange(HEAD_DIM) < HEAD_DIM // 2, -1.0, 1.0).astype(F32)
    cc = jnp.tile(jnp.concatenate([cr, ccol], axis=-1), (1, 2))
    ss = jnp.tile(jnp.concatenate([sr, scol], axis=-1), (1, 2)) * sign
    return cc, ss


def kernel(x, norm_mix, norm_mlp, w_in_even, w_out_even, ret_decay_logit, ret_norm, swa_q_norm,
           swa_k_norm, swa_sink, t5_table, w_in_odd, w_out_odd, ax_q_norm, ax_k_norm, w_mlp_up,
           w_mlp_down):
    b, s, d = x.shape
    t = b * s
    depth = norm_mix.shape[0]
    ret_heads = ret_decay_logit.shape[-1]
    ret_q = ret_heads * RET_DK
    ret_v = ret_heads * RET_DV
    swa_heads = swa_sink.shape[-1]
    swa_q = swa_heads * HEAD_DIM
    swa_kv = SWA_KV_HEADS * HEAD_DIM
    ax_q = w_out_odd.shape[1]
    ax_heads = ax_q // HEAD_DIM
    ax_kv = AX_KV_HEADS * HEAD_DIM

    x2 = x.reshape(t, d)
    ret_tabs = _retention_tables(s)
    ax_tabs = _axial_tables(s)

    for layer in range(depth):
        i = layer // 2
        if layer % 2 == 0:
            proj = norm_proj(x2, norm_mix[layer], w_in_even, i, tm=DENSE_ROWS)
            proj = proj.reshape(b, s, -1)
            ya = retention(proj, ret_decay_logit[i], ret_norm[i], *ret_tabs, n_heads=ret_heads,
                           q_off=0, k_off=ret_q, v_off=2 * ret_q, g_off=2 * ret_q + ret_v, cs=16)
            off = 2 * ret_q + 2 * ret_v
            yb = window_attention(proj, swa_sink[i], t5_table, swa_q_norm[i], swa_k_norm[i],
                                  n_heads=swa_heads, n_kv=SWA_KV_HEADS, q_off=off,
                                  k_off=off + swa_q, v_off=off + swa_q + swa_kv)
            acts, w_out = [ya.reshape(t, -1), yb.reshape(t, -1)], w_out_even
        else:
            qp, kp, vt = axial_proj(x2, norm_mix[layer], w_in_odd[i], ax_tabs,
                                    ax_q_norm[i], ax_k_norm[i], seq=s, n_heads=ax_heads,
                                    n_kv=AX_KV_HEADS, tm=DENSE_ROWS)
            y = flash_attention(qp.reshape(b, s, -1), kp.reshape(b, s, -1), vt, n_heads=ax_heads,
                                n_kv=AX_KV_HEADS, tq=1024, tk=512)
            acts, w_out = [y.reshape(t, -1)], w_out_odd
        x2 = out_mlp(x2, acts, w_out, i, norm_mlp[layer], w_mlp_up, w_mlp_down, layer,
                     tm=DENSE_ROWS)
    return x2.reshape(b, s, d)
```

```python
import functools
import math

import jax
import jax.numpy as jnp
from jax import lax
from jax.experimental import pallas as pl
from jax.experimental.pallas import tpu as pltpu

F32 = jnp.float32
BF16 = jnp.bfloat16

EPS = 1e-6
NEG_INF = -1e30
LOG2E = math.log2(math.e)
HEAD_DIM = 128
BLOCK = 128
GRID_W = 64
RET_DK = 128
RET_DV = 256
RET_CHUNK = 128
RET_THETA = 10000.0
SWA_KV_HEADS = 2
SWA_QBLOCKS = 2
WINDOW = 128
T5_BUCKETS = 32
T5_MAX_DIST = 128
AX_KV_HEADS = 2
AX_THETA = 10000.0

VMEM_LIMIT_BYTES = 56 * 1024 * 1024
DENSE_ROWS = 512


def _params(semantics):
    return pltpu.CompilerParams(dimension_semantics=semantics, vmem_limit_bytes=VMEM_LIMIT_BYTES)


def _rms(x, gain):
    ms = jnp.mean(x * x, axis=-1, keepdims=True)
    return x * lax.rsqrt(ms + EPS) * gain


def _dot(a, b):
    return jnp.dot(a, b, preferred_element_type=F32)


def _dot_nt(a, b):
    return lax.dot_general(a, b, (((1,), (1,)), ((), ())), preferred_element_type=F32)


def _dot_tn(a, b):
    return lax.dot_general(a, b, (((0,), (0,)), ((), ())), preferred_element_type=F32)


def _resident(shape, row=0):
    idx = (row,) + (0,) * (len(shape) - 1)
    return pl.BlockSpec(shape, lambda i: idx, pipeline_mode=pl.Buffered(1))


STAGE_BYTES = 2 * 1024 * 1024


def _stage_rows(rows, cols):
    r = 1
    while 2 * r * cols * 4 <= STAGE_BYTES and rows % (2 * r) == 0:
        r *= 2
    return r


def _stream_cast(src_hbm, index, dst_ref, stage_ref, sem_ref):
    rows = stage_ref.shape[1]
    n = src_hbm.shape[1] // rows

    def copy(c):
        return pltpu.make_async_copy(src_hbm.at[index, pl.ds(c * rows, rows)], stage_ref.at[c % 2],
                                     sem_ref.at[c % 2])

    copy(0).start()
    for c in range(n):
        if c + 1 < n:
            copy(c + 1).start()
        copy(c).wait()
        dst_ref[c * rows:(c + 1) * rows, :] = stage_ref[c % 2].astype(dst_ref.dtype)


def _norm_proj_kernel(x_ref, g_ref, w_hbm, o_ref, w_ref, stage_ref, sem_ref, *, w_index):
    @pl.when(pl.program_id(0) == 0)
    def _():
        _stream_cast(w_hbm, w_index, w_ref, stage_ref, sem_ref)

    h = _rms(x_ref[...], g_ref[...]).astype(BF16)
    o_ref[...] = _dot(h, w_ref[...]).astype(o_ref.dtype)


def norm_proj(x2, gain, w, w_index, *, tm):
    t, d = x2.shape
    n = w.shape[2]
    return pl.pallas_call(
        functools.partial(_norm_proj_kernel, w_index=w_index),
        grid=(t // tm,),
        in_specs=[pl.BlockSpec((tm, d), lambda i: (i, 0)), _resident((1, d)),
                  pl.BlockSpec(memory_space=pl.ANY)],
        out_specs=pl.BlockSpec((tm, n), lambda i: (i, 0)),
        out_shape=jax.ShapeDtypeStruct((t, n), BF16),
        scratch_shapes=[pltpu.VMEM((d, n), BF16), pltpu.VMEM((2, _stage_rows(d, n), n), F32),
                        pltpu.SemaphoreType.DMA((2,))],
        compiler_params=_params(("arbitrary",)),
        name="norm_proj",
    )(x2, gain.reshape(1, d), w)


def _out_mlp_kernel(x_ref, *refs, n_act, wo_index, mlp_index):
    act_refs = refs[:n_act]
    (wo_hbm, g_ref, wu_hbm, wd_hbm, o_ref,
     wo_ref, wu_ref, wd_ref, stage_wide, stage_narrow, sem_wide, sem_narrow) = refs[n_act:]

    @pl.when(pl.program_id(0) == 0)
    def _():
        _stream_cast(wo_hbm, wo_index, wo_ref, stage_narrow, sem_narrow)
        _stream_cast(wu_hbm, mlp_index, wu_ref, stage_wide, sem_wide)
        _stream_cast(wd_hbm, mlp_index, wd_ref, stage_narrow, sem_narrow)

    y = x_ref[...]
    k = act_refs[0].shape[1]
    for s, a_ref in enumerate(act_refs):
        y = y + _dot(a_ref[...], wo_ref[s * k:(s + 1) * k, :])
    h = _rms(y, g_ref[...]).astype(BF16)
    a = jnp.square(jnp.maximum(_dot(h, wu_ref[...]), 0.0)).astype(BF16)
    o_ref[...] = y + _dot(a, wd_ref[...])


def out_mlp(x2, acts, w_out, wo_index, gain, wu, wd, mlp_index, *, tm):
    t, d = x2.shape
    ff = wu.shape[2]
    k = acts[0].shape[1]
    assert all(a.shape[1] == k for a in acts) and len(acts) * k == w_out.shape[1]
    rows_wide, rows_narrow = _stage_rows(d, ff), _stage_rows(math.gcd(ff, w_out.shape[1]), d)

    def row_tile(width):
        return pl.BlockSpec((tm, width), lambda i: (i, 0))

    hbm = pl.BlockSpec(memory_space=pl.ANY)
    return pl.pallas_call(
        functools.partial(_out_mlp_kernel, n_act=len(acts), wo_index=wo_index, mlp_index=mlp_index),
        grid=(t // tm,),
        in_specs=[row_tile(d)] + [row_tile(k) for _ in acts] + [hbm, _resident((1, d)), hbm, hbm],
        out_specs=row_tile(d),
        out_shape=jax.ShapeDtypeStruct((t, d), F32),
        scratch_shapes=[
            pltpu.VMEM(w_out.shape[1:], BF16), pltpu.VMEM((d, ff), BF16), pltpu.VMEM((ff, d), BF16),
            pltpu.VMEM((2, rows_wide, ff), F32), pltpu.VMEM((2, rows_narrow, d), F32),
            pltpu.SemaphoreType.DMA((2,)), pltpu.SemaphoreType.DMA((2,)),
        ],
        compiler_params=_params(("arbitrary",)),
        name="out_mlp",
    )(x2, *acts, w_out, gain.reshape(1, d), wu, wd)


def _log_sigmoid(x):
    return -(jnp.maximum(-x, 0.0) + jnp.log1p(jnp.exp(-jnp.abs(x))))


def _ret_kernel(dl_ref, q_ref, k_ref, v_ref, g_ref, cos_ref, sin_ref, gn_ref, o_ref,
                sb_ref, sf_ref, cb_ref, dec_ref, *, cs):
    C = RET_CHUNK
    h = pl.program_id(1)
    phase = pl.program_id(2)
    t = pl.program_id(3)
    nsteps = pl.num_programs(3)

    lgf_w = _log_sigmoid(jnp.full((1, RET_DV), dl_ref[0, h], F32))
    lgb_w = _log_sigmoid(jnp.full((1, RET_DV), dl_ref[1, h], F32))

    @pl.when((phase == 0) & (t == 0))
    def _():
        row = lax.broadcasted_iota(jnp.int32, (C, C), 0).astype(F32)
        col = lax.broadcasted_iota(jnp.int32, (C, C), 1).astype(F32)
        lgf = lgf_w[:, :C]
        lgb = lgb_w[:, :C]
        diff = row - col
        dec_ref[0] = jnp.where(diff >= 0, jnp.exp(jnp.maximum(diff, 0.0) * lgf),
                               jnp.exp(jnp.maximum(-diff, 0.0) * lgb))
        dec_ref[1] = jnp.exp((row + 1.0) * lgf)
        dec_ref[2] = jnp.exp((C - 1.0 - row) * lgf)
        dec_ref[3] = jnp.exp((C - row) * lgb)
        dec_ref[4] = jnp.exp(row * lgb)
        sb_ref[...] = jnp.zeros_like(sb_ref)
        sf_ref[...] = jnp.zeros_like(sf_ref)

    def rope(a, rows):
        return a * cos_ref[rows, :] + pltpu.roll(a, RET_DK // 2, 1) * sin_ref[rows, :]

    @pl.when(phase == 0)
    def _():
        sb = sb_ref[...]
        cdec = jnp.exp(C * lgb_w)
        for ci in reversed(range(cs)):
            rows = slice(ci * C, (ci + 1) * C)
            qr = rope(q_ref[0, rows, :].astype(F32), rows)
            kr = rope(k_ref[0, rows, :].astype(F32), rows) * (RET_DK ** -0.5)
            c = (nsteps - 1 - t) * cs + ci
            cb_ref[c] = _dot((qr * dec_ref[3]).astype(BF16), sb.astype(BF16))
            sb = cdec * sb + _dot_tn((kr * dec_ref[4]).astype(BF16), v_ref[0, rows, :])
        sb_ref[...] = sb

    @pl.when(phase == 1)
    def _():
        sf = sf_ref[...]
        cdec = jnp.exp(C * lgf_w)
        for ci in range(cs):
            rows = slice(ci * C, (ci + 1) * C)
            qr = rope(q_ref[0, rows, :].astype(F32), rows)
            kr = rope(k_ref[0, rows, :].astype(F32), rows) * (RET_DK ** -0.5)
            v = v_ref[0, rows, :]
            s = _dot_nt(qr.astype(BF16), kr.astype(BF16))
            inner = _dot((s * dec_ref[0]).astype(BF16), v)
            cross = _dot((qr * dec_ref[1]).astype(BF16), sf.astype(BF16))
            sf = cdec * sf + _dot_tn((kr * dec_ref[2]).astype(BF16), v)
            tot = inner + cross + cb_ref[t * cs + ci]
            y = _rms(tot, gn_ref[...])
            g = g_ref[0, rows, :].astype(F32)
            o_ref[0, rows, :] = (g * jax.nn.sigmoid(g) * y).astype(o_ref.dtype)
        sf_ref[...] = sf


def retention(proj, decay_logit, gn_gain, cos2, sin2, *, n_heads, q_off, k_off, v_off, g_off, cs):
    b, s, _ = proj.shape
    C = RET_CHUNK
    nc = s // C
    ts = cs * C
    nsteps = nc // cs
    ret_v = n_heads * RET_DV

    def step(p, t):
        return jnp.where(p == 0, nsteps - 1 - t, t)

    def spec(width, off):
        base = off // width
        return pl.BlockSpec((1, ts, width), lambda bi, h, p, t: (bi, step(p, t), base + h))

    kern = functools.partial(_ret_kernel, cs=cs)
    return pl.pallas_call(
        kern,
        grid=(b, n_heads, 2, nsteps),
        in_specs=[
            pl.BlockSpec(memory_space=pltpu.SMEM),
            spec(RET_DK, q_off),
            spec(RET_DK, k_off),
            spec(RET_DV, v_off),
            spec(RET_DV, g_off),
            pl.BlockSpec((ts, RET_DK), lambda bi, h, p, t: (step(p, t), 0)),
            pl.BlockSpec((ts, RET_DK), lambda bi, h, p, t: (step(p, t), 0)),
            pl.BlockSpec((1, RET_DV), lambda bi, h, p, t: (0, h)),
        ],
        out_specs=pl.BlockSpec((1, ts, RET_DV), lambda bi, h, p, t: (bi, jnp.where(p == 0, 0, t), h)),
        out_shape=jax.ShapeDtypeStruct((b, s, ret_v), BF16),
        scratch_shapes=[
            pltpu.VMEM((RET_DK, RET_DV), F32),
            pltpu.VMEM((RET_DK, RET_DV), F32),
            pltpu.VMEM((nc, C, RET_DV), F32),
            pltpu.VMEM((5, C, C), F32),
        ],
        compiler_params=_params(("parallel", "parallel", "arbitrary", "arbitrary")),
        name="retention",
    )(decay_logit, proj, proj, proj, proj, cos2, sin2, gn_gain.reshape(1, ret_v))


def _swa_kernel(sink_ref, q_ref, kp_ref, km_ref, kn_ref, vp_ref, vm_ref, vn_ref,
                t5_ref, qg_ref, kg_ref, o_ref, bias_ref, *, n_heads, n_kv, nb):
    step = pl.program_id(1)
    D = HEAD_DIM
    G = n_heads // n_kv
    KW = 3 * BLOCK
    QW = G * BLOCK
    lane_head = lax.broadcasted_iota(jnp.int32, (1, QW), 1) // BLOCK

    def per_head_row(ref, kv, *idx):
        row = jnp.full((1, QW), ref[(*idx, kv * G)], F32)
        for g in range(1, G):
            row = jnp.where(lane_head == g, ref[(*idx, kv * G + g)], row)
        return row

    @pl.when((pl.program_id(0) == 0) & (step == 0))
    def _():
        j = lax.broadcasted_iota(jnp.int32, (KW, QW), 0)
        r = lax.broadcasted_iota(jnp.int32, (KW, QW), 1) % BLOCK
        in_band = jnp.abs(j - BLOCK - r) <= WINDOW
        for kv in range(n_kv):
            bias = jnp.where(in_band, t5_ref[kv] * LOG2E, NEG_INF)
            for e in range(4):
                keep = (j >= BLOCK if e & 1 else True) & (j < 2 * BLOCK if e & 2 else True)
                bias_ref[e, kv] = bias if e == 0 else jnp.where(keep, bias, NEG_INF)

    qg = qg_ref[...] * (D ** -0.5 * LOG2E)
    kg = kg_ref[...]
    for kv in range(n_kv):
        sl = slice(kv * D, (kv + 1) * D)
        kall = jnp.concatenate([kp_ref[0][:, sl], km_ref[0][:, sl], kn_ref[0][:, sl]], axis=0)
        kall = _rms(kall.astype(F32), kg).astype(BF16)
        vall = jnp.concatenate([vp_ref[0][:, sl], vm_ref[0][:, sl], vn_ref[0][:, sl]], axis=0)
        sink = per_head_row(sink_ref, kv) * LOG2E
        for qb in range(SWA_QBLOCKS):
            i = step * SWA_QBLOCKS + qb
            edge = (i == 0).astype(jnp.int32) + 2 * (i == nb - 1).astype(jnp.int32)
            rows = slice(qb * BLOCK, (qb + 1) * BLOCK)
            k3 = kall[qb * BLOCK:qb * BLOCK + KW]
            v3 = vall[qb * BLOCK:qb * BLOCK + KW]
            qs = jnp.concatenate(
                [_rms(q_ref[0, rows, (kv * G + g) * D:(kv * G + g + 1) * D].astype(F32), qg).astype(BF16)
                 for g in range(G)], axis=0)
            st = _dot_nt(k3, qs) + bias_ref[edge, kv]
            m = jnp.maximum(jnp.max(st, axis=0, keepdims=True), sink)
            p = jnp.exp2(st - m)
            denom = jnp.sum(p, axis=0, keepdims=True) + jnp.exp2(sink - m)
            o = (_dot_tn(v3, p.astype(BF16)) / denom).T
            for g in range(G):
                h = kv * G + g
                o_ref[0, rows, h * D:(h + 1) * D] = o[g * BLOCK:(g + 1) * BLOCK].astype(o_ref.dtype)


def _t5_bucket(rel):
    nb = T5_BUCKETS // 2
    max_exact = nb // 2
    ret = jnp.where(rel > 0, nb, 0)
    n = jnp.abs(rel)
    nf = jnp.maximum(n, 1).astype(jnp.float32)
    large = max_exact + (jnp.log(nf / max_exact) / math.log(T5_MAX_DIST / max_exact)
                         * (nb - max_exact)).astype(jnp.int32)
    large = jnp.minimum(large, nb - 1)
    return ret + jnp.where(n < max_exact, n, large)


def window_attention(proj, sink, t5_table, q_gain, k_gain, *, n_heads, n_kv, q_off, k_off, v_off):
    b, s, _ = proj.shape
    D = HEAD_DIM
    nb = s // BLOCK
    qw = n_heads * D
    kw = n_kv * D
    G = n_heads // n_kv
    n, L = 3 * BLOCK, 4 * BLOCK
    rel = jnp.arange(-(2 * BLOCK - 1), 2 * BLOCK)
    tbl = t5_table.astype(F32)[_t5_bucket(rel)]
    v = jnp.pad(tbl[::-1].T, ((0, 0), (0, 1)))
    skew = jnp.tile(v, (1, n + 1))[:, :n * (L + 1)].reshape(n_heads, n, L + 1)[:, :, :BLOCK]
    t5 = skew[:, ::-1, :]
    t5 = t5.reshape(n_kv, G, n, BLOCK).transpose(0, 2, 1, 3).reshape(n_kv, n, G * BLOCK)

    QB = SWA_QBLOCKS
    assert nb % QB == 0

    def kv_specs(off):
        base = off // kw
        return [pl.BlockSpec((1, BLOCK, kw), lambda bi, j: (bi, jnp.maximum(j * QB - 1, 0), base)),
                pl.BlockSpec((1, QB * BLOCK, kw), lambda bi, j: (bi, j, base)),
                pl.BlockSpec((1, BLOCK, kw), lambda bi, j: (bi, jnp.minimum(j * QB + QB, nb - 1), base))]

    kern = functools.partial(_swa_kernel, n_heads=n_heads, n_kv=n_kv, nb=nb)
    return pl.pallas_call(
        kern,
        grid=(b, nb // QB),
        in_specs=[
            pl.BlockSpec(memory_space=pltpu.SMEM),
            pl.BlockSpec((1, QB * BLOCK, qw), lambda bi, j: (bi, j, q_off // qw)),
            *kv_specs(k_off), *kv_specs(v_off),
            pl.BlockSpec((n_kv, 3 * BLOCK, G * BLOCK), lambda bi, j: (0, 0, 0)),
            pl.BlockSpec((1, D), lambda bi, j: (0, 0)),
            pl.BlockSpec((1, D), lambda bi, j: (0, 0)),
        ],
        out_specs=pl.BlockSpec((1, QB * BLOCK, qw), lambda bi, j: (bi, j, 0)),
        out_shape=jax.ShapeDtypeStruct((b, s, qw), BF16),
        scratch_shapes=[pltpu.VMEM((4, n_kv, 3 * BLOCK, G * BLOCK), F32)],
        compiler_params=_params(("arbitrary", "arbitrary")),
        name="window_attention",
    )(sink, proj, proj, proj, proj, proj, proj, proj, t5,
      q_gain.reshape(1, D), k_gain.reshape(1, D))


def _axial_head_perm():
    q4 = HEAD_DIM // 4
    return [blk * q4 + j for blk in (0, 2, 1, 3) for j in range(q4)]


def _axial_rope(a, cc, ss):
    return a * cc + pltpu.roll(a, HEAD_DIM // 2, 1) * ss


ONES_ROWS = 16


def _axial_proj_kernel(x_ref, g_ref, w_ref, cc_ref, ss_ref, qg_ref, kg_ref, qo_ref, ko_ref,
                       vt_ref, proj_ref, *, n_heads, n_kv):
    D = HEAD_DIM
    i = pl.program_id(0)

    @pl.when(i == 0)
    def _():
        proj_ref[1] = jnp.zeros(proj_ref.shape[1:], proj_ref.dtype)

    proj = proj_ref[(i + 1) % 2]
    cc, ss = cc_ref[...], ss_ref[...]
    qg = qg_ref[...] * (D ** -0.5 * LOG2E)
    kg = kg_ref[...]
    for hh in range(n_heads):
        hs = slice(hh * D, (hh + 1) * D)
        qo_ref[:, hs] = _axial_rope(_rms(proj[:, hs], qg), cc, ss).astype(qo_ref.dtype)
    k_off, v_off = n_heads * D, (n_heads + n_kv) * D
    for hh in range(n_kv):
        hs = slice(hh * D, (hh + 1) * D)
        k = _rms(proj[:, k_off + hh * D:k_off + (hh + 1) * D], kg)
        ko_ref[:, hs] = _axial_rope(k, cc, ss).astype(ko_ref.dtype)
        vt_ref[0, hh, :D, :] = proj[:, v_off + hh * D:v_off + (hh + 1) * D].T.astype(vt_ref.dtype)
        vt_ref[0, hh, D:, :] = jnp.ones((ONES_ROWS, vt_ref.shape[-1]), vt_ref.dtype)

    h = _rms(x_ref[...], g_ref[...]).astype(BF16)
    proj_ref[i % 2] = _dot(h, w_ref[...])


def axial_proj(x2, gain, w, tables, q_gain, k_gain, *, seq, n_heads, n_kv, tm):
    t, d = x2.shape
    D = HEAD_DIM
    qw, kw = n_heads * D, n_kv * D
    assert w.shape[1] == qw + 2 * kw and seq % tm == 0
    perm = jnp.asarray(_axial_head_perm())
    q4 = D // 4
    w4 = w[:, :qw + kw].reshape(d, n_heads + n_kv, 4, q4)
    wqk = jnp.stack([w4[:, :, blk] for blk in (0, 2, 1, 3)], axis=2)
    w = jnp.concatenate([wqk.reshape(d, qw + kw), w[:, qw + kw:]], axis=1).astype(BF16)
    q_gain, k_gain = q_gain[perm], k_gain[perm]
    spb = seq // tm
    n = t // tm
    kern = functools.partial(_axial_proj_kernel, n_heads=n_heads, n_kv=n_kv)

    def done(i):
        return jnp.maximum(i - 1, 0)

    tab = pl.BlockSpec((tm, D), lambda i: (done(i) % spb, 0))
    return pl.pallas_call(
        kern,
        grid=(n + 1,),
        in_specs=[pl.BlockSpec((tm, d), lambda i: (jnp.minimum(i, n - 1), 0)), _resident((1, d)),
                  _resident((d, w.shape[1])), tab, tab, _resident((1, D)), _resident((1, D))],
        out_specs=[
            pl.BlockSpec((tm, qw), lambda i: (done(i), 0)),
            pl.BlockSpec((tm, kw), lambda i: (done(i), 0)),
            pl.BlockSpec((1, n_kv, D + ONES_ROWS, tm), lambda i: (done(i) // spb, 0, 0, done(i) % spb)),
        ],
        out_shape=[jax.ShapeDtypeStruct((t, qw), BF16), jax.ShapeDtypeStruct((t, kw), BF16),
                   jax.ShapeDtypeStruct((t // seq, n_kv, D + ONES_ROWS, seq), BF16)],
        scratch_shapes=[pltpu.VMEM((2, tm, w.shape[1]), F32)],
        compiler_params=_params(("arbitrary",)),
        name="axial_proj",
    )(x2, gain.reshape(1, d), w, *tables, q_gain.reshape(1, D), k_gain.reshape(1, D))


FAST_SUM_MIN = 2.0 ** -80
FAST_SUM_MAX = 2.0 ** 100


def _flash_kernel(q_ref, k_ref, vt_ref, o_ref, acc_ref, kn_ref, p_ref, l_ref, *, G, tq, tk, nk):
    D = HEAD_DIM
    R = G * tq
    qs = jnp.concatenate([q_ref[0][:, g * D:(g + 1) * D] for g in range(G)], axis=0)

    @pl.when(pl.program_id(2) == 0)
    def _():
        def kbody(c, mx):
            start = pl.multiple_of(c * tk, tk)
            kc = k_ref[0, pl.ds(start, tk), :].astype(F32)
            return jnp.maximum(mx, jnp.max(jnp.sum(kc * kc, axis=-1, keepdims=True), axis=0, keepdims=True))
        kn2 = lax.fori_loop(0, nk, kbody, jnp.zeros((1, 1), F32))
        kn_ref[...] = jnp.broadcast_to(kn2, kn_ref.shape)

    def scores(c):
        start = pl.multiple_of(c * tk, tk)
        return _dot_nt(k_ref[0, pl.ds(start, tk), :], qs)

    def vt_chunk(c, rows=D + ONES_ROWS):
        return vt_ref[0, 0, :rows, pl.ds(pl.multiple_of(c * tk, tk), tk)]


    def produce_fixed(c, slot, shift):
        p = jnp.exp2(scores(c) - shift)
        l_ref[...] += jnp.sum(p, axis=0, keepdims=True)
        p_ref[slot] = p.astype(BF16)

    def consume_fixed(c, slot, shift):
        acc_ref[:D, :] += _dot(vt_chunk(c, D), p_ref[slot])

    def first_pass(shift):
        def body(c2, carry):
            c = 2 * c2
            produce_fixed(c + 1, 1, shift)
            consume_fixed(c, 0, shift)
            produce_fixed(c + 2, 0, shift)
            consume_fixed(c + 1, 1, shift)
            return carry

        acc_ref[...] = jnp.zeros_like(acc_ref)
        l_ref[...] = jnp.zeros_like(l_ref)
        produce_fixed(0, 0, shift)
        lax.fori_loop(0, nk // 2 - 1, body, 0)
        produce_fixed(nk - 1, 1, shift)
        consume_fixed(nk - 2, 0, shift)
        consume_fixed(nk - 1, 1, shift)

    def running_max_pass():
        def body(c, m_old):
            st = scores(c)
            m_new = jnp.maximum(m_old, jnp.max(st, axis=0, keepdims=True))
            alpha = jnp.exp2(m_old - m_new)
            p = jnp.exp2((st - m_new).astype(BF16))
            acc_ref[...] = alpha * acc_ref[...] + _dot(vt_chunk(c), p)
            return m_new

        acc_ref[...] = jnp.zeros_like(acc_ref)
        lax.fori_loop(0, nk, body, jnp.full((1, R), -jnp.inf, F32))

    def write_out(sums):
        out = (acc_ref[:D, :] / sums).T
        for g in range(G):
            o_ref[0, :, g * D:(g + 1) * D] = out[g * tq:(g + 1) * tq].astype(o_ref.dtype)

    qf = qs.astype(F32)
    qn2 = _dot_nt(jnp.ones((8, D), BF16), (qf * qf).astype(BF16))[:1]
    first_pass(jnp.sqrt(qn2 * kn_ref[:1, :1]))
    sums = l_ref[...]
    trusted = (jnp.min(sums) >= FAST_SUM_MIN) & (jnp.max(sums) <= FAST_SUM_MAX)

    @pl.when(trusted)
    def _():
        write_out(sums)

    @pl.when(jnp.logical_not(trusted))
    def _():
        running_max_pass()
        write_out(acc_ref[D:D + 1, :])


def flash_attention(q, k, vt, *, n_heads, n_kv, tq, tk):
    b, s, _ = q.shape
    D = HEAD_DIM
    G = n_heads // n_kv
    assert s % (2 * tk) == 0 and s // tk >= 2
    kern = functools.partial(_flash_kernel, G=G, tq=tq, tk=tk, nk=s // tk)
    return pl.pallas_call(
        kern,
        grid=(b, n_kv, s // tq),
        in_specs=[
            pl.BlockSpec((1, tq, G * D), lambda bi, kv, qi: (bi, qi, kv)),
            pl.BlockSpec((1, s, D), lambda bi, kv, qi: (bi, 0, kv)),
            pl.BlockSpec((1, 1, D + ONES_ROWS, s), lambda bi, kv, qi: (bi, kv, 0, 0)),
        ],
        out_specs=pl.BlockSpec((1, tq, G * D), lambda bi, kv, qi: (bi, qi, kv)),
        out_shape=jax.ShapeDtypeStruct((b, s, n_heads * D), BF16),
        scratch_shapes=[pltpu.VMEM((D + ONES_ROWS, G * tq), F32), pltpu.VMEM((8, D), F32), pltpu.VMEM((2, tk, G * tq), BF16),
                        pltpu.VMEM((1, G * tq), F32)],
        compiler_params=_params(("parallel", "parallel", "arbitrary")),
        name="flash_attention",
    )(q, k, vt)


def _rope_angles(pos, dim, theta):
    inv = theta ** (-jnp.arange(0, dim, 2, dtype=jnp.float32) / dim)
    return pos.astype(jnp.float32)[:, None] * inv[None, :]


def _retention_tables(s):
    ang = _rope_angles(jnp.arange(s), RET_DK, RET_THETA)
    c, sn = jnp.cos(ang), jnp.sin(ang)
    sign = jnp.where(jnp.arange(RET_DK) < RET_DK // 2, -1.0, 1.0).astype(F32)
    return jnp.tile(c, (1, 2)), jnp.tile(sn, (1, 2)) * sign


def _axial_tables(s):
    rows = s // GRID_W
    half = HEAD_DIM // 2
    ar = _rope_angles(jnp.arange(rows), half, AX_THETA)
    ac = _rope_angles(jnp.arange(GRID_W), half, AX_THETA)
    cr, sr = (jnp.repeat(f(ar), GRID_W, axis=0) for f in (jnp.cos, jnp.sin))
    ccol, scol = (jnp.tile(f(ac), (rows, 1)) for f in (jnp.cos, jnp.sin))
    sign = jnp.where(jnp.arange(HEAD_DIM) < HEAD_DIM // 2, -1.0, 1.0).astype(F32)
    cc = jnp.tile(jnp.concatenate([cr, ccol], axis=-1), (1, 2))
    ss = jnp.tile(jnp.concatenate([sr, scol], axis=-1), (1, 2)) * sign
    return cc, ss


def kernel(x, norm_mix, norm_mlp, w_in_even, w_out_even, ret_decay_logit, ret_norm, swa_q_norm,
           swa_k_norm, swa_sink, t5_table, w_in_odd, w_out_odd, ax_q_norm, ax_k_norm, w_mlp_up,
           w_mlp_down):
    b, s, d = x.shape
    t = b * s
    depth = norm_mix.shape[0]
    ret_heads = ret_decay_logit.shape[-1]
    ret_q = ret_heads * RET_DK
    ret_v = ret_heads * RET_DV
    swa_heads = swa_sink.shape[-1]
    swa_q = swa_heads * HEAD_DIM
    swa_kv = SWA_KV_HEADS * HEAD_DIM
    ax_q = w_out_odd.shape[1]
    ax_heads = ax_q // HEAD_DIM
    ax_kv = AX_KV_HEADS * HEAD_DIM

    x2 = x.reshape(t, d)
    ret_tabs = _retention_tables(s)
    ax_tabs = _axial_tables(s)

    for layer in range(depth):
        i = layer // 2
        if layer % 2 == 0:
            proj = norm_proj(x2, norm_mix[layer], w_in_even, i, tm=DENSE_ROWS)
            proj = proj.reshape(b, s, -1)
            ya = retention(proj, ret_decay_logit[i], ret_norm[i], *ret_tabs, n_heads=ret_heads,
                           q_off=0, k_off=ret_q, v_off=2 * ret_q, g_off=2 * ret_q + ret_v, cs=16)
            off = 2 * ret_q + 2 * ret_v
            yb = window_attention(proj, swa_sink[i], t5_table, swa_q_norm[i], swa_k_norm[i],
                                  n_heads=swa_heads, n_kv=SWA_KV_HEADS, q_off=off,
                                  k_off=off + swa_q, v_off=off + swa_q + swa_kv)
            acts, w_out = [ya.reshape(t, -1), yb.reshape(t, -1)], w_out_even
        else:
            qp, kp, vt = axial_proj(x2, norm_mix[layer], w_in_odd[i], ax_tabs,
                                    ax_q_norm[i], ax_k_norm[i], seq=s, n_heads=ax_heads,
                                    n_kv=AX_KV_HEADS, tm=DENSE_ROWS)
            y = flash_attention(qp.reshape(b, s, -1), kp.reshape(b, s, -1), vt, n_heads=ax_heads,
                                n_kv=AX_KV_HEADS, tq=1024, tk=512)
            acts, w_out = [y.reshape(t, -1)], w_out_odd
        x2 = out_mlp(x2, acts, w_out, i, norm_mlp[layer], w_mlp_up, w_mlp_down, layer,
                     tm=DENSE_ROWS)
    return x2.reshape(b, s, d)
```

```python
import functools
import math

import jax
import jax.numpy as jnp
from jax import lax
from jax.experimental import pallas as pl
from jax.experimental.pallas import tpu as pltpu

F32 = jnp.float32
BF16 = jnp.bfloat16

EPS = 1e-6
NEG_INF = -1e30
LOG2E = math.log2(math.e)
HEAD_DIM = 128
BLOCK = 128
GRID_W = 64
RET_DK = 128
RET_DV = 256
RET_CHUNK = 128
RET_THETA = 10000.0
SWA_KV_HEADS = 2
SWA_QBLOCKS = 4
WINDOW = 128
T5_BUCKETS = 32
T5_MAX_DIST = 128
AX_KV_HEADS = 2
AX_THETA = 10000.0

VMEM_LIMIT_BYTES = 56 * 1024 * 1024
DENSE_ROWS = 512


def _params(semantics):
    return pltpu.CompilerParams(dimension_semantics=semantics, vmem_limit_bytes=VMEM_LIMIT_BYTES)


def _rms(x, gain):
    ms = jnp.mean(x * x, axis=-1, keepdims=True)
    return x * lax.rsqrt(ms + EPS) * gain


def _dot(a, b):
    return jnp.dot(a, b, preferred_element_type=F32)


def _dot_nt(a, b):
    return lax.dot_general(a, b, (((1,), (1,)), ((), ())), preferred_element_type=F32)


def _dot_tn(a, b):
    return lax.dot_general(a, b, (((0,), (0,)), ((), ())), preferred_element_type=F32)


def _resident(shape, row=0):
    idx = (row,) + (0,) * (len(shape) - 1)
    return pl.BlockSpec(shape, lambda i: idx, pipeline_mode=pl.Buffered(1))


STAGE_BYTES = 2 * 1024 * 1024


def _stage_rows(rows, cols):
    r = 1
    while 2 * r * cols * 4 <= STAGE_BYTES and rows % (2 * r) == 0:
        r *= 2
    return r


def _stream_cast(src_hbm, index, dst_ref, stage_ref, sem_ref):
    rows = stage_ref.shape[1]
    n = src_hbm.shape[1] // rows

    def copy(c):
        return pltpu.make_async_copy(src_hbm.at[index, pl.ds(c * rows, rows)], stage_ref.at[c % 2],
                                     sem_ref.at[c % 2])

    copy(0).start()
    for c in range(n):
        if c + 1 < n:
            copy(c + 1).start()
        copy(c).wait()
        dst_ref[c * rows:(c + 1) * rows, :] = stage_ref[c % 2].astype(dst_ref.dtype)


def _norm_proj_kernel(x_ref, g_ref, w_hbm, o_ref, w_ref, stage_ref, sem_ref, *, w_index):
    @pl.when(pl.program_id(0) == 0)
    def _():
        _stream_cast(w_hbm, w_index, w_ref, stage_ref, sem_ref)

    h = _rms(x_ref[...], g_ref[...]).astype(BF16)
    o_ref[...] = _dot(h, w_ref[...]).astype(o_ref.dtype)


def norm_proj(x2, gain, w, w_index, *, tm):
    t, d = x2.shape
    n = w.shape[2]
    return pl.pallas_call(
        functools.partial(_norm_proj_kernel, w_index=w_index),
        grid=(t // tm,),
        in_specs=[pl.BlockSpec((tm, d), lambda i: (i, 0)), _resident((1, d)),
                  pl.BlockSpec(memory_space=pl.ANY)],
        out_specs=pl.BlockSpec((tm, n), lambda i: (i, 0)),
        out_shape=jax.ShapeDtypeStruct((t, n), BF16),
        scratch_shapes=[pltpu.VMEM((d, n), BF16), pltpu.VMEM((2, _stage_rows(d, n), n), F32),
                        pltpu.SemaphoreType.DMA((2,))],
        compiler_params=_params(("arbitrary",)),
        name="norm_proj",
    )(x2, gain.reshape(1, d), w)


def _out_mlp_kernel(x_ref, *refs, n_act, wo_index, mlp_index):
    act_refs = refs[:n_act]
    (wo_hbm, g_ref, wu_hbm, wd_hbm, o_ref,
     wo_ref, wu_ref, wd_ref, stage_wide, stage_narrow, sem_wide, sem_narrow) = refs[n_act:]

    @pl.when(pl.program_id(0) == 0)
    def _():
        _stream_cast(wo_hbm, wo_index, wo_ref, stage_narrow, sem_narrow)
        _stream_cast(wu_hbm, mlp_index, wu_ref, stage_wide, sem_wide)
        _stream_cast(wd_hbm, mlp_index, wd_ref, stage_narrow, sem_narrow)

    y = x_ref[...]
    k = act_refs[0].shape[1]
    for s, a_ref in enumerate(act_refs):
        y = y + _dot(a_ref[...], wo_ref[s * k:(s + 1) * k, :])
    h = _rms(y, g_ref[...]).astype(BF16)
    a = jnp.square(jnp.maximum(_dot(h, wu_ref[...]), 0.0)).astype(BF16)
    o_ref[...] = y + _dot(a, wd_ref[...])


def out_mlp(x2, acts, w_out, wo_index, gain, wu, wd, mlp_index, *, tm):
    t, d = x2.shape
    ff = wu.shape[2]
    k = acts[0].shape[1]
    assert all(a.shape[1] == k for a in acts) and len(acts) * k == w_out.shape[1]
    rows_wide, rows_narrow = _stage_rows(d, ff), _stage_rows(math.gcd(ff, w_out.shape[1]), d)

    def row_tile(width):
        return pl.BlockSpec((tm, width), lambda i: (i, 0))

    hbm = pl.BlockSpec(memory_space=pl.ANY)
    return pl.pallas_call(
        functools.partial(_out_mlp_kernel, n_act=len(acts), wo_index=wo_index, mlp_index=mlp_index),
        grid=(t // tm,),
        in_specs=[row_tile(d)] + [row_tile(k) for _ in acts] + [hbm, _resident((1, d)), hbm, hbm],
        out_specs=row_tile(d),
        out_shape=jax.ShapeDtypeStruct((t, d), F32),
        scratch_shapes=[
            pltpu.VMEM(w_out.shape[1:], BF16), pltpu.VMEM((d, ff), BF16), pltpu.VMEM((ff, d), BF16),
            pltpu.VMEM((2, rows_wide, ff), F32), pltpu.VMEM((2, rows_narrow, d), F32),
            pltpu.SemaphoreType.DMA((2,)), pltpu.SemaphoreType.DMA((2,)),
        ],
        compiler_params=_params(("arbitrary",)),
        name="out_mlp",
    )(x2, *acts, w_out, gain.reshape(1, d), wu, wd)


def _log_sigmoid(x):
    return -(jnp.maximum(-x, 0.0) + jnp.log1p(jnp.exp(-jnp.abs(x))))


def _ret_kernel(dl_ref, q_ref, k_ref, v_ref, g_ref, cos_ref, sin_ref, gn_ref, o_ref,
                sb_ref, sf_ref, cb_ref, dec_ref, *, cs):
    C = RET_CHUNK
    h = pl.program_id(1)
    phase = pl.program_id(2)
    t = pl.program_id(3)
    nsteps = pl.num_programs(3)

    lgf_w = _log_sigmoid(jnp.full((1, RET_DV), dl_ref[0, h], F32))
    lgb_w = _log_sigmoid(jnp.full((1, RET_DV), dl_ref[1, h], F32))

    @pl.when((phase == 0) & (t == 0))
    def _():
        row = lax.broadcasted_iota(jnp.int32, (C, C), 0).astype(F32)
        col = lax.broadcasted_iota(jnp.int32, (C, C), 1).astype(F32)
        lgf = lgf_w[:, :C]
        lgb = lgb_w[:, :C]
        diff = row - col
        dec_ref[0] = jnp.where(diff >= 0, jnp.exp(jnp.maximum(diff, 0.0) * lgf),
                               jnp.exp(jnp.maximum(-diff, 0.0) * lgb))
        dec_ref[1] = jnp.exp((row + 1.0) * lgf)
        dec_ref[2] = jnp.exp((C - 1.0 - row) * lgf)
        dec_ref[3] = jnp.exp((C - row) * lgb)
        dec_ref[4] = jnp.exp(row * lgb)
        sb_ref[...] = jnp.zeros_like(sb_ref)
        sf_ref[...] = jnp.zeros_like(sf_ref)

    def rope(a, rows):
        return a * cos_ref[rows, :] + pltpu.roll(a, RET_DK // 2, 1) * sin_ref[rows, :]

    @pl.when(phase == 0)
    def _():
        sb = sb_ref[...]
        cdec = jnp.exp(C * lgb_w)
        for ci in reversed(range(cs)):
            rows = slice(ci * C, (ci + 1) * C)
            qr = rope(q_ref[0, rows, :].astype(F32), rows)
            kr = rope(k_ref[0, rows, :].astype(F32), rows) * (RET_DK ** -0.5)
            c = (nsteps - 1 - t) * cs + ci
            cb_ref[c] = _dot((qr * dec_ref[3]).astype(BF16), sb.astype(BF16))
            sb = cdec * sb + _dot_tn((kr * dec_ref[4]).astype(BF16), v_ref[0, rows, :])
        sb_ref[...] = sb

    @pl.when(phase == 1)
    def _():
        sf = sf_ref[...]
        cdec = jnp.exp(C * lgf_w)
        for ci in range(cs):
            rows = slice(ci * C, (ci + 1) * C)
            qr = rope(q_ref[0, rows, :].astype(F32), rows)
            kr = rope(k_ref[0, rows, :].astype(F32), rows) * (RET_DK ** -0.5)
            v = v_ref[0, rows, :]
            s = _dot_nt(qr.astype(BF16), kr.astype(BF16))
            inner = _dot((s * dec_ref[0]).astype(BF16), v)
            cross = _dot((qr * dec_ref[1]).astype(BF16), sf.astype(BF16))
            sf = cdec * sf + _dot_tn((kr * dec_ref[2]).astype(BF16), v)
            tot = inner + cross + cb_ref[t * cs + ci]
            y = _rms(tot, gn_ref[...])
            g = g_ref[0, rows, :].astype(F32)
            o_ref[0, rows, :] = (g * jax.nn.sigmoid(g) * y).astype(o_ref.dtype)
        sf_ref[...] = sf


def retention(proj, decay_logit, gn_gain, cos2, sin2, *, n_heads, q_off, k_off, v_off, g_off, cs):
    b, s, _ = proj.shape
    C = RET_CHUNK
    nc = s // C
    ts = cs * C
    nsteps = nc // cs
    ret_v = n_heads * RET_DV

    def step(p, t):
        return jnp.where(p == 0, nsteps - 1 - t, t)

    def spec(width, off):
        base = off // width
        return pl.BlockSpec((1, ts, width), lambda bi, h, p, t: (bi, step(p, t), base + h))

    kern = functools.partial(_ret_kernel, cs=cs)
    return pl.pallas_call(
        kern,
        grid=(b, n_heads, 2, nsteps),
        in_specs=[
            pl.BlockSpec(memory_space=pltpu.SMEM),
            spec(RET_DK, q_off),
            spec(RET_DK, k_off),
            spec(RET_DV, v_off),
            spec(RET_DV, g_off),
            pl.BlockSpec((ts, RET_DK), lambda bi, h, p, t: (step(p, t), 0)),
            pl.BlockSpec((ts, RET_DK), lambda bi, h, p, t: (step(p, t), 0)),
            pl.BlockSpec((1, RET_DV), lambda bi, h, p, t: (0, h)),
        ],
        out_specs=pl.BlockSpec((1, ts, RET_DV), lambda bi, h, p, t: (bi, jnp.where(p == 0, 0, t), h)),
        out_shape=jax.ShapeDtypeStruct((b, s, ret_v), BF16),
        scratch_shapes=[
            pltpu.VMEM((RET_DK, RET_DV), F32),
            pltpu.VMEM((RET_DK, RET_DV), F32),
            pltpu.VMEM((nc, C, RET_DV), F32),
            pltpu.VMEM((5, C, C), F32),
        ],
        compiler_params=_params(("parallel", "parallel", "arbitrary", "arbitrary")),
        name="retention",
    )(decay_logit, proj, proj, proj, proj, cos2, sin2, gn_gain.reshape(1, ret_v))


def _swa_kernel(sink_ref, q_ref, kp_ref, km_ref, kn_ref, vp_ref, vm_ref, vn_ref,
                t5_ref, qg_ref, kg_ref, o_ref, bias_ref, *, n_heads, n_kv, nb):
    step = pl.program_id(1)
    D = HEAD_DIM
    G = n_heads // n_kv
    KW = 3 * BLOCK
    QW = G * BLOCK
    lane_head = lax.broadcasted_iota(jnp.int32, (1, QW), 1) // BLOCK

    def per_head_row(ref, kv, *idx):
        row = jnp.full((1, QW), ref[(*idx, kv * G)], F32)
        for g in range(1, G):
            row = jnp.where(lane_head == g, ref[(*idx, kv * G + g)], row)
        return row

    @pl.when((pl.program_id(0) == 0) & (step == 0))
    def _():
        j = lax.broadcasted_iota(jnp.int32, (KW, QW), 0)
        r = lax.broadcasted_iota(jnp.int32, (KW, QW), 1) % BLOCK
        in_band = jnp.abs(j - BLOCK - r) <= WINDOW
        for kv in range(n_kv):
            bias = jnp.where(in_band, t5_ref[kv] * LOG2E, NEG_INF)
            for e in range(4):
                keep = (j >= BLOCK if e & 1 else True) & (j < 2 * BLOCK if e & 2 else True)
                bias_ref[e, kv] = bias if e == 0 else jnp.where(keep, bias, NEG_INF)

    qg = qg_ref[...] * (D ** -0.5 * LOG2E)
    kg = kg_ref[...]
    for kv in range(n_kv):
        sl = slice(kv * D, (kv + 1) * D)
        kall = jnp.concatenate([kp_ref[0][:, sl], km_ref[0][:, sl], kn_ref[0][:, sl]], axis=0)
        kall = _rms(kall.astype(F32), kg).astype(BF16)
        vall = jnp.concatenate([vp_ref[0][:, sl], vm_ref[0][:, sl], vn_ref[0][:, sl]], axis=0)
        sink = per_head_row(sink_ref, kv) * LOG2E
        for qb in range(SWA_QBLOCKS):
            i = step * SWA_QBLOCKS + qb
            edge = (i == 0).astype(jnp.int32) + 2 * (i == nb - 1).astype(jnp.int32)
            rows = slice(qb * BLOCK, (qb + 1) * BLOCK)
            k3 = kall[qb * BLOCK:qb * BLOCK + KW]
            v3 = vall[qb * BLOCK:qb * BLOCK + KW]
            qs = jnp.concatenate(
                [_rms(q_ref[0, rows, (kv * G + g) * D:(kv * G + g + 1) * D].astype(F32), qg).astype(BF16)
                 for g in range(G)], axis=0)
            st = _dot_nt(k3, qs) + bias_ref[edge, kv]
            m = jnp.maximum(jnp.max(st, axis=0, keepdims=True), sink)
            p = jnp.exp2(st - m)
            denom = jnp.sum(p, axis=0, keepdims=True) + jnp.exp2(sink - m)
            o = (_dot_tn(v3, p.astype(BF16)) / denom).T
            for g in range(G):
                h = kv * G + g
                o_ref[0, rows, h * D:(h + 1) * D] = o[g * BLOCK:(g + 1) * BLOCK].astype(o_ref.dtype)


def _t5_bucket(rel):
    nb = T5_BUCKETS // 2
    max_exact = nb // 2
    ret = jnp.where(rel > 0, nb, 0)
    n = jnp.abs(rel)
    nf = jnp.maximum(n, 1).astype(jnp.float32)
    large = max_exact + (jnp.log(nf / max_exact) / math.log(T5_MAX_DIST / max_exact)
                         * (nb - max_exact)).astype(jnp.int32)
    large = jnp.minimum(large, nb - 1)
    return ret + jnp.where(n < max_exact, n, large)


def window_attention(proj, sink, t5_table, q_gain, k_gain, *, n_heads, n_kv, q_off, k_off, v_off):
    b, s, _ = proj.shape
    D = HEAD_DIM
    nb = s // BLOCK
    qw = n_heads * D
    kw = n_kv * D
    G = n_heads // n_kv
    n, L = 3 * BLOCK, 4 * BLOCK
    rel = jnp.arange(-(2 * BLOCK - 1), 2 * BLOCK)
    tbl = t5_table.astype(F32)[_t5_bucket(rel)]
    v = jnp.pad(tbl[::-1].T, ((0, 0), (0, 1)))
    skew = jnp.tile(v, (1, n + 1))[:, :n * (L + 1)].reshape(n_heads, n, L + 1)[:, :, :BLOCK]
    t5 = skew[:, ::-1, :]
    t5 = t5.reshape(n_kv, G, n, BLOCK).transpose(0, 2, 1, 3).reshape(n_kv, n, G * BLOCK)

    QB = SWA_QBLOCKS
    assert nb % QB == 0

    def kv_specs(off):
        base = off // kw
        return [pl.BlockSpec((1, BLOCK, kw), lambda bi, j: (bi, jnp.maximum(j * QB - 1, 0), base)),
                pl.BlockSpec((1, QB * BLOCK, kw), lambda bi, j: (bi, j, base)),
                pl.BlockSpec((1, BLOCK, kw), lambda bi, j: (bi, jnp.minimum(j * QB + QB, nb - 1), base))]

    kern = functools.partial(_swa_kernel, n_heads=n_heads, n_kv=n_kv, nb=nb)
    return pl.pallas_call(
        kern,
        grid=(b, nb // QB),
        in_specs=[
            pl.BlockSpec(memory_space=pltpu.SMEM),
            pl.BlockSpec((1, QB * BLOCK, qw), lambda bi, j: (bi, j, q_off // qw)),
            *kv_specs(k_off), *kv_specs(v_off),
            pl.BlockSpec((n_kv, 3 * BLOCK, G * BLOCK), lambda bi, j: (0, 0, 0)),
            pl.BlockSpec((1, D), lambda bi, j: (0, 0)),
            pl.BlockSpec((1, D), lambda bi, j: (0, 0)),
        ],
        out_specs=pl.BlockSpec((1, QB * BLOCK, qw), lambda bi, j: (bi, j, 0)),
        out_shape=jax.ShapeDtypeStruct((b, s, qw), BF16),
        scratch_shapes=[pltpu.VMEM((4, n_kv, 3 * BLOCK, G * BLOCK), F32)],
        compiler_params=_params(("arbitrary", "arbitrary")),
        name="window_attention",
    )(sink, proj, proj, proj, proj, proj, proj, proj, t5,
      q_gain.reshape(1, D), k_gain.reshape(1, D))


def _axial_head_perm():
    q4 = HEAD_DIM // 4
    return [blk * q4 + j for blk in (0, 2, 1, 3) for j in range(q4)]


def _axial_rope(a, cc, ss):
    return a * cc + pltpu.roll(a, HEAD_DIM // 2, 1) * ss


ONES_ROWS = 16


def _axial_proj_kernel(x_ref, g_ref, w_ref, cc_ref, ss_ref, qg_ref, kg_ref, qo_ref, ko_ref,
                       vt_ref, proj_ref, *, n_heads, n_kv):
    D = HEAD_DIM
    i = pl.program_id(0)

    @pl.when(i == 0)
    def _():
        proj_ref[1] = jnp.zeros(proj_ref.shape[1:], proj_ref.dtype)

    proj = proj_ref[(i + 1) % 2]
    cc, ss = cc_ref[...], ss_ref[...]
    qg = qg_ref[...] * (D ** -0.5 * LOG2E)
    kg = kg_ref[...]
    for hh in range(n_heads):
        hs = slice(hh * D, (hh + 1) * D)
        qo_ref[:, hs] = _axial_rope(_rms(proj[:, hs], qg), cc, ss).astype(qo_ref.dtype)
    k_off, v_off = n_heads * D, (n_heads + n_kv) * D
    for hh in range(n_kv):
        hs = slice(hh * D, (hh + 1) * D)
        k = _rms(proj[:, k_off + hh * D:k_off + (hh + 1) * D], kg)
        ko_ref[:, hs] = _axial_rope(k, cc, ss).astype(ko_ref.dtype)
        vt_ref[0, hh, :D, :] = proj[:, v_off + hh * D:v_off + (hh + 1) * D].T.astype(vt_ref.dtype)
        vt_ref[0, hh, D:, :] = jnp.ones((ONES_ROWS, vt_ref.shape[-1]), vt_ref.dtype)

    h = _rms(x_ref[...], g_ref[...]).astype(BF16)
    proj_ref[i % 2] = _dot(h, w_ref[...])


def axial_proj(x2, gain, w, tables, q_gain, k_gain, *, seq, n_heads, n_kv, tm):
    t, d = x2.shape
    D = HEAD_DIM
    qw, kw = n_heads * D, n_kv * D
    assert w.shape[1] == qw + 2 * kw and seq % tm == 0
    perm = jnp.asarray(_axial_head_perm())
    q4 = D // 4
    w4 = w[:, :qw + kw].reshape(d, n_heads + n_kv, 4, q4)
    wqk = jnp.stack([w4[:, :, blk] for blk in (0, 2, 1, 3)], axis=2)
    w = jnp.concatenate([wqk.reshape(d, qw + kw), w[:, qw + kw:]], axis=1).astype(BF16)
    q_gain, k_gain = q_gain[perm], k_gain[perm]
    spb = seq // tm
    n = t // tm
    kern = functools.partial(_axial_proj_kernel, n_heads=n_heads, n_kv=n_kv)

    def done(i):
        return jnp.maximum(i - 1, 0)

    tab = pl.BlockSpec((tm, D), lambda i: (done(i) % spb, 0))
    return pl.pallas_call(
        kern,
        grid=(n + 1,),
        in_specs=[pl.BlockSpec((tm, d), lambda i: (jnp.minimum(i, n - 1), 0)), _resident((1, d)),
                  _resident((d, w.shape[1])), tab, tab, _resident((1, D)), _resident((1, D))],
        out_specs=[
            pl.BlockSpec((tm, qw), lambda i: (done(i), 0)),
            pl.BlockSpec((tm, kw), lambda i: (done(i), 0)),
            pl.BlockSpec((1, n_kv, D + ONES_ROWS, tm), lambda i: (done(i) // spb, 0, 0, done(i) % spb)),
        ],
        out_shape=[jax.ShapeDtypeStruct((t, qw), BF16), jax.ShapeDtypeStruct((t, kw), BF16),
                   jax.ShapeDtypeStruct((t // seq, n_kv, D + ONES_ROWS, seq), BF16)],
        scratch_shapes=[pltpu.VMEM((2, tm, w.shape[1]), F32)],
        compiler_params=_params(("arbitrary",)),
        name="axial_proj",
    )(x2, gain.reshape(1, d), w, *tables, q_gain.reshape(1, D), k_gain.reshape(1, D))


FAST_SUM_MIN = 2.0 ** -80
FAST_SUM_MAX = 2.0 ** 100


def _flash_kernel(q_ref, k_ref, vt_ref, o_ref, acc_ref, kn_ref, p_ref, l_ref, *, G, tq, tk, nk):
    D = HEAD_DIM
    R = G * tq
    qs = jnp.concatenate([q_ref[0][:, g * D:(g + 1) * D] for g in range(G)], axis=0)

    @pl.when(pl.program_id(2) == 0)
    def _():
        def kbody(c, mx):
            start = pl.multiple_of(c * tk, tk)
            kc = k_ref[0, pl.ds(start, tk), :].astype(F32)
            return jnp.maximum(mx, jnp.max(jnp.sum(kc * kc, axis=-1, keepdims=True), axis=0, keepdims=True))
        kn2 = lax.fori_loop(0, nk, kbody, jnp.zeros((1, 1), F32))
        kn_ref[...] = jnp.broadcast_to(kn2, kn_ref.shape)

    def scores(c):
        start = pl.multiple_of(c * tk, tk)
        return _dot_nt(k_ref[0, pl.ds(start, tk), :], qs)

    def vt_chunk(c, rows=D + ONES_ROWS):
        return vt_ref[0, 0, :rows, pl.ds(pl.multiple_of(c * tk, tk), tk)]


    def produce_fixed(c, slot, shift):
        p = jnp.exp2(scores(c) - shift)
        l_ref[...] += jnp.sum(p, axis=0, keepdims=True)
        p_ref[slot] = p.astype(BF16)

    def consume_fixed(c, slot, shift):
        acc_ref[:D, :] += _dot(vt_chunk(c, D), p_ref[slot])

    def first_pass(shift):
        def body(c2, carry):
            c = 2 * c2
            produce_fixed(c + 1, 1, shift)
            consume_fixed(c, 0, shift)
            produce_fixed(c + 2, 0, shift)
            consume_fixed(c + 1, 1, shift)
            return carry

        acc_ref[...] = jnp.zeros_like(acc_ref)
        l_ref[...] = jnp.zeros_like(l_ref)
        produce_fixed(0, 0, shift)
        lax.fori_loop(0, nk // 2 - 1, body, 0)
        produce_fixed(nk - 1, 1, shift)
        consume_fixed(nk - 2, 0, shift)
        consume_fixed(nk - 1, 1, shift)

    def running_max_pass():
        def body(c, m_old):
            st = scores(c)
            m_new = jnp.maximum(m_old, jnp.max(st, axis=0, keepdims=True))
            alpha = jnp.exp2(m_old - m_new)
            p = jnp.exp2((st - m_new).astype(BF16))
            acc_ref[...] = alpha * acc_ref[...] + _dot(vt_chunk(c), p)
            return m_new

        acc_ref[...] = jnp.zeros_like(acc_ref)
        lax.fori_loop(0, nk, body, jnp.full((1, R), -jnp.inf, F32))

    def write_out(sums):
        out = (acc_ref[:D, :] / sums).T
        for g in range(G):
            o_ref[0, :, g * D:(g + 1) * D] = out[g * tq:(g + 1) * tq].astype(o_ref.dtype)

    qf = qs.astype(F32)
    qn2 = _dot_nt(jnp.ones((8, D), BF16), (qf * qf).astype(BF16))[:1]
    first_pass(jnp.sqrt(qn2 * kn_ref[:1, :1]))
    sums = l_ref[...]
    trusted = (jnp.min(sums) >= FAST_SUM_MIN) & (jnp.max(sums) <= FAST_SUM_MAX)

    @pl.when(trusted)
    def _():
        write_out(sums)

    @pl.when(jnp.logical_not(trusted))
    def _():
        running_max_pass()
        write_out(acc_ref[D:D + 1, :])


def flash_attention(q, k, vt, *, n_heads, n_kv, tq, tk):
    b, s, _ = q.shape
    D = HEAD_DIM
    G = n_heads // n_kv
    assert s % (2 * tk) == 0 and s // tk >= 2
    kern = functools.partial(_flash_kernel, G=G, tq=tq, tk=tk, nk=s // tk)
    return pl.pallas_call(
        kern,
        grid=(b, n_kv, s // tq),
        in_specs=[
            pl.BlockSpec((1, tq, G * D), lambda bi, kv, qi: (bi, qi, kv)),
            pl.BlockSpec((1, s, D), lambda bi, kv, qi: (bi, 0, kv)),
            pl.BlockSpec((1, 1, D + ONES_ROWS, s), lambda bi, kv, qi: (bi, kv, 0, 0)),
        ],
        out_specs=pl.BlockSpec((1, tq, G * D), lambda bi, kv, qi: (bi, qi, kv)),
        out_shape=jax.ShapeDtypeStruct((b, s, n_heads * D), BF16),
        scratch_shapes=[pltpu.VMEM((D + ONES_ROWS, G * tq), F32), pltpu.VMEM((8, D), F32), pltpu.VMEM((2, tk, G * tq), BF16),
                        pltpu.VMEM((1, G * tq), F32)],
        compiler_params=_params(("parallel", "parallel", "arbitrary")),
        name="flash_attention",
    )(q, k, vt)


def _rope_angles(pos, dim, theta):
    inv = theta ** (-jnp.arange(0, dim, 2, dtype=jnp.float32) / dim)
    return pos.astype(jnp.float32)[:, None] * inv[None, :]


def _retention_tables(s):
    ang = _rope_angles(jnp.arange(s), RET_DK, RET_THETA)
    c, sn = jnp.cos(ang), jnp.sin(ang)
    sign = jnp.where(jnp.arange(RET_DK) < RET_DK // 2, -1.0, 1.0).astype(F32)
    return jnp.tile(c, (1, 2)), jnp.tile(sn, (1, 2)) * sign


def _axial_tables(s):
    rows = s // GRID_W
    half = HEAD_DIM // 2
    ar = _rope_angles(jnp.arange(rows), half, AX_THETA)
    ac = _rope_angles(jnp.arange(GRID_W), half, AX_THETA)
    cr, sr = (jnp.repeat(f(ar), GRID_W, axis=0) for f in (jnp.cos, jnp.sin))
    ccol, scol = (jnp.tile(f(ac), (rows, 1)) for f in (jnp.cos, jnp.sin))
    sign = jnp.where(jnp.arange(HEAD_DIM) < HEAD_DIM // 2, -1.0, 1.0).astype(F32)
    cc = jnp.tile(jnp.concatenate([cr, ccol], axis=-1), (1, 2))
    ss = jnp.tile(jnp.concatenate([sr, scol], axis=-1), (1, 2)) * sign
    return cc, ss


def kernel(x, norm_mix, norm_mlp, w_in_even, w_out_even, ret_decay_logit, ret_norm, swa_q_norm,
           swa_k_norm, swa_sink, t5_table, w_in_odd, w_out_odd, ax_q_norm, ax_k_norm, w_mlp_up,
           w_mlp_down):
    b, s, d = x.shape
    t = b * s
    depth = norm_mix.shape[0]
    ret_heads = ret_decay_logit.shape[-1]
    ret_q = ret_heads * RET_DK
    ret_v = ret_heads * RET_DV
    swa_heads = swa_sink.shape[-1]
    swa_q = swa_heads * HEAD_DIM
    swa_kv = SWA_KV_HEADS * HEAD_DIM
    ax_q = w_out_odd.shape[1]
    ax_heads = ax_q // HEAD_DIM
    ax_kv = AX_KV_HEADS * HEAD_DIM

    x2 = x.reshape(t, d)
    ret_tabs = _retention_tables(s)
    ax_tabs = _axial_tables(s)

    for layer in range(depth):
        i = layer // 2
        if layer % 2 == 0:
            proj = norm_proj(x2, norm_mix[layer], w_in_even, i, tm=DENSE_ROWS)
            proj = proj.reshape(b, s, -1)
            ya = retention(proj, ret_decay_logit[i], ret_norm[i], *ret_tabs, n_heads=ret_heads,
                           q_off=0, k_off=ret_q, v_off=2 * ret_q, g_off=2 * ret_q + ret_v, cs=32)
            off = 2 * ret_q + 2 * ret_v
            yb = window_attention(proj, swa_sink[i], t5_table, swa_q_norm[i], swa_k_norm[i],
                                  n_heads=swa_heads, n_kv=SWA_KV_HEADS, q_off=off,
                                  k_off=off + swa_q, v_off=off + swa_q + swa_kv)
            acts, w_out = [ya.reshape(t, -1), yb.reshape(t, -1)], w_out_even
        else:
            qp, kp, vt = axial_proj(x2, norm_mix[layer], w_in_odd[i], ax_tabs,
                                    ax_q_norm[i], ax_k_norm[i], seq=s, n_heads=ax_heads,
                                    n_kv=AX_KV_HEADS, tm=DENSE_ROWS)
            y = flash_attention(qp.reshape(b, s, -1), kp.reshape(b, s, -1), vt, n_heads=ax_heads,
                                n_kv=AX_KV_HEADS, tq=1024, tk=512)
            acts, w_out = [y.reshape(t, -1)], w_out_odd
        x2 = out_mlp(x2, acts, w_out, i, norm_mlp[layer], w_mlp_up, w_mlp_down, layer,
                     tm=DENSE_ROWS)
    return x2.reshape(b, s, d)
```

```python
import functools
import math

import jax
import jax.numpy as jnp
from jax import lax
from jax.experimental import pallas as pl
from jax.experimental.pallas import tpu as pltpu

F32 = jnp.float32
BF16 = jnp.bfloat16

EPS = 1e-6
NEG_INF = -1e30
LOG2E = math.log2(math.e)
HEAD_DIM = 128
BLOCK = 128
GRID_W = 64
RET_DK = 128
RET_DV = 256
RET_CHUNK = 128
RET_THETA = 10000.0
SWA_KV_HEADS = 2
SWA_QBLOCKS = 4
WINDOW = 128
T5_BUCKETS = 32
T5_MAX_DIST = 128
AX_KV_HEADS = 2
AX_THETA = 10000.0

VMEM_LIMIT_BYTES = 56 * 1024 * 1024
DENSE_ROWS = 512


def _params(semantics):
    return pltpu.CompilerParams(dimension_semantics=semantics, vmem_limit_bytes=VMEM_LIMIT_BYTES)


def _rms(x, gain):
    ms = jnp.mean(x * x, axis=-1, keepdims=True)
    return x * lax.rsqrt(ms + EPS) * gain


def _dot(a, b):
    return jnp.dot(a, b, preferred_element_type=F32)


def _dot_nt(a, b):
    return lax.dot_general(a, b, (((1,), (1,)), ((), ())), preferred_element_type=F32)


def _dot_tn(a, b):
    return lax.dot_general(a, b, (((0,), (0,)), ((), ())), preferred_element_type=F32)


def _resident(shape, row=0):
    idx = (row,) + (0,) * (len(shape) - 1)
    return pl.BlockSpec(shape, lambda i: idx, pipeline_mode=pl.Buffered(1))


STAGE_BYTES = 2 * 1024 * 1024


def _stage_rows(rows, cols):
    r = 1
    while 2 * r * cols * 4 <= STAGE_BYTES and rows % (2 * r) == 0:
        r *= 2
    return r


def _stream_cast(src_hbm, index, dst_ref, stage_ref, sem_ref):
    rows = stage_ref.shape[1]
    n = src_hbm.shape[1] // rows

    def copy(c):
        return pltpu.make_async_copy(src_hbm.at[index, pl.ds(c * rows, rows)], stage_ref.at[c % 2],
                                     sem_ref.at[c % 2])

    copy(0).start()
    for c in range(n):
        if c + 1 < n:
            copy(c + 1).start()
        copy(c).wait()
        dst_ref[c * rows:(c + 1) * rows, :] = stage_ref[c % 2].astype(dst_ref.dtype)


def _norm_proj_kernel(x_ref, g_ref, w_hbm, o_ref, w_ref, stage_ref, sem_ref, *, w_index):
    @pl.when(pl.program_id(0) == 0)
    def _():
        _stream_cast(w_hbm, w_index, w_ref, stage_ref, sem_ref)

    h = _rms(x_ref[...], g_ref[...]).astype(BF16)
    o_ref[...] = _dot(h, w_ref[...]).astype(o_ref.dtype)


def norm_proj(x2, gain, w, w_index, *, tm):
    t, d = x2.shape
    n = w.shape[2]
    return pl.pallas_call(
        functools.partial(_norm_proj_kernel, w_index=w_index),
        grid=(t // tm,),
        in_specs=[pl.BlockSpec((tm, d), lambda i: (i, 0)), _resident((1, d)),
                  pl.BlockSpec(memory_space=pl.ANY)],
        out_specs=pl.BlockSpec((tm, n), lambda i: (i, 0)),
        out_shape=jax.ShapeDtypeStruct((t, n), BF16),
        scratch_shapes=[pltpu.VMEM((d, n), BF16), pltpu.VMEM((2, _stage_rows(d, n), n), F32),
                        pltpu.SemaphoreType.DMA((2,))],
        compiler_params=_params(("arbitrary",)),
        name="norm_proj",
    )(x2, gain.reshape(1, d), w)


def _out_mlp_kernel(x_ref, *refs, n_act, wo_index, mlp_index):
    act_refs = refs[:n_act]
    (wo_hbm, g_ref, wu_hbm, wd_hbm, o_ref,
     wo_ref, wu_ref, wd_ref, stage_wide, stage_narrow, sem_wide, sem_narrow) = refs[n_act:]

    @pl.when(pl.program_id(0) == 0)
    def _():
        _stream_cast(wo_hbm, wo_index, wo_ref, stage_narrow, sem_narrow)
        _stream_cast(wu_hbm, mlp_index, wu_ref, stage_wide, sem_wide)
        _stream_cast(wd_hbm, mlp_index, wd_ref, stage_narrow, sem_narrow)

    y = x_ref[...]
    k = act_refs[0].shape[1]
    for s, a_ref in enumerate(act_refs):
        y = y + _dot(a_ref[...], wo_ref[s * k:(s + 1) * k, :])
    h = _rms(y, g_ref[...]).astype(BF16)
    a = jnp.square(jnp.maximum(_dot(h, wu_ref[...]), 0.0)).astype(BF16)
    o_ref[...] = y + _dot(a, wd_ref[...])


def out_mlp(x2, acts, w_out, wo_index, gain, wu, wd, mlp_index, *, tm):
    t, d = x2.shape
    ff = wu.shape[2]
    k = acts[0].shape[1]
    assert all(a.shape[1] == k for a in acts) and len(acts) * k == w_out.shape[1]
    rows_wide, rows_narrow = _stage_rows(d, ff), _stage_rows(math.gcd(ff, w_out.shape[1]), d)

    def row_tile(width):
        return pl.BlockSpec((tm, width), lambda i: (i, 0))

    hbm = pl.BlockSpec(memory_space=pl.ANY)
    return pl.pallas_call(
        functools.partial(_out_mlp_kernel, n_act=len(acts), wo_index=wo_index, mlp_index=mlp_index),
        grid=(t // tm,),
        in_specs=[row_tile(d)] + [row_tile(k) for _ in acts] + [hbm, _resident((1, d)), hbm, hbm],
        out_specs=row_tile(d),
        out_shape=jax.ShapeDtypeStruct((t, d), F32),
        scratch_shapes=[
            pltpu.VMEM(w_out.shape[1:], BF16), pltpu.VMEM((d, ff), BF16), pltpu.VMEM((ff, d), BF16),
            pltpu.VMEM((2, rows_wide, ff), F32), pltpu.VMEM((2, rows_narrow, d), F32),
            pltpu.SemaphoreType.DMA((2,)), pltpu.SemaphoreType.DMA((2,)),
        ],
        compiler_params=_params(("arbitrary",)),
        name="out_mlp",
    )(x2, *acts, w_out, gain.reshape(1, d), wu, wd)


def _log_sigmoid(x):
    return -(jnp.maximum(-x, 0.0) + jnp.log1p(jnp.exp(-jnp.abs(x))))


def _ret_kernel(dl_ref, q_ref, k_ref, v_ref, g_ref, cos_ref, sin_ref, gn_ref, o_ref,
                sb_ref, sf_ref, cb_ref, dec_ref, *, cs):
    C = RET_CHUNK
    h = pl.program_id(1)
    phase = pl.program_id(2)
    t = pl.program_id(3)
    nsteps = pl.num_programs(3)

    lgf_w = _log_sigmoid(jnp.full((1, RET_DV), dl_ref[0, h], F32))
    lgb_w = _log_sigmoid(jnp.full((1, RET_DV), dl_ref[1, h], F32))

    @pl.when((phase == 0) & (t == 0))
    def _():
        row = lax.broadcasted_iota(jnp.int32, (C, C), 0).astype(F32)
        col = lax.broadcasted_iota(jnp.int32, (C, C), 1).astype(F32)
        lgf = lgf_w[:, :C]
        lgb = lgb_w[:, :C]
        diff = row - col
        dec_ref[0] = jnp.where(diff >= 0, jnp.exp(jnp.maximum(diff, 0.0) * lgf),
                               jnp.exp(jnp.maximum(-diff, 0.0) * lgb))
        dec_ref[1] = jnp.exp((row + 1.0) * lgf)
        dec_ref[2] = jnp.exp((C - 1.0 - row) * lgf)
        dec_ref[3] = jnp.exp((C - row) * lgb)
        dec_ref[4] = jnp.exp(row * lgb)
        sb_ref[...] = jnp.zeros_like(sb_ref)
        sf_ref[...] = jnp.zeros_like(sf_ref)

    def rope(a, rows):
        return a * cos_ref[rows, :] + pltpu.roll(a, RET_DK // 2, 1) * sin_ref[rows, :]

    @pl.when(phase == 0)
    def _():
        sb = sb_ref[...]
        cdec = jnp.exp(C * lgb_w)
        for ci in reversed(range(cs)):
            rows = slice(ci * C, (ci + 1) * C)
            qr = rope(q_ref[0, rows, :].astype(F32), rows)
            kr = rope(k_ref[0, rows, :].astype(F32), rows) * (RET_DK ** -0.5)
            c = (nsteps - 1 - t) * cs + ci
            cb_ref[c] = _dot((qr * dec_ref[3]).astype(BF16), sb.astype(BF16))
            sb = cdec * sb + _dot_tn((kr * dec_ref[4]).astype(BF16), v_ref[0, rows, :])
        sb_ref[...] = sb

    @pl.when(phase == 1)
    def _():
        sf = sf_ref[...]
        cdec = jnp.exp(C * lgf_w)
        for ci in range(cs):
            rows = slice(ci * C, (ci + 1) * C)
            qr = rope(q_ref[0, rows, :].astype(F32), rows)
            kr = rope(k_ref[0, rows, :].astype(F32), rows) * (RET_DK ** -0.5)
            v = v_ref[0, rows, :]
            s = _dot_nt(qr.astype(BF16), kr.astype(BF16))
            inner = _dot((s * dec_ref[0]).astype(BF16), v)
            cross = _dot((qr * dec_ref[1]).astype(BF16), sf.astype(BF16))
            sf = cdec * sf + _dot_tn((kr * dec_ref[2]).astype(BF16), v)
            tot = inner + cross + cb_ref[t * cs + ci]
            y = _rms(tot, gn_ref[...])
            g = g_ref[0, rows, :].astype(F32)
            o_ref[0, rows, :] = (g * jax.nn.sigmoid(g) * y).astype(o_ref.dtype)
        sf_ref[...] = sf


def retention(proj, decay_logit, gn_gain, cos2, sin2, *, n_heads, q_off, k_off, v_off, g_off, cs):
    b, s, _ = proj.shape
    C = RET_CHUNK
    nc = s // C
    assert nc % cs == 0
    ts = cs * C
    nsteps = nc // cs
    ret_v = n_heads * RET_DV

    def step(p, t):
        return jnp.where(p == 0, nsteps - 1 - t, t)

    def spec(width, off):
        base = off // width
        return pl.BlockSpec((1, ts, width), lambda bi, h, p, t: (bi, step(p, t), base + h))

    kern = functools.partial(_ret_kernel, cs=cs)
    return pl.pallas_call(
        kern,
        grid=(b, n_heads, 2, nsteps),
        in_specs=[
            pl.BlockSpec(memory_space=pltpu.SMEM),
            spec(RET_DK, q_off),
            spec(RET_DK, k_off),
            spec(RET_DV, v_off),
            spec(RET_DV, g_off),
            pl.BlockSpec((ts, RET_DK), lambda bi, h, p, t: (step(p, t), 0)),
            pl.BlockSpec((ts, RET_DK), lambda bi, h, p, t: (step(p, t), 0)),
            pl.BlockSpec((1, RET_DV), lambda bi, h, p, t: (0, h)),
        ],
        out_specs=pl.BlockSpec((1, ts, RET_DV), lambda bi, h, p, t: (bi, jnp.where(p == 0, 0, t), h)),
        out_shape=jax.ShapeDtypeStruct((b, s, ret_v), BF16),
        scratch_shapes=[
            pltpu.VMEM((RET_DK, RET_DV), F32),
            pltpu.VMEM((RET_DK, RET_DV), F32),
            pltpu.VMEM((nc, C, RET_DV), F32),
            pltpu.VMEM((5, C, C), F32),
        ],
        compiler_params=_params(("parallel", "parallel", "arbitrary", "arbitrary")),
        name="retention",
    )(decay_logit, proj, proj, proj, proj, cos2, sin2, gn_gain.reshape(1, ret_v))


def _swa_kernel(sink_ref, q_ref, kp_ref, km_ref, kn_ref, vp_ref, vm_ref, vn_ref,
                t5_ref, qg_ref, kg_ref, o_ref, bias_ref, *, n_heads, n_kv, nb):
    step = pl.program_id(1)
    D = HEAD_DIM
    G = n_heads // n_kv
    KW = 3 * BLOCK
    QW = G * BLOCK
    lane_head = lax.broadcasted_iota(jnp.int32, (1, QW), 1) // BLOCK

    def per_head_row(ref, kv, *idx):
        row = jnp.full((1, QW), ref[(*idx, kv * G)], F32)
        for g in range(1, G):
            row = jnp.where(lane_head == g, ref[(*idx, kv * G + g)], row)
        return row

    @pl.when((pl.program_id(0) == 0) & (step == 0))
    def _():
        j = lax.broadcasted_iota(jnp.int32, (KW, QW), 0)
        r = lax.broadcasted_iota(jnp.int32, (KW, QW), 1) % BLOCK
        in_band = jnp.abs(j - BLOCK - r) <= WINDOW
        for kv in range(n_kv):
            bias = jnp.where(in_band, t5_ref[kv] * LOG2E, NEG_INF)
            for e in range(4):
                keep = (j >= BLOCK if e & 1 else True) & (j < 2 * BLOCK if e & 2 else True)
                bias_ref[e, kv] = bias if e == 0 else jnp.where(keep, bias, NEG_INF)

    qg = qg_ref[...] * (D ** -0.5 * LOG2E)
    kg = kg_ref[...]
    for kv in range(n_kv):
        sl = slice(kv * D, (kv + 1) * D)
        kall = jnp.concatenate([kp_ref[0][:, sl], km_ref[0][:, sl], kn_ref[0][:, sl]], axis=0)
        kall = _rms(kall.astype(F32), kg).astype(BF16)
        vall = jnp.concatenate([vp_ref[0][:, sl], vm_ref[0][:, sl], vn_ref[0][:, sl]], axis=0)
        sink = per_head_row(sink_ref, kv) * LOG2E
        for qb in range(SWA_QBLOCKS):
            i = step * SWA_QBLOCKS + qb
            edge = (i == 0).astype(jnp.int32) + 2 * (i == nb - 1).astype(jnp.int32)
            rows = slice(qb * BLOCK, (qb + 1) * BLOCK)
            k3 = kall[qb * BLOCK:qb * BLOCK + KW]
            v3 = vall[qb * BLOCK:qb * BLOCK + KW]
            qs = jnp.concatenate(
                [_rms(q_ref[0, rows, (kv * G + g) * D:(kv * G + g + 1) * D].astype(F32), qg).astype(BF16)
                 for g in range(G)], axis=0)
            st = _dot_nt(k3, qs) + bias_ref[edge, kv]
            m = jnp.maximum(jnp.max(st, axis=0, keepdims=True), sink)
            p = jnp.exp2(st - m)
            denom = jnp.sum(p, axis=0, keepdims=True) + jnp.exp2(sink - m)
            o = (_dot_tn(v3, p.astype(BF16)) / denom).T
            for g in range(G):
                h = kv * G + g
                o_ref[0, rows, h * D:(h + 1) * D] = o[g * BLOCK:(g + 1) * BLOCK].astype(o_ref.dtype)


def _t5_bucket(rel):
    nb = T5_BUCKETS // 2
    max_exact = nb // 2
    ret = jnp.where(rel > 0, nb, 0)
    n = jnp.abs(rel)
    nf = jnp.maximum(n, 1).astype(jnp.float32)
    large = max_exact + (jnp.log(nf / max_exact) / math.log(T5_MAX_DIST / max_exact)
                         * (nb - max_exact)).astype(jnp.int32)
    large = jnp.minimum(large, nb - 1)
    return ret + jnp.where(n < max_exact, n, large)


def window_attention(proj, sink, t5_table, q_gain, k_gain, *, n_heads, n_kv, q_off, k_off, v_off):
    b, s, _ = proj.shape
    D = HEAD_DIM
    nb = s // BLOCK
    qw = n_heads * D
    kw = n_kv * D
    G = n_heads // n_kv
    n, L = 3 * BLOCK, 4 * BLOCK
    rel = jnp.arange(-(2 * BLOCK - 1), 2 * BLOCK)
    tbl = t5_table.astype(F32)[_t5_bucket(rel)]
    v = jnp.pad(tbl[::-1].T, ((0, 0), (0, 1)))
    skew = jnp.tile(v, (1, n + 1))[:, :n * (L + 1)].reshape(n_heads, n, L + 1)[:, :, :BLOCK]
    t5 = skew[:, ::-1, :]
    t5 = t5.reshape(n_kv, G, n, BLOCK).transpose(0, 2, 1, 3).reshape(n_kv, n, G * BLOCK)

    QB = SWA_QBLOCKS
    assert nb % QB == 0

    def kv_specs(off):
        base = off // kw
        return [pl.BlockSpec((1, BLOCK, kw), lambda bi, j: (bi, jnp.maximum(j * QB - 1, 0), base)),
                pl.BlockSpec((1, QB * BLOCK, kw), lambda bi, j: (bi, j, base)),
                pl.BlockSpec((1, BLOCK, kw), lambda bi, j: (bi, jnp.minimum(j * QB + QB, nb - 1), base))]

    kern = functools.partial(_swa_kernel, n_heads=n_heads, n_kv=n_kv, nb=nb)
    return pl.pallas_call(
        kern,
        grid=(b, nb // QB),
        in_specs=[
            pl.BlockSpec(memory_space=pltpu.SMEM),
            pl.BlockSpec((1, QB * BLOCK, qw), lambda bi, j: (bi, j, q_off // qw)),
            *kv_specs(k_off), *kv_specs(v_off),
            pl.BlockSpec((n_kv, 3 * BLOCK, G * BLOCK), lambda bi, j: (0, 0, 0)),
            pl.BlockSpec((1, D), lambda bi, j: (0, 0)),
            pl.BlockSpec((1, D), lambda bi, j: (0, 0)),
        ],
        out_specs=pl.BlockSpec((1, QB * BLOCK, qw), lambda bi, j: (bi, j, 0)),
        out_shape=jax.ShapeDtypeStruct((b, s, qw), BF16),
        scratch_shapes=[pltpu.VMEM((4, n_kv, 3 * BLOCK, G * BLOCK), F32)],
        compiler_params=_params(("arbitrary", "arbitrary")),
        name="window_attention",
    )(sink, proj, proj, proj, proj, proj, proj, proj, t5,
      q_gain.reshape(1, D), k_gain.reshape(1, D))


def _axial_head_perm():
    q4 = HEAD_DIM // 4
    return [blk * q4 + j for blk in (0, 2, 1, 3) for j in range(q4)]


def _axial_rope(a, cc, ss):
    return a * cc + pltpu.roll(a, HEAD_DIM // 2, 1) * ss


ONES_ROWS = 16


def _axial_proj_kernel(x_ref, g_ref, w_ref, cc_ref, ss_ref, qg_ref, kg_ref, qo_ref, ko_ref,
                       vt_ref, proj_ref, *, n_heads, n_kv):
    D = HEAD_DIM
    i = pl.program_id(0)

    @pl.when(i == 0)
    def _():
        proj_ref[1] = jnp.zeros(proj_ref.shape[1:], proj_ref.dtype)

    proj = proj_ref[(i + 1) % 2]
    cc, ss = cc_ref[...], ss_ref[...]
    qg = qg_ref[...] * (D ** -0.5 * LOG2E)
    kg = kg_ref[...]
    for hh in range(n_heads):
        hs = slice(hh * D, (hh + 1) * D)
        qo_ref[:, hs] = _axial_rope(_rms(proj[:, hs], qg), cc, ss).astype(qo_ref.dtype)
    k_off, v_off = n_heads * D, (n_heads + n_kv) * D
    for hh in range(n_kv):
        hs = slice(hh * D, (hh + 1) * D)
        k = _rms(proj[:, k_off + hh * D:k_off + (hh + 1) * D], kg)
        ko_ref[:, hs] = _axial_rope(k, cc, ss).astype(ko_ref.dtype)
        vt_ref[0, hh, :D, :] = proj[:, v_off + hh * D:v_off + (hh + 1) * D].T.astype(vt_ref.dtype)
        vt_ref[0, hh, D:, :] = jnp.ones((ONES_ROWS, vt_ref.shape[-1]), vt_ref.dtype)

    h = _rms(x_ref[...], g_ref[...]).astype(BF16)
    proj_ref[i % 2] = _dot(h, w_ref[...])


def axial_proj(x2, gain, w, tables, q_gain, k_gain, *, seq, n_heads, n_kv, tm):
    t, d = x2.shape
    D = HEAD_DIM
    qw, kw = n_heads * D, n_kv * D
    assert w.shape[1] == qw + 2 * kw and seq % tm == 0
    perm = jnp.asarray(_axial_head_perm())
    q4 = D // 4
    w4 = w[:, :qw + kw].reshape(d, n_heads + n_kv, 4, q4)
    wqk = jnp.stack([w4[:, :, blk] for blk in (0, 2, 1, 3)], axis=2)
    w = jnp.concatenate([wqk.reshape(d, qw + kw), w[:, qw + kw:]], axis=1).astype(BF16)
    q_gain, k_gain = q_gain[perm], k_gain[perm]
    spb = seq // tm
    n = t // tm
    kern = functools.partial(_axial_proj_kernel, n_heads=n_heads, n_kv=n_kv)

    def done(i):
        return jnp.maximum(i - 1, 0)

    tab = pl.BlockSpec((tm, D), lambda i: (done(i) % spb, 0))
    return pl.pallas_call(
        kern,
        grid=(n + 1,),
        in_specs=[pl.BlockSpec((tm, d), lambda i: (jnp.minimum(i, n - 1), 0)), _resident((1, d)),
                  _resident((d, w.shape[1])), tab, tab, _resident((1, D)), _resident((1, D))],
        out_specs=[
            pl.BlockSpec((tm, qw), lambda i: (done(i), 0)),
            pl.BlockSpec((tm, kw), lambda i: (done(i), 0)),
            pl.BlockSpec((1, n_kv, D + ONES_ROWS, tm), lambda i: (done(i) // spb, 0, 0, done(i) % spb)),
        ],
        out_shape=[jax.ShapeDtypeStruct((t, qw), BF16), jax.ShapeDtypeStruct((t, kw), BF16),
                   jax.ShapeDtypeStruct((t // seq, n_kv, D + ONES_ROWS, seq), BF16)],
        scratch_shapes=[pltpu.VMEM((2, tm, w.shape[1]), F32)],
        compiler_params=_params(("arbitrary",)),
        name="axial_proj",
    )(x2, gain.reshape(1, d), w, *tables, q_gain.reshape(1, D), k_gain.reshape(1, D))


FAST_SUM_MIN = 2.0 ** -80
FAST_SUM_MAX = 2.0 ** 100


def _flash_kernel(q_ref, qn_ref, k_ref, vt_ref, o_ref, acc_ref, kn_ref, p_ref, l_ref, lc_ref,
                  *, G, tq, tk, nk):
    D = HEAD_DIM
    R = G * tq
    qi = pl.program_id(2)

    @pl.when(qi == 0)
    def _():
        def kbody(c, mx):
            start = pl.multiple_of(c * tk, tk)
            kc = k_ref[0, pl.ds(start, tk), :].astype(F32)
            return jnp.maximum(mx, jnp.max(jnp.sum(kc * kc, axis=-1, keepdims=True), axis=0, keepdims=True))
        kn2 = lax.fori_loop(0, nk, kbody, jnp.zeros((1, 1), F32))
        kn_ref[...] = jnp.broadcast_to(kn2, kn_ref.shape)

    def stack_heads(ref):
        return jnp.concatenate([ref[0][:, g * D:(g + 1) * D] for g in range(G)], axis=0)

    def bound(qt):
        qf = qt.astype(F32)
        qn2 = _dot_nt(jnp.ones((8, D), BF16), (qf * qf).astype(BF16))[:1]
        return jnp.sqrt(qn2 * kn_ref[:1, :1])

    def scores(qt, c):
        start = pl.multiple_of(c * tk, tk)
        return _dot_nt(k_ref[0, pl.ds(start, tk), :], qt)

    def vt_chunk(c, rows=D + ONES_ROWS):
        return vt_ref[0, 0, :rows, pl.ds(pl.multiple_of(c * tk, tk), tk)]

    def produce(qt, shift, sums_ref, c, slot):
        p = jnp.exp2(scores(qt, c) - shift)
        sums_ref[...] += jnp.sum(p, axis=0, keepdims=True)
        p_ref[slot] = p.astype(BF16)

    def consume(c, slot):
        acc_ref[:D, :] += _dot(vt_chunk(c, D), p_ref[slot])

    qs = stack_heads(q_ref)
    shift = bound(qs)
    acc_ref[...] = jnp.zeros_like(acc_ref)

    @pl.when(qi == 0)
    def _():
        l_ref[...] = jnp.zeros_like(l_ref)
        produce(qs, shift, l_ref, 0, 0)

    @pl.when(qi > 0)
    def _():
        l_ref[...] = lc_ref[...]

    def body(c2, carry):
        c = 2 * c2
        produce(qs, shift, l_ref, c + 1, 1)
        consume(c, 0)
        produce(qs, shift, l_ref, c + 2, 0)
        consume(c + 1, 1)
        return carry

    lax.fori_loop(0, nk // 2 - 1, body, 0)
    produce(qs, shift, l_ref, nk - 1, 1)
    consume(nk - 2, 0)
    qs_next = stack_heads(qn_ref)
    lc_ref[...] = jnp.zeros_like(lc_ref)
    produce(qs_next, bound(qs_next), lc_ref, 0, 0)
    consume(nk - 1, 1)

    def running_max_pass():
        def rbody(c, m_old):
            st = scores(qs, c)
            m_new = jnp.maximum(m_old, jnp.max(st, axis=0, keepdims=True))
            alpha = jnp.exp2(m_old - m_new)
            p = jnp.exp2((st - m_new).astype(BF16))
            acc_ref[...] = alpha * acc_ref[...] + _dot(vt_chunk(c), p)
            return m_new

        acc_ref[...] = jnp.zeros_like(acc_ref)
        lax.fori_loop(0, nk, rbody, jnp.full((1, R), -jnp.inf, F32))

    def write_out(sums):
        out = (acc_ref[:D, :] / sums).T
        for g in range(G):
            o_ref[0, :, g * D:(g + 1) * D] = out[g * tq:(g + 1) * tq].astype(o_ref.dtype)

    sums = l_ref[...]
    trusted = (jnp.min(sums) >= FAST_SUM_MIN) & (jnp.max(sums) <= FAST_SUM_MAX)

    @pl.when(trusted)
    def _():
        write_out(sums)

    @pl.when(jnp.logical_not(trusted))
    def _():
        running_max_pass()
        write_out(acc_ref[D:D + 1, :])


def flash_attention(q, k, vt, *, n_heads, n_kv, tq, tk):
    b, s, _ = q.shape
    D = HEAD_DIM
    G = n_heads // n_kv
    assert s % (2 * tk) == 0 and s // tk >= 2
    nq = s // tq
    kern = functools.partial(_flash_kernel, G=G, tq=tq, tk=tk, nk=s // tk)
    return pl.pallas_call(
        kern,
        grid=(b, n_kv, nq),
        in_specs=[
            pl.BlockSpec((1, tq, G * D), lambda bi, kv, qi: (bi, qi, kv)),
            pl.BlockSpec((1, tq, G * D), lambda bi, kv, qi: (bi, jnp.minimum(qi + 1, nq - 1), kv)),
            pl.BlockSpec((1, s, D), lambda bi, kv, qi: (bi, 0, kv)),
            pl.BlockSpec((1, 1, D + ONES_ROWS, s), lambda bi, kv, qi: (bi, kv, 0, 0)),
        ],
        out_specs=pl.BlockSpec((1, tq, G * D), lambda bi, kv, qi: (bi, qi, kv)),
        out_shape=jax.ShapeDtypeStruct((b, s, n_heads * D), BF16),
        scratch_shapes=[pltpu.VMEM((D + ONES_ROWS, G * tq), F32), pltpu.VMEM((8, D), F32), pltpu.VMEM((2, tk, G * tq), BF16),
                        pltpu.VMEM((1, G * tq), F32), pltpu.VMEM((1, G * tq), F32)],
        compiler_params=_params(("parallel", "parallel", "arbitrary")),
        name="flash_attention",
    )(q, q, k, vt)


def _rope_angles(pos, dim, theta):
    inv = theta ** (-jnp.arange(0, dim, 2, dtype=jnp.float32) / dim)
    return pos.astype(jnp.float32)[:, None] * inv[None, :]


def _retention_tables(s):
    ang = _rope_angles(jnp.arange(s), RET_DK, RET_THETA)
    c, sn = jnp.cos(ang), jnp.sin(ang)
    sign = jnp.where(jnp.arange(RET_DK) < RET_DK // 2, -1.0, 1.0).astype(F32)
    return jnp.tile(c, (1, 2)), jnp.tile(sn, (1, 2)) * sign


def _axial_tables(s):
    rows = s // GRID_W
    half = HEAD_DIM // 2
    ar = _rope_angles(jnp.arange(rows), half, AX_THETA)
    ac = _rope_angles(jnp.arange(GRID_W), half, AX_THETA)
    cr, sr = (jnp.repeat(f(ar), GRID_W, axis=0) for f in (jnp.cos, jnp.sin))
    ccol, scol = (jnp.tile(f(ac), (rows, 1)) for f in (jnp.cos, jnp.sin))
    sign = jnp.where(jnp.arange(HEAD_DIM) < HEAD_DIM // 2, -1.0, 1.0).astype(F32)
    cc = jnp.tile(jnp.concatenate([cr, ccol], axis=-1), (1, 2))
    ss = jnp.tile(jnp.concatenate([sr, scol], axis=-1), (1, 2)) * sign
    return cc, ss


def kernel(x, norm_mix, norm_mlp, w_in_even, w_out_even, ret_decay_logit, ret_norm, swa_q_norm,
           swa_k_norm, swa_sink, t5_table, w_in_odd, w_out_odd, ax_q_norm, ax_k_norm, w_mlp_up,
           w_mlp_down):
    b, s, d = x.shape
    t = b * s
    depth = norm_mix.shape[0]
    ret_heads = ret_decay_logit.shape[-1]
    ret_q = ret_heads * RET_DK
    ret_v = ret_heads * RET_DV
    swa_heads = swa_sink.shape[-1]
    swa_q = swa_heads * HEAD_DIM
    swa_kv = SWA_KV_HEADS * HEAD_DIM
    ax_q = w_out_odd.shape[1]
    ax_heads = ax_q // HEAD_DIM
    ax_kv = AX_KV_HEADS * HEAD_DIM

    x2 = x.reshape(t, d)
    ret_tabs = _retention_tables(s)
    ax_tabs = _axial_tables(s)

    for layer in range(depth):
        i = layer // 2
        if layer % 2 == 0:
            proj = norm_proj(x2, norm_mix[layer], w_in_even, i, tm=DENSE_ROWS)
            proj = proj.reshape(b, s, -1)
            ya = retention(proj, ret_decay_logit[i], ret_norm[i], *ret_tabs, n_heads=ret_heads,
                           q_off=0, k_off=ret_q, v_off=2 * ret_q, g_off=2 * ret_q + ret_v, cs=32)
            off = 2 * ret_q + 2 * ret_v
            yb = window_attention(proj, swa_sink[i], t5_table, swa_q_norm[i], swa_k_norm[i],
                                  n_heads=swa_heads, n_kv=SWA_KV_HEADS, q_off=off,
                                  k_off=off + swa_q, v_off=off + swa_q + swa_kv)
            acts, w_out = [ya.reshape(t, -1), yb.reshape(t, -1)], w_out_even
        else:
            qp, kp, vt = axial_proj(x2, norm_mix[layer], w_in_odd[i], ax_tabs,
                                    ax_q_norm[i], ax_k_norm[i], seq=s, n_heads=ax_heads,
                                    n_kv=AX_KV_HEADS, tm=DENSE_ROWS)
            y = flash_attention(qp.reshape(b, s, -1), kp.reshape(b, s, -1), vt, n_heads=ax_heads,
                                n_kv=AX_KV_HEADS, tq=1024, tk=512)
            acts, w_out = [y.reshape(t, -1)], w_out_odd
        x2 = out_mlp(x2, acts, w_out, i, norm_mlp[layer], w_mlp_up, w_mlp_down, layer,
                     tm=DENSE_ROWS)
    return x2.reshape(b, s, d)
```

```python
import functools
import math

import jax
import jax.numpy as jnp
from jax import lax
from jax.experimental import pallas as pl
from jax.experimental.pallas import tpu as pltpu

F32 = jnp.float32
BF16 = jnp.bfloat16

EPS = 1e-6
NEG_INF = -1e30
LOG2E = math.log2(math.e)
HEAD_DIM = 128
BLOCK = 128
GRID_W = 64
RET_DK = 128
RET_DV = 256
RET_CHUNK = 128
RET_THETA = 10000.0
SWA_KV_HEADS = 2
SWA_QBLOCKS = 8
WINDOW = 128
T5_BUCKETS = 32
T5_MAX_DIST = 128
AX_KV_HEADS = 2
AX_THETA = 10000.0

VMEM_LIMIT_BYTES = 56 * 1024 * 1024
DENSE_ROWS = 512


def _params(semantics):
    return pltpu.CompilerParams(dimension_semantics=semantics, vmem_limit_bytes=VMEM_LIMIT_BYTES)


def _rms(x, gain):
    ms = jnp.mean(x * x, axis=-1, keepdims=True)
    return x * lax.rsqrt(ms + EPS) * gain


def _dot(a, b):
    return jnp.dot(a, b, preferred_element_type=F32)


def _dot_nt(a, b):
    return lax.dot_general(a, b, (((1,), (1,)), ((), ())), preferred_element_type=F32)


def _dot_tn(a, b):
    return lax.dot_general(a, b, (((0,), (0,)), ((), ())), preferred_element_type=F32)


def _resident(shape, row=0):
    idx = (row,) + (0,) * (len(shape) - 1)
    return pl.BlockSpec(shape, lambda i: idx, pipeline_mode=pl.Buffered(1))


STAGE_BYTES = 2 * 1024 * 1024


def _stage_rows(rows, cols):
    r = 1
    while 2 * r * cols * 4 <= STAGE_BYTES and rows % (2 * r) == 0:
        r *= 2
    return r


def _stream_cast(src_hbm, index, dst_ref, stage_ref, sem_ref):
    rows = stage_ref.shape[1]
    n = src_hbm.shape[1] // rows

    def copy(c):
        return pltpu.make_async_copy(src_hbm.at[index, pl.ds(c * rows, rows)], stage_ref.at[c % 2],
                                     sem_ref.at[c % 2])

    copy(0).start()
    for c in range(n):
        if c + 1 < n:
            copy(c + 1).start()
        copy(c).wait()
        dst_ref[c * rows:(c + 1) * rows, :] = stage_ref[c % 2].astype(dst_ref.dtype)


def _norm_proj_kernel(x_ref, g_ref, w_hbm, o_ref, w_ref, stage_ref, sem_ref, *, w_index):
    @pl.when(pl.program_id(0) == 0)
    def _():
        _stream_cast(w_hbm, w_index, w_ref, stage_ref, sem_ref)

    h = _rms(x_ref[...], g_ref[...]).astype(BF16)
    o_ref[...] = _dot(h, w_ref[...]).astype(o_ref.dtype)


def norm_proj(x2, gain, w, w_index, *, tm):
    t, d = x2.shape
    n = w.shape[2]
    return pl.pallas_call(
        functools.partial(_norm_proj_kernel, w_index=w_index),
        grid=(t // tm,),
        in_specs=[pl.BlockSpec((tm, d), lambda i: (i, 0)), _resident((1, d)),
                  pl.BlockSpec(memory_space=pl.ANY)],
        out_specs=pl.BlockSpec((tm, n), lambda i: (i, 0)),
        out_shape=jax.ShapeDtypeStruct((t, n), BF16),
        scratch_shapes=[pltpu.VMEM((d, n), BF16), pltpu.VMEM((2, _stage_rows(d, n), n), F32),
                        pltpu.SemaphoreType.DMA((2,))],
        compiler_params=_params(("arbitrary",)),
        name="norm_proj",
    )(x2, gain.reshape(1, d), w)


def _out_mlp_kernel(x_ref, *refs, n_act, wo_index, mlp_index):
    act_refs = refs[:n_act]
    (wo_hbm, g_ref, wu_hbm, wd_hbm, o_ref,
     wo_ref, wu_ref, wd_ref, stage_wide, stage_narrow, sem_wide, sem_narrow) = refs[n_act:]

    @pl.when(pl.program_id(0) == 0)
    def _():
        _stream_cast(wo_hbm, wo_index, wo_ref, stage_narrow, sem_narrow)
        _stream_cast(wu_hbm, mlp_index, wu_ref, stage_wide, sem_wide)
        _stream_cast(wd_hbm, mlp_index, wd_ref, stage_narrow, sem_narrow)

    y = x_ref[...]
    k = act_refs[0].shape[1]
    for s, a_ref in enumerate(act_refs):
        y = y + _dot(a_ref[...], wo_ref[s * k:(s + 1) * k, :])
    h = _rms(y, g_ref[...]).astype(BF16)
    a = jnp.square(jnp.maximum(_dot(h, wu_ref[...]), 0.0)).astype(BF16)
    o_ref[...] = y + _dot(a, wd_ref[...])


def out_mlp(x2, acts, w_out, wo_index, gain, wu, wd, mlp_index, *, tm):
    t, d = x2.shape
    ff = wu.shape[2]
    k = acts[0].shape[1]
    assert all(a.shape[1] == k for a in acts) and len(acts) * k == w_out.shape[1]
    rows_wide, rows_narrow = _stage_rows(d, ff), _stage_rows(math.gcd(ff, w_out.shape[1]), d)

    def row_tile(width):
        return pl.BlockSpec((tm, width), lambda i: (i, 0))

    hbm = pl.BlockSpec(memory_space=pl.ANY)
    return pl.pallas_call(
        functools.partial(_out_mlp_kernel, n_act=len(acts), wo_index=wo_index, mlp_index=mlp_index),
        grid=(t // tm,),
        in_specs=[row_tile(d)] + [row_tile(k) for _ in acts] + [hbm, _resident((1, d)), hbm, hbm],
        out_specs=row_tile(d),
        out_shape=jax.ShapeDtypeStruct((t, d), F32),
        scratch_shapes=[
            pltpu.VMEM(w_out.shape[1:], BF16), pltpu.VMEM((d, ff), BF16), pltpu.VMEM((ff, d), BF16),
            pltpu.VMEM((2, rows_wide, ff), F32), pltpu.VMEM((2, rows_narrow, d), F32),
            pltpu.SemaphoreType.DMA((2,)), pltpu.SemaphoreType.DMA((2,)),
        ],
        compiler_params=_params(("arbitrary",)),
        name="out_mlp",
    )(x2, *acts, w_out, gain.reshape(1, d), wu, wd)


def _log_sigmoid(x):
    return -(jnp.maximum(-x, 0.0) + jnp.log1p(jnp.exp(-jnp.abs(x))))


def _ret_kernel(dl_ref, q_ref, k_ref, v_ref, g_ref, cos_ref, sin_ref, gn_ref, o_ref,
                sb_ref, sf_ref, cb_ref, dec_ref, *, cs):
    C = RET_CHUNK
    h = pl.program_id(1)
    phase = pl.program_id(2)
    t = pl.program_id(3)
    nsteps = pl.num_programs(3)

    lgf_w = _log_sigmoid(jnp.full((1, RET_DV), dl_ref[0, h], F32))
    lgb_w = _log_sigmoid(jnp.full((1, RET_DV), dl_ref[1, h], F32))

    @pl.when((phase == 0) & (t == 0))
    def _():
        row = lax.broadcasted_iota(jnp.int32, (C, C), 0).astype(F32)
        col = lax.broadcasted_iota(jnp.int32, (C, C), 1).astype(F32)
        lgf = lgf_w[:, :C]
        lgb = lgb_w[:, :C]
        diff = row - col
        dec_ref[0] = jnp.where(diff >= 0, jnp.exp(jnp.maximum(diff, 0.0) * lgf),
                               jnp.exp(jnp.maximum(-diff, 0.0) * lgb))
        dec_ref[1] = jnp.exp((row + 1.0) * lgf)
        dec_ref[2] = jnp.exp((C - 1.0 - row) * lgf)
        dec_ref[3] = jnp.exp((C - row) * lgb)
        dec_ref[4] = jnp.exp(row * lgb)
        sb_ref[...] = jnp.zeros_like(sb_ref)
        sf_ref[...] = jnp.zeros_like(sf_ref)

    def rope(a, rows):
        return a * cos_ref[rows, :] + pltpu.roll(a, RET_DK // 2, 1) * sin_ref[rows, :]

    @pl.when(phase == 0)
    def _():
        sb = sb_ref[...]
        cdec = jnp.exp(C * lgb_w)
        for ci in reversed(range(cs)):
            rows = slice(ci * C, (ci + 1) * C)
            qr = rope(q_ref[0, rows, :].astype(F32), rows)
            kr = rope(k_ref[0, rows, :].astype(F32), rows) * (RET_DK ** -0.5)
            c = (nsteps - 1 - t) * cs + ci
            cb_ref[c] = _dot((qr * dec_ref[3]).astype(BF16), sb.astype(BF16))
            sb = cdec * sb + _dot_tn((kr * dec_ref[4]).astype(BF16), v_ref[0, rows, :])
        sb_ref[...] = sb

    @pl.when(phase == 1)
    def _():
        sf = sf_ref[...]
        cdec = jnp.exp(C * lgf_w)
        for ci in range(cs):
            rows = slice(ci * C, (ci + 1) * C)
            qr = rope(q_ref[0, rows, :].astype(F32), rows)
            kr = rope(k_ref[0, rows, :].astype(F32), rows) * (RET_DK ** -0.5)
            v = v_ref[0, rows, :]
            s = _dot_nt(qr.astype(BF16), kr.astype(BF16))
            lhs = jnp.concatenate([(s * dec_ref[0]).astype(BF16), (qr * dec_ref[1]).astype(BF16)], axis=1)
            rhs = jnp.concatenate([v, sf.astype(BF16)], axis=0)
            sf = cdec * sf + _dot_tn((kr * dec_ref[2]).astype(BF16), v)
            tot = _dot(lhs, rhs) + cb_ref[t * cs + ci]
            y = _rms(tot, gn_ref[...])
            g = g_ref[0, rows, :].astype(F32)
            o_ref[0, rows, :] = (g * jax.nn.sigmoid(g) * y).astype(o_ref.dtype)
        sf_ref[...] = sf


def retention(proj, decay_logit, gn_gain, cos2, sin2, *, n_heads, q_off, k_off, v_off, g_off, cs):
    b, s, _ = proj.shape
    C = RET_CHUNK
    nc = s // C
    assert nc % cs == 0
    ts = cs * C
    nsteps = nc // cs
    ret_v = n_heads * RET_DV

    def step(p, t):
        return jnp.where(p == 0, nsteps - 1 - t, t)

    def spec(width, off):
        base = off // width
        return pl.BlockSpec((1, ts, width), lambda bi, h, p, t: (bi, step(p, t), base + h))

    kern = functools.partial(_ret_kernel, cs=cs)
    return pl.pallas_call(
        kern,
        grid=(b, n_heads, 2, nsteps),
        in_specs=[
            pl.BlockSpec(memory_space=pltpu.SMEM),
            spec(RET_DK, q_off),
            spec(RET_DK, k_off),
            spec(RET_DV, v_off),
            spec(RET_DV, g_off),
            pl.BlockSpec((ts, RET_DK), lambda bi, h, p, t: (step(p, t), 0)),
            pl.BlockSpec((ts, RET_DK), lambda bi, h, p, t: (step(p, t), 0)),
            pl.BlockSpec((1, RET_DV), lambda bi, h, p, t: (0, h)),
        ],
        out_specs=pl.BlockSpec((1, ts, RET_DV), lambda bi, h, p, t: (bi, jnp.where(p == 0, 0, t), h)),
        out_shape=jax.ShapeDtypeStruct((b, s, ret_v), BF16),
        scratch_shapes=[
            pltpu.VMEM((RET_DK, RET_DV), F32),
            pltpu.VMEM((RET_DK, RET_DV), F32),
            pltpu.VMEM((nc, C, RET_DV), F32),
            pltpu.VMEM((5, C, C), F32),
        ],
        compiler_params=_params(("parallel", "parallel", "arbitrary", "arbitrary")),
        name="retention",
    )(decay_logit, proj, proj, proj, proj, cos2, sin2, gn_gain.reshape(1, ret_v))


def _swa_kernel(sink_ref, q_ref, kp_ref, km_ref, kn_ref, vp_ref, vm_ref, vn_ref,
                t5_ref, qg_ref, kg_ref, o_ref, bias_ref, *, n_heads, n_kv, nb):
    step = pl.program_id(1)
    D = HEAD_DIM
    G = n_heads // n_kv
    KW = 3 * BLOCK
    QW = G * BLOCK
    lane_head = lax.broadcasted_iota(jnp.int32, (1, QW), 1) // BLOCK

    def per_head_row(ref, kv, *idx):
        row = jnp.full((1, QW), ref[(*idx, kv * G)], F32)
        for g in range(1, G):
            row = jnp.where(lane_head == g, ref[(*idx, kv * G + g)], row)
        return row

    @pl.when((pl.program_id(0) == 0) & (step == 0))
    def _():
        j = lax.broadcasted_iota(jnp.int32, (KW, QW), 0)
        r = lax.broadcasted_iota(jnp.int32, (KW, QW), 1) % BLOCK
        in_band = jnp.abs(j - BLOCK - r) <= WINDOW
        for kv in range(n_kv):
            bias = jnp.where(in_band, t5_ref[kv] * LOG2E, NEG_INF)
            for e in range(4):
                keep = (j >= BLOCK if e & 1 else True) & (j < 2 * BLOCK if e & 2 else True)
                bias_ref[e, kv] = bias if e == 0 else jnp.where(keep, bias, NEG_INF)

    qg = qg_ref[...] * (D ** -0.5 * LOG2E)
    kg = kg_ref[...]
    for kv in range(n_kv):
        sl = slice(kv * D, (kv + 1) * D)
        kall = jnp.concatenate([kp_ref[0][:, sl], km_ref[0][:, sl], kn_ref[0][:, sl]], axis=0)
        kall = _rms(kall.astype(F32), kg).astype(BF16)
        vall = jnp.concatenate([vp_ref[0][:, sl], vm_ref[0][:, sl], vn_ref[0][:, sl]], axis=0)
        sink = per_head_row(sink_ref, kv) * LOG2E
        for qb in range(SWA_QBLOCKS):
            i = step * SWA_QBLOCKS + qb
            edge = (i == 0).astype(jnp.int32) + 2 * (i == nb - 1).astype(jnp.int32)
            rows = slice(qb * BLOCK, (qb + 1) * BLOCK)
            k3 = kall[qb * BLOCK:qb * BLOCK + KW]
            v3 = vall[qb * BLOCK:qb * BLOCK + KW]
            qs = jnp.concatenate(
                [_rms(q_ref[0, rows, (kv * G + g) * D:(kv * G + g + 1) * D].astype(F32), qg).astype(BF16)
                 for g in range(G)], axis=0)
            st = _dot_nt(k3, qs) + bias_ref[edge, kv]
            m = jnp.maximum(jnp.max(st, axis=0, keepdims=True), sink)
            p = jnp.exp2(st - m)
            denom = jnp.sum(p, axis=0, keepdims=True) + jnp.exp2(sink - m)
            o = (_dot_tn(v3, p.astype(BF16)) / denom).T
            for g in range(G):
                h = kv * G + g
                o_ref[0, rows, h * D:(h + 1) * D] = o[g * BLOCK:(g + 1) * BLOCK].astype(o_ref.dtype)


def _t5_bucket(rel):
    nb = T5_BUCKETS // 2
    max_exact = nb // 2
    ret = jnp.where(rel > 0, nb, 0)
    n = jnp.abs(rel)
    nf = jnp.maximum(n, 1).astype(jnp.float32)
    large = max_exact + (jnp.log(nf / max_exact) / math.log(T5_MAX_DIST / max_exact)
                         * (nb - max_exact)).astype(jnp.int32)
    large = jnp.minimum(large, nb - 1)
    return ret + jnp.where(n < max_exact, n, large)


def window_attention(proj, sink, t5_table, q_gain, k_gain, *, n_heads, n_kv, q_off, k_off, v_off):
    b, s, _ = proj.shape
    D = HEAD_DIM
    nb = s // BLOCK
    qw = n_heads * D
    kw = n_kv * D
    G = n_heads // n_kv
    n, L = 3 * BLOCK, 4 * BLOCK
    rel = jnp.arange(-(2 * BLOCK - 1), 2 * BLOCK)
    tbl = t5_table.astype(F32)[_t5_bucket(rel)]
    v = jnp.pad(tbl[::-1].T, ((0, 0), (0, 1)))
    skew = jnp.tile(v, (1, n + 1))[:, :n * (L + 1)].reshape(n_heads, n, L + 1)[:, :, :BLOCK]
    t5 = skew[:, ::-1, :]
    t5 = t5.reshape(n_kv, G, n, BLOCK).transpose(0, 2, 1, 3).reshape(n_kv, n, G * BLOCK)

    QB = SWA_QBLOCKS
    assert nb % QB == 0

    def kv_specs(off):
        base = off // kw
        return [pl.BlockSpec((1, BLOCK, kw), lambda bi, j: (bi, jnp.maximum(j * QB - 1, 0), base)),
                pl.BlockSpec((1, QB * BLOCK, kw), lambda bi, j: (bi, j, base)),
                pl.BlockSpec((1, BLOCK, kw), lambda bi, j: (bi, jnp.minimum(j * QB + QB, nb - 1), base))]

    kern = functools.partial(_swa_kernel, n_heads=n_heads, n_kv=n_kv, nb=nb)
    return pl.pallas_call(
        kern,
        grid=(b, nb // QB),
        in_specs=[
            pl.BlockSpec(memory_space=pltpu.SMEM),
            pl.BlockSpec((1, QB * BLOCK, qw), lambda bi, j: (bi, j, q_off // qw)),
            *kv_specs(k_off), *kv_specs(v_off),
            pl.BlockSpec((n_kv, 3 * BLOCK, G * BLOCK), lambda bi, j: (0, 0, 0)),
            pl.BlockSpec((1, D), lambda bi, j: (0, 0)),
            pl.BlockSpec((1, D), lambda bi, j: (0, 0)),
        ],
        out_specs=pl.BlockSpec((1, QB * BLOCK, qw), lambda bi, j: (bi, j, 0)),
        out_shape=jax.ShapeDtypeStruct((b, s, qw), BF16),
        scratch_shapes=[pltpu.VMEM((4, n_kv, 3 * BLOCK, G * BLOCK), F32)],
        compiler_params=_params(("arbitrary", "arbitrary")),
        name="window_attention",
    )(sink, proj, proj, proj, proj, proj, proj, proj, t5,
      q_gain.reshape(1, D), k_gain.reshape(1, D))


def _axial_head_perm():
    q4 = HEAD_DIM // 4
    return [blk * q4 + j for blk in (0, 2, 1, 3) for j in range(q4)]


def _axial_rope(a, cc, ss):
    return a * cc + pltpu.roll(a, HEAD_DIM // 2, 1) * ss


ONES_ROWS = 16


def _axial_proj_kernel(x_ref, g_ref, w_ref, cc_ref, ss_ref, qg_ref, kg_ref, qo_ref, ko_ref,
                       vt_ref, proj_ref, *, n_heads, n_kv):
    D = HEAD_DIM
    i = pl.program_id(0)

    @pl.when(i == 0)
    def _():
        proj_ref[1] = jnp.zeros(proj_ref.shape[1:], proj_ref.dtype)

    proj = proj_ref[(i + 1) % 2]
    cc, ss = cc_ref[...], ss_ref[...]
    qg = qg_ref[...] * (D ** -0.5 * LOG2E)
    kg = kg_ref[...]
    for hh in range(n_heads):
        hs = slice(hh * D, (hh + 1) * D)
        qo_ref[:, hs] = _axial_rope(_rms(proj[:, hs], qg), cc, ss).astype(qo_ref.dtype)
    k_off, v_off = n_heads * D, (n_heads + n_kv) * D
    for hh in range(n_kv):
        hs = slice(hh * D, (hh + 1) * D)
        k = _rms(proj[:, k_off + hh * D:k_off + (hh + 1) * D], kg)
        ko_ref[:, hs] = _axial_rope(k, cc, ss).astype(ko_ref.dtype)
        vt_ref[0, hh, :D, :] = proj[:, v_off + hh * D:v_off + (hh + 1) * D].T.astype(vt_ref.dtype)
        vt_ref[0, hh, D:, :] = jnp.ones((ONES_ROWS, vt_ref.shape[-1]), vt_ref.dtype)

    h = _rms(x_ref[...], g_ref[...]).astype(BF16)
    proj_ref[i % 2] = _dot(h, w_ref[...])


def axial_proj(x2, gain, w, tables, q_gain, k_gain, *, seq, n_heads, n_kv, tm):
    t, d = x2.shape
    D = HEAD_DIM
    qw, kw = n_heads * D, n_kv * D
    assert w.shape[1] == qw + 2 * kw and seq % tm == 0
    perm = jnp.asarray(_axial_head_perm())
    q4 = D // 4
    w4 = w[:, :qw + kw].reshape(d, n_heads + n_kv, 4, q4)
    wqk = jnp.stack([w4[:, :, blk] for blk in (0, 2, 1, 3)], axis=2)
    w = jnp.concatenate([wqk.reshape(d, qw + kw), w[:, qw + kw:]], axis=1).astype(BF16)
    q_gain, k_gain = q_gain[perm], k_gain[perm]
    spb = seq // tm
    n = t // tm
    kern = functools.partial(_axial_proj_kernel, n_heads=n_heads, n_kv=n_kv)

    def done(i):
        return jnp.maximum(i - 1, 0)

    tab = pl.BlockSpec((tm, D), lambda i: (done(i) % spb, 0))
    return pl.pallas_call(
        kern,
        grid=(n + 1,),
        in_specs=[pl.BlockSpec((tm, d), lambda i: (jnp.minimum(i, n - 1), 0)), _resident((1, d)),
                  _resident((d, w.shape[1])), tab, tab, _resident((1, D)), _resident((1, D))],
        out_specs=[
            pl.BlockSpec((tm, qw), lambda i: (done(i), 0)),
            pl.BlockSpec((tm, kw), lambda i: (done(i), 0)),
            pl.BlockSpec((1, n_kv, D + ONES_ROWS, tm), lambda i: (done(i) // spb, 0, 0, done(i) % spb)),
        ],
        out_shape=[jax.ShapeDtypeStruct((t, qw), BF16), jax.ShapeDtypeStruct((t, kw), BF16),
                   jax.ShapeDtypeStruct((t // seq, n_kv, D + ONES_ROWS, seq), BF16)],
        scratch_shapes=[pltpu.VMEM((2, tm, w.shape[1]), F32)],
        compiler_params=_params(("arbitrary",)),
        name="axial_proj",
    )(x2, gain.reshape(1, d), w, *tables, q_gain.reshape(1, D), k_gain.reshape(1, D))


FAST_SUM_MIN = 2.0 ** -80
FAST_SUM_MAX = 2.0 ** 100


def _flash_kernel(q_ref, k_ref, vt_ref, o_ref, acc_ref, kn_ref, p_ref, l_ref, *, G, tq, tk, nk):
    D = HEAD_DIM
    R = G * tq
    qs = jnp.concatenate([q_ref[0][:, g * D:(g + 1) * D] for g in range(G)], axis=0)

    @pl.when(pl.program_id(2) == 0)
    def _():
        def kbody(c, mx):
            start = pl.multiple_of(c * tk, tk)
            kc = k_ref[0, pl.ds(start, tk), :].astype(F32)
            return jnp.maximum(mx, jnp.max(jnp.sum(kc * kc, axis=-1, keepdims=True), axis=0, keepdims=True))
        kn2 = lax.fori_loop(0, nk, kbody, jnp.zeros((1, 1), F32))
        kn_ref[...] = jnp.broadcast_to(kn2, kn_ref.shape)

    def scores(c):
        start = pl.multiple_of(c * tk, tk)
        return _dot_nt(k_ref[0, pl.ds(start, tk), :], qs)

    def vt_chunk(c, rows=D + ONES_ROWS):
        return vt_ref[0, 0, :rows, pl.ds(pl.multiple_of(c * tk, tk), tk)]


    def produce_fixed(c, slot, shift):
        p = jnp.exp2(scores(c) - shift)
        l_ref[...] += jnp.sum(p, axis=0, keepdims=True)
        p_ref[slot] = p.astype(BF16)

    def consume_fixed(c, slot, shift):
        acc_ref[:D, :] += _dot(vt_chunk(c, D), p_ref[slot])

    def first_pass(shift):
        def body(c2, carry):
            c = 2 * c2
            produce_fixed(c + 1, 1, shift)
            consume_fixed(c, 0, shift)
            produce_fixed(c + 2, 0, shift)
            consume_fixed(c + 1, 1, shift)
            return carry

        acc_ref[...] = jnp.zeros_like(acc_ref)
        l_ref[...] = jnp.zeros_like(l_ref)
        produce_fixed(0, 0, shift)
        lax.fori_loop(0, nk // 2 - 1, body, 0)
        produce_fixed(nk - 1, 1, shift)
        consume_fixed(nk - 2, 0, shift)
        consume_fixed(nk - 1, 1, shift)

    def running_max_pass():
        def body(c, m_old):
            st = scores(c)
            m_new = jnp.maximum(m_old, jnp.max(st, axis=0, keepdims=True))
            alpha = jnp.exp2(m_old - m_new)
            p = jnp.exp2((st - m_new).astype(BF16))
            acc_ref[...] = alpha * acc_ref[...] + _dot(vt_chunk(c), p)
            return m_new

        acc_ref[...] = jnp.zeros_like(acc_ref)
        lax.fori_loop(0, nk, body, jnp.full((1, R), -jnp.inf, F32))

    def write_out(sums):
        out = (acc_ref[:D, :] / sums).T
        for g in range(G):
            o_ref[0, :, g * D:(g + 1) * D] = out[g * tq:(g + 1) * tq].astype(o_ref.dtype)

    qf = qs.astype(F32)
    qn2 = _dot_nt(jnp.ones((8, D), BF16), (qf * qf).astype(BF16))[:1]
    first_pass(jnp.sqrt(qn2 * kn_ref[:1, :1]))
    sums = l_ref[...]
    trusted = (jnp.min(sums) >= FAST_SUM_MIN) & (jnp.max(sums) <= FAST_SUM_MAX)

    @pl.when(trusted)
    def _():
        write_out(sums)

    @pl.when(jnp.logical_not(trusted))
    def _():
        running_max_pass()
        write_out(acc_ref[D:D + 1, :])


def flash_attention(q, k, vt, *, n_heads, n_kv, tq, tk):
    b, s, _ = q.shape
    D = HEAD_DIM
    G = n_heads // n_kv
    assert s % (2 * tk) == 0 and s // tk >= 2
    kern = functools.partial(_flash_kernel, G=G, tq=tq, tk=tk, nk=s // tk)
    return pl.pallas_call(
        kern,
        grid=(b, n_kv, s // tq),
        in_specs=[
            pl.BlockSpec((1, tq, G * D), lambda bi, kv, qi: (bi, qi, kv)),
            pl.BlockSpec((1, s, D), lambda bi, kv, qi: (bi, 0, kv)),
            pl.BlockSpec((1, 1, D + ONES_ROWS, s), lambda bi, kv, qi: (bi, kv, 0, 0)),
        ],
        out_specs=pl.BlockSpec((1, tq, G * D), lambda bi, kv, qi: (bi, qi, kv)),
        out_shape=jax.ShapeDtypeStruct((b, s, n_heads * D), BF16),
        scratch_shapes=[pltpu.VMEM((D + ONES_ROWS, G * tq), F32), pltpu.VMEM((8, D), F32), pltpu.VMEM((2, tk, G * tq), BF16),
                        pltpu.VMEM((1, G * tq), F32)],
        compiler_params=_params(("parallel", "parallel", "arbitrary")),
        name="flash_attention",
    )(q, k, vt)


def _rope_angles(pos, dim, theta):
    inv = theta ** (-jnp.arange(0, dim, 2, dtype=jnp.float32) / dim)
    return pos.astype(jnp.float32)[:, None] * inv[None, :]


def _retention_tables(s):
    ang = _rope_angles(jnp.arange(s), RET_DK, RET_THETA)
    c, sn = jnp.cos(ang), jnp.sin(ang)
    sign = jnp.where(jnp.arange(RET_DK) < RET_DK // 2, -1.0, 1.0).astype(F32)
    return jnp.tile(c, (1, 2)), jnp.tile(sn, (1, 2)) * sign


def _axial_tables(s):
    rows = s // GRID_W
    half = HEAD_DIM // 2
    ar = _rope_angles(jnp.arange(rows), half, AX_THETA)
    ac = _rope_angles(jnp.arange(GRID_W), half, AX_THETA)
    cr, sr = (jnp.repeat(f(ar), GRID_W, axis=0) for f in (jnp.cos, jnp.sin))
    ccol, scol = (jnp.tile(f(ac), (rows, 1)) for f in (jnp.cos, jnp.sin))
    sign = jnp.where(jnp.arange(HEAD_DIM) < HEAD_DIM // 2, -1.0, 1.0).astype(F32)
    cc = jnp.tile(jnp.concatenate([cr, ccol], axis=-1), (1, 2))
    ss = jnp.tile(jnp.concatenate([sr, scol], axis=-1), (1, 2)) * sign
    return cc, ss


def kernel(x, norm_mix, norm_mlp, w_in_even, w_out_even, ret_decay_logit, ret_norm, swa_q_norm,
           swa_k_norm, swa_sink, t5_table, w_in_odd, w_out_odd, ax_q_norm, ax_k_norm, w_mlp_up,
           w_mlp_down):
    b, s, d = x.shape
    t = b * s
    depth = norm_mix.shape[0]
    ret_heads = ret_decay_logit.shape[-1]
    ret_q = ret_heads * RET_DK
    ret_v = ret_heads * RET_DV
    swa_heads = swa_sink.shape[-1]
    swa_q = swa_heads * HEAD_DIM
    swa_kv = SWA_KV_HEADS * HEAD_DIM
    ax_q = w_out_odd.shape[1]
    ax_heads = ax_q // HEAD_DIM
    ax_kv = AX_KV_HEADS * HEAD_DIM

    x2 = x.reshape(t, d)
    ret_tabs = _retention_tables(s)
    ax_tabs = _axial_tables(s)

    for layer in range(depth):
        i = layer // 2
        if layer % 2 == 0:
            proj = norm_proj(x2, norm_mix[layer], w_in_even, i, tm=DENSE_ROWS)
            proj = proj.reshape(b, s, -1)
            ya = retention(proj, ret_decay_logit[i], ret_norm[i], *ret_tabs, n_heads=ret_heads,
                           q_off=0, k_off=ret_q, v_off=2 * ret_q, g_off=2 * ret_q + ret_v, cs=32)
            off = 2 * ret_q + 2 * ret_v
            yb = window_attention(proj, swa_sink[i], t5_table, swa_q_norm[i], swa_k_norm[i],
                                  n_heads=swa_heads, n_kv=SWA_KV_HEADS, q_off=off,
                                  k_off=off + swa_q, v_off=off + swa_q + swa_kv)
            acts, w_out = [ya.reshape(t, -1), yb.reshape(t, -1)], w_out_even
        else:
            qp, kp, vt = axial_proj(x2, norm_mix[layer], w_in_odd[i], ax_tabs,
                                    ax_q_norm[i], ax_k_norm[i], seq=s, n_heads=ax_heads,
                                    n_kv=AX_KV_HEADS, tm=DENSE_ROWS)
            y = flash_attention(qp.reshape(b, s, -1), kp.reshape(b, s, -1), vt, n_heads=ax_heads,
                                n_kv=AX_KV_HEADS, tq=1024, tk=512)
            acts, w_out = [y.reshape(t, -1)], w_out_odd
        x2 = out_mlp(x2, acts, w_out, i, norm_mlp[layer], w_mlp_up, w_mlp_down, layer,
                     tm=DENSE_ROWS)
    return x2.reshape(b, s, d)
```

```python
import functools
import math

import jax
import jax.numpy as jnp
from jax import lax
from jax.experimental import pallas as pl
from jax.experimental.pallas import tpu as pltpu

F32 = jnp.float32
BF16 = jnp.bfloat16

EPS = 1e-6
NEG_INF = -1e30
LOG2E = math.log2(math.e)
HEAD_DIM = 128
BLOCK = 128
GRID_W = 64
RET_DK = 128
RET_DV = 256
RET_CHUNK = 128
RET_THETA = 10000.0
SWA_KV_HEADS = 2
SWA_QBLOCKS = 8
WINDOW = 128
T5_BUCKETS = 32
T5_MAX_DIST = 128
AX_KV_HEADS = 2
AX_THETA = 10000.0

VMEM_LIMIT_BYTES = 56 * 1024 * 1024
DENSE_ROWS = 512


def _params(semantics):
    return pltpu.CompilerParams(dimension_semantics=semantics, vmem_limit_bytes=VMEM_LIMIT_BYTES)


def _rms(x, gain):
    ms = jnp.mean(x * x, axis=-1, keepdims=True)
    return x * lax.rsqrt(ms + EPS) * gain


def _dot(a, b):
    return jnp.dot(a, b, preferred_element_type=F32)


def _dot_nt(a, b):
    return lax.dot_general(a, b, (((1,), (1,)), ((), ())), preferred_element_type=F32)


def _dot_tn(a, b):
    return lax.dot_general(a, b, (((0,), (0,)), ((), ())), preferred_element_type=F32)


def _resident(shape, row=0):
    idx = (row,) + (0,) * (len(shape) - 1)
    return pl.BlockSpec(shape, lambda i: idx, pipeline_mode=pl.Buffered(1))


STAGE_BYTES = 2 * 1024 * 1024


def _stage_rows(rows, cols):
    r = 1
    while 2 * r * cols * 4 <= STAGE_BYTES and rows % (2 * r) == 0:
        r *= 2
    return r


def _stream_cast(src_hbm, index, dst_ref, stage_ref, sem_ref):
    rows = stage_ref.shape[1]
    n = src_hbm.shape[1] // rows

    def copy(c):
        return pltpu.make_async_copy(src_hbm.at[index, pl.ds(c * rows, rows)], stage_ref.at[c % 2],
                                     sem_ref.at[c % 2])

    copy(0).start()
    for c in range(n):
        if c + 1 < n:
            copy(c + 1).start()
        copy(c).wait()
        dst_ref[c * rows:(c + 1) * rows, :] = stage_ref[c % 2].astype(dst_ref.dtype)


def _norm_proj_kernel(x_ref, g_ref, w_hbm, o_ref, w_ref, stage_ref, sem_ref, *, w_index):
    @pl.when(pl.program_id(0) == 0)
    def _():
        _stream_cast(w_hbm, w_index, w_ref, stage_ref, sem_ref)

    h = _rms(x_ref[...], g_ref[...]).astype(BF16)
    o_ref[...] = _dot(h, w_ref[...]).astype(o_ref.dtype)


def norm_proj(x2, gain, w, w_index, *, tm):
    t, d = x2.shape
    n = w.shape[2]
    return pl.pallas_call(
        functools.partial(_norm_proj_kernel, w_index=w_index),
        grid=(t // tm,),
        in_specs=[pl.BlockSpec((tm, d), lambda i: (i, 0)), _resident((1, d)),
                  pl.BlockSpec(memory_space=pl.ANY)],
        out_specs=pl.BlockSpec((tm, n), lambda i: (i, 0)),
        out_shape=jax.ShapeDtypeStruct((t, n), BF16),
        scratch_shapes=[pltpu.VMEM((d, n), BF16), pltpu.VMEM((2, _stage_rows(d, n), n), F32),
                        pltpu.SemaphoreType.DMA((2,))],
        compiler_params=_params(("arbitrary",)),
        name="norm_proj",
    )(x2, gain.reshape(1, d), w)


def _out_mlp_kernel(x_ref, *refs, n_act, wo_index, mlp_index):
    act_refs = refs[:n_act]
    (wo_hbm, g_ref, wu_hbm, wd_hbm, o_ref,
     wo_ref, wu_ref, wd_ref, stage_wide, stage_narrow, sem_wide, sem_narrow) = refs[n_act:]

    @pl.when(pl.program_id(0) == 0)
    def _():
        _stream_cast(wo_hbm, wo_index, wo_ref, stage_narrow, sem_narrow)
        _stream_cast(wu_hbm, mlp_index, wu_ref, stage_wide, sem_wide)
        _stream_cast(wd_hbm, mlp_index, wd_ref, stage_narrow, sem_narrow)

    y = x_ref[...]
    k = act_refs[0].shape[1]
    for s, a_ref in enumerate(act_refs):
        y = y + _dot(a_ref[...], wo_ref[s * k:(s + 1) * k, :])
    h = _rms(y, g_ref[...]).astype(BF16)
    a = jnp.square(jnp.maximum(_dot(h, wu_ref[...]), 0.0)).astype(BF16)
    o_ref[...] = y + _dot(a, wd_ref[...])


def out_mlp(x2, acts, w_out, wo_index, gain, wu, wd, mlp_index, *, tm):
    t, d = x2.shape
    ff = wu.shape[2]
    k = acts[0].shape[1]
    assert all(a.shape[1] == k for a in acts) and len(acts) * k == w_out.shape[1]
    rows_wide, rows_narrow = _stage_rows(d, ff), _stage_rows(math.gcd(ff, w_out.shape[1]), d)

    def row_tile(width):
        return pl.BlockSpec((tm, width), lambda i: (i, 0))

    hbm = pl.BlockSpec(memory_space=pl.ANY)
    return pl.pallas_call(
        functools.partial(_out_mlp_kernel, n_act=len(acts), wo_index=wo_index, mlp_index=mlp_index),
        grid=(t // tm,),
        in_specs=[row_tile(d)] + [row_tile(k) for _ in acts] + [hbm, _resident((1, d)), hbm, hbm],
        out_specs=row_tile(d),
        out_shape=jax.ShapeDtypeStruct((t, d), F32),
        scratch_shapes=[
            pltpu.VMEM(w_out.shape[1:], BF16), pltpu.VMEM((d, ff), BF16), pltpu.VMEM((ff, d), BF16),
            pltpu.VMEM((2, rows_wide, ff), F32), pltpu.VMEM((2, rows_narrow, d), F32),
            pltpu.SemaphoreType.DMA((2,)), pltpu.SemaphoreType.DMA((2,)),
        ],
        compiler_params=_params(("arbitrary",)),
        name="out_mlp",
    )(x2, *acts, w_out, gain.reshape(1, d), wu, wd)


def _log_sigmoid(x):
    return -(jnp.maximum(-x, 0.0) + jnp.log1p(jnp.exp(-jnp.abs(x))))


def _ret_kernel(dl_ref, q_ref, k_ref, v_ref, g_ref, cos_ref, sin_ref, gn_ref, o_ref,
                sb_ref, sf_ref, cb_ref, dec_ref, *, cs):
    C = RET_CHUNK
    h = pl.program_id(1)
    phase = pl.program_id(2)
    t = pl.program_id(3)
    nsteps = pl.num_programs(3)

    lgf_w = _log_sigmoid(jnp.full((1, RET_DV), dl_ref[0, h], F32))
    lgb_w = _log_sigmoid(jnp.full((1, RET_DV), dl_ref[1, h], F32))

    @pl.when((phase == 0) & (t == 0))
    def _():
        row = lax.broadcasted_iota(jnp.int32, (C, C), 0).astype(F32)
        col = lax.broadcasted_iota(jnp.int32, (C, C), 1).astype(F32)
        lgf = lgf_w[:, :C]
        lgb = lgb_w[:, :C]
        diff = row - col
        dec_ref[0] = jnp.where(diff >= 0, jnp.exp(jnp.maximum(diff, 0.0) * lgf),
                               jnp.exp(jnp.maximum(-diff, 0.0) * lgb))
        dec_ref[1] = jnp.exp((row + 1.0) * lgf)
        dec_ref[2] = jnp.exp((C - 1.0 - row) * lgf)
        dec_ref[3] = jnp.exp((C - row) * lgb)
        dec_ref[4] = jnp.exp(row * lgb)
        sb_ref[...] = jnp.zeros_like(sb_ref)
        sf_ref[...] = jnp.zeros_like(sf_ref)

    def rope(a, rows):
        return a * cos_ref[rows, :] + pltpu.roll(a, RET_DK // 2, 1) * sin_ref[rows, :]

    @pl.when(phase == 0)
    def _():
        sb = sb_ref[...]
        cdec = jnp.exp(C * lgb_w)
        for ci in reversed(range(cs)):
            rows = slice(ci * C, (ci + 1) * C)
            qr = rope(q_ref[0, rows, :].astype(F32), rows)
            kr = rope(k_ref[0, rows, :].astype(F32), rows) * (RET_DK ** -0.5)
            c = (nsteps - 1 - t) * cs + ci
            cb_ref[c] = _dot((qr * dec_ref[3]).astype(BF16), sb.astype(BF16))
            sb = cdec * sb + _dot_tn((kr * dec_ref[4]).astype(BF16), v_ref[0, rows, :])
        sb_ref[...] = sb

    @pl.when(phase == 1)
    def _():
        sf = sf_ref[...]
        cdec = jnp.exp(C * lgf_w)
        for ci in range(cs):
            rows = slice(ci * C, (ci + 1) * C)
            qr = rope(q_ref[0, rows, :].astype(F32), rows)
            kr = rope(k_ref[0, rows, :].astype(F32), rows) * (RET_DK ** -0.5)
            v = v_ref[0, rows, :]
            s = _dot_nt(qr.astype(BF16), kr.astype(BF16))
            lhs = jnp.concatenate([(s * dec_ref[0]).astype(BF16), (qr * dec_ref[1]).astype(BF16)], axis=1)
            rhs = jnp.concatenate([v, sf.astype(BF16)], axis=0)
            sf = cdec * sf + _dot_tn((kr * dec_ref[2]).astype(BF16), v)
            tot = _dot(lhs, rhs) + cb_ref[t * cs + ci]
            y = _rms(tot, gn_ref[...])
            g = g_ref[0, rows, :].astype(F32)
            o_ref[0, rows, :] = (g * jax.nn.sigmoid(g) * y).astype(o_ref.dtype)
        sf_ref[...] = sf


def retention(proj, decay_logit, gn_gain, cos2, sin2, *, n_heads, q_off, k_off, v_off, g_off, cs):
    b, s, _ = proj.shape
    C = RET_CHUNK
    nc = s // C
    assert nc % cs == 0
    ts = cs * C
    nsteps = nc // cs
    ret_v = n_heads * RET_DV

    def step(p, t):
        return jnp.where(p == 0, nsteps - 1 - t, t)

    def spec(width, off):
        base = off // width
        return pl.BlockSpec((1, ts, width), lambda bi, h, p, t: (bi, step(p, t), base + h))

    kern = functools.partial(_ret_kernel, cs=cs)
    return pl.pallas_call(
        kern,
        grid=(b, n_heads, 2, nsteps),
        in_specs=[
            pl.BlockSpec(memory_space=pltpu.SMEM),
            spec(RET_DK, q_off),
            spec(RET_DK, k_off),
            spec(RET_DV, v_off),
            spec(RET_DV, g_off),
            pl.BlockSpec((ts, RET_DK), lambda bi, h, p, t: (step(p, t), 0)),
            pl.BlockSpec((ts, RET_DK), lambda bi, h, p, t: (step(p, t), 0)),
            pl.BlockSpec((1, RET_DV), lambda bi, h, p, t: (0, h)),
        ],
        out_specs=pl.BlockSpec((1, ts, RET_DV), lambda bi, h, p, t: (bi, jnp.where(p == 0, 0, t), h)),
        out_shape=jax.ShapeDtypeStruct((b, s, ret_v), BF16),
        scratch_shapes=[
            pltpu.VMEM((RET_DK, RET_DV), F32),
            pltpu.VMEM((RET_DK, RET_DV), F32),
            pltpu.VMEM((nc, C, RET_DV), F32),
            pltpu.VMEM((5, C, C), F32),
        ],
        compiler_params=_params(("parallel", "parallel", "arbitrary", "arbitrary")),
        name="retention",
    )(decay_logit, proj, proj, proj, proj, cos2, sin2, gn_gain.reshape(1, ret_v))


def _swa_kernel(sink_ref, q_ref, kp_ref, km_ref, kn_ref, vp_ref, vm_ref, vn_ref,
                t5_ref, qg_ref, kg_ref, o_ref, bias_ref, *, n_heads, n_kv, nb):
    step = pl.program_id(1)
    D = HEAD_DIM
    G = n_heads // n_kv
    KW = 3 * BLOCK
    QW = G * BLOCK
    lane_head = lax.broadcasted_iota(jnp.int32, (1, QW), 1) // BLOCK

    def per_head_row(ref, kv, *idx):
        row = jnp.full((1, QW), ref[(*idx, kv * G)], F32)
        for g in range(1, G):
            row = jnp.where(lane_head == g, ref[(*idx, kv * G + g)], row)
        return row

    @pl.when((pl.program_id(0) == 0) & (step == 0))
    def _():
        j = lax.broadcasted_iota(jnp.int32, (KW, QW), 0)
        r = lax.broadcasted_iota(jnp.int32, (KW, QW), 1) % BLOCK
        in_band = jnp.abs(j - BLOCK - r) <= WINDOW
        for kv in range(n_kv):
            bias = jnp.where(in_band, t5_ref[kv] * LOG2E, NEG_INF)
            for e in range(4):
                keep = (j >= BLOCK if e & 1 else True) & (j < 2 * BLOCK if e & 2 else True)
                bias_ref[e, kv] = bias if e == 0 else jnp.where(keep, bias, NEG_INF)

    qg = qg_ref[...] * (D ** -0.5 * LOG2E)
    kg = kg_ref[...]
    for kv in range(n_kv):
        sl = slice(kv * D, (kv + 1) * D)
        kall = jnp.concatenate([kp_ref[0][:, sl], km_ref[0][:, sl], kn_ref[0][:, sl]], axis=0)
        kall = _rms(kall.astype(F32), kg).astype(BF16)
        vall = jnp.concatenate([vp_ref[0][:, sl], vm_ref[0][:, sl], vn_ref[0][:, sl]], axis=0)
        sink = per_head_row(sink_ref, kv) * LOG2E
        for qb in range(SWA_QBLOCKS):
            i = step * SWA_QBLOCKS + qb
            edge = (i == 0).astype(jnp.int32) + 2 * (i == nb - 1).astype(jnp.int32)
            rows = slice(qb * BLOCK, (qb + 1) * BLOCK)
            k3 = kall[qb * BLOCK:qb * BLOCK + KW]
            v3 = vall[qb * BLOCK:qb * BLOCK + KW]
            qs = jnp.concatenate(
                [_rms(q_ref[0, rows, (kv * G + g) * D:(kv * G + g + 1) * D].astype(F32), qg).astype(BF16)
                 for g in range(G)], axis=0)
            st = _dot_nt(k3, qs) + bias_ref[edge, kv]
            m = jnp.maximum(jnp.max(st, axis=0, keepdims=True), sink)
            p = jnp.exp2(st - m)
            denom = jnp.sum(p, axis=0, keepdims=True) + jnp.exp2(sink - m)
            o = (_dot_tn(v3, p.astype(BF16)) / denom).T
            for g in range(G):
                h = kv * G + g
                o_ref[0, rows, h * D:(h + 1) * D] = o[g * BLOCK:(g + 1) * BLOCK].astype(o_ref.dtype)


def _t5_bucket(rel):
    nb = T5_BUCKETS // 2
    max_exact = nb // 2
    ret = jnp.where(rel > 0, nb, 0)
    n = jnp.abs(rel)
    nf = jnp.maximum(n, 1).astype(jnp.float32)
    large = max_exact + (jnp.log(nf / max_exact) / math.log(T5_MAX_DIST / max_exact)
                         * (nb - max_exact)).astype(jnp.int32)
    large = jnp.minimum(large, nb - 1)
    return ret + jnp.where(n < max_exact, n, large)


def window_attention(proj, sink, t5_table, q_gain, k_gain, *, n_heads, n_kv, q_off, k_off, v_off):
    b, s, _ = proj.shape
    D = HEAD_DIM
    nb = s // BLOCK
    qw = n_heads * D
    kw = n_kv * D
    G = n_heads // n_kv
    n, L = 3 * BLOCK, 4 * BLOCK
    rel = jnp.arange(-(2 * BLOCK - 1), 2 * BLOCK)
    tbl = t5_table.astype(F32)[_t5_bucket(rel)]
    v = jnp.pad(tbl[::-1].T, ((0, 0), (0, 1)))
    skew = jnp.tile(v, (1, n + 1))[:, :n * (L + 1)].reshape(n_heads, n, L + 1)[:, :, :BLOCK]
    t5 = skew[:, ::-1, :]
    t5 = t5.reshape(n_kv, G, n, BLOCK).transpose(0, 2, 1, 3).reshape(n_kv, n, G * BLOCK)

    QB = SWA_QBLOCKS
    assert nb % QB == 0

    def kv_specs(off):
        base = off // kw
        return [pl.BlockSpec((1, BLOCK, kw), lambda bi, j: (bi, jnp.maximum(j * QB - 1, 0), base)),
                pl.BlockSpec((1, QB * BLOCK, kw), lambda bi, j: (bi, j, base)),
                pl.BlockSpec((1, BLOCK, kw), lambda bi, j: (bi, jnp.minimum(j * QB + QB, nb - 1), base))]

    kern = functools.partial(_swa_kernel, n_heads=n_heads, n_kv=n_kv, nb=nb)
    return pl.pallas_call(
        kern,
        grid=(b, nb // QB),
        in_specs=[
            pl.BlockSpec(memory_space=pltpu.SMEM),
            pl.BlockSpec((1, QB * BLOCK, qw), lambda bi, j: (bi, j, q_off // qw)),
            *kv_specs(k_off), *kv_specs(v_off),
            pl.BlockSpec((n_kv, 3 * BLOCK, G * BLOCK), lambda bi, j: (0, 0, 0)),
            pl.BlockSpec((1, D), lambda bi, j: (0, 0)),
            pl.BlockSpec((1, D), lambda bi, j: (0, 0)),
        ],
        out_specs=pl.BlockSpec((1, QB * BLOCK, qw), lambda bi, j: (bi, j, 0)),
        out_shape=jax.ShapeDtypeStruct((b, s, qw), BF16),
        scratch_shapes=[pltpu.VMEM((4, n_kv, 3 * BLOCK, G * BLOCK), F32)],
        compiler_params=_params(("arbitrary", "arbitrary")),
        name="window_attention",
    )(sink, proj, proj, proj, proj, proj, proj, proj, t5,
      q_gain.reshape(1, D), k_gain.reshape(1, D))


def _axial_head_perm():
    q4 = HEAD_DIM // 4
    return [blk * q4 + j for blk in (0, 2, 1, 3) for j in range(q4)]


def _axial_rope(a, cc, ss):
    return a * cc + pltpu.roll(a, HEAD_DIM // 2, 1) * ss


ONES_ROWS = 16


def _axial_proj_kernel(x_ref, g_ref, w_ref, cc_ref, ss_ref, qg_ref, kg_ref, qo_ref, ko_ref,
                       vt_ref, proj_ref, *, n_heads, n_kv):
    D = HEAD_DIM
    i = pl.program_id(0)

    @pl.when(i == 0)
    def _():
        proj_ref[1] = jnp.zeros(proj_ref.shape[1:], proj_ref.dtype)

    def step(done, cur):
        proj = proj_ref[done]
        cc, ss = cc_ref[...], ss_ref[...]
        qg = qg_ref[...] * (D ** -0.5 * LOG2E)
        kg = kg_ref[...]
        for hh in range(n_heads):
            hs = slice(hh * D, (hh + 1) * D)
            qo_ref[:, hs] = _axial_rope(_rms(proj[:, hs], qg), cc, ss).astype(qo_ref.dtype)
        k_off, v_off = n_heads * D, (n_heads + n_kv) * D
        for hh in range(n_kv):
            hs = slice(hh * D, (hh + 1) * D)
            k = _rms(proj[:, k_off + hh * D:k_off + (hh + 1) * D], kg)
            ko_ref[:, hs] = _axial_rope(k, cc, ss).astype(ko_ref.dtype)
            vt_ref[0, hh, :D, :] = proj[:, v_off + hh * D:v_off + (hh + 1) * D].T.astype(vt_ref.dtype)
            vt_ref[0, hh, D:, :] = jnp.ones((ONES_ROWS, vt_ref.shape[-1]), vt_ref.dtype)

        h = _rms(x_ref[...], g_ref[...]).astype(BF16)
        proj_ref[cur] = _dot(h, w_ref[...])

    for parity in range(2):
        pl.when(i % 2 == parity)(functools.partial(step, 1 - parity, parity))


def axial_proj(x2, gain, w, tables, q_gain, k_gain, *, seq, n_heads, n_kv, tm):
    t, d = x2.shape
    D = HEAD_DIM
    qw, kw = n_heads * D, n_kv * D
    assert w.shape[1] == qw + 2 * kw and seq % tm == 0
    perm = jnp.asarray(_axial_head_perm())
    q4 = D // 4
    w4 = w[:, :qw + kw].reshape(d, n_heads + n_kv, 4, q4)
    wqk = jnp.stack([w4[:, :, blk] for blk in (0, 2, 1, 3)], axis=2)
    w = jnp.concatenate([wqk.reshape(d, qw + kw), w[:, qw + kw:]], axis=1).astype(BF16)
    q_gain, k_gain = q_gain[perm], k_gain[perm]
    spb = seq // tm
    n = t // tm
    kern = functools.partial(_axial_proj_kernel, n_heads=n_heads, n_kv=n_kv)

    def done(i):
        return jnp.maximum(i - 1, 0)

    tab = pl.BlockSpec((tm, D), lambda i: (done(i) % spb, 0))
    return pl.pallas_call(
        kern,
        grid=(n + 1,),
        in_specs=[pl.BlockSpec((tm, d), lambda i: (jnp.minimum(i, n - 1), 0)), _resident((1, d)),
                  _resident((d, w.shape[1])), tab, tab, _resident((1, D)), _resident((1, D))],
        out_specs=[
            pl.BlockSpec((tm, qw), lambda i: (done(i), 0)),
            pl.BlockSpec((tm, kw), lambda i: (done(i), 0)),
            pl.BlockSpec((1, n_kv, D + ONES_ROWS, tm), lambda i: (done(i) // spb, 0, 0, done(i) % spb)),
        ],
        out_shape=[jax.ShapeDtypeStruct((t, qw), BF16), jax.ShapeDtypeStruct((t, kw), BF16),
                   jax.ShapeDtypeStruct((t // seq, n_kv, D + ONES_ROWS, seq), BF16)],
        scratch_shapes=[pltpu.VMEM((2, tm, w.shape[1]), F32)],
        compiler_params=_params(("arbitrary",)),
        name="axial_proj",
    )(x2, gain.reshape(1, d), w, *tables, q_gain.reshape(1, D), k_gain.reshape(1, D))


FAST_SUM_MIN = 2.0 ** -80
FAST_SUM_MAX = 2.0 ** 100


def _flash_kernel(q_ref, k_ref, vt_ref, o_ref, acc_ref, kn_ref, p_ref, l_ref, *, G, tq, tk, nk):
    D = HEAD_DIM
    R = G * tq
    qs = jnp.concatenate([q_ref[0][:, g * D:(g + 1) * D] for g in range(G)], axis=0)

    @pl.when(pl.program_id(2) == 0)
    def _():
        def kbody(c, mx):
            start = pl.multiple_of(c * tk, tk)
            kc = k_ref[0, pl.ds(start, tk), :].astype(F32)
            return jnp.maximum(mx, jnp.max(jnp.sum(kc * kc, axis=-1, keepdims=True), axis=0, keepdims=True))
        kn2 = lax.fori_loop(0, nk, kbody, jnp.zeros((1, 1), F32))
        kn_ref[...] = jnp.broadcast_to(kn2, kn_ref.shape)

    def scores(c):
        start = pl.multiple_of(c * tk, tk)
        return _dot_nt(k_ref[0, pl.ds(start, tk), :], qs)

    def vt_chunk(c, rows=D + ONES_ROWS):
        return vt_ref[0, 0, :rows, pl.ds(pl.multiple_of(c * tk, tk), tk)]


    def produce_fixed(c, slot, shift):
        p = jnp.exp2(scores(c) - shift)
        l_ref[...] += jnp.sum(p, axis=0, keepdims=True)
        p_ref[slot] = p.astype(BF16)

    def consume_fixed(c, slot, shift):
        acc_ref[:D, :] += _dot(vt_chunk(c, D), p_ref[slot])

    def first_pass(shift):
        def body(c2, carry):
            c = 2 * c2
            produce_fixed(c + 1, 1, shift)
            consume_fixed(c, 0, shift)
            produce_fixed(c + 2, 0, shift)
            consume_fixed(c + 1, 1, shift)
            return carry

        acc_ref[...] = jnp.zeros_like(acc_ref)
        l_ref[...] = jnp.zeros_like(l_ref)
        produce_fixed(0, 0, shift)
        lax.fori_loop(0, nk // 2 - 1, body, 0)
        produce_fixed(nk - 1, 1, shift)
        consume_fixed(nk - 2, 0, shift)
        consume_fixed(nk - 1, 1, shift)

    def running_max_pass():
        def body(c, m_old):
            st = scores(c)
            m_new = jnp.maximum(m_old, jnp.max(st, axis=0, keepdims=True))
            alpha = jnp.exp2(m_old - m_new)
            p = jnp.exp2((st - m_new).astype(BF16))
            acc_ref[...] = alpha * acc_ref[...] + _dot(vt_chunk(c), p)
            return m_new

        acc_ref[...] = jnp.zeros_like(acc_ref)
        lax.fori_loop(0, nk, body, jnp.full((1, R), -jnp.inf, F32))

    def write_out(sums):
        out = (acc_ref[:D, :] / sums).T
        for g in range(G):
            o_ref[0, :, g * D:(g + 1) * D] = out[g * tq:(g + 1) * tq].astype(o_ref.dtype)

    qf = qs.astype(F32)
    qn2 = _dot_nt(jnp.ones((8, D), BF16), (qf * qf).astype(BF16))[:1]
    first_pass(jnp.sqrt(qn2 * kn_ref[:1, :1]))
    sums = l_ref[...]
    trusted = (jnp.min(sums) >= FAST_SUM_MIN) & (jnp.max(sums) <= FAST_SUM_MAX)

    @pl.when(trusted)
    def _():
        write_out(sums)

    @pl.when(jnp.logical_not(trusted))
    def _():
        running_max_pass()
        write_out(acc_ref[D:D + 1, :])


def flash_attention(q, k, vt, *, n_heads, n_kv, tq, tk):
    b, s, _ = q.shape
    D = HEAD_DIM
    G = n_heads // n_kv
    assert s % (2 * tk) == 0 and s // tk >= 2
    kern = functools.partial(_flash_kernel, G=G, tq=tq, tk=tk, nk=s // tk)
    return pl.pallas_call(
        kern,
        grid=(b, n_kv, s // tq),
        in_specs=[
            pl.BlockSpec((1, tq, G * D), lambda bi, kv, qi: (bi, qi, kv)),
            pl.BlockSpec((1, s, D), lambda bi, kv, qi: (bi, 0, kv)),
            pl.BlockSpec((1, 1, D + ONES_ROWS, s), lambda bi, kv, qi: (bi, kv, 0, 0)),
        ],
        out_specs=pl.BlockSpec((1, tq, G * D), lambda bi, kv, qi: (bi, qi, kv)),
        out_shape=jax.ShapeDtypeStruct((b, s, n_heads * D), BF16),
        scratch_shapes=[pltpu.VMEM((D + ONES_ROWS, G * tq), F32), pltpu.VMEM((8, D), F32), pltpu.VMEM((2, tk, G * tq), BF16),
                        pltpu.VMEM((1, G * tq), F32)],
        compiler_params=_params(("parallel", "parallel", "arbitrary")),
        name="flash_attention",
    )(q, k, vt)


def _rope_angles(pos, dim, theta):
    inv = theta ** (-jnp.arange(0, dim, 2, dtype=jnp.float32) / dim)
    return pos.astype(jnp.float32)[:, None] * inv[None, :]


def _retention_tables(s):
    ang = _rope_angles(jnp.arange(s), RET_DK, RET_THETA)
    c, sn = jnp.cos(ang), jnp.sin(ang)
    sign = jnp.where(jnp.arange(RET_DK) < RET_DK // 2, -1.0, 1.0).astype(F32)
    return jnp.tile(c, (1, 2)), jnp.tile(sn, (1, 2)) * sign


def _axial_tables(s):
    rows = s // GRID_W
    half = HEAD_DIM // 2
    ar = _rope_angles(jnp.arange(rows), half, AX_THETA)
    ac = _rope_angles(jnp.arange(GRID_W), half, AX_THETA)
    cr, sr = (jnp.repeat(f(ar), GRID_W, axis=0) for f in (jnp.cos, jnp.sin))
    ccol, scol = (jnp.tile(f(ac), (rows, 1)) for f in (jnp.cos, jnp.sin))
    sign = jnp.where(jnp.arange(HEAD_DIM) < HEAD_DIM // 2, -1.0, 1.0).astype(F32)
    cc = jnp.tile(jnp.concatenate([cr, ccol], axis=-1), (1, 2))
    ss = jnp.tile(jnp.concatenate([sr, scol], axis=-1), (1, 2)) * sign
    return cc, ss


def kernel(x, norm_mix, norm_mlp, w_in_even, w_out_even, ret_decay_logit, ret_norm, swa_q_norm,
           swa_k_norm, swa_sink, t5_table, w_in_odd, w_out_odd, ax_q_norm, ax_k_norm, w_mlp_up,
           w_mlp_down):
    b, s, d = x.shape
    t = b * s
    depth = norm_mix.shape[0]
    ret_heads = ret_decay_logit.shape[-1]
    ret_q = ret_heads * RET_DK
    ret_v = ret_heads * RET_DV
    swa_heads = swa_sink.shape[-1]
    swa_q = swa_heads * HEAD_DIM
    swa_kv = SWA_KV_HEADS * HEAD_DIM
    ax_q = w_out_odd.shape[1]
    ax_heads = ax_q // HEAD_DIM
    ax_kv = AX_KV_HEADS * HEAD_DIM

    x2 = x.reshape(t, d)
    ret_tabs = _retention_tables(s)
    ax_tabs = _axial_tables(s)

    for layer in range(depth):
        i = layer // 2
        if layer % 2 == 0:
            proj = norm_proj(x2, norm_mix[layer], w_in_even, i, tm=DENSE_ROWS)
            proj = proj.reshape(b, s, -1)
            ya = retention(proj, ret_decay_logit[i], ret_norm[i], *ret_tabs, n_heads=ret_heads,
                           q_off=0, k_off=ret_q, v_off=2 * ret_q, g_off=2 * ret_q + ret_v, cs=32)
            off = 2 * ret_q + 2 * ret_v
            yb = window_attention(proj, swa_sink[i], t5_table, swa_q_norm[i], swa_k_norm[i],
                                  n_heads=swa_heads, n_kv=SWA_KV_HEADS, q_off=off,
                                  k_off=off + swa_q, v_off=off + swa_q + swa_kv)
            acts, w_out = [ya.reshape(t, -1), yb.reshape(t, -1)], w_out_even
        else:
            qp, kp, vt = axial_proj(x2, norm_mix[layer], w_in_odd[i], ax_tabs,
                                    ax_q_norm[i], ax_k_norm[i], seq=s, n_heads=ax_heads,
                                    n_kv=AX_KV_HEADS, tm=DENSE_ROWS)
            y = flash_attention(qp.reshape(b, s, -1), kp.reshape(b, s, -1), vt, n_heads=ax_heads,
                                n_kv=AX_KV_HEADS, tq=1024, tk=512)
            acts, w_out = [y.reshape(t, -1)], w_out_odd
        x2 = out_mlp(x2, acts, w_out, i, norm_mlp[layer], w_mlp_up, w_mlp_down, layer,
                     tm=DENSE_ROWS)
    return x2.reshape(b, s, d)
```

```python
import functools
import math

import jax
import jax.numpy as jnp
from jax import lax
from jax.experimental import pallas as pl
from jax.experimental.pallas import tpu as pltpu

F32 = jnp.float32
BF16 = jnp.bfloat16

EPS = 1e-6
NEG_INF = -1e30
LOG2E = math.log2(math.e)
HEAD_DIM = 128
BLOCK = 128
GRID_W = 64
RET_DK = 128
RET_DV = 256
RET_CHUNK = 128
RET_THETA = 10000.0
SWA_KV_HEADS = 2
SWA_QBLOCKS = 8
WINDOW = 128
T5_BUCKETS = 32
T5_MAX_DIST = 128
AX_KV_HEADS = 2
AX_THETA = 10000.0

VMEM_LIMIT_BYTES = 56 * 1024 * 1024
DENSE_ROWS = 512


def _params(semantics):
    return pltpu.CompilerParams(dimension_semantics=semantics, vmem_limit_bytes=VMEM_LIMIT_BYTES)


def _rms(x, gain):
    ms = jnp.mean(x * x, axis=-1, keepdims=True)
    return x * lax.rsqrt(ms + EPS) * gain


def _dot(a, b):
    return jnp.dot(a, b, preferred_element_type=F32)


def _dot_nt(a, b):
    return lax.dot_general(a, b, (((1,), (1,)), ((), ())), preferred_element_type=F32)


def _dot_tn(a, b):
    return lax.dot_general(a, b, (((0,), (0,)), ((), ())), preferred_element_type=F32)


def _resident(shape, row=0):
    idx = (row,) + (0,) * (len(shape) - 1)
    return pl.BlockSpec(shape, lambda i: idx, pipeline_mode=pl.Buffered(1))


STAGE_BYTES = 2 * 1024 * 1024


def _stage_rows(rows, cols):
    r = 1
    while 2 * r * cols * 4 <= STAGE_BYTES and rows % (2 * r) == 0:
        r *= 2
    return r


def _stream_cast(src_hbm, index, dst_ref, stage_ref, sem_ref):
    rows = stage_ref.shape[1]
    n = src_hbm.shape[1] // rows

    def copy(c):
        return pltpu.make_async_copy(src_hbm.at[index, pl.ds(c * rows, rows)], stage_ref.at[c % 2],
                                     sem_ref.at[c % 2])

    copy(0).start()
    for c in range(n):
        if c + 1 < n:
            copy(c + 1).start()
        copy(c).wait()
        dst_ref[c * rows:(c + 1) * rows, :] = stage_ref[c % 2].astype(dst_ref.dtype)


def _norm_proj_kernel(x_ref, g_ref, w_hbm, o_ref, w_ref, stage_ref, sem_ref, *, w_index):
    @pl.when(pl.program_id(0) == 0)
    def _():
        _stream_cast(w_hbm, w_index, w_ref, stage_ref, sem_ref)

    h = _rms(x_ref[...], g_ref[...]).astype(BF16)
    o_ref[...] = _dot(h, w_ref[...]).astype(o_ref.dtype)


def norm_proj(x2, gain, w, w_index, *, tm):
    t, d = x2.shape
    n = w.shape[2]
    return pl.pallas_call(
        functools.partial(_norm_proj_kernel, w_index=w_index),
        grid=(t // tm,),
        in_specs=[pl.BlockSpec((tm, d), lambda i: (i, 0)), _resident((1, d)),
                  pl.BlockSpec(memory_space=pl.ANY)],
        out_specs=pl.BlockSpec((tm, n), lambda i: (i, 0)),
        out_shape=jax.ShapeDtypeStruct((t, n), BF16),
        scratch_shapes=[pltpu.VMEM((d, n), BF16), pltpu.VMEM((2, _stage_rows(d, n), n), F32),
                        pltpu.SemaphoreType.DMA((2,))],
        compiler_params=_params(("arbitrary",)),
        name="norm_proj",
    )(x2, gain.reshape(1, d), w)


def _out_mlp_kernel(x_ref, *refs, n_act, wo_index, mlp_index):
    act_refs = refs[:n_act]
    (wo_hbm, g_ref, wu_hbm, wd_hbm, o_ref,
     wo_ref, wu_ref, wd_ref, stage_wide, stage_narrow, sem_wide, sem_narrow) = refs[n_act:]

    @pl.when(pl.program_id(0) == 0)
    def _():
        _stream_cast(wo_hbm, wo_index, wo_ref, stage_narrow, sem_narrow)
        _stream_cast(wu_hbm, mlp_index, wu_ref, stage_wide, sem_wide)
        _stream_cast(wd_hbm, mlp_index, wd_ref, stage_narrow, sem_narrow)

    y = x_ref[...]
    k = act_refs[0].shape[1]
    for s, a_ref in enumerate(act_refs):
        y = y + _dot(a_ref[...], wo_ref[s * k:(s + 1) * k, :])
    h = _rms(y, g_ref[...]).astype(BF16)
    a = jnp.square(jnp.maximum(_dot(h, wu_ref[...]), 0.0)).astype(BF16)
    o_ref[...] = y + _dot(a, wd_ref[...])


def out_mlp(x2, acts, w_out, wo_index, gain, wu, wd, mlp_index, *, tm):
    t, d = x2.shape
    ff = wu.shape[2]
    k = acts[0].shape[1]
    assert all(a.shape[1] == k for a in acts) and len(acts) * k == w_out.shape[1]
    rows_wide, rows_narrow = _stage_rows(d, ff), _stage_rows(math.gcd(ff, w_out.shape[1]), d)

    def row_tile(width):
        return pl.BlockSpec((tm, width), lambda i: (i, 0))

    hbm = pl.BlockSpec(memory_space=pl.ANY)
    return pl.pallas_call(
        functools.partial(_out_mlp_kernel, n_act=len(acts), wo_index=wo_index, mlp_index=mlp_index),
        grid=(t // tm,),
        in_specs=[row_tile(d)] + [row_tile(k) for _ in acts] + [hbm, _resident((1, d)), hbm, hbm],
        out_specs=row_tile(d),
        out_shape=jax.ShapeDtypeStruct((t, d), F32),
        scratch_shapes=[
            pltpu.VMEM(w_out.shape[1:], BF16), pltpu.VMEM((d, ff), BF16), pltpu.VMEM((ff, d), BF16),
            pltpu.VMEM((2, rows_wide, ff), F32), pltpu.VMEM((2, rows_narrow, d), F32),
            pltpu.SemaphoreType.DMA((2,)), pltpu.SemaphoreType.DMA((2,)),
        ],
        compiler_params=_params(("arbitrary",)),
        name="out_mlp",
    )(x2, *acts, w_out, gain.reshape(1, d), wu, wd)


def _log_sigmoid(x):
    return -(jnp.maximum(-x, 0.0) + jnp.log1p(jnp.exp(-jnp.abs(x))))


def _ret_kernel(dl_ref, q_ref, k_ref, v_ref, g_ref, cos_ref, sin_ref, gn_ref, o_ref,
                sb_ref, sf_ref, cb_ref, dec_ref, *, cs):
    C = RET_CHUNK
    h = pl.program_id(1)
    phase = pl.program_id(2)
    t = pl.program_id(3)
    nsteps = pl.num_programs(3)

    lgf_w = _log_sigmoid(jnp.full((1, RET_DV), dl_ref[0, h], F32))
    lgb_w = _log_sigmoid(jnp.full((1, RET_DV), dl_ref[1, h], F32))

    @pl.when((phase == 0) & (t == 0))
    def _():
        row = lax.broadcasted_iota(jnp.int32, (C, C), 0).astype(F32)
        col = lax.broadcasted_iota(jnp.int32, (C, C), 1).astype(F32)
        lgf = lgf_w[:, :C]
        lgb = lgb_w[:, :C]
        diff = row - col
        dec_ref[0] = jnp.where(diff >= 0, jnp.exp(jnp.maximum(diff, 0.0) * lgf),
                               jnp.exp(jnp.maximum(-diff, 0.0) * lgb))
        dec_ref[1] = jnp.exp((row + 1.0) * lgf)
        dec_ref[2] = jnp.exp((C - 1.0 - row) * lgf)
        dec_ref[3] = jnp.exp((C - row) * lgb)
        dec_ref[4] = jnp.exp(row * lgb)
        sb_ref[...] = jnp.zeros_like(sb_ref)
        sf_ref[...] = jnp.zeros_like(sf_ref)

    def rope(a, rows):
        return a * cos_ref[rows, :] + pltpu.roll(a, RET_DK // 2, 1) * sin_ref[rows, :]

    @pl.when(phase == 0)
    def _():
        sb = sb_ref[...]
        cdec = jnp.exp(C * lgb_w)
        for ci in reversed(range(cs)):
            rows = slice(ci * C, (ci + 1) * C)
            qr = rope(q_ref[0, rows, :].astype(F32), rows)
            kr = rope(k_ref[0, rows, :].astype(F32), rows) * (RET_DK ** -0.5)
            c = (nsteps - 1 - t) * cs + ci
            cb_ref[c] = _dot((qr * dec_ref[3]).astype(BF16), sb.astype(BF16))
            sb = cdec * sb + _dot_tn((kr * dec_ref[4]).astype(BF16), v_ref[0, rows, :])
        sb_ref[...] = sb

    @pl.when(phase == 1)
    def _():
        sf = sf_ref[...]
        cdec = jnp.exp(C * lgf_w)
        for ci in range(cs):
            rows = slice(ci * C, (ci + 1) * C)
            qr = rope(q_ref[0, rows, :].astype(F32), rows)
            kr = rope(k_ref[0, rows, :].astype(F32), rows) * (RET_DK ** -0.5)
            v = v_ref[0, rows, :]
            s = _dot_nt(qr.astype(BF16), kr.astype(BF16))
            lhs = jnp.concatenate([(s * dec_ref[0]).astype(BF16), (qr * dec_ref[1]).astype(BF16)], axis=1)
            rhs = jnp.concatenate([v, sf.astype(BF16)], axis=0)
            sf = cdec * sf + _dot_tn((kr * dec_ref[2]).astype(BF16), v)
            tot = _dot(lhs, rhs) + cb_ref[t * cs + ci]
            y = _rms(tot, gn_ref[...])
            g = g_ref[0, rows, :].astype(F32)
            o_ref[0, rows, :] = (g * jax.nn.sigmoid(g) * y).astype(o_ref.dtype)
        sf_ref[...] = sf


def retention(proj, decay_logit, gn_gain, cos2, sin2, *, n_heads, q_off, k_off, v_off, g_off, cs):
    b, s, _ = proj.shape
    C = RET_CHUNK
    nc = s // C
    assert nc % cs == 0
    ts = cs * C
    nsteps = nc // cs
    ret_v = n_heads * RET_DV

    def step(p, t):
        return jnp.where(p == 0, nsteps - 1 - t, t)

    def spec(width, off):
        base = off // width
        return pl.BlockSpec((1, ts, width), lambda bi, h, p, t: (bi, step(p, t), base + h))

    kern = functools.partial(_ret_kernel, cs=cs)
    return pl.pallas_call(
        kern,
        grid=(b, n_heads, 2, nsteps),
        in_specs=[
            pl.BlockSpec(memory_space=pltpu.SMEM),
            spec(RET_DK, q_off),
            spec(RET_DK, k_off),
            spec(RET_DV, v_off),
            spec(RET_DV, g_off),
            pl.BlockSpec((ts, RET_DK), lambda bi, h, p, t: (step(p, t), 0)),
            pl.BlockSpec((ts, RET_DK), lambda bi, h, p, t: (step(p, t), 0)),
            pl.BlockSpec((1, RET_DV), lambda bi, h, p, t: (0, h)),
        ],
        out_specs=pl.BlockSpec((1, ts, RET_DV), lambda bi, h, p, t: (bi, jnp.where(p == 0, 0, t), h)),
        out_shape=jax.ShapeDtypeStruct((b, s, ret_v), BF16),
        scratch_shapes=[
            pltpu.VMEM((RET_DK, RET_DV), F32),
            pltpu.VMEM((RET_DK, RET_DV), F32),
            pltpu.VMEM((nc, C, RET_DV), F32),
            pltpu.VMEM((5, C, C), F32),
        ],
        compiler_params=_params(("parallel", "parallel", "arbitrary", "arbitrary")),
        name="retention",
    )(decay_logit, proj, proj, proj, proj, cos2, sin2, gn_gain.reshape(1, ret_v))


def _swa_kernel(sink_ref, q_ref, kp_ref, km_ref, kn_ref, vp_ref, vm_ref, vn_ref,
                t5_ref, qg_ref, kg_ref, o_ref, bias_ref, *, n_heads, n_kv, nb):
    step = pl.program_id(1)
    D = HEAD_DIM
    G = n_heads // n_kv
    KW = 3 * BLOCK
    QW = G * BLOCK
    lane_head = lax.broadcasted_iota(jnp.int32, (1, QW), 1) // BLOCK

    def per_head_row(ref, kv, *idx):
        row = jnp.full((1, QW), ref[(*idx, kv * G)], F32)
        for g in range(1, G):
            row = jnp.where(lane_head == g, ref[(*idx, kv * G + g)], row)
        return row

    @pl.when((pl.program_id(0) == 0) & (step == 0))
    def _():
        j = lax.broadcasted_iota(jnp.int32, (KW, QW), 0)
        r = lax.broadcasted_iota(jnp.int32, (KW, QW), 1) % BLOCK
        in_band = jnp.abs(j - BLOCK - r) <= WINDOW
        for kv in range(n_kv):
            bias = jnp.where(in_band, t5_ref[kv] * LOG2E, NEG_INF)
            for e in range(4):
                keep = (j >= BLOCK if e & 1 else True) & (j < 2 * BLOCK if e & 2 else True)
                bias_ref[e, kv] = bias if e == 0 else jnp.where(keep, bias, NEG_INF)

    qg = qg_ref[...] * (D ** -0.5 * LOG2E)
    kg = kg_ref[...]
    for kv in range(n_kv):
        sl = slice(kv * D, (kv + 1) * D)
        kall = jnp.concatenate([kp_ref[0][:, sl], km_ref[0][:, sl], kn_ref[0][:, sl]], axis=0)
        kall = _rms(kall.astype(F32), kg).astype(BF16)
        vall = jnp.concatenate([vp_ref[0][:, sl], vm_ref[0][:, sl], vn_ref[0][:, sl]], axis=0)
        sink = per_head_row(sink_ref, kv) * LOG2E
        for qb in range(SWA_QBLOCKS):
            i = step * SWA_QBLOCKS + qb
            edge = (i == 0).astype(jnp.int32) + 2 * (i == nb - 1).astype(jnp.int32)
            rows = slice(qb * BLOCK, (qb + 1) * BLOCK)
            k3 = kall[qb * BLOCK:qb * BLOCK + KW]
            v3 = vall[qb * BLOCK:qb * BLOCK + KW]
            qs = jnp.concatenate(
                [_rms(q_ref[0, rows, (kv * G + g) * D:(kv * G + g + 1) * D].astype(F32), qg).astype(BF16)
                 for g in range(G)], axis=0)
            st = _dot_nt(k3, qs) + bias_ref[edge, kv]
            m = jnp.maximum(jnp.max(st, axis=0, keepdims=True), sink)
            p = jnp.exp2(st - m)
            denom = jnp.sum(p, axis=0, keepdims=True) + jnp.exp2(sink - m)
            o = (_dot_tn(v3, p.astype(BF16)) / denom).T
            for g in range(G):
                h = kv * G + g
                o_ref[0, rows, h * D:(h + 1) * D] = o[g * BLOCK:(g + 1) * BLOCK].astype(o_ref.dtype)


def _t5_bucket(rel):
    nb = T5_BUCKETS // 2
    max_exact = nb // 2
    ret = jnp.where(rel > 0, nb, 0)
    n = jnp.abs(rel)
    nf = jnp.maximum(n, 1).astype(jnp.float32)
    large = max_exact + (jnp.log(nf / max_exact) / math.log(T5_MAX_DIST / max_exact)
                         * (nb - max_exact)).astype(jnp.int32)
    large = jnp.minimum(large, nb - 1)
    return ret + jnp.where(n < max_exact, n, large)


def window_attention(proj, sink, t5_table, q_gain, k_gain, *, n_heads, n_kv, q_off, k_off, v_off):
    b, s, _ = proj.shape
    D = HEAD_DIM
    nb = s // BLOCK
    qw = n_heads * D
    kw = n_kv * D
    G = n_heads // n_kv
    n, L = 3 * BLOCK, 4 * BLOCK
    rel = jnp.arange(-(2 * BLOCK - 1), 2 * BLOCK)
    tbl = t5_table.astype(F32)[_t5_bucket(rel)]
    v = jnp.pad(tbl[::-1].T, ((0, 0), (0, 1)))
    skew = jnp.tile(v, (1, n + 1))[:, :n * (L + 1)].reshape(n_heads, n, L + 1)[:, :, :BLOCK]
    t5 = skew[:, ::-1, :]
    t5 = t5.reshape(n_kv, G, n, BLOCK).transpose(0, 2, 1, 3).reshape(n_kv, n, G * BLOCK)

    QB = SWA_QBLOCKS
    assert nb % QB == 0

    def kv_specs(off):
        base = off // kw
        return [pl.BlockSpec((1, BLOCK, kw), lambda bi, j: (bi, jnp.maximum(j * QB - 1, 0), base)),
                pl.BlockSpec((1, QB * BLOCK, kw), lambda bi, j: (bi, j, base)),
                pl.BlockSpec((1, BLOCK, kw), lambda bi, j: (bi, jnp.minimum(j * QB + QB, nb - 1), base))]

    kern = functools.partial(_swa_kernel, n_heads=n_heads, n_kv=n_kv, nb=nb)
    return pl.pallas_call(
        kern,
        grid=(b, nb // QB),
        in_specs=[
            pl.BlockSpec(memory_space=pltpu.SMEM),
            pl.BlockSpec((1, QB * BLOCK, qw), lambda bi, j: (bi, j, q_off // qw)),
            *kv_specs(k_off), *kv_specs(v_off),
            pl.BlockSpec((n_kv, 3 * BLOCK, G * BLOCK), lambda bi, j: (0, 0, 0)),
            pl.BlockSpec((1, D), lambda bi, j: (0, 0)),
            pl.BlockSpec((1, D), lambda bi, j: (0, 0)),
        ],
        out_specs=pl.BlockSpec((1, QB * BLOCK, qw), lambda bi, j: (bi, j, 0)),
        out_shape=jax.ShapeDtypeStruct((b, s, qw), BF16),
        scratch_shapes=[pltpu.VMEM((4, n_kv, 3 * BLOCK, G * BLOCK), F32)],
        compiler_params=_params(("arbitrary", "arbitrary")),
        name="window_attention",
    )(sink, proj, proj, proj, proj, proj, proj, proj, t5,
      q_gain.reshape(1, D), k_gain.reshape(1, D))


def _axial_head_perm():
    q4 = HEAD_DIM // 4
    return [blk * q4 + j for blk in (0, 2, 1, 3) for j in range(q4)]


def _axial_rope(a, cc, ss):
    return a * cc + pltpu.roll(a, HEAD_DIM // 2, 1) * ss


ONES_ROWS = 16


def _axial_proj_kernel(x_ref, g_ref, w_ref, cc_ref, ss_ref, qg_ref, kg_ref, qo_ref, ko_ref,
                       vt_ref, proj_ref, *, n_heads, n_kv):
    D = HEAD_DIM
    i = pl.program_id(0)

    @pl.when(i == 0)
    def _():
        proj_ref[1] = jnp.zeros(proj_ref.shape[1:], proj_ref.dtype)

    def step(done, cur):
        proj = proj_ref[done]
        cc, ss = cc_ref[...], ss_ref[...]
        qg = qg_ref[...] * (D ** -0.5 * LOG2E)
        kg = kg_ref[...]
        for hh in range(n_heads):
            hs = slice(hh * D, (hh + 1) * D)
            qo_ref[:, hs] = _axial_rope(_rms(proj[:, hs], qg), cc, ss).astype(qo_ref.dtype)
        k_off, v_off = n_heads * D, (n_heads + n_kv) * D
        for hh in range(n_kv):
            hs = slice(hh * D, (hh + 1) * D)
            k = _rms(proj[:, k_off + hh * D:k_off + (hh + 1) * D], kg)
            ko_ref[:, hs] = _axial_rope(k, cc, ss).astype(ko_ref.dtype)
            vt_ref[0, hh, :D, :] = proj[:, v_off + hh * D:v_off + (hh + 1) * D].T.astype(vt_ref.dtype)
            vt_ref[0, hh, D:, :] = jnp.ones((ONES_ROWS, vt_ref.shape[-1]), vt_ref.dtype)

        h = _rms(x_ref[...], g_ref[...]).astype(BF16)
        proj_ref[cur] = _dot(h, w_ref[...])

    for parity in range(2):
        pl.when(i % 2 == parity)(functools.partial(step, 1 - parity, parity))


def axial_proj(x2, gain, w, tables, q_gain, k_gain, *, seq, n_heads, n_kv, tm):
    t, d = x2.shape
    D = HEAD_DIM
    qw, kw = n_heads * D, n_kv * D
    assert w.shape[1] == qw + 2 * kw and seq % tm == 0
    perm = jnp.asarray(_axial_head_perm())
    q4 = D // 4
    w4 = w[:, :qw + kw].reshape(d, n_heads + n_kv, 4, q4)
    wqk = jnp.stack([w4[:, :, blk] for blk in (0, 2, 1, 3)], axis=2)
    w = jnp.concatenate([wqk.reshape(d, qw + kw), w[:, qw + kw:]], axis=1).astype(BF16)
    q_gain, k_gain = q_gain[perm], k_gain[perm]
    spb = seq // tm
    n = t // tm
    kern = functools.partial(_axial_proj_kernel, n_heads=n_heads, n_kv=n_kv)

    def done(i):
        return jnp.maximum(i - 1, 0)

    tab = pl.BlockSpec((tm, D), lambda i: (done(i) % spb, 0))
    return pl.pallas_call(
        kern,
        grid=(n + 1,),
        in_specs=[pl.BlockSpec((tm, d), lambda i: (jnp.minimum(i, n - 1), 0)), _resident((1, d)),
                  _resident((d, w.shape[1])), tab, tab, _resident((1, D)), _resident((1, D))],
        out_specs=[
            pl.BlockSpec((tm, qw), lambda i: (done(i), 0)),
            pl.BlockSpec((tm, kw), lambda i: (done(i), 0)),
            pl.BlockSpec((1, n_kv, D + ONES_ROWS, tm), lambda i: (done(i) // spb, 0, 0, done(i) % spb)),
        ],
        out_shape=[jax.ShapeDtypeStruct((t, qw), BF16), jax.ShapeDtypeStruct((t, kw), BF16),
                   jax.ShapeDtypeStruct((t // seq, n_kv, D + ONES_ROWS, seq), BF16)],
        scratch_shapes=[pltpu.VMEM((2, tm, w.shape[1]), F32)],
        compiler_params=_params(("arbitrary",)),
        name="axial_proj",
    )(x2, gain.reshape(1, d), w, *tables, q_gain.reshape(1, D), k_gain.reshape(1, D))


FAST_SUM_MIN = 2.0 ** -80
FAST_SUM_MAX = 2.0 ** 100


def _flash_kernel(q_ref, k_ref, vt_ref, o_ref, acc_ref, kn_ref, p_ref, l_ref, *, G, tq, tk, nk):
    D = HEAD_DIM
    R = G * tq
    qs = jnp.concatenate([q_ref[0][:, g * D:(g + 1) * D] for g in range(G)], axis=0)

    @pl.when(pl.program_id(2) == 0)
    def _():
        def kbody(c, mx):
            start = pl.multiple_of(c * tk, tk)
            kc = k_ref[0, pl.ds(start, tk), :].astype(F32)
            return jnp.maximum(mx, jnp.max(jnp.sum(kc * kc, axis=-1, keepdims=True), axis=0, keepdims=True))
        kn2 = lax.fori_loop(0, nk, kbody, jnp.zeros((1, 1), F32))
        kn_ref[...] = jnp.broadcast_to(kn2, kn_ref.shape)

    def scores(c):
        start = pl.multiple_of(c * tk, tk)
        return _dot_nt(k_ref[0, pl.ds(start, tk), :], qs)

    def vt_chunk(c, rows=D + ONES_ROWS):
        return vt_ref[0, 0, :rows, pl.ds(pl.multiple_of(c * tk, tk), tk)]


    def produce_fixed(c, slot, shift):
        p = jnp.exp2(scores(c) - shift)
        l_ref[...] += jnp.sum(p, axis=0, keepdims=True)
        p_ref[slot] = p.astype(BF16)

    def consume_fixed(c, slot, shift):
        acc_ref[:D, :] += _dot(vt_chunk(c, D), p_ref[slot])

    def first_pass(shift):
        def body(c2, carry):
            c = 2 * c2
            produce_fixed(c + 1, 1, shift)
            consume_fixed(c, 0, shift)
            produce_fixed(c + 2, 0, shift)
            consume_fixed(c + 1, 1, shift)
            return carry

        acc_ref[...] = jnp.zeros_like(acc_ref)
        l_ref[...] = jnp.zeros_like(l_ref)
        produce_fixed(0, 0, shift)
        lax.fori_loop(0, nk // 2 - 1, body, 0)
        produce_fixed(nk - 1, 1, shift)
        consume_fixed(nk - 2, 0, shift)
        consume_fixed(nk - 1, 1, shift)

    def running_max_pass():
        def body(c, m_old):
            st = scores(c)
            m_new = jnp.maximum(m_old, jnp.max(st, axis=0, keepdims=True))
            alpha = jnp.exp2(m_old - m_new)
            p = jnp.exp2((st - m_new).astype(BF16))
            acc_ref[...] = alpha * acc_ref[...] + _dot(vt_chunk(c), p)
            return m_new

        acc_ref[...] = jnp.zeros_like(acc_ref)
        lax.fori_loop(0, nk, body, jnp.full((1, R), -jnp.inf, F32))

    def write_out(sums):
        out = (acc_ref[:D, :] / sums).T
        for g in range(G):
            o_ref[0, :, g * D:(g + 1) * D] = out[g * tq:(g + 1) * tq].astype(o_ref.dtype)

    qf = qs.astype(F32)
    qn2 = _dot_nt(jnp.ones((8, D), BF16), (qf * qf).astype(BF16))[:1]
    first_pass(jnp.sqrt(qn2 * kn_ref[:1, :1]))
    sums = l_ref[...]
    trusted = (jnp.min(sums) >= FAST_SUM_MIN) & (jnp.max(sums) <= FAST_SUM_MAX)

    @pl.when(trusted)
    def _():
        write_out(sums)

    @pl.when(jnp.logical_not(trusted))
    def _():
        running_max_pass()
        write_out(acc_ref[D:D + 1, :])


def flash_attention(q, k, vt, *, n_heads, n_kv, tq, tk):
    b, s, _ = q.shape
    D = HEAD_DIM
    G = n_heads // n_kv
    assert s % (2 * tk) == 0 and s // tk >= 2
    kern = functools.partial(_flash_kernel, G=G, tq=tq, tk=tk, nk=s // tk)
    return pl.pallas_call(
        kern,
        grid=(b, n_kv, s // tq),
        in_specs=[
            pl.BlockSpec((1, tq, G * D), lambda bi, kv, qi: (bi, qi, kv)),
            pl.BlockSpec((1, s, D), lambda bi, kv, qi: (bi, 0, kv)),
            pl.BlockSpec((1, 1, D + ONES_ROWS, s), lambda bi, kv, qi: (bi, kv, 0, 0)),
        ],
        out_specs=pl.BlockSpec((1, tq, G * D), lambda bi, kv, qi: (bi, qi, kv)),
        out_shape=jax.ShapeDtypeStruct((b, s, n_heads * D), BF16),
        scratch_shapes=[pltpu.VMEM((D + ONES_ROWS, G * tq), F32), pltpu.VMEM((8, D), F32), pltpu.VMEM((2, tk, G * tq), BF16),
                        pltpu.VMEM((1, G * tq), F32)],
        compiler_params=_params(("parallel", "parallel", "arbitrary")),
        name="flash_attention",
    )(q, k, vt)


def _rope_angles(pos, dim, theta):
    inv = theta ** (-jnp.arange(0, dim, 2, dtype=jnp.float32) / dim)
    return pos.astype(jnp.float32)[:, None] * inv[None, :]


def _retention_tables(s):
    ang = _rope_angles(jnp.arange(s), RET_DK, RET_THETA)
    c, sn = jnp.cos(ang), jnp.sin(ang)
    sign = jnp.where(jnp.arange(RET_DK) < RET_DK // 2, -1.0, 1.0).astype(F32)
    return jnp.tile(c, (1, 2)), jnp.tile(sn, (1, 2)) * sign


def _axial_tables(s):
    rows = s // GRID_W
    half = HEAD_DIM // 2
    ar = _rope_angles(jnp.arange(rows), half, AX_THETA)
    ac = _rope_angles(jnp.arange(GRID_W), half, AX_THETA)
    cr, sr = (jnp.repeat(f(ar), GRID_W, axis=0) for f in (jnp.cos, jnp.sin))
    ccol, scol = (jnp.tile(f(ac), (rows, 1)) for f in (jnp.cos, jnp.sin))
    sign = jnp.where(jnp.arange(HEAD_DIM) < HEAD_DIM // 2, -1.0, 1.0).astype(F32)
    cc = jnp.tile(jnp.concatenate([cr, ccol], axis=-1), (1, 2))
    ss = jnp.tile(jnp.concatenate([sr, scol], axis=-1), (1, 2)) * sign
    return cc, ss


def kernel(x, norm_mix, norm_mlp, w_in_even, w_out_even, ret_decay_logit, ret_norm, swa_q_norm,
           swa_k_norm, swa_sink, t5_table, w_in_odd, w_out_odd, ax_q_norm, ax_k_norm, w_mlp_up,
           w_mlp_down):
    b, s, d = x.shape
    t = b * s
    depth = norm_mix.shape[0]
    ret_heads = ret_decay_logit.shape[-1]
    ret_q = ret_heads * RET_DK
    ret_v = ret_heads * RET_DV
    swa_heads = swa_sink.shape[-1]
    swa_q = swa_heads * HEAD_DIM
    swa_kv = SWA_KV_HEADS * HEAD_DIM
    ax_q = w_out_odd.shape[1]
    ax_heads = ax_q // HEAD_DIM

    x2 = x.reshape(t, d)
    ret_tabs = _retention_tables(s)
    ax_tabs = _axial_tables(s)

    for layer in range(depth):
        i = layer // 2
        if layer % 2 == 0:
            proj = norm_proj(x2, norm_mix[layer], w_in_even, i, tm=DENSE_ROWS)
            proj = proj.reshape(b, s, -1)
            ya = retention(proj, ret_decay_logit[i], ret_norm[i], *ret_tabs, n_heads=ret_heads,
                           q_off=0, k_off=ret_q, v_off=2 * ret_q, g_off=2 * ret_q + ret_v, cs=32)
            off = 2 * ret_q + 2 * ret_v
            yb = window_attention(proj, swa_sink[i], t5_table, swa_q_norm[i], swa_k_norm[i],
                                  n_heads=swa_heads, n_kv=SWA_KV_HEADS, q_off=off,
                                  k_off=off + swa_q, v_off=off + swa_q + swa_kv)
            acts, w_out = [ya.reshape(t, -1), yb.reshape(t, -1)], w_out_even
        else:
            qp, kp, vt = axial_proj(x2, norm_mix[layer], w_in_odd[i], ax_tabs,
                                    ax_q_norm[i], ax_k_norm[i], seq=s, n_heads=ax_heads,
                                    n_kv=AX_KV_HEADS, tm=2 * DENSE_ROWS)
            y = flash_attention(qp.reshape(b, s, -1), kp.reshape(b, s, -1), vt, n_heads=ax_heads,
                                n_kv=AX_KV_HEADS, tq=1024, tk=512)
            acts, w_out = [y.reshape(t, -1)], w_out_odd
        x2 = out_mlp(x2, acts, w_out, i, norm_mlp[layer], w_mlp_up, w_mlp_down, layer,
                     tm=DENSE_ROWS)
    return x2.reshape(b, s, d)
```
